```python
import jax, jax.numpy as jnp
from jax import lax
import numpy as np

D_MODEL = 1024
BATCH = 8
SEQ = 4096
DEPTH = 1

HEAD_DIM = 64
N_HEADS = (D_MODEL // 2) // HEAD_DIM
N_KV_HEADS = 2
GROUP = N_HEADS // N_KV_HEADS
Q_COLS = N_HEADS * HEAD_DIM
KV_COLS = N_KV_HEADS * HEAD_DIM
WINDOW = 128
ATTN_BLOCK = 128
NUM_BUCKETS = 32
MAX_DISTANCE = 128
CONV_CH = D_MODEL - Q_COLS
CONV_WIDTH = 31
IN_COLS = Q_COLS + 2 * KV_COLS + 2 * CONV_CH
N_EXPERTS = 256
TOP_K = 8
N_GROUPS = 8
TOPK_GROUPS = 4
EXPERT_HIDDEN = 256
SHARED_HIDDEN = 256
ROUTED_SCALE = 2.5
MOE_BLOCK = 128
EPS = 1e-6

kernel_name = "hybrid_conv_swa_sink_t5bias_moe_adaln"


def _rms_norm(x):
    xf = x.astype(jnp.float32)
    return (xf * lax.rsqrt(jnp.mean(xf * xf, axis=-1, keepdims=True) + EPS)).astype(x.dtype)


def _layer_norm(x, g, b):
    xf = x.astype(jnp.float32)
    mu = jnp.mean(xf, axis=-1, keepdims=True)
    var = jnp.mean(jnp.square(xf - mu), axis=-1, keepdims=True)
    y = (xf - mu) * lax.rsqrt(var + EPS) * g.astype(jnp.float32) + b.astype(jnp.float32)
    return y.astype(x.dtype)


def _t5_causal_buckets(dist):
    n = np.maximum(dist, 0)
    max_exact = NUM_BUCKETS // 2
    large = max_exact + (np.log(np.maximum(n, 1) / max_exact) / np.log(MAX_DISTANCE / max_exact)
                         * (NUM_BUCKETS - max_exact)).astype(np.int32)
    large = np.minimum(large, NUM_BUCKETS - 1)
    return np.where(n < max_exact, n, large).astype(np.int32)


def _sliding_window_attention(q, k, v, sinks, rel_bias):
    B, S, _ = q.shape
    L = ATTN_BLOCK
    nb = S // L
    q = q.reshape(B, nb, L, N_KV_HEADS, GROUP, HEAD_DIM)
    k = k.reshape(B, nb, L, N_KV_HEADS, HEAD_DIM)
    v = v.reshape(B, nb, L, N_KV_HEADS, HEAD_DIM)
    pad = ((0, 0), (1, 0), (0, 0), (0, 0), (0, 0))
    kk = jnp.concatenate([jnp.pad(k, pad)[:, :-1], k], axis=2)
    vv = jnp.concatenate([jnp.pad(v, pad)[:, :-1], v], axis=2)
    qi = np.arange(L)[:, None]
    ki = np.arange(2 * L)[None, :]
    dist = qi + L - ki
    band = (dist >= 0) & (dist < WINDOW)
    not_first = np.arange(nb)[:, None, None] > 0
    mask = band[None] & (not_first | (ki >= L)[None])
    bias = rel_bias.astype(jnp.float32)[_t5_causal_buckets(dist)]
    bias = jnp.transpose(bias, (2, 0, 1)).reshape(N_KV_HEADS, GROUP, L, 2 * L)
    logits = jnp.einsum('bnqkgd,bnskd->bnkgqs', q, kk,
                        preferred_element_type=jnp.float32) * (HEAD_DIM ** -0.5) + bias
    logits = jnp.where(jnp.asarray(mask)[None, :, None, None], logits, -jnp.inf)
    sink = sinks.astype(jnp.float32).reshape(1, 1, N_KV_HEADS, GROUP, 1, 1)
    m = jnp.maximum(jnp.max(logits, axis=-1, keepdims=True), sink)
    p = jnp.exp(logits - m)
    denom = jnp.sum(p, axis=-1) + jnp.exp(sink - m)[..., 0]
    o = jnp.einsum('bnkgqs,bnskd->bnqkgd', p, vv.astype(jnp.float32))
    o = o / jnp.transpose(denom, (0, 1, 4, 2, 3))[..., None]
    return o.reshape(B, S, Q_COLS).astype(q.dtype)


def _conv_module(a, gate, conv_w, conv_b, ln_g, ln_b):
    u = a * jax.nn.sigmoid(gate)
    u = lax.conv_general_dilated(u, conv_w[:, None, :], window_strides=(1,),
                                 padding=[(CONV_WIDTH - 1, 0)],
                                 dimension_numbers=('NWC', 'WIO', 'NWC'),
                                 feature_group_count=CONV_CH) + conv_b
    return jax.nn.silu(_layer_norm(u, ln_g, ln_b))


def _moe(h, w_router, router_bias, w_gate, w_up, w_down, w_sh_gate, w_sh_up, w_sh_down):
    B, S, D = h.shape
    T = B * S
    ht = h.reshape(T, D)
    scores = jax.nn.sigmoid(jnp.dot(ht, w_router, preferred_element_type=jnp.float32))
    sel = scores + router_bias.astype(jnp.float32)
    grp = sel.reshape(T, N_GROUPS, N_EXPERTS // N_GROUPS)
    grp_score = jnp.sum(lax.top_k(grp, 2)[0], axis=-1)
    _, gidx = lax.top_k(grp_score, TOPK_GROUPS)
    gmask = jnp.sum(jax.nn.one_hot(gidx, N_GROUPS, dtype=jnp.float32), axis=-2) > 0
    emask = jnp.repeat(gmask, N_EXPERTS // N_GROUPS, axis=-1)
    _, eidx = lax.top_k(jnp.where(emask, sel, -jnp.inf), TOP_K)
    gw = jnp.take_along_axis(scores, eidx, axis=-1)
    gw = gw / jnp.sum(gw, axis=-1, keepdims=True) * ROUTED_SCALE
    A = T * TOP_K
    flat_e = eidx.reshape(A)
    order = jnp.argsort(flat_e)
    se = flat_e[order]
    stok = (order // TOP_K).astype(jnp.int32)
    sw = gw.reshape(A)[order]
    counts = jnp.bincount(flat_e, length=N_EXPERTS)
    starts = jnp.cumsum(counts) - counts
    padded = (counts + MOE_BLOCK - 1) // MOE_BLOCK * MOE_BLOCK
    pends = jnp.cumsum(padded)
    pstarts = pends - padded
    dest = pstarts[se] + jnp.arange(A, dtype=jnp.int32) - starts[se]
    n_blocks = (A + MOE_BLOCK - 1) // MOE_BLOCK + N_EXPERTS
    P = n_blocks * MOE_BLOCK
    row_tok = jnp.full((P,), T, jnp.int32).at[dest].set(stok)
    row_w = jnp.zeros((P,), jnp.float32).at[dest].set(sw)
    blk_e = jnp.minimum(jnp.searchsorted(pends, jnp.arange(n_blocks, dtype=jnp.int32) * MOE_BLOCK,
                                         side='right'), N_EXPERTS - 1)
    h_pad = jnp.concatenate([ht, jnp.zeros((1, D), ht.dtype)], axis=0)

    def step(acc, blk):
        tok, w, e = blk
        xb = h_pad[tok]
        hid = jax.nn.silu(xb @ w_gate[e]) * (xb @ w_up[e])
        out = (hid @ w_down[e]).astype(jnp.float32) * w[:, None]
        return acc.at[tok].add(out), None

    acc, _ = lax.scan(step, jnp.zeros((T + 1, D), jnp.float32),
                      (row_tok.reshape(n_blocks, MOE_BLOCK), row_w.reshape(n_blocks, MOE_BLOCK), blk_e))
    shared = (jax.nn.silu(ht @ w_sh_gate) * (ht @ w_sh_up)) @ w_sh_down
    return (acc[:T] + shared.astype(jnp.float32)).astype(h.dtype).reshape(B, S, D)


def setup_inputs(seed: int = 0) -> dict:
    key = jax.random.key(seed)
    ks = jax.random.split(key, 24)

    def nrm(k, shape, s):
        return jax.random.normal(k, shape, jnp.float32) * s

    D = D_MODEL
    return {
        "x": nrm(ks[0], (BATCH, SEQ, D), 1.0),
        "c": nrm(ks[1], (BATCH, D), 1.0),
        "w_ada": nrm(ks[2], (DEPTH, D, 6 * D), 0.5 * D ** -0.5),
        "b_ada": nrm(ks[3], (DEPTH, 6 * D), 0.02),
        "w_in": nrm(ks[4], (DEPTH, D, IN_COLS), D ** -0.5),
        "attn_sinks": nrm(ks[5], (DEPTH, N_HEADS), 1.0),
        "rel_bias": nrm(ks[6], (NUM_BUCKETS, N_HEADS), 0.5),
        "attn_norm_g": 1.0 + nrm(ks[7], (DEPTH, Q_COLS), 0.02),
        "conv_w": nrm(ks[8], (DEPTH, CONV_WIDTH, CONV_CH), CONV_WIDTH ** -0.5),
        "conv_b": nrm(ks[9], (DEPTH, CONV_CH), 0.02),
        "conv_ln_g": 1.0 + nrm(ks[10], (DEPTH, CONV_CH), 0.02),
        "conv_ln_b": nrm(ks[11], (DEPTH, CONV_CH), 0.02),
        "w_out": nrm(ks[12], (DEPTH, D, D), D ** -0.5),
        "w_router": nrm(ks[13], (DEPTH, D, N_EXPERTS), D ** -0.5),
        "router_bias": nrm(ks[14], (DEPTH, N_EXPERTS), 0.01),
        "w_exp_gate": nrm(ks[15], (DEPTH, N_EXPERTS, D, EXPERT_HIDDEN), D ** -0.5),
        "w_exp_up": nrm(ks[16], (DEPTH, N_EXPERTS, D, EXPERT_HIDDEN), D ** -0.5),
        "w_exp_down": nrm(ks[17], (DEPTH, N_EXPERTS, EXPERT_HIDDEN, D), EXPERT_HIDDEN ** -0.5),
        "w_sh_gate": nrm(ks[18], (DEPTH, D, SHARED_HIDDEN), D ** -0.5),
        "w_sh_up": nrm(ks[19], (DEPTH, D, SHARED_HIDDEN), D ** -0.5),
        "w_sh_down": nrm(ks[20], (DEPTH, SHARED_HIDDEN, D), SHARED_HIDDEN ** -0.5),
        "final_norm_g": 1.0 + nrm(ks[21], (D,), 0.02),
    }


def reference(x, c, w_ada, b_ada, w_in, attn_sinks, rel_bias, attn_norm_g, conv_w, conv_b,
              conv_ln_g, conv_ln_b, w_out, w_router, router_bias, w_exp_gate, w_exp_up,
              w_exp_down, w_sh_gate, w_sh_up, w_sh_down, final_norm_g):
    s_q = Q_COLS
    s_k = s_q + KV_COLS
    s_v = s_k + KV_COLS
    s_a = s_v + CONV_CH
    for l in range(DEPTH):
        mod = jnp.dot(jax.nn.silu(c), w_ada[l]) + b_ada[l]
        sh1, sc1, g1, sh2, sc2, g2 = jnp.split(mod, 6, axis=-1)
        h = _rms_norm(x) * (1 + sc1[:, None]) + sh1[:, None]
        proj = h @ w_in[l]
        q = proj[..., :s_q]
        k = proj[..., s_q:s_k]
        v = proj[..., s_k:s_v]
        a = proj[..., s_v:s_a]
        gt = proj[..., s_a:]
        attn = _rms_norm(_sliding_window_attention(q, k, v, attn_sinks[l], rel_bias)) * attn_norm_g[l]
        conv = _conv_module(a, gt, conv_w[l], conv_b[l], conv_ln_g[l], conv_ln_b[l])
        mixed = jnp.concatenate([attn, conv], axis=-1) @ w_out[l]
        x = x + g1[:, None] * mixed
        h2 = _rms_norm(x) * (1 + sc2[:, None]) + sh2[:, None]
        x = x + g2[:, None] * _moe(h2, w_router[l], router_bias[l], w_exp_gate[l], w_exp_up[l],
                                    w_exp_down[l], w_sh_gate[l], w_sh_up[l], w_sh_down[l])
    return _rms_norm(x) * final_norm_g
```

```python
import functools

import numpy as np
import jax
import jax.numpy as jnp
from jax import lax
from jax.experimental import pallas as pl
from jax.experimental.pallas import tpu as pltpu

F32 = jnp.float32
BF16 = jnp.bfloat16

D_MODEL = 1024
HEAD_DIM = 64
N_HEADS = 8
N_KV_HEADS = 2
GROUP = N_HEADS // N_KV_HEADS
Q_COLS = N_HEADS * HEAD_DIM
KV_COLS = N_KV_HEADS * HEAD_DIM
ATTN_BLOCK = 128
WINDOW = 128
NUM_BUCKETS = 32
MAX_DISTANCE = 128
CONV_CH = D_MODEL - Q_COLS
CONV_WIDTH = 31
IN_COLS = Q_COLS + 2 * KV_COLS + 2 * CONV_CH
N_EXPERTS = 256
TOP_K = 8
N_GROUPS = 8
GROUP_SIZE = N_EXPERTS // N_GROUPS
TOPK_GROUPS = 4
EXPERT_HIDDEN = 256
ROUTED_SCALE = 2.5
EPS = 1e-6

MIX_ROWS = 512
HALO = 32
ROUTE_ROWS = 512
EXPERT_ROWS = 128
COMBINE_ROWS = 64
VMEM_LIMIT = 56 * 1024 * 1024


def _sigmoid(v):
    return 1.0 / (1.0 + jnp.exp(-v))


def _rms(v):
    return v * lax.rsqrt(jnp.mean(v * v, axis=-1, keepdims=True) + EPS)


def _split_bf16(a):
    hi = a.astype(BF16)
    lo = (a - hi.astype(F32)).astype(BF16)
    return hi, lo


def _dot(a, b):
    return jnp.dot(a, b, preferred_element_type=F32)


def _dot_nt(a, b):
    return lax.dot_general(a, b, (((1,), (1,)), ((), ())), preferred_element_type=F32)


def _dot3(a, b, dot):
    ah, al = _split_bf16(a)
    bh, bl = _split_bf16(b)
    return dot(ah, bh) + (dot(ah, bl) + dot(al, bh))


def _ada_kernel(c_ref, w_ref, b_ref, o_ref):
    c = c_ref[...]
    s = c * _sigmoid(c)
    o_ref[...] = _dot3(s, w_ref[...], _dot) + b_ref[...]


def _ada_call(c, w_ada, b_ada):
    bsz, d = c.shape
    n = w_ada.shape[1]
    tn = 1536
    return pl.pallas_call(
        _ada_kernel,
        grid=(n // tn,),
        in_specs=[
            pl.BlockSpec((bsz, d), lambda i: (0, 0)),
            pl.BlockSpec((d, tn), lambda i: (0, i)),
            pl.BlockSpec((1, tn), lambda i: (0, i)),
        ],
        out_specs=pl.BlockSpec((bsz, tn), lambda i: (0, i)),
        out_shape=jax.ShapeDtypeStruct((bsz, n), F32),
        compiler_params=pltpu.CompilerParams(
            dimension_semantics=("arbitrary",), vmem_limit_bytes=VMEM_LIMIT),
        name="ada",
    )(c, w_ada, b_ada.reshape(1, n))


def _bucket_table():
    qi = np.arange(ATTN_BLOCK)[:, None]
    ki = np.arange(2 * ATTN_BLOCK)[None, :]
    dist = qi + ATTN_BLOCK - ki
    n = np.maximum(dist, 0)
    max_exact = NUM_BUCKETS // 2
    large = max_exact + (np.log(np.maximum(n, 1) / max_exact) / np.log(MAX_DISTANCE / max_exact)
                         * (NUM_BUCKETS - max_exact)).astype(np.int32)
    large = np.minimum(large, NUM_BUCKETS - 1)
    bkt = np.where(n < max_exact, n, large).astype(np.int32)
    band = (dist >= 0) & (dist < WINDOW)
    return np.where(band, bkt, -1).astype(np.int32)


def _mixer_kernel(x_ref, mod_ref, win_ref, sinks_ref, relb_ref, bkt_ref, ang_ref, cw_ref, cb_ref,
                  lng_ref, lnb_ref, wout_ref, wrt_ref, wsg_ref, wsu_ref, wsd_ref,
                  xs1_ref, h2_ref, lgt_ref,
                  kv_ref, uext_ref, bias_ref):
    tm = x_ref.shape[1]
    nblk = tm // ATTN_BLOCK
    j = pl.program_id(1)

    @pl.when((pl.program_id(0) == 0) & (j == 0))
    def _build_bias():
        bkt = bkt_ref[...]
        for h in range(N_HEADS):
            acc = jnp.full(bkt.shape, -jnp.inf, F32)
            for b in range(NUM_BUCKETS):
                acc = jnp.where(bkt == b, relb_ref[b, h], acc)
            bias_ref[h] = acc

    @pl.when(j == 0)
    def _reset_history():
        kv_ref[...] = jnp.zeros(kv_ref.shape, F32)
        uext_ref[0:HALO, :] = jnp.zeros((HALO, CONV_CH), F32)

    x = x_ref[0]
    mod = mod_ref[0]
    sh1, sc1, g1 = mod[0:1], mod[1:2], mod[2:3]
    sh2, sc2, g2 = mod[3:4], mod[4:5], mod[5:6]

    h = _rms(x) * (1.0 + sc1) + sh1
    proj = _dot(h.astype(BF16), win_ref[...])

    qb = (proj[:, :Q_COLS] * (HEAD_DIM ** -0.5)).astype(BF16)
    kv_cur = proj[:, Q_COLS:Q_COLS + 2 * KV_COLS]
    kvb = jnp.concatenate([kv_ref[...], kv_cur], axis=0).astype(BF16)
    kv_ref[...] = kv_cur[tm - ATTN_BLOCK:, :]
    not_first = j > 0
    col = lax.broadcasted_iota(jnp.int32, (GROUP * ATTN_BLOCK, 2 * ATTN_BLOCK), 1)
    attn_rows = []
    for i in range(nblk):
        r0, r1, r2 = i * ATTN_BLOCK, (i + 1) * ATTN_BLOCK, (i + 2) * ATTN_BLOCK
        heads = [None] * N_HEADS
        for kh in range(N_KV_HEADS):
            kpc = kvb[r0:r2, kh * HEAD_DIM:(kh + 1) * HEAD_DIM]
            vpc = kvb[r0:r2, KV_COLS + kh * HEAD_DIM:KV_COLS + (kh + 1) * HEAD_DIM]
            qg = jnp.concatenate(
                [qb[r0:r1, (kh * GROUP + g) * HEAD_DIM:(kh * GROUP + g + 1) * HEAD_DIM]
                 for g in range(GROUP)], axis=0)
            bias = bias_ref[kh * GROUP:(kh + 1) * GROUP].reshape(GROUP * ATTN_BLOCK, 2 * ATTN_BLOCK)
            logit = _dot_nt(qg, kpc) + bias
            if i == 0:
                logit = jnp.where((col >= ATTN_BLOCK) | not_first, logit, -jnp.inf)
            sink = jnp.concatenate(
                [jnp.full((ATTN_BLOCK, 1), sinks_ref[kh * GROUP + g], F32) for g in range(GROUP)], axis=0)
            m = jnp.maximum(jnp.max(logit, axis=-1, keepdims=True), sink)
            p = jnp.exp(logit - m)
            den = jnp.sum(p, axis=-1, keepdims=True) + jnp.exp(sink - m)
            o = _dot(p.astype(BF16), vpc) / den
            for g in range(GROUP):
                heads[kh * GROUP + g] = o[g * ATTN_BLOCK:(g + 1) * ATTN_BLOCK, :]
        attn_rows.append(jnp.concatenate(heads, axis=1))
    attn = jnp.concatenate(attn_rows, axis=0)
    attn = _rms(attn) * ang_ref[...]

    a = proj[:, Q_COLS + 2 * KV_COLS:Q_COLS + 2 * KV_COLS + CONV_CH]
    gt = proj[:, Q_COLS + 2 * KV_COLS + CONV_CH:]
    uext_ref[HALO:HALO + tm, :] = a * _sigmoid(gt)
    cw = cw_ref[...]
    base = HALO - (CONV_WIDTH - 1)
    acc = jnp.zeros((tm, CONV_CH), F32)
    for t in range(CONV_WIDTH):
        acc = acc + cw[t:t + 1, :] * uext_ref[base + t:base + t + tm, :]
    uext_ref[0:HALO, :] = uext_ref[tm:tm + HALO, :]
    cv = acc + cb_ref[...]
    mu = jnp.mean(cv, axis=-1, keepdims=True)
    var = jnp.mean(jnp.square(cv - mu), axis=-1, keepdims=True)
    cv = (cv - mu) * lax.rsqrt(var + EPS) * lng_ref[...] + lnb_ref[...]
    cv = cv * _sigmoid(cv)

    mixed = (_dot(attn.astype(BF16), wout_ref[0:Q_COLS, :])
             + _dot(cv.astype(BF16), wout_ref[Q_COLS:, :]))
    x1 = x + g1 * mixed

    h2 = _rms(x1) * (1.0 + sc2) + sh2
    h2_ref[0] = h2
    lgt_ref[...] = _dot3(wrt_ref[...], h2, _dot_nt)
    h2b = h2.astype(BF16)
    sg = _dot(h2b, wsg_ref[...])
    su = _dot(h2b, wsu_ref[...])
    shared = _dot(((sg * _sigmoid(sg)) * su).astype(BF16), wsd_ref[...])
    xs1_ref[0] = x1 + g2 * shared


def _mixer_call(x, mod3, w_in_b, sinks, rel_bias, ang, cw, cb, lng, lnb, w_out_b, w_rt, wsg_b, wsu_b, wsd_b):
    bsz, seq, d = x.shape
    tm = MIX_ROWS
    nj = seq // tm
    bkt = jnp.asarray(_bucket_table())
    full = lambda shape: pl.BlockSpec(shape, lambda b, j: (0,) * len(shape))
    smem = pl.BlockSpec(memory_space=pltpu.SMEM)
    return pl.pallas_call(
        _mixer_kernel,
        grid=(bsz, nj),
        in_specs=[
            pl.BlockSpec((1, tm, d), lambda b, j: (b, j, 0)),
            pl.BlockSpec((1, 6, d), lambda b, j: (b, 0, 0)),
            full((d, IN_COLS)),
            smem, smem,
            full((ATTN_BLOCK, 2 * ATTN_BLOCK)),
            full((1, Q_COLS)),
            full((CONV_WIDTH, CONV_CH)),
            full((1, CONV_CH)), full((1, CONV_CH)), full((1, CONV_CH)),
            full((d, d)),
            full((N_EXPERTS, d)),
            full((d, EXPERT_HIDDEN)), full((d, EXPERT_HIDDEN)), full((EXPERT_HIDDEN, d)),
        ],
        out_specs=[
            pl.BlockSpec((1, tm, d), lambda b, j: (b, j, 0)),
            pl.BlockSpec((1, tm, d), lambda b, j: (b, j, 0)),
            pl.BlockSpec((N_EXPERTS, tm), lambda b, j: (0, b * nj + j)),
        ],
        out_shape=[
            jax.ShapeDtypeStruct((bsz, seq, d), F32),
            jax.ShapeDtypeStruct((bsz, seq, d), F32),
            jax.ShapeDtypeStruct((N_EXPERTS, bsz * seq), F32),
        ],
        scratch_shapes=[
            pltpu.VMEM((ATTN_BLOCK, 2 * KV_COLS), F32),
            pltpu.VMEM((HALO + tm, CONV_CH), F32),
            pltpu.VMEM((N_HEADS, ATTN_BLOCK, 2 * ATTN_BLOCK), F32),
        ],
        compiler_params=pltpu.CompilerParams(
            dimension_semantics=("arbitrary", "arbitrary"), vmem_limit_bytes=VMEM_LIMIT),
        name="mixer",
    )(x, mod3, w_in_b, sinks, rel_bias, bkt, ang, cw, cb, lng, lnb, w_out_b, w_rt, wsg_b, wsu_b, wsd_b)


def _route_kernel(lgt_ref, rb_ref, eidx_ref, gwt_ref, rank_ref, cnt_ref, base_ref):
    tr = lgt_ref.shape[1]
    i = pl.program_id(0)

    @pl.when(i == 0)
    def _():
        base_ref[...] = jnp.zeros(base_ref.shape, F32)

    scores = _sigmoid(lgt_ref[...])
    sel = scores + rb_ref[...]
    neg = -jnp.inf

    sel3 = sel.reshape(N_GROUPS, GROUP_SIZE, tr)
    loc = lax.broadcasted_iota(jnp.int32, sel3.shape, 1)
    m1 = jnp.max(sel3, axis=1, keepdims=True)
    i1 = jnp.min(jnp.where(sel3 == m1, loc, GROUP_SIZE), axis=1, keepdims=True)
    m2 = jnp.max(jnp.where(loc == i1, neg, sel3), axis=1, keepdims=True)
    gscore = (m1 + m2).reshape(N_GROUPS, tr)

    gio = lax.broadcasted_iota(jnp.int32, gscore.shape, 0)
    gmask = jnp.zeros(gscore.shape, jnp.bool_)
    cur = gscore
    for _ in range(TOPK_GROUPS):
        m = jnp.max(cur, axis=0, keepdims=True)
        idx = jnp.min(jnp.where(cur == m, gio, N_GROUPS), axis=0, keepdims=True)
        pick = gio == idx
        gmask = gmask | pick
        cur = jnp.where(pick, neg, cur)
    emask = jnp.broadcast_to(gmask.reshape(N_GROUPS, 1, tr), sel3.shape).reshape(N_EXPERTS, tr)

    rio = lax.broadcasted_iota(jnp.int32, sel.shape, 0)
    cur = jnp.where(emask, sel, neg)
    picks, eidx, gsc = [], [], []
    for _ in range(TOP_K):
        m = jnp.max(cur, axis=0, keepdims=True)
        idx = jnp.min(jnp.where(cur == m, rio, N_EXPERTS), axis=0, keepdims=True)
        pick = rio == idx
        picks.append(pick)
        eidx.append(idx)
        gsc.append(jnp.sum(jnp.where(pick, scores, 0.0), axis=0, keepdims=True))
        cur = jnp.where(pick, neg, cur)
    gsum = gsc[0]
    for k in range(1, TOP_K):
        gsum = gsum + gsc[k]
    gw = jnp.concatenate([g / gsum * ROUTED_SCALE for g in gsc], axis=0)
    eidx_ref[...] = jnp.concatenate(eidx, axis=0)

    gw_pad = jnp.concatenate([gw, jnp.zeros((128 - TOP_K, tr), F32)], axis=0)
    gwt_ref[...] = gw_pad.T[:, 0:TOP_K]

    chosen = picks[0]
    for k in range(1, TOP_K):
        chosen = chosen | picks[k]
    onehot = jnp.where(chosen, 1.0, 0.0)
    tri = (lax.broadcasted_iota(jnp.int32, (tr, tr), 0)
           < lax.broadcasted_iota(jnp.int32, (tr, tr), 1))
    before = _dot(onehot.astype(BF16), jnp.where(tri, 1.0, 0.0).astype(BF16))
    pos = base_ref[...] + before
    rank_ref[...] = jnp.concatenate(
        [jnp.sum(jnp.where(p, pos, 0.0), axis=0, keepdims=True) for p in picks], axis=0).astype(jnp.int32)
    total = base_ref[...] + jnp.sum(onehot, axis=1, keepdims=True)
    base_ref[...] = total
    cnt_ref[...] = total


def _route_call(lgt, router_bias):
    e, t = lgt.shape
    tr = ROUTE_ROWS
    return pl.pallas_call(
        _route_kernel,
        grid=(t // tr,),
        in_specs=[
            pl.BlockSpec((e, tr), lambda i: (0, i)),
            pl.BlockSpec((e, 1), lambda i: (0, 0)),
        ],
        out_specs=[
            pl.BlockSpec((TOP_K, tr), lambda i: (0, i)),
            pl.BlockSpec((tr, TOP_K), lambda i: (i, 0)),
            pl.BlockSpec((TOP_K, tr), lambda i: (0, i)),
            pl.BlockSpec((e, 1), lambda i: (0, 0)),
        ],
        out_shape=[
            jax.ShapeDtypeStruct((TOP_K, t), jnp.int32),
            jax.ShapeDtypeStruct((t, TOP_K), F32),
            jax.ShapeDtypeStruct((TOP_K, t), jnp.int32),
            jax.ShapeDtypeStruct((e, 1), F32),
        ],
        scratch_shapes=[pltpu.VMEM((e, 1), F32)],
        compiler_params=pltpu.CompilerParams(
            dimension_semantics=("arbitrary",), vmem_limit_bytes=VMEM_LIMIT),
        name="route",
    )(lgt, router_bias.reshape(e, 1))


def _row_gather(src_hbm, idx_ref, dst_ref, sem, n):
    def body(r, carry):
        pltpu.make_async_copy(src_hbm.at[pl.ds(idx_ref[r], 1)], dst_ref.at[pl.ds(r, 1)], sem).start()
        return carry
    lax.fori_loop(0, n, body, 0, unroll=8)


def _row_gather_wait(src_hbm, dst_ref, sem):
    pltpu.make_async_copy(src_hbm.at[pl.ds(0, dst_ref.shape[0])], dst_ref, sem).wait()


def _expert_kernel(blk_e_ref, nvalid_ref, tok_ref, tokn_ref, h2_hbm, wg_ref, wu_ref, wd_ref,
                   y_ref, xbuf, sem, wgb, wub, wdb):
    b = pl.program_id(0)
    nvalid = nvalid_ref[0]
    slot = b % 2

    @pl.when((b == 0) & (nvalid > 0))
    def _():
        _row_gather(h2_hbm, tok_ref.at[0, 0], xbuf.at[0], sem.at[0], EXPERT_ROWS)

    @pl.when(b + 1 < nvalid)
    def _():
        _row_gather(h2_hbm, tokn_ref.at[0, 0], xbuf.at[1 - slot], sem.at[1 - slot], EXPERT_ROWS)

    new_expert = (b == 0) | (blk_e_ref[b] != blk_e_ref[jnp.maximum(b - 1, 0)])

    @pl.when((b < nvalid) & new_expert)
    def _():
        wgb[...] = wg_ref[0].astype(BF16)
        wub[...] = wu_ref[0].astype(BF16)
        wdb[...] = wd_ref[0].astype(BF16)

    @pl.when(b < nvalid)
    def _():
        _row_gather_wait(h2_hbm, xbuf.at[slot], sem.at[slot])
        xb = xbuf[slot].astype(BF16)
        g = _dot(xb, wgb[...])
        u = _dot(xb, wub[...])
        hid = (g * _sigmoid(g)) * u
        y_ref[...] = _dot(hid.astype(BF16), wdb[...])

    @pl.when(b >= nvalid)
    def _():
        y_ref[...] = jnp.zeros(y_ref.shape, F32)


def _expert_call(blk_e, nvalid, row_tok, h2, w_gate, w_up, w_down):
    nblk = blk_e.shape[0]
    d = h2.shape[1]
    rows = EXPERT_ROWS
    tok3 = row_tok.reshape(nblk, 1, rows)
    grid_spec = pltpu.PrefetchScalarGridSpec(
        num_scalar_prefetch=2,
        grid=(nblk,),
        in_specs=[
            pl.BlockSpec((1, 1, rows), lambda b, be, nv: (b, 0, 0), memory_space=pltpu.SMEM),
            pl.BlockSpec((1, 1, rows), lambda b, be, nv: (jnp.minimum(b + 1, nblk - 1), 0, 0),
                         memory_space=pltpu.SMEM),
            pl.BlockSpec(memory_space=pl.ANY),
            pl.BlockSpec((1, d, EXPERT_HIDDEN), lambda b, be, nv: (be[b], 0, 0)),
            pl.BlockSpec((1, d, EXPERT_HIDDEN), lambda b, be, nv: (be[b], 0, 0)),
            pl.BlockSpec((1, EXPERT_HIDDEN, d), lambda b, be, nv: (be[b], 0, 0)),
        ],
        out_specs=pl.BlockSpec((rows, d), lambda b, be, nv: (b, 0)),
        scratch_shapes=[
            pltpu.VMEM((2, rows, d), F32),
            pltpu.SemaphoreType.DMA((2,)),
            pltpu.VMEM((d, EXPERT_HIDDEN), BF16),
            pltpu.VMEM((d, EXPERT_HIDDEN), BF16),
            pltpu.VMEM((EXPERT_HIDDEN, d), BF16),
        ],
    )
    return pl.pallas_call(
        _expert_kernel,
        grid_spec=grid_spec,
        out_shape=jax.ShapeDtypeStruct((nblk * rows, d), F32),
        compiler_params=pltpu.CompilerParams(
            dimension_semantics=("arbitrary",), vmem_limit_bytes=VMEM_LIMIT),
        name="experts",
    )(blk_e, nvalid, tok3, tok3, h2, w_gate, w_up, w_down)


def _combine_kernel(dst_ref, dstn_ref, gwt_ref, xs1_ref, mod_ref, fg_ref, y_hbm, o_ref, ybuf, sem):
    i = pl.program_id(0)
    n = pl.num_programs(0)
    tc = o_ref.shape[0]
    slot = i % 2

    def gather(idx_ref, s):
        for k in range(TOP_K):
            _row_gather(y_hbm, idx_ref.at[0, k], ybuf.at[s, k], sem.at[s], tc)

    @pl.when(i == 0)
    def _():
        gather(dst_ref, 0)

    @pl.when(i + 1 < n)
    def _():
        gather(dstn_ref, 1 - slot)

    for k in range(TOP_K):
        _row_gather_wait(y_hbm, ybuf.at[slot, k], sem.at[slot])
    gwt = gwt_ref[...]
    acc = gwt[:, 0:1] * ybuf[slot, 0]
    for k in range(1, TOP_K):
        acc = acc + gwt[:, k:k + 1] * ybuf[slot, k]
    g2 = mod_ref[0][5:6]
    x2 = xs1_ref[...] + g2 * acc
    o_ref[...] = _rms(x2) * fg_ref[...]


def _combine_call(dest, gwt, xs1, mod3, final_g, y, seq):
    t, d = xs1.shape
    tc = COMBINE_ROWS
    n = t // tc
    dst3 = dest.reshape(TOP_K, n, tc).transpose(1, 0, 2)
    per_seq = seq // tc
    return pl.pallas_call(
        _combine_kernel,
        grid=(n,),
        in_specs=[
            pl.BlockSpec((1, TOP_K, tc), lambda i: (i, 0, 0), memory_space=pltpu.SMEM),
            pl.BlockSpec((1, TOP_K, tc), lambda i: (jnp.minimum(i + 1, n - 1), 0, 0),
                         memory_space=pltpu.SMEM),
            pl.BlockSpec((tc, TOP_K), lambda i: (i, 0)),
            pl.BlockSpec((tc, d), lambda i: (i, 0)),
            pl.BlockSpec((1, 6, d), lambda i: (i // per_seq, 0, 0)),
            pl.BlockSpec((1, d), lambda i: (0, 0)),
            pl.BlockSpec(memory_space=pl.ANY),
        ],
        out_specs=pl.BlockSpec((tc, d), lambda i: (i, 0)),
        out_shape=jax.ShapeDtypeStruct((t, d), F32),
        scratch_shapes=[
            pltpu.VMEM((2, TOP_K, tc, d), F32),
            pltpu.SemaphoreType.DMA((2,)),
        ],
        compiler_params=pltpu.CompilerParams(
            dimension_semantics=("arbitrary",), vmem_limit_bytes=VMEM_LIMIT),
        name="combine",
    )(dst3, dst3, gwt, xs1, mod3, final_g.reshape(1, d), y)


def kernel(x, c, w_ada, b_ada, w_in, attn_sinks, rel_bias, attn_norm_g, conv_w, conv_b, conv_ln_g,
           conv_ln_b, w_out, w_router, router_bias, w_exp_gate, w_exp_up, w_exp_down, w_sh_gate,
           w_sh_up, w_sh_down, final_norm_g):
    bsz, seq, d = x.shape
    t = bsz * seq
    depth = w_ada.shape[0]
    assert depth == 1 and seq % MIX_ROWS == 0 and t % ROUTE_ROWS == 0 and t % COMBINE_ROWS == 0

    mod3 = _ada_call(c, w_ada[0], b_ada[0]).reshape(bsz, 6, d)
    xs1, h2, lgt = _mixer_call(
        x, mod3, w_in[0].astype(BF16), attn_sinks[0], rel_bias,
        attn_norm_g[0].reshape(1, Q_COLS), conv_w[0], conv_b[0].reshape(1, CONV_CH),
        conv_ln_g[0].reshape(1, CONV_CH), conv_ln_b[0].reshape(1, CONV_CH),
        w_out[0].astype(BF16), w_router[0].T,
        w_sh_gate[0].astype(BF16), w_sh_up[0].astype(BF16), w_sh_down[0].astype(BF16))
    xs1 = xs1.reshape(t, d)
    h2 = h2.reshape(t, d)

    eidx, gwt, rank, counts = _route_call(lgt, router_bias[0])

    rows = EXPERT_ROWS
    counts = counts.reshape(N_EXPERTS).astype(jnp.int32)
    padded = (counts + rows - 1) // rows * rows
    pends = jnp.cumsum(padded)
    pstarts = pends - padded
    dest = pstarts[eidx] + rank
    nblk = (t * TOP_K) // rows + N_EXPERTS
    tok = jnp.broadcast_to(jnp.arange(t, dtype=jnp.int32)[None, :], (TOP_K, t))
    row_tok = jnp.zeros((nblk * rows,), jnp.int32).at[dest.reshape(-1)].set(tok.reshape(-1))
    blk_start = jnp.arange(nblk, dtype=jnp.int32) * rows
    blk_e = jnp.minimum(jnp.searchsorted(pends, blk_start, side='right'), N_EXPERTS - 1).astype(jnp.int32)
    nvalid = (pends[-1] // rows).astype(jnp.int32).reshape(1)

    y = _expert_call(blk_e, nvalid, row_tok, h2, w_exp_gate[0], w_exp_up[0], w_exp_down[0])
    out = _combine_call(dest, gwt, xs1, mod3, final_norm_g, y, seq)
    return out.reshape(bsz, seq, d)
```

```python
import numpy as np
import jax
import jax.numpy as jnp
from jax import lax
from jax.experimental import pallas as pl
from jax.experimental.pallas import tpu as pltpu

F32 = jnp.float32
BF16 = jnp.bfloat16
U32 = jnp.uint32
I32 = jnp.int32

D_MODEL = 1024
HEAD_DIM = 64
N_HEADS = 8
N_KV_HEADS = 2
GROUP = N_HEADS // N_KV_HEADS
Q_COLS = N_HEADS * HEAD_DIM
KV_COLS = N_KV_HEADS * HEAD_DIM
ATTN_BLOCK = 128
WINDOW = 128
NUM_BUCKETS = 32
MAX_DISTANCE = 128
CONV_CH = D_MODEL - Q_COLS
CONV_WIDTH = 31
IN_COLS = Q_COLS + 2 * KV_COLS + 2 * CONV_CH
N_EXPERTS = 256
TOP_K = 8
N_GROUPS = 8
GROUP_SIZE = N_EXPERTS // N_GROUPS
TOPK_GROUPS = 4
EXPERT_HIDDEN = 256
ROUTED_SCALE = 2.5
EPS = 1e-6

MIX_ROWS = 512
HALO = 32
TILE = 256
LROWS = TILE * TOP_K
SORT_CHUNK = 512
EXPERT_ROWS = 256
LANES = 128
PACK_W = D_MODEL // 2
PACK_S = PACK_W // LANES
VMEM_LIMIT = 56 * 1024 * 1024


def _sigmoid(v):
    return 1.0 / (1.0 + jnp.exp(-v))


def _rms(v):
    return v * lax.rsqrt(jnp.mean(v * v, axis=-1, keepdims=True) + EPS)


def _split_bf16(a):
    hi = a.astype(BF16)
    lo = (a - hi.astype(F32)).astype(BF16)
    return hi, lo


def _dot(a, b):
    return jnp.dot(a, b, preferred_element_type=F32)


def _dot_nt(a, b):
    return lax.dot_general(a, b, (((1,), (1,)), ((), ())), preferred_element_type=F32)


def _dot_tn(a, b):
    return lax.dot_general(a, b, (((0,), (0,)), ((), ())), preferred_element_type=F32)


def _dot3(a, b, dot):
    ah, al = _split_bf16(a)
    bh, bl = _split_bf16(b)
    return dot(ah, bh) + (dot(ah, bl) + dot(al, bh))


def _pack_rows(v):
    hi = lax.bitcast_convert_type(v[:, :PACK_W], U32) & jnp.uint32(0xFFFF0000)
    lo = lax.bitcast_convert_type(v[:, PACK_W:], U32) >> 16
    return hi | lo


def _unpack_rows(u):
    hi = lax.bitcast_convert_type(u & jnp.uint32(0xFFFF0000), F32)
    lo = lax.bitcast_convert_type(u << 16, F32)
    return jnp.concatenate([hi, lo], axis=1).astype(BF16)


def _load_packed(ref3, r0, n):
    flat = ref3.reshape(ref3.shape[0] * PACK_S, LANES)
    return jnp.concatenate(
        [flat[pl.ds(r0 * PACK_S + c, n, stride=PACK_S), :] for c in range(PACK_S)], axis=1)


def _store_packed(ref3, r0, n, u):
    flat = ref3.reshape(ref3.shape[0] * PACK_S, LANES)
    for c in range(PACK_S):
        flat[pl.ds(r0 * PACK_S + c, n, stride=PACK_S), :] = u[:, c * LANES:(c + 1) * LANES]


def _ada_kernel(c_ref, w_ref, b_ref, o_ref):
    c = c_ref[...]
    s = c * _sigmoid(c)
    o_ref[...] = _dot3(s, w_ref[...], _dot) + b_ref[...]


def _ada_call(c, w_ada, b_ada):
    bsz, d = c.shape
    n = w_ada.shape[1]
    tn = 1536
    return pl.pallas_call(
        _ada_kernel,
        grid=(n // tn,),
        in_specs=[
            pl.BlockSpec((bsz, d), lambda i: (0, 0)),
            pl.BlockSpec((d, tn), lambda i: (0, i)),
            pl.BlockSpec((1, tn), lambda i: (0, i)),
        ],
        out_specs=pl.BlockSpec((bsz, tn), lambda i: (0, i)),
        out_shape=jax.ShapeDtypeStruct((bsz, n), F32),
        compiler_params=pltpu.CompilerParams(
            dimension_semantics=("arbitrary",), vmem_limit_bytes=VMEM_LIMIT),
        name="ada",
    )(c, w_ada, b_ada.reshape(1, n))


def _bucket_table():
    qi = np.arange(ATTN_BLOCK)[:, None]
    ki = np.arange(2 * ATTN_BLOCK)[None, :]
    dist = qi + ATTN_BLOCK - ki
    n = np.maximum(dist, 0)
    max_exact = NUM_BUCKETS // 2
    large = max_exact + (np.log(np.maximum(n, 1) / max_exact) / np.log(MAX_DISTANCE / max_exact)
                         * (NUM_BUCKETS - max_exact)).astype(np.int32)
    large = np.minimum(large, NUM_BUCKETS - 1)
    bkt = np.where(n < max_exact, n, large).astype(np.int32)
    band = (dist >= 0) & (dist < WINDOW)
    return np.where(band, bkt, -1).astype(np.int32)


def _mixer_kernel(x_ref, mod_ref, win_ref, sinks_ref, relb_ref, bkt_ref, ang_ref, cw_ref, cb_ref,
                  lng_ref, lnb_ref, wout_ref, wrt_ref, wsg_ref, wsu_ref, wsd_ref,
                  xs1_ref, h2_ref, lgt_ref,
                  kv_ref, uext_ref, bias_ref):
    tm = x_ref.shape[1]
    nblk = tm // ATTN_BLOCK
    j = pl.program_id(1)

    @pl.when((pl.program_id(0) == 0) & (j == 0))
    def _build_bias():
        bkt = bkt_ref[...]
        for h in range(N_HEADS):
            acc = jnp.full(bkt.shape, -jnp.inf, F32)
            for b in range(NUM_BUCKETS):
                acc = jnp.where(bkt == b, relb_ref[b, h], acc)
            bias_ref[h] = acc

    @pl.when(j == 0)
    def _reset_history():
        kv_ref[...] = jnp.zeros(kv_ref.shape, F32)
        uext_ref[0:HALO, :] = jnp.zeros((HALO, CONV_CH), F32)

    x = x_ref[0]
    mod = mod_ref[0]
    sh1, sc1, g1 = mod[0:1], mod[1:2], mod[2:3]
    sh2, sc2, g2 = mod[3:4], mod[4:5], mod[5:6]

    h = _rms(x) * (1.0 + sc1) + sh1
    proj = _dot(h.astype(BF16), win_ref[...])

    qb = (proj[:, :Q_COLS] * (HEAD_DIM ** -0.5)).astype(BF16)
    kv_cur = proj[:, Q_COLS:Q_COLS + 2 * KV_COLS]
    kvb = jnp.concatenate([kv_ref[...], kv_cur], axis=0).astype(BF16)
    kv_ref[...] = kv_cur[tm - ATTN_BLOCK:, :]
    not_first = j > 0
    col = lax.broadcasted_iota(jnp.int32, (GROUP * ATTN_BLOCK, 2 * ATTN_BLOCK), 1)
    attn_rows = []
    for i in range(nblk):
        r0, r1, r2 = i * ATTN_BLOCK, (i + 1) * ATTN_BLOCK, (i + 2) * ATTN_BLOCK
        heads = [None] * N_HEADS
        for kh in range(N_KV_HEADS):
            kpc = kvb[r0:r2, kh * HEAD_DIM:(kh + 1) * HEAD_DIM]
            vpc = kvb[r0:r2, KV_COLS + kh * HEAD_DIM:KV_COLS + (kh + 1) * HEAD_DIM]
            qg = jnp.concatenate(
                [qb[r0:r1, (kh * GROUP + g) * HEAD_DIM:(kh * GROUP + g + 1) * HEAD_DIM]
                 for g in range(GROUP)], axis=0)
            bias = bias_ref[kh * GROUP:(kh + 1) * GROUP].reshape(GROUP * ATTN_BLOCK, 2 * ATTN_BLOCK)
            logit = _dot_nt(qg, kpc) + bias
            if i == 0:
                logit = jnp.where((col >= ATTN_BLOCK) | not_first, logit, -jnp.inf)
            sink = jnp.concatenate(
                [jnp.full((ATTN_BLOCK, 1), sinks_ref[kh * GROUP + g], F32) for g in range(GROUP)], axis=0)
            m = jnp.maximum(jnp.max(logit, axis=-1, keepdims=True), sink)
            p = jnp.exp(logit - m)
            den = jnp.sum(p, axis=-1, keepdims=True) + jnp.exp(sink - m)
            o = _dot(p.astype(BF16), vpc) / den
            for g in range(GROUP):
                heads[kh * GROUP + g] = o[g * ATTN_BLOCK:(g + 1) * ATTN_BLOCK, :]
        attn_rows.append(jnp.concatenate(heads, axis=1))
    attn = jnp.concatenate(attn_rows, axis=0)
    attn = _rms(attn) * ang_ref[...]

    a = proj[:, Q_COLS + 2 * KV_COLS:Q_COLS + 2 * KV_COLS + CONV_CH]
    gt = proj[:, Q_COLS + 2 * KV_COLS + CONV_CH:]
    uext_ref[HALO:HALO + tm, :] = a * _sigmoid(gt)
    cw = cw_ref[...]
    base = HALO - (CONV_WIDTH - 1)
    acc = jnp.zeros((tm, CONV_CH), F32)
    for t in range(CONV_WIDTH):
        acc = acc + cw[t:t + 1, :] * uext_ref[base + t:base + t + tm, :]
    uext_ref[0:HALO, :] = uext_ref[tm:tm + HALO, :]
    cv = acc + cb_ref[...]
    mu = jnp.mean(cv, axis=-1, keepdims=True)
    var = jnp.mean(jnp.square(cv - mu), axis=-1, keepdims=True)
    cv = (cv - mu) * lax.rsqrt(var + EPS) * lng_ref[...] + lnb_ref[...]
    cv = cv * _sigmoid(cv)

    mixed = (_dot(attn.astype(BF16), wout_ref[0:Q_COLS, :])
             + _dot(cv.astype(BF16), wout_ref[Q_COLS:, :]))
    x1 = x + g1 * mixed

    h2 = _rms(x1) * (1.0 + sc2) + sh2
    lgt_ref[...] = _dot3(wrt_ref[...], h2, _dot_nt)
    h2b = h2.astype(BF16)
    h2_ref[0] = h2b
    sg = _dot(h2b, wsg_ref[...])
    su = _dot(h2b, wsu_ref[...])
    shared = _dot(((sg * _sigmoid(sg)) * su).astype(BF16), wsd_ref[...])
    xs1_ref[0] = x1 + g2 * shared


def _mixer_call(x, mod3, w_in_b, sinks, rel_bias, ang, cw, cb, lng, lnb, w_out_b, w_rt, wsg_b, wsu_b, wsd_b):
    bsz, seq, d = x.shape
    tm = MIX_ROWS
    nj = seq // tm
    bkt = jnp.asarray(_bucket_table())
    full = lambda shape: pl.BlockSpec(shape, lambda b, j: (0,) * len(shape))
    smem = pl.BlockSpec(memory_space=pltpu.SMEM)
    return pl.pallas_call(
        _mixer_kernel,
        grid=(bsz, nj),
        in_specs=[
            pl.BlockSpec((1, tm, d), lambda b, j: (b, j, 0)),
            pl.BlockSpec((1, 6, d), lambda b, j: (b, 0, 0)),
            full((d, IN_COLS)),
            smem, smem,
            full((ATTN_BLOCK, 2 * ATTN_BLOCK)),
            full((1, Q_COLS)),
            full((CONV_WIDTH, CONV_CH)),
            full((1, CONV_CH)), full((1, CONV_CH)), full((1, CONV_CH)),
            full((d, d)),
            full((N_EXPERTS, d)),
            full((d, EXPERT_HIDDEN)), full((d, EXPERT_HIDDEN)), full((EXPERT_HIDDEN, d)),
        ],
        out_specs=[
            pl.BlockSpec((1, tm, d), lambda b, j: (b, j, 0)),
            pl.BlockSpec((1, tm, d), lambda b, j: (b, j, 0)),
            pl.BlockSpec((N_EXPERTS, tm), lambda b, j: (0, b * nj + j)),
        ],
        out_shape=[
            jax.ShapeDtypeStruct((bsz, seq, d), F32),
            jax.ShapeDtypeStruct((bsz, seq, d), BF16),
            jax.ShapeDtypeStruct((N_EXPERTS, bsz * seq), F32),
        ],
        scratch_shapes=[
            pltpu.VMEM((ATTN_BLOCK, 2 * KV_COLS), F32),
            pltpu.VMEM((HALO + tm, CONV_CH), F32),
            pltpu.VMEM((N_HEADS, ATTN_BLOCK, 2 * ATTN_BLOCK), F32),
        ],
        compiler_params=pltpu.CompilerParams(
            dimension_semantics=("arbitrary", "arbitrary"), vmem_limit_bytes=VMEM_LIMIT),
        name="mixer",
    )(x, mod3, w_in_b, sinks, rel_bias, bkt, ang, cw, cb, lng, lnb, w_out_b, w_rt, wsg_b, wsu_b, wsd_b)


def _route_kernel(lgt_ref, rb_ref, lp_ref, gw_ref, cnt_ref):
    tr = lgt_ref.shape[1]
    i = pl.program_id(0)

    @pl.when(i == 0)
    def _():
        cnt_ref[...] = jnp.zeros(cnt_ref.shape, F32)

    scores = _sigmoid(lgt_ref[...])
    sel = scores + rb_ref[...]
    neg = -jnp.inf

    sel3 = sel.reshape(N_GROUPS, GROUP_SIZE, tr)
    loc = lax.broadcasted_iota(I32, sel3.shape, 1)
    m1 = jnp.max(sel3, axis=1, keepdims=True)
    i1 = jnp.min(jnp.where(sel3 == m1, loc, GROUP_SIZE), axis=1, keepdims=True)
    m2 = jnp.max(jnp.where(loc == i1, neg, sel3), axis=1, keepdims=True)
    gscore = (m1 + m2).reshape(N_GROUPS, tr)

    gio = lax.broadcasted_iota(I32, gscore.shape, 0)
    gmask = jnp.zeros(gscore.shape, jnp.bool_)
    cur = gscore
    for _ in range(TOPK_GROUPS):
        m = jnp.max(cur, axis=0, keepdims=True)
        idx = jnp.min(jnp.where(cur == m, gio, N_GROUPS), axis=0, keepdims=True)
        pick = gio == idx
        gmask = gmask | pick
        cur = jnp.where(pick, neg, cur)
    emask = jnp.broadcast_to(gmask.reshape(N_GROUPS, 1, tr), sel3.shape).reshape(N_EXPERTS, tr)

    rio = lax.broadcasted_iota(I32, sel.shape, 0)
    cur = jnp.where(emask, sel, neg)
    picks, gsc = [], []
    for _ in range(TOP_K):
        m = jnp.max(cur, axis=0, keepdims=True)
        idx = jnp.min(jnp.where(cur == m, rio, N_EXPERTS), axis=0, keepdims=True)
        pick = rio == idx
        picks.append(pick)
        gsc.append(jnp.sum(jnp.where(pick, scores, 0.0), axis=0, keepdims=True))
        cur = jnp.where(pick, neg, cur)
    gsum = gsc[0]
    for k in range(1, TOP_K):
        gsum = gsum + gsc[k]
    gw_ref[...] = jnp.concatenate([g / gsum * ROUTED_SCALE for g in gsc], axis=0)

    chosen = picks[0]
    for k in range(1, TOP_K):
        chosen = chosen | picks[k]
    onehot = jnp.where(chosen, 1.0, 0.0)
    tri = (lax.broadcasted_iota(I32, (tr, tr), 0) < lax.broadcasted_iota(I32, (tr, tr), 1))
    before = _dot(onehot.astype(BF16), jnp.where(tri, 1.0, 0.0).astype(BF16))
    n = jnp.sum(onehot, axis=1, keepdims=True)
    low = (lax.broadcasted_iota(I32, (N_EXPERTS, N_EXPERTS), 1)
           < lax.broadcasted_iota(I32, (N_EXPERTS, N_EXPERTS), 0))
    start = _dot(jnp.where(low, 1.0, 0.0).astype(BF16),
                 jnp.broadcast_to(n, (N_EXPERTS, LANES)).astype(BF16))[:, 0:1]
    pos = start + before
    lp_ref[...] = jnp.concatenate(
        [jnp.sum(jnp.where(p, pos, 0.0), axis=0, keepdims=True) for p in picks], axis=0).astype(I32)

    lane = lax.broadcasted_iota(I32, cnt_ref.shape, 1)
    cnt_ref[...] += jnp.where(lane == i, n, 0.0)


def _route_call(lgt, router_bias):
    e, t = lgt.shape
    ntiles = t // TILE
    ntp = (ntiles + LANES - 1) // LANES * LANES
    return pl.pallas_call(
        _route_kernel,
        grid=(ntiles,),
        in_specs=[
            pl.BlockSpec((e, TILE), lambda i: (0, i)),
            pl.BlockSpec((e, 1), lambda i: (0, 0)),
        ],
        out_specs=[
            pl.BlockSpec((TOP_K, TILE), lambda i: (0, i)),
            pl.BlockSpec((TOP_K, TILE), lambda i: (0, i)),
            pl.BlockSpec((e, ntp), lambda i: (0, 0)),
        ],
        out_shape=[
            jax.ShapeDtypeStruct((TOP_K, t), I32),
            jax.ShapeDtypeStruct((TOP_K, t), F32),
            jax.ShapeDtypeStruct((e, ntp), F32),
        ],
        compiler_params=pltpu.CompilerParams(
            dimension_semantics=("arbitrary",), vmem_limit_bytes=VMEM_LIMIT),
        name="route",
    )(lgt, router_bias.reshape(e, 1))


_T_LOCAL, _T_LEN, _T_GLOBAL = 0, 1, 2


def _segment_copies(tab_ref, half, local_ref, global_hbm, sem, to_global):
    r = 3 * half

    def body(e, carry):
        n = tab_ref[0, r + _T_LEN, e]

        @pl.when(n > 0)
        def _():
            loc = local_ref.at[pl.ds(tab_ref[0, r + _T_LOCAL, e], n)]
            glo = global_hbm.at[pl.ds(tab_ref[0, r + _T_GLOBAL, e], n)]
            if to_global:
                pltpu.make_async_copy(loc, glo, sem).start()
            else:
                pltpu.make_async_copy(glo, loc, sem).start()
        return carry

    lax.fori_loop(0, N_EXPERTS, body, 0)


def _segment_wait(local_ref, global_hbm, sem):
    pltpu.make_async_copy(global_hbm.at[pl.ds(0, LROWS)], local_ref, sem).wait()


def _dispatch_kernel(tab_ref, pad_ref, h2_ref, lp_ref, xs_hbm, xl0, xl1, zb, sem):
    s = pl.program_id(0)
    ns = pl.num_programs(0)

    @pl.when(s == 0)
    def _zero_block_padding():
        zb[...] = jnp.zeros(zb.shape, U32)

        def copy(e):
            n = pad_ref[1, e]
            return pltpu.make_async_copy(zb.at[pl.ds(0, n)], xs_hbm.at[pl.ds(pad_ref[0, e], n)], sem.at[2])

        def start(e, carry):
            @pl.when(pad_ref[1, e] > 0)
            def _():
                copy(e).start()
            return carry

        def wait(e, carry):
            @pl.when(pad_ref[1, e] > 0)
            def _():
                copy(e).wait()
            return carry

        lax.fori_loop(0, N_EXPERTS, start, 0)
        lax.fori_loop(0, N_EXPERTS, wait, 0)

        def tail(b):
            return pltpu.make_async_copy(zb, xs_hbm.at[pl.ds(b * EXPERT_ROWS, EXPERT_ROWS)], sem.at[2])

        def tail_start(b, carry):
            tail(b).start()
            return carry

        def tail_wait(b, carry):
            tail(b).wait()
            return carry

        lax.fori_loop(pad_ref[2, 0], pad_ref[2, 1], tail_start, 0)
        lax.fori_loop(pad_ref[2, 0], pad_ref[2, 1], tail_wait, 0)

    for half, xl in ((0, xl0), (1, xl1)):
        @pl.when(s > 0)
        def _():
            _segment_wait(xl, xs_hbm, sem.at[half])

        rows = h2_ref[half * TILE:(half + 1) * TILE, :]
        lp = lp_ref[:, half * TILE:(half + 1) * TILE]
        for c in range(LROWS // SORT_CHUNK):
            j = lax.broadcasted_iota(I32, (SORT_CHUNK, TILE), 0) + c * SORT_CHUNK
            hit = j == lp[0:1, :]
            for k in range(1, TOP_K):
                hit = hit | (j == lp[k:k + 1, :])
            onehot = jnp.where(hit, 1.0, 0.0).astype(BF16)
            _store_packed(xl, c * SORT_CHUNK, SORT_CHUNK, _pack_rows(_dot(onehot, rows)))
        _segment_copies(tab_ref, half, xl, xs_hbm, sem.at[half], to_global=True)

    @pl.when(s == ns - 1)
    def _():
        _segment_wait(xl0, xs_hbm, sem.at[0])
        _segment_wait(xl1, xs_hbm, sem.at[1])


def _dispatch_call(tab, pad, h2, lp, nrows):
    t, d = h2.shape
    ns = t // (2 * TILE)
    return pl.pallas_call(
        _dispatch_kernel,
        grid=(ns,),
        in_specs=[
            pl.BlockSpec((1, 6, N_EXPERTS), lambda s: (s, 0, 0), memory_space=pltpu.SMEM),
            pl.BlockSpec(memory_space=pltpu.SMEM),
            pl.BlockSpec((2 * TILE, d), lambda s: (s, 0)),
            pl.BlockSpec((TOP_K, 2 * TILE), lambda s: (0, s)),
        ],
        out_specs=pl.BlockSpec(memory_space=pl.ANY),
        out_shape=jax.ShapeDtypeStruct((nrows, PACK_S, LANES), U32),
        scratch_shapes=[
            pltpu.VMEM((LROWS, PACK_S, LANES), U32),
            pltpu.VMEM((LROWS, PACK_S, LANES), U32),
            pltpu.VMEM((EXPERT_ROWS, PACK_S, LANES), U32),
            pltpu.SemaphoreType.DMA((3,)),
        ],
        compiler_params=pltpu.CompilerParams(
            dimension_semantics=("arbitrary",), vmem_limit_bytes=VMEM_LIMIT),
        name="dispatch",
    )(tab, pad, h2, lp)


def _expert_kernel(blk_e_ref, nvalid_ref, xs_ref, wg_ref, wu_ref, wd_ref, y_ref, wgb, wub, wdb):
    b = pl.program_id(0)
    nvalid = nvalid_ref[0]
    new_expert = (b == 0) | (blk_e_ref[b] != blk_e_ref[jnp.maximum(b - 1, 0)])

    @pl.when((b < nvalid) & new_expert)
    def _():
        wgb[...] = wg_ref[0].astype(BF16)
        wub[...] = wu_ref[0].astype(BF16)
        wdb[...] = wd_ref[0].astype(BF16)

    @pl.when(b < nvalid)
    def _():
        xb = _unpack_rows(_load_packed(xs_ref, 0, EXPERT_ROWS))
        g = _dot(xb, wgb[...])
        u = _dot(xb, wub[...])
        hid = (g * _sigmoid(g)) * u
        y = _dot(hid.astype(BF16), wdb[...])
        _store_packed(y_ref, 0, EXPERT_ROWS, _pack_rows(y.astype(BF16).astype(F32)))

    @pl.when(b >= nvalid)
    def _():
        y_ref[...] = jnp.zeros(y_ref.shape, U32)


def _expert_call(blk_e, nvalid, xs, w_gate, w_up, w_down):
    nblk = blk_e.shape[0]
    d = w_gate.shape[1]
    rows = EXPERT_ROWS
    blk = lambda b, be, nv: (jnp.minimum(b, nv[0] - 1), 0, 0)
    wsel = lambda b, be, nv: (be[jnp.minimum(b, nv[0] - 1)], 0, 0)
    grid_spec = pltpu.PrefetchScalarGridSpec(
        num_scalar_prefetch=2,
        grid=(nblk,),
        in_specs=[
            pl.BlockSpec((rows, PACK_S, LANES), blk),
            pl.BlockSpec((1, d, EXPERT_HIDDEN), wsel),
            pl.BlockSpec((1, d, EXPERT_HIDDEN), wsel),
            pl.BlockSpec((1, EXPERT_HIDDEN, d), wsel),
        ],
        out_specs=pl.BlockSpec((rows, PACK_S, LANES), lambda b, be, nv: (b, 0, 0)),
        scratch_shapes=[
            pltpu.VMEM((d, EXPERT_HIDDEN), BF16),
            pltpu.VMEM((d, EXPERT_HIDDEN), BF16),
            pltpu.VMEM((EXPERT_HIDDEN, d), BF16),
        ],
    )
    return pl.pallas_call(
        _expert_kernel,
        grid_spec=grid_spec,
        out_shape=jax.ShapeDtypeStruct((nblk * rows, PACK_S, LANES), U32),
        compiler_params=pltpu.CompilerParams(
            dimension_semantics=("arbitrary",), vmem_limit_bytes=VMEM_LIMIT),
        name="experts",
    )(blk_e, nvalid, xs, w_gate, w_up, w_down)


def _combine_kernel(tab_ref, tabn_ref, lp_ref, gw_ref, xs1_ref, mod_ref, fg_ref, y_hbm, o_ref, yl0, yl1, sem):
    s = pl.program_id(0)
    ns = pl.num_programs(0)

    @pl.when(s == 0)
    def _():
        _segment_copies(tab_ref, 0, yl0, y_hbm, sem.at[0], to_global=False)

    _segment_copies(tab_ref, 1, yl1, y_hbm, sem.at[1], to_global=False)
    g2 = mod_ref[0][5:6]
    for half, yl in ((0, yl0), (1, yl1)):
        _segment_wait(yl, y_hbm, sem.at[half])
        lp = lp_ref[:, half * TILE:(half + 1) * TILE]
        gw = gw_ref[:, half * TILE:(half + 1) * TILE]
        acc = jnp.zeros((TILE, D_MODEL), F32)
        for c in range(LROWS // SORT_CHUNK):
            j = lax.broadcasted_iota(I32, (SORT_CHUNK, TILE), 0) + c * SORT_CHUNK
            wt = jnp.zeros((SORT_CHUNK, TILE), F32)
            for k in range(TOP_K):
                wt = jnp.where(j == lp[k:k + 1, :], gw[k:k + 1, :], wt)
            yv = _unpack_rows(_load_packed(yl, c * SORT_CHUNK, SORT_CHUNK))
            acc = acc + _dot_tn(wt.astype(BF16), yv)
        if half == 0:
            @pl.when(s + 1 < ns)
            def _():
                _segment_copies(tabn_ref, 0, yl0, y_hbm, sem.at[0], to_global=False)
        x2 = xs1_ref[half * TILE:(half + 1) * TILE, :] + g2 * acc
        o_ref[half * TILE:(half + 1) * TILE, :] = _rms(x2) * fg_ref[...]


def _combine_call(tab, lp, gw, xs1, mod3, final_g, y, seq):
    t, d = xs1.shape
    ns = t // (2 * TILE)
    per_seq = seq // (2 * TILE)
    return pl.pallas_call(
        _combine_kernel,
        grid=(ns,),
        in_specs=[
            pl.BlockSpec((1, 6, N_EXPERTS), lambda s: (s, 0, 0), memory_space=pltpu.SMEM),
            pl.BlockSpec((1, 6, N_EXPERTS), lambda s: (jnp.minimum(s + 1, ns - 1), 0, 0),
                         memory_space=pltpu.SMEM),
            pl.BlockSpec((TOP_K, 2 * TILE), lambda s: (0, s)),
            pl.BlockSpec((TOP_K, 2 * TILE), lambda s: (0, s)),
            pl.BlockSpec((2 * TILE, d), lambda s: (s, 0)),
            pl.BlockSpec((1, 6, d), lambda s: (s // per_seq, 0, 0)),
            pl.BlockSpec((1, d), lambda s: (0, 0)),
            pl.BlockSpec(memory_space=pl.ANY),
        ],
        out_specs=pl.BlockSpec((2 * TILE, d), lambda s: (s, 0)),
        out_shape=jax.ShapeDtypeStruct((t, d), F32),
        scratch_shapes=[
            pltpu.VMEM((LROWS, PACK_S, LANES), U32),
            pltpu.VMEM((LROWS, PACK_S, LANES), U32),
            pltpu.SemaphoreType.DMA((2,)),
        ],
        compiler_params=pltpu.CompilerParams(
            dimension_semantics=("arbitrary",), vmem_limit_bytes=VMEM_LIMIT),
        name="combine",
    )(tab, tab, lp, gw, xs1, mod3, final_g.reshape(1, d), y)


def kernel(x, c, w_ada, b_ada, w_in, attn_sinks, rel_bias, attn_norm_g, conv_w, conv_b, conv_ln_g,
           conv_ln_b, w_out, w_router, router_bias, w_exp_gate, w_exp_up, w_exp_down, w_sh_gate,
           w_sh_up, w_sh_down, final_norm_g):
    bsz, seq, d = x.shape
    t = bsz * seq
    assert w_ada.shape[0] == 1 and d == D_MODEL
    assert seq % MIX_ROWS == 0 and seq % (2 * TILE) == 0

    mod3 = _ada_call(c, w_ada[0], b_ada[0]).reshape(bsz, 6, d)
    xs1, h2, lgt = _mixer_call(
        x, mod3, w_in[0].astype(BF16), attn_sinks[0], rel_bias,
        attn_norm_g[0].reshape(1, Q_COLS), conv_w[0], conv_b[0].reshape(1, CONV_CH),
        conv_ln_g[0].reshape(1, CONV_CH), conv_ln_b[0].reshape(1, CONV_CH),
        w_out[0].astype(BF16), w_router[0].T,
        w_sh_gate[0].astype(BF16), w_sh_up[0].astype(BF16), w_sh_down[0].astype(BF16))
    xs1 = xs1.reshape(t, d)
    h2 = h2.reshape(t, d)

    lp, gw, cnt = _route_call(lgt, router_bias[0])

    rows = EXPERT_ROWS
    ntiles = t // TILE
    n = cnt[:, :ntiles].T.astype(I32)
    local = jnp.cumsum(n, axis=1) - n
    earlier = jnp.cumsum(n, axis=0) - n
    total = jnp.sum(n, axis=0)
    padded = (total + rows - 1) // rows * rows
    pends = jnp.cumsum(padded)
    pstarts = pends - padded
    tab = jnp.stack([local, n, pstarts[None, :] + earlier], axis=1)
    tab = tab.reshape(ntiles // 2, 6, N_EXPERTS)
    nblk = (t * TOP_K) // rows + N_EXPERTS
    blk_start = jnp.arange(nblk, dtype=I32) * rows
    blk_e = jnp.minimum(jnp.sum((pends[None, :] <= blk_start[:, None]).astype(I32), axis=1), N_EXPERTS - 1)
    nvalid = (pends[-1] // rows).astype(I32).reshape(1)
    unused = jnp.zeros((N_EXPERTS,), I32).at[0].set(nvalid[0]).at[1].set(nblk)
    pad = jnp.stack([pstarts + total, padded - total, unused], axis=0)

    xs = _dispatch_call(tab, pad, h2, lp, nblk * rows)
    y = _expert_call(blk_e, nvalid, xs, w_exp_gate[0], w_exp_up[0], w_exp_down[0])
    out = _combine_call(tab, lp, gw, xs1, mod3, final_norm_g, y, seq)
    return out.reshape(bsz, seq, d)
```

```python
import numpy as np
import jax
import jax.numpy as jnp
from jax import lax
from jax.experimental import pallas as pl
from jax.experimental.pallas import tpu as pltpu

F32 = jnp.float32
BF16 = jnp.bfloat16
U32 = jnp.uint32
I32 = jnp.int32

D_MODEL = 1024
HEAD_DIM = 64
N_HEADS = 8
N_KV_HEADS = 2
GROUP = N_HEADS // N_KV_HEADS
Q_COLS = N_HEADS * HEAD_DIM
KV_COLS = N_KV_HEADS * HEAD_DIM
ATTN_BLOCK = 128
WINDOW = 128
NUM_BUCKETS = 32
MAX_DISTANCE = 128
CONV_CH = D_MODEL - Q_COLS
CONV_WIDTH = 31
IN_COLS = Q_COLS + 2 * KV_COLS + 2 * CONV_CH
N_EXPERTS = 256
TOP_K = 8
N_GROUPS = 8
GROUP_SIZE = N_EXPERTS // N_GROUPS
TOPK_GROUPS = 4
EXPERT_HIDDEN = 256
ROUTED_SCALE = 2.5
EPS = 1e-6

MIX_ROWS = 512
HALO = 32
TILE = 256
LROWS = TILE * TOP_K
SORT_CHUNK = 512
EXPERT_ROWS = 256
LANES = 128
SUBLANES = 8
PACK_W = D_MODEL // 2
PACK_S = PACK_W // LANES
VMEM_LIMIT = 56 * 1024 * 1024


def _sigmoid(v):
    return 1.0 / (1.0 + jnp.exp(-v))


def _rms(v):
    return v * lax.rsqrt(jnp.mean(v * v, axis=-1, keepdims=True) + EPS)


def _split_bf16(a):
    hi = a.astype(BF16)
    lo = (a - hi.astype(F32)).astype(BF16)
    return hi, lo


def _dot(a, b):
    return jnp.dot(a, b, preferred_element_type=F32)


def _dot_nt(a, b):
    return lax.dot_general(a, b, (((1,), (1,)), ((), ())), preferred_element_type=F32)


def _dot_tn(a, b):
    return lax.dot_general(a, b, (((0,), (0,)), ((), ())), preferred_element_type=F32)


def _dot3(a, b, dot):
    ah, al = _split_bf16(a)
    bh, bl = _split_bf16(b)
    return dot(ah, bh) + (dot(ah, bl) + dot(al, bh))


def _pack_rows(v):
    hi = lax.bitcast_convert_type(v[:, :PACK_W], U32) & jnp.uint32(0xFFFF0000)
    lo = lax.bitcast_convert_type(v[:, PACK_W:], U32) >> 16
    return hi | lo


def _unpack_rows(u):
    hi = lax.bitcast_convert_type(u & jnp.uint32(0xFFFF0000), F32)
    lo = lax.bitcast_convert_type(u << 16, F32)
    return jnp.concatenate([hi, lo], axis=1).astype(BF16)


def _load_packed(ref3, r0, n):
    flat = ref3.reshape(ref3.shape[0] * PACK_S, LANES)
    return jnp.concatenate(
        [flat[pl.ds(r0 * PACK_S + c, n, stride=PACK_S), :] for c in range(PACK_S)], axis=1)


def _store_packed(ref3, r0, n, u):
    flat = ref3.reshape(ref3.shape[0] * PACK_S, LANES)
    for c in range(PACK_S):
        flat[pl.ds(r0 * PACK_S + c, n, stride=PACK_S), :] = u[:, c * LANES:(c + 1) * LANES]


def _ada_kernel(c_ref, w_ref, b_ref, o_ref):
    c = c_ref[...]
    s = c * _sigmoid(c)
    o_ref[...] = _dot3(s, w_ref[...], _dot) + b_ref[...]


def _ada_call(c, w_ada, b_ada):
    bsz, d = c.shape
    n = w_ada.shape[1]
    tn = 1536
    return pl.pallas_call(
        _ada_kernel,
        grid=(n // tn,),
        in_specs=[
            pl.BlockSpec((bsz, d), lambda i: (0, 0)),
            pl.BlockSpec((d, tn), lambda i: (0, i)),
            pl.BlockSpec((1, tn), lambda i: (0, i)),
        ],
        out_specs=pl.BlockSpec((bsz, tn), lambda i: (0, i)),
        out_shape=jax.ShapeDtypeStruct((bsz, n), F32),
        compiler_params=pltpu.CompilerParams(
            dimension_semantics=("arbitrary",), vmem_limit_bytes=VMEM_LIMIT),
        name="ada",
    )(c, w_ada, b_ada.reshape(1, n))


def _bucket_table():
    qi = np.arange(ATTN_BLOCK)[:, None]
    ki = np.arange(2 * ATTN_BLOCK)[None, :]
    dist = qi + ATTN_BLOCK - ki
    n = np.maximum(dist, 0)
    max_exact = NUM_BUCKETS // 2
    large = max_exact + (np.log(np.maximum(n, 1) / max_exact) / np.log(MAX_DISTANCE / max_exact)
                         * (NUM_BUCKETS - max_exact)).astype(np.int32)
    large = np.minimum(large, NUM_BUCKETS - 1)
    bkt = np.where(n < max_exact, n, large).astype(np.int32)
    band = (dist >= 0) & (dist < WINDOW)
    return np.where(band, bkt, -1).astype(np.int32)


def _mixer_kernel(x_ref, mod_ref, win_ref, sinks_ref, relb_ref, bkt_ref, ang_ref, cw_ref, cb_ref,
                  lng_ref, lnb_ref, wout_ref, wrt_ref, wsg_ref, wsu_ref, wsd_ref,
                  xs1_ref, h2_ref, lgt_ref,
                  kv_ref, uext_ref, bias_ref):
    tm = x_ref.shape[1]
    nblk = tm // ATTN_BLOCK
    j = pl.program_id(1)

    @pl.when((pl.program_id(0) == 0) & (j == 0))
    def _build_bias():
        bkt = bkt_ref[...]
        for h in range(N_HEADS):
            acc = jnp.full(bkt.shape, -jnp.inf, F32)
            for b in range(NUM_BUCKETS):
                acc = jnp.where(bkt == b, relb_ref[b, h], acc)
            bias_ref[h] = acc

    @pl.when(j == 0)
    def _reset_history():
        kv_ref[...] = jnp.zeros(kv_ref.shape, F32)
        uext_ref[0:HALO, :] = jnp.zeros((HALO, CONV_CH), F32)
        uext_ref[HALO + tm:HALO + tm + SUBLANES, :] = jnp.zeros((SUBLANES, CONV_CH), F32)

    x = x_ref[0]
    mod = mod_ref[0]
    sh1, sc1, g1 = mod[0:1], mod[1:2], mod[2:3]
    sh2, sc2, g2 = mod[3:4], mod[4:5], mod[5:6]

    h = _rms(x) * (1.0 + sc1) + sh1
    proj = _dot(h.astype(BF16), win_ref[...])

    qb = (proj[:, :Q_COLS] * (HEAD_DIM ** -0.5)).astype(BF16)
    kv_cur = proj[:, Q_COLS:Q_COLS + 2 * KV_COLS]
    kvb = jnp.concatenate([kv_ref[...], kv_cur], axis=0).astype(BF16)
    kv_ref[...] = kv_cur[tm - ATTN_BLOCK:, :]
    not_first = j > 0
    col = lax.broadcasted_iota(jnp.int32, (GROUP * ATTN_BLOCK, 2 * ATTN_BLOCK), 1)
    attn_rows = []
    for i in range(nblk):
        r0, r1, r2 = i * ATTN_BLOCK, (i + 1) * ATTN_BLOCK, (i + 2) * ATTN_BLOCK
        heads = [None] * N_HEADS
        for kh in range(N_KV_HEADS):
            kpc = kvb[r0:r2, kh * HEAD_DIM:(kh + 1) * HEAD_DIM]
            vpc = kvb[r0:r2, KV_COLS + kh * HEAD_DIM:KV_COLS + (kh + 1) * HEAD_DIM]
            qg = jnp.concatenate(
                [qb[r0:r1, (kh * GROUP + g) * HEAD_DIM:(kh * GROUP + g + 1) * HEAD_DIM]
                 for g in range(GROUP)], axis=0)
            bias = bias_ref[kh * GROUP:(kh + 1) * GROUP].reshape(GROUP * ATTN_BLOCK, 2 * ATTN_BLOCK)
            logit = _dot_nt(qg, kpc) + bias
            if i == 0:
                logit = jnp.where((col >= ATTN_BLOCK) | not_first, logit, -jnp.inf)
            sink = jnp.concatenate(
                [jnp.full((ATTN_BLOCK, 1), sinks_ref[kh * GROUP + g], F32) for g in range(GROUP)], axis=0)
            m = jnp.maximum(jnp.max(logit, axis=-1, keepdims=True), sink)
            p = jnp.exp(logit - m)
            den = jnp.sum(p, axis=-1, keepdims=True) + jnp.exp(sink - m)
            o = _dot(p.astype(BF16), vpc) / den
            for g in range(GROUP):
                heads[kh * GROUP + g] = o[g * ATTN_BLOCK:(g + 1) * ATTN_BLOCK, :]
        attn_rows.append(jnp.concatenate(heads, axis=1))
    attn = jnp.concatenate(attn_rows, axis=0)
    attn = _rms(attn) * ang_ref[...]

    a = proj[:, Q_COLS + 2 * KV_COLS:Q_COLS + 2 * KV_COLS + CONV_CH]
    gt = proj[:, Q_COLS + 2 * KV_COLS + CONV_CH:]
    uext_ref[HALO:HALO + tm, :] = a * _sigmoid(gt)
    cw = cw_ref[...]
    base = HALO - (CONV_WIDTH - 1)
    acc = None
    for res in range(SUBLANES):
        part = None
        for hi in range((HALO + SUBLANES - 1) // SUBLANES + 1):
            t = hi * SUBLANES + res - base
            if 0 <= t < CONV_WIDTH:
                term = cw[t:t + 1, :] * uext_ref[hi * SUBLANES:hi * SUBLANES + tm + SUBLANES, :]
                part = term if part is None else part + term
        part = part[res:res + tm, :]
        acc = part if acc is None else acc + part
    uext_ref[0:HALO, :] = uext_ref[tm:tm + HALO, :]
    cv = acc + cb_ref[...]
    mu = jnp.mean(cv, axis=-1, keepdims=True)
    var = jnp.mean(jnp.square(cv - mu), axis=-1, keepdims=True)
    cv = (cv - mu) * lax.rsqrt(var + EPS) * lng_ref[...] + lnb_ref[...]
    cv = cv * _sigmoid(cv)

    mixed = (_dot(attn.astype(BF16), wout_ref[0:Q_COLS, :])
             + _dot(cv.astype(BF16), wout_ref[Q_COLS:, :]))
    x1 = x + g1 * mixed

    h2 = _rms(x1) * (1.0 + sc2) + sh2
    lgt_ref[...] = _dot3(wrt_ref[...], h2, _dot_nt)
    h2b = h2.astype(BF16)
    h2_ref[0] = h2b
    sg = _dot(h2b, wsg_ref[...])
    su = _dot(h2b, wsu_ref[...])
    shared = _dot(((sg * _sigmoid(sg)) * su).astype(BF16), wsd_ref[...])
    xs1_ref[0] = x1 + g2 * shared


def _mixer_call(x, mod3, w_in_b, sinks, rel_bias, ang, cw, cb, lng, lnb, w_out_b, w_rt, wsg_b, wsu_b, wsd_b):
    bsz, seq, d = x.shape
    tm = MIX_ROWS
    nj = seq // tm
    bkt = jnp.asarray(_bucket_table())
    full = lambda shape: pl.BlockSpec(shape, lambda b, j: (0,) * len(shape))
    smem = pl.BlockSpec(memory_space=pltpu.SMEM)
    return pl.pallas_call(
        _mixer_kernel,
        grid=(bsz, nj),
        in_specs=[
            pl.BlockSpec((1, tm, d), lambda b, j: (b, j, 0)),
            pl.BlockSpec((1, 6, d), lambda b, j: (b, 0, 0)),
            full((d, IN_COLS)),
            smem, smem,
            full((ATTN_BLOCK, 2 * ATTN_BLOCK)),
            full((1, Q_COLS)),
            full((CONV_WIDTH, CONV_CH)),
            full((1, CONV_CH)), full((1, CONV_CH)), full((1, CONV_CH)),
            full((d, d)),
            full((N_EXPERTS, d)),
            full((d, EXPERT_HIDDEN)), full((d, EXPERT_HIDDEN)), full((EXPERT_HIDDEN, d)),
        ],
        out_specs=[
            pl.BlockSpec((1, tm, d), lambda b, j: (b, j, 0)),
            pl.BlockSpec((1, tm, d), lambda b, j: (b, j, 0)),
            pl.BlockSpec((N_EXPERTS, tm), lambda b, j: (0, b * nj + j)),
        ],
        out_shape=[
            jax.ShapeDtypeStruct((bsz, seq, d), F32),
            jax.ShapeDtypeStruct((bsz, seq, d), BF16),
            jax.ShapeDtypeStruct((N_EXPERTS, bsz * seq), F32),
        ],
        scratch_shapes=[
            pltpu.VMEM((ATTN_BLOCK, 2 * KV_COLS), F32),
            pltpu.VMEM((HALO + tm + SUBLANES, CONV_CH), F32),
            pltpu.VMEM((N_HEADS, ATTN_BLOCK, 2 * ATTN_BLOCK), F32),
        ],
        compiler_params=pltpu.CompilerParams(
            dimension_semantics=("arbitrary", "arbitrary"), vmem_limit_bytes=VMEM_LIMIT),
        name="mixer",
    )(x, mod3, w_in_b, sinks, rel_bias, bkt, ang, cw, cb, lng, lnb, w_out_b, w_rt, wsg_b, wsu_b, wsd_b)


def _route_kernel(lgt_ref, rb_ref, lp_ref, gw_ref, cnt_ref):
    tr = lgt_ref.shape[1]
    i = pl.program_id(0)

    @pl.when(i == 0)
    def _():
        cnt_ref[...] = jnp.zeros(cnt_ref.shape, F32)

    scores = _sigmoid(lgt_ref[...])
    sel = scores + rb_ref[...]
    neg = -jnp.inf

    sel3 = sel.reshape(N_GROUPS, GROUP_SIZE, tr)
    loc = lax.broadcasted_iota(I32, sel3.shape, 1)
    m1 = jnp.max(sel3, axis=1, keepdims=True)
    i1 = jnp.min(jnp.where(sel3 == m1, loc, GROUP_SIZE), axis=1, keepdims=True)
    m2 = jnp.max(jnp.where(loc == i1, neg, sel3), axis=1, keepdims=True)
    gscore = (m1 + m2).reshape(N_GROUPS, tr)

    gio = lax.broadcasted_iota(I32, gscore.shape, 0)
    gmask = jnp.zeros(gscore.shape, jnp.bool_)
    cur = gscore
    for _ in range(TOPK_GROUPS):
        m = jnp.max(cur, axis=0, keepdims=True)
        idx = jnp.min(jnp.where(cur == m, gio, N_GROUPS), axis=0, keepdims=True)
        pick = gio == idx
        gmask = gmask | pick
        cur = jnp.where(pick, neg, cur)
    emask = jnp.broadcast_to(gmask.reshape(N_GROUPS, 1, tr), sel3.shape).reshape(N_EXPERTS, tr)

    rio = lax.broadcasted_iota(I32, sel.shape, 0)
    cur = jnp.where(emask, sel, neg)
    picks, gsc = [], []
    for _ in range(TOP_K):
        m = jnp.max(cur, axis=0, keepdims=True)
        idx = jnp.min(jnp.where(cur == m, rio, N_EXPERTS), axis=0, keepdims=True)
        pick = rio == idx
        picks.append(pick)
        gsc.append(jnp.sum(jnp.where(pick, scores, 0.0), axis=0, keepdims=True))
        cur = jnp.where(pick, neg, cur)
    gsum = gsc[0]
    for k in range(1, TOP_K):
        gsum = gsum + gsc[k]
    gw_ref[...] = jnp.concatenate([g / gsum * ROUTED_SCALE for g in gsc], axis=0)

    chosen = picks[0]
    for k in range(1, TOP_K):
        chosen = chosen | picks[k]
    onehot = jnp.where(chosen, 1.0, 0.0)
    tri = (lax.broadcasted_iota(I32, (tr, tr), 0) < lax.broadcasted_iota(I32, (tr, tr), 1))
    before = _dot(onehot.astype(BF16), jnp.where(tri, 1.0, 0.0).astype(BF16))
    n = jnp.sum(onehot, axis=1, keepdims=True)
    low = (lax.broadcasted_iota(I32, (N_EXPERTS, N_EXPERTS), 1)
           < lax.broadcasted_iota(I32, (N_EXPERTS, N_EXPERTS), 0))
    start = _dot(jnp.where(low, 1.0, 0.0).astype(BF16),
                 jnp.broadcast_to(n, (N_EXPERTS, LANES)).astype(BF16))[:, 0:1]
    pos = start + before
    lp_ref[...] = jnp.concatenate(
        [jnp.sum(jnp.where(p, pos, 0.0), axis=0, keepdims=True) for p in picks], axis=0).astype(I32)

    lane = lax.broadcasted_iota(I32, cnt_ref.shape, 1)
    cnt_ref[...] += jnp.where(lane == i, n, 0.0)


def _route_call(lgt, router_bias):
    e, t = lgt.shape
    ntiles = t // TILE
    ntp = (ntiles + LANES - 1) // LANES * LANES
    return pl.pallas_call(
        _route_kernel,
        grid=(ntiles,),
        in_specs=[
            pl.BlockSpec((e, TILE), lambda i: (0, i)),
            pl.BlockSpec((e, 1), lambda i: (0, 0)),
        ],
        out_specs=[
            pl.BlockSpec((TOP_K, TILE), lambda i: (0, i)),
            pl.BlockSpec((TOP_K, TILE), lambda i: (0, i)),
            pl.BlockSpec((e, ntp), lambda i: (0, 0)),
        ],
        out_shape=[
            jax.ShapeDtypeStruct((TOP_K, t), I32),
            jax.ShapeDtypeStruct((TOP_K, t), F32),
            jax.ShapeDtypeStruct((e, ntp), F32),
        ],
        compiler_params=pltpu.CompilerParams(
            dimension_semantics=("arbitrary",), vmem_limit_bytes=VMEM_LIMIT),
        name="route",
    )(lgt, router_bias.reshape(e, 1))


_T_LOCAL, _T_LEN, _T_GLOBAL, _T_PER_TILE = 0, 1, 2, 3
_T_EMPTY = 2 * _T_PER_TILE
_T_ROWS = 8
LBUF_ROWS = LROWS + N_EXPERTS


def _segment_copies(tab_ref, half, local_ref, global_hbm, sem, to_global):
    r = _T_PER_TILE * half

    def body(e, carry):
        n = tab_ref[0, r + _T_LEN, e]
        loc = local_ref.at[pl.ds(tab_ref[0, r + _T_LOCAL, e], n)]
        glo = global_hbm.at[pl.ds(tab_ref[0, r + _T_GLOBAL, e], n)]
        if to_global:
            pltpu.make_async_copy(loc, glo, sem).start()
        else:
            pltpu.make_async_copy(glo, loc, sem).start()
        return carry

    lax.fori_loop(0, N_EXPERTS, body, 0, unroll=8)


def _segment_wait(local_ref, global_hbm, sem, nempty):
    rows = LROWS + nempty
    pltpu.make_async_copy(global_hbm.at[pl.ds(0, rows)], local_ref.at[pl.ds(0, rows)], sem).wait()


TAIL_ROWS = EXPERT_ROWS + 2 * N_EXPERTS


def _dispatch_kernel(tab_ref, h2_ref, lp_ref, xs_hbm, xl0, xl1, zb, sem, pending):
    s = pl.program_id(0)
    ns = pl.num_programs(0)

    @pl.when(s == 0)
    def _zero_tail():
        zb[...] = jnp.zeros(zb.shape, U32)
        cp = pltpu.make_async_copy(zb, xs_hbm.at[pl.ds(xs_hbm.shape[0] - TAIL_ROWS, TAIL_ROWS)], sem.at[2])
        cp.start()
        cp.wait()

    @pl.when(s > 0)
    def _():
        _segment_wait(xl0, xs_hbm, sem.at[0], pending[0])
        _segment_wait(xl1, xs_hbm, sem.at[1], pending[1])

    for half, xl in ((0, xl0), (1, xl1)):
        rows = h2_ref[half * TILE:(half + 1) * TILE, :]
        lp = lp_ref[:, half * TILE:(half + 1) * TILE]
        for c in range(LROWS // SORT_CHUNK):
            j = lax.broadcasted_iota(I32, (SORT_CHUNK, TILE), 0) + c * SORT_CHUNK
            onehot = jnp.zeros((SORT_CHUNK, TILE), F32)
            for k in range(TOP_K):
                onehot = jnp.where(j == lp[k:k + 1, :], 1.0, onehot)
            _store_packed(xl, c * SORT_CHUNK, SORT_CHUNK, _pack_rows(_dot(onehot.astype(BF16), rows)))
        _segment_copies(tab_ref, half, xl, xs_hbm, sem.at[half], to_global=True)
        pending[half] = tab_ref[0, _T_EMPTY, half]

    @pl.when(s == ns - 1)
    def _():
        _segment_wait(xl0, xs_hbm, sem.at[0], pending[0])
        _segment_wait(xl1, xs_hbm, sem.at[1], pending[1])


def _dispatch_call(tab, h2, lp, nrows):
    t, d = h2.shape
    ns = t // (2 * TILE)
    return pl.pallas_call(
        _dispatch_kernel,
        grid=(ns,),
        in_specs=[
            pl.BlockSpec((1, _T_ROWS, N_EXPERTS), lambda s: (s, 0, 0), memory_space=pltpu.SMEM),
            pl.BlockSpec((2 * TILE, d), lambda s: (s, 0)),
            pl.BlockSpec((TOP_K, 2 * TILE), lambda s: (0, s)),
        ],
        out_specs=pl.BlockSpec(memory_space=pl.ANY),
        out_shape=jax.ShapeDtypeStruct((nrows, PACK_S, LANES), U32),
        scratch_shapes=[
            pltpu.VMEM((LBUF_ROWS, PACK_S, LANES), U32),
            pltpu.VMEM((LBUF_ROWS, PACK_S, LANES), U32),
            pltpu.VMEM((TAIL_ROWS, PACK_S, LANES), U32),
            pltpu.SemaphoreType.DMA((3,)),
            pltpu.SMEM((2,), I32),
        ],
        compiler_params=pltpu.CompilerParams(
            dimension_semantics=("arbitrary",), vmem_limit_bytes=VMEM_LIMIT),
        name="dispatch",
    )(tab, h2, lp)


def _expert_kernel(start_ref, count_ref, wg_ref, wu_ref, wd_ref, xs_hbm, y_hbm,
                   xbuf, ybuf, isem, osem, wgb, wub, wdb):
    e = pl.program_id(0)
    rows = EXPERT_ROWS
    start = start_ref[e]
    n = count_ref[e]
    nchunks = (n + rows - 1) // rows

    def fetch(c, slot):
        return pltpu.make_async_copy(xs_hbm.at[pl.ds(start + c * rows, rows)],
                                     xbuf.at[pl.ds(slot * rows, rows)], isem.at[slot])

    def flush(c, slot, m):
        return pltpu.make_async_copy(ybuf.at[pl.ds(slot * rows, m)],
                                     y_hbm.at[pl.ds(start + c * rows, m)], osem.at[slot])

    @pl.when(nchunks > 0)
    def _():
        fetch(0, 0).start()

    wgb[...] = wg_ref[0].astype(BF16)
    wub[...] = wu_ref[0].astype(BF16)
    wdb[...] = wd_ref[0].astype(BF16)

    def chunk(c, carry):
        slot = c % 2

        @pl.when(c + 1 < nchunks)
        def _():
            fetch(c + 1, 1 - slot).start()

        fetch(c, slot).wait()

        @pl.when(c >= 2)
        def _():
            flush(c - 2, slot, rows).wait()

        xb = _unpack_rows(_load_packed(xbuf, slot * rows, rows))
        g = _dot(xb, wgb[...])
        u = _dot(xb, wub[...])
        hid = (g * _sigmoid(g)) * u
        y = _dot(hid.astype(BF16), wdb[...])
        _store_packed(ybuf, slot * rows, rows, _pack_rows(y.astype(BF16).astype(F32)))
        flush(c, slot, jnp.minimum(rows, n - c * rows)).start()
        return carry

    lax.fori_loop(0, nchunks, chunk, 0)

    @pl.when(nchunks >= 2)
    def _():
        flush(nchunks - 2, nchunks % 2, rows).wait()

    @pl.when(nchunks >= 1)
    def _():
        last = nchunks - 1
        flush(last, last % 2, n - last * rows).wait()


def _expert_call(starts, counts, xs, w_gate, w_up, w_down, nrows):
    ne, d, hid = w_gate.shape
    rows = EXPERT_ROWS
    wsel = lambda e, st, ct: (e, 0, 0)
    grid_spec = pltpu.PrefetchScalarGridSpec(
        num_scalar_prefetch=2,
        grid=(ne,),
        in_specs=[
            pl.BlockSpec((1, d, hid), wsel),
            pl.BlockSpec((1, d, hid), wsel),
            pl.BlockSpec((1, hid, d), wsel),
            pl.BlockSpec(memory_space=pl.ANY),
        ],
        out_specs=pl.BlockSpec(memory_space=pl.ANY),
        scratch_shapes=[
            pltpu.VMEM((2 * rows, PACK_S, LANES), U32),
            pltpu.VMEM((2 * rows, PACK_S, LANES), U32),
            pltpu.SemaphoreType.DMA((2,)),
            pltpu.SemaphoreType.DMA((2,)),
            pltpu.VMEM((d, hid), BF16),
            pltpu.VMEM((d, hid), BF16),
            pltpu.VMEM((hid, d), BF16),
        ],
    )
    return pl.pallas_call(
        _expert_kernel,
        grid_spec=grid_spec,
        out_shape=jax.ShapeDtypeStruct((nrows, PACK_S, LANES), U32),
        compiler_params=pltpu.CompilerParams(
            dimension_semantics=("arbitrary",), vmem_limit_bytes=VMEM_LIMIT),
        name="experts",
    )(starts, counts, w_gate, w_up, w_down, xs)


def _combine_kernel(tab_ref, tabn_ref, lp_ref, gw_ref, xs1_ref, mod_ref, fg_ref, y_hbm, o_ref, yl0, yl1, sem):
    s = pl.program_id(0)
    ns = pl.num_programs(0)

    @pl.when(s == 0)
    def _():
        _segment_copies(tab_ref, 0, yl0, y_hbm, sem.at[0], to_global=False)

    g2 = mod_ref[0][5:6]
    for half, yl in ((0, yl0), (1, yl1)):
        _segment_wait(yl, y_hbm, sem.at[half], tab_ref[0, _T_EMPTY, half])
        if half == 0:
            _segment_copies(tab_ref, 1, yl1, y_hbm, sem.at[1], to_global=False)
        lp = lp_ref[:, half * TILE:(half + 1) * TILE]
        gw = gw_ref[:, half * TILE:(half + 1) * TILE]
        acc = jnp.zeros((TILE, D_MODEL), F32)
        for c in range(LROWS // SORT_CHUNK):
            j = lax.broadcasted_iota(I32, (SORT_CHUNK, TILE), 0) + c * SORT_CHUNK
            wt = jnp.zeros((SORT_CHUNK, TILE), F32)
            for k in range(TOP_K):
                wt = jnp.where(j == lp[k:k + 1, :], gw[k:k + 1, :], wt)
            yv = _unpack_rows(_load_packed(yl, c * SORT_CHUNK, SORT_CHUNK))
            acc = acc + _dot_tn(wt.astype(BF16), yv)
        if half == 0:
            _segment_copies(tabn_ref, 0, yl0, y_hbm, sem.at[0], to_global=False)
        x2 = xs1_ref[half * TILE:(half + 1) * TILE, :] + g2 * acc
        o_ref[half * TILE:(half + 1) * TILE, :] = _rms(x2) * fg_ref[...]

    @pl.when(s == ns - 1)
    def _():
        _segment_wait(yl0, y_hbm, sem.at[0], tab_ref[0, _T_EMPTY, 0])


def _combine_call(tab, lp, gw, xs1, mod3, final_g, y, seq):
    t, d = xs1.shape
    ns = t // (2 * TILE)
    per_seq = seq // (2 * TILE)
    return pl.pallas_call(
        _combine_kernel,
        grid=(ns,),
        in_specs=[
            pl.BlockSpec((1, _T_ROWS, N_EXPERTS), lambda s: (s, 0, 0), memory_space=pltpu.SMEM),
            pl.BlockSpec((1, _T_ROWS, N_EXPERTS), lambda s: (jnp.minimum(s + 1, ns - 1), 0, 0),
                         memory_space=pltpu.SMEM),
            pl.BlockSpec((TOP_K, 2 * TILE), lambda s: (0, s)),
            pl.BlockSpec((TOP_K, 2 * TILE), lambda s: (0, s)),
            pl.BlockSpec((2 * TILE, d), lambda s: (s, 0)),
            pl.BlockSpec((1, 6, d), lambda s: (s // per_seq, 0, 0)),
            pl.BlockSpec((1, d), lambda s: (0, 0)),
            pl.BlockSpec(memory_space=pl.ANY),
        ],
        out_specs=pl.BlockSpec((2 * TILE, d), lambda s: (s, 0)),
        out_shape=jax.ShapeDtypeStruct((t, d), F32),
        scratch_shapes=[
            pltpu.VMEM((LBUF_ROWS, PACK_S, LANES), U32),
            pltpu.VMEM((LBUF_ROWS, PACK_S, LANES), U32),
            pltpu.SemaphoreType.DMA((2,)),
        ],
        compiler_params=pltpu.CompilerParams(
            dimension_semantics=("arbitrary",), vmem_limit_bytes=VMEM_LIMIT),
        name="combine",
    )(tab, tab, lp, gw, xs1, mod3, final_g.reshape(1, d), y)


def kernel(x, c, w_ada, b_ada, w_in, attn_sinks, rel_bias, attn_norm_g, conv_w, conv_b, conv_ln_g,
           conv_ln_b, w_out, w_router, router_bias, w_exp_gate, w_exp_up, w_exp_down, w_sh_gate,
           w_sh_up, w_sh_down, final_norm_g):
    bsz, seq, d = x.shape
    t = bsz * seq
    assert w_ada.shape[0] == 1 and d == D_MODEL
    assert seq % MIX_ROWS == 0 and seq % (2 * TILE) == 0

    mod3 = _ada_call(c, w_ada[0], b_ada[0]).reshape(bsz, 6, d)
    xs1, h2, lgt = _mixer_call(
        x, mod3, w_in[0].astype(BF16), attn_sinks[0], rel_bias,
        attn_norm_g[0].reshape(1, Q_COLS), conv_w[0], conv_b[0].reshape(1, CONV_CH),
        conv_ln_g[0].reshape(1, CONV_CH), conv_ln_b[0].reshape(1, CONV_CH),
        w_out[0].astype(BF16), w_router[0].T,
        w_sh_gate[0].astype(BF16), w_sh_up[0].astype(BF16), w_sh_down[0].astype(BF16))
    xs1 = xs1.reshape(t, d)
    h2 = h2.reshape(t, d)

    lp, gw, cnt = _route_call(lgt, router_bias[0])

    ntiles = t // TILE
    nassign = t * TOP_K
    n = cnt[:, :ntiles].T.astype(I32)
    local = jnp.cumsum(n, axis=1) - n
    earlier = jnp.cumsum(n, axis=0) - n
    total = jnp.sum(n, axis=0)
    starts = jnp.cumsum(total) - total
    glob = starts[None, :] + earlier
    empty = n == 0
    length = jnp.maximum(n, 1)
    eid = jnp.arange(N_EXPERTS, dtype=I32)[None, :]
    parity = (jnp.arange(ntiles, dtype=I32) % 2)[:, None]
    spare_global = nassign + EXPERT_ROWS + parity * N_EXPERTS + eid
    nempty = jnp.sum(empty.astype(I32), axis=1).reshape(ntiles // 2, 2)
    misc = jnp.pad(nempty, ((0, 0), (0, N_EXPERTS - 2)))[:, None, :]

    def table(loc, glo):
        per_tile = jnp.stack([loc, length, glo], axis=1).reshape(ntiles // 2, 2 * _T_PER_TILE, N_EXPERTS)
        fill = jnp.zeros((ntiles // 2, _T_ROWS - 2 * _T_PER_TILE - 1, N_EXPERTS), I32)
        return jnp.concatenate([per_tile, misc, fill], axis=1)

    tab_out = table(jnp.where(empty, 0, local), jnp.where(empty, spare_global, glob))
    tab_back = table(jnp.where(empty, LROWS + eid, local), jnp.where(empty, 0, glob))

    xs = _dispatch_call(tab_out, h2, lp, nassign + TAIL_ROWS)
    y = _expert_call(starts, total, xs, w_exp_gate[0], w_exp_up[0], w_exp_down[0], nassign)
    out = _combine_call(tab_back, lp, gw, xs1, mod3, final_norm_g, y, seq)
    return out.reshape(bsz, seq, d)
```

```python
import numpy as np
import jax
import jax.numpy as jnp
from jax import lax
from jax.experimental import pallas as pl
from jax.experimental.pallas import tpu as pltpu

F32 = jnp.float32
BF16 = jnp.bfloat16
U32 = jnp.uint32
I32 = jnp.int32

D_MODEL = 1024
HEAD_DIM = 64
N_HEADS = 8
N_KV_HEADS = 2
GROUP = N_HEADS // N_KV_HEADS
Q_COLS = N_HEADS * HEAD_DIM
KV_COLS = N_KV_HEADS * HEAD_DIM
ATTN_BLOCK = 128
WINDOW = 128
NUM_BUCKETS = 32
MAX_DISTANCE = 128
CONV_CH = D_MODEL - Q_COLS
CONV_WIDTH = 31
IN_COLS = Q_COLS + 2 * KV_COLS + 2 * CONV_CH
N_EXPERTS = 256
TOP_K = 8
N_GROUPS = 8
GROUP_SIZE = N_EXPERTS // N_GROUPS
TOPK_GROUPS = 4
EXPERT_HIDDEN = 256
ROUTED_SCALE = 2.5
EPS = 1e-6

MIX_ROWS = 512
HALO = 32
TILE = 256
LROWS = TILE * TOP_K
SORT_CHUNK = 512
EXPERT_ROWS = 256
LANES = 128
SUBLANES = 8
PACK_W = D_MODEL // 2
PACK_S = PACK_W // LANES
VMEM_LIMIT = 56 * 1024 * 1024


def _sigmoid(v):
    return 1.0 / (1.0 + jnp.exp(-v))


def _rms(v):
    return v * lax.rsqrt(jnp.mean(v * v, axis=-1, keepdims=True) + EPS)


def _split_bf16(a):
    hi = a.astype(BF16)
    lo = (a - hi.astype(F32)).astype(BF16)
    return hi, lo


def _dot(a, b):
    return jnp.dot(a, b, preferred_element_type=F32)


def _dot_nt(a, b):
    return lax.dot_general(a, b, (((1,), (1,)), ((), ())), preferred_element_type=F32)


def _dot_tn(a, b):
    return lax.dot_general(a, b, (((0,), (0,)), ((), ())), preferred_element_type=F32)


def _dot3(a, b, dot):
    ah, al = _split_bf16(a)
    bh, bl = _split_bf16(b)
    return dot(ah, bh) + (dot(ah, bl) + dot(al, bh))


def _pack_rows(v):
    hi = lax.bitcast_convert_type(v[:, :PACK_W], U32) & jnp.uint32(0xFFFF0000)
    lo = lax.bitcast_convert_type(v[:, PACK_W:], U32) >> 16
    return hi | lo


def _unpack_rows(u):
    hi = lax.bitcast_convert_type(u & jnp.uint32(0xFFFF0000), F32)
    lo = lax.bitcast_convert_type(u << 16, F32)
    return jnp.concatenate([hi, lo], axis=1).astype(BF16)


def _load_flat(flat, r0, n):
    return jnp.concatenate(
        [flat[pl.ds(r0 * PACK_S + c, n, stride=PACK_S), :] for c in range(PACK_S)], axis=1)


def _store_flat(flat, r0, n, u):
    for c in range(PACK_S):
        flat[pl.ds(r0 * PACK_S + c, n, stride=PACK_S), :] = u[:, c * LANES:(c + 1) * LANES]


def _load_packed(ref3, r0, n):
    return _load_flat(ref3.reshape(ref3.shape[0] * PACK_S, LANES), r0, n)


def _store_packed(ref3, r0, n, u):
    _store_flat(ref3.reshape(ref3.shape[0] * PACK_S, LANES), r0, n, u)


def _ada_kernel(c_ref, w_ref, b_ref, o_ref):
    c = c_ref[...]
    s = c * _sigmoid(c)
    o_ref[...] = _dot3(s, w_ref[...], _dot) + b_ref[...]


def _ada_call(c, w_ada, b_ada):
    bsz, d = c.shape
    n = w_ada.shape[1]
    tn = 1536
    return pl.pallas_call(
        _ada_kernel,
        grid=(n // tn,),
        in_specs=[
            pl.BlockSpec((bsz, d), lambda i: (0, 0)),
            pl.BlockSpec((d, tn), lambda i: (0, i)),
            pl.BlockSpec((1, tn), lambda i: (0, i)),
        ],
        out_specs=pl.BlockSpec((bsz, tn), lambda i: (0, i)),
        out_shape=jax.ShapeDtypeStruct((bsz, n), F32),
        compiler_params=pltpu.CompilerParams(
            dimension_semantics=("arbitrary",), vmem_limit_bytes=VMEM_LIMIT),
        name="ada",
    )(c, w_ada, b_ada.reshape(1, n))


def _bucket_table():
    qi = np.arange(ATTN_BLOCK)[:, None]
    ki = np.arange(2 * ATTN_BLOCK)[None, :]
    dist = qi + ATTN_BLOCK - ki
    n = np.maximum(dist, 0)
    max_exact = NUM_BUCKETS // 2
    large = max_exact + (np.log(np.maximum(n, 1) / max_exact) / np.log(MAX_DISTANCE / max_exact)
                         * (NUM_BUCKETS - max_exact)).astype(np.int32)
    large = np.minimum(large, NUM_BUCKETS - 1)
    bkt = np.where(n < max_exact, n, large).astype(np.int32)
    band = (dist >= 0) & (dist < WINDOW)
    return np.where(band, bkt, -1).astype(np.int32)


def _mixer_kernel(x_ref, mod_ref, win_ref, sinks_ref, relb_ref, bkt_ref, ang_ref, cw_ref, cb_ref,
                  lng_ref, lnb_ref, wout_ref, wrt_ref, wsg_ref, wsu_ref, wsd_ref,
                  xs1_ref, h2_ref, lgt_ref,
                  kv_ref, uext_ref, bias_ref):
    tm = x_ref.shape[1]
    nblk = tm // ATTN_BLOCK
    j = pl.program_id(1)

    @pl.when((pl.program_id(0) == 0) & (j == 0))
    def _build_bias():
        bkt = bkt_ref[...]
        for h in range(N_HEADS):
            acc = jnp.full(bkt.shape, -jnp.inf, F32)
            for b in range(NUM_BUCKETS):
                acc = jnp.where(bkt == b, relb_ref[b, h], acc)
            bias_ref[h] = acc

    @pl.when(j == 0)
    def _reset_history():
        kv_ref[...] = jnp.zeros(kv_ref.shape, F32)
        uext_ref[0:HALO, :] = jnp.zeros((HALO, CONV_CH), F32)
        uext_ref[HALO + tm:HALO + tm + SUBLANES, :] = jnp.zeros((SUBLANES, CONV_CH), F32)

    x = x_ref[0]
    mod = mod_ref[0]
    sh1, sc1, g1 = mod[0:1], mod[1:2], mod[2:3]
    sh2, sc2, g2 = mod[3:4], mod[4:5], mod[5:6]

    h = _rms(x) * (1.0 + sc1) + sh1
    proj = _dot(h.astype(BF16), win_ref[...])

    qb = (proj[:, :Q_COLS] * (HEAD_DIM ** -0.5)).astype(BF16)
    kv_cur = proj[:, Q_COLS:Q_COLS + 2 * KV_COLS]
    kvb = jnp.concatenate([kv_ref[...], kv_cur], axis=0).astype(BF16)
    kv_ref[...] = kv_cur[tm - ATTN_BLOCK:, :]
    not_first = j > 0
    col = lax.broadcasted_iota(jnp.int32, (GROUP * ATTN_BLOCK, 2 * ATTN_BLOCK), 1)
    attn_rows = []
    for i in range(nblk):
        r0, r1, r2 = i * ATTN_BLOCK, (i + 1) * ATTN_BLOCK, (i + 2) * ATTN_BLOCK
        heads = [None] * N_HEADS
        for kh in range(N_KV_HEADS):
            kpc = kvb[r0:r2, kh * HEAD_DIM:(kh + 1) * HEAD_DIM]
            vpc = kvb[r0:r2, KV_COLS + kh * HEAD_DIM:KV_COLS + (kh + 1) * HEAD_DIM]
            qg = jnp.concatenate(
                [qb[r0:r1, (kh * GROUP + g) * HEAD_DIM:(kh * GROUP + g + 1) * HEAD_DIM]
                 for g in range(GROUP)], axis=0)
            bias = bias_ref[kh * GROUP:(kh + 1) * GROUP].reshape(GROUP * ATTN_BLOCK, 2 * ATTN_BLOCK)
            logit = _dot_nt(qg, kpc) + bias
            if i == 0:
                logit = jnp.where((col >= ATTN_BLOCK) | not_first, logit, -jnp.inf)
            sink = jnp.concatenate(
                [jnp.full((ATTN_BLOCK, 1), sinks_ref[kh * GROUP + g], F32) for g in range(GROUP)], axis=0)
            m = jnp.maximum(jnp.max(logit, axis=-1, keepdims=True), sink)
            p = jnp.exp(logit - m)
            den = jnp.sum(p, axis=-1, keepdims=True) + jnp.exp(sink - m)
            o = _dot(p.astype(BF16), vpc) / den
            for g in range(GROUP):
                heads[kh * GROUP + g] = o[g * ATTN_BLOCK:(g + 1) * ATTN_BLOCK, :]
        attn_rows.append(jnp.concatenate(heads, axis=1))
    attn = jnp.concatenate(attn_rows, axis=0)
    attn = _rms(attn) * ang_ref[...]

    a = proj[:, Q_COLS + 2 * KV_COLS:Q_COLS + 2 * KV_COLS + CONV_CH]
    gt = proj[:, Q_COLS + 2 * KV_COLS + CONV_CH:]
    uext_ref[HALO:HALO + tm, :] = a * _sigmoid(gt)
    cw = cw_ref[...]
    base = HALO - (CONV_WIDTH - 1)
    acc = None
    for res in range(SUBLANES):
        part = None
        for hi in range((HALO + SUBLANES - 1) // SUBLANES + 1):
            t = hi * SUBLANES + res - base
            if 0 <= t < CONV_WIDTH:
                term = cw[t:t + 1, :] * uext_ref[hi * SUBLANES:hi * SUBLANES + tm + SUBLANES, :]
                part = term if part is None else part + term
        part = part[res:res + tm, :]
        acc = part if acc is None else acc + part
    uext_ref[0:HALO, :] = uext_ref[tm:tm + HALO, :]
    cv = acc + cb_ref[...]
    mu = jnp.mean(cv, axis=-1, keepdims=True)
    var = jnp.mean(jnp.square(cv - mu), axis=-1, keepdims=True)
    cv = (cv - mu) * lax.rsqrt(var + EPS) * lng_ref[...] + lnb_ref[...]
    cv = cv * _sigmoid(cv)

    mixed = (_dot(attn.astype(BF16), wout_ref[0:Q_COLS, :])
             + _dot(cv.astype(BF16), wout_ref[Q_COLS:, :]))
    x1 = x + g1 * mixed

    h2 = _rms(x1) * (1.0 + sc2) + sh2
    lgt_ref[...] = _dot3(wrt_ref[...], h2, _dot_nt)
    h2b = h2.astype(BF16)
    h2_ref[0] = h2b
    sg = _dot(h2b, wsg_ref[...])
    su = _dot(h2b, wsu_ref[...])
    shared = _dot(((sg * _sigmoid(sg)) * su).astype(BF16), wsd_ref[...])
    xs1_ref[0] = x1 + g2 * shared


def _mixer_call(x, mod3, w_in_b, sinks, rel_bias, ang, cw, cb, lng, lnb, w_out_b, w_rt, wsg_b, wsu_b, wsd_b):
    bsz, seq, d = x.shape
    tm = MIX_ROWS
    nj = seq // tm
    bkt = jnp.asarray(_bucket_table())
    full = lambda shape: pl.BlockSpec(shape, lambda b, j: (0,) * len(shape))
    smem = pl.BlockSpec(memory_space=pltpu.SMEM)
    return pl.pallas_call(
        _mixer_kernel,
        grid=(bsz, nj),
        in_specs=[
            pl.BlockSpec((1, tm, d), lambda b, j: (b, j, 0)),
            pl.BlockSpec((1, 6, d), lambda b, j: (b, 0, 0)),
            full((d, IN_COLS)),
            smem, smem,
            full((ATTN_BLOCK, 2 * ATTN_BLOCK)),
            full((1, Q_COLS)),
            full((CONV_WIDTH, CONV_CH)),
            full((1, CONV_CH)), full((1, CONV_CH)), full((1, CONV_CH)),
            full((d, d)),
            full((N_EXPERTS, d)),
            full((d, EXPERT_HIDDEN)), full((d, EXPERT_HIDDEN)), full((EXPERT_HIDDEN, d)),
        ],
        out_specs=[
            pl.BlockSpec((1, tm, d), lambda b, j: (b, j, 0)),
            pl.BlockSpec((1, tm, d), lambda b, j: (b, j, 0)),
            pl.BlockSpec((N_EXPERTS, tm), lambda b, j: (0, b * nj + j)),
        ],
        out_shape=[
            jax.ShapeDtypeStruct((bsz, seq, d), F32),
            jax.ShapeDtypeStruct((bsz, seq, d), BF16),
            jax.ShapeDtypeStruct((N_EXPERTS, bsz * seq), F32),
        ],
        scratch_shapes=[
            pltpu.VMEM((ATTN_BLOCK, 2 * KV_COLS), F32),
            pltpu.VMEM((HALO + tm + SUBLANES, CONV_CH), F32),
            pltpu.VMEM((N_HEADS, ATTN_BLOCK, 2 * ATTN_BLOCK), F32),
        ],
        compiler_params=pltpu.CompilerParams(
            dimension_semantics=("arbitrary", "arbitrary"), vmem_limit_bytes=VMEM_LIMIT),
        name="mixer",
    )(x, mod3, w_in_b, sinks, rel_bias, bkt, ang, cw, cb, lng, lnb, w_out_b, w_rt, wsg_b, wsu_b, wsd_b)


def _route_kernel(lgt_ref, rb_ref, lp_ref, gw_ref, cnt_ref):
    tr = lgt_ref.shape[1]
    i = pl.program_id(0)

    @pl.when(i == 0)
    def _():
        cnt_ref[...] = jnp.zeros(cnt_ref.shape, F32)

    scores = _sigmoid(lgt_ref[...])
    sel = scores + rb_ref[...]
    neg = -jnp.inf

    sel3 = sel.reshape(N_GROUPS, GROUP_SIZE, tr)
    loc = lax.broadcasted_iota(I32, sel3.shape, 1)
    m1 = jnp.max(sel3, axis=1, keepdims=True)
    i1 = jnp.min(jnp.where(sel3 == m1, loc, GROUP_SIZE), axis=1, keepdims=True)
    m2 = jnp.max(jnp.where(loc == i1, neg, sel3), axis=1, keepdims=True)
    gscore = (m1 + m2).reshape(N_GROUPS, tr)

    gio = lax.broadcasted_iota(I32, gscore.shape, 0)
    gmask = jnp.zeros(gscore.shape, jnp.bool_)
    cur = gscore
    for _ in range(TOPK_GROUPS):
        m = jnp.max(cur, axis=0, keepdims=True)
        idx = jnp.min(jnp.where(cur == m, gio, N_GROUPS), axis=0, keepdims=True)
        pick = gio == idx
        gmask = gmask | pick
        cur = jnp.where(pick, neg, cur)
    emask = jnp.broadcast_to(gmask.reshape(N_GROUPS, 1, tr), sel3.shape).reshape(N_EXPERTS, tr)

    rio = lax.broadcasted_iota(I32, sel.shape, 0)
    cur = jnp.where(emask, sel, neg)
    picks, gsc = [], []
    for _ in range(TOP_K):
        m = jnp.max(cur, axis=0, keepdims=True)
        idx = jnp.min(jnp.where(cur == m, rio, N_EXPERTS), axis=0, keepdims=True)
        pick = rio == idx
        picks.append(pick)
        gsc.append(jnp.sum(jnp.where(pick, scores, 0.0), axis=0, keepdims=True))
        cur = jnp.where(pick, neg, cur)
    gsum = gsc[0]
    for k in range(1, TOP_K):
        gsum = gsum + gsc[k]
    gw_ref[...] = jnp.concatenate([g / gsum * ROUTED_SCALE for g in gsc], axis=0)

    chosen = picks[0]
    for k in range(1, TOP_K):
        chosen = chosen | picks[k]
    onehot = jnp.where(chosen, 1.0, 0.0)
    tri = (lax.broadcasted_iota(I32, (tr, tr), 0) < lax.broadcasted_iota(I32, (tr, tr), 1))
    before = _dot(onehot.astype(BF16), jnp.where(tri, 1.0, 0.0).astype(BF16))
    n = jnp.sum(onehot, axis=1, keepdims=True)
    low = (lax.broadcasted_iota(I32, (N_EXPERTS, N_EXPERTS), 1)
           < lax.broadcasted_iota(I32, (N_EXPERTS, N_EXPERTS), 0))
    start = _dot(jnp.where(low, 1.0, 0.0).astype(BF16),
                 jnp.broadcast_to(n, (N_EXPERTS, LANES)).astype(BF16))[:, 0:1]
    pos = start + before
    lp_ref[...] = jnp.concatenate(
        [jnp.sum(jnp.where(p, pos, 0.0), axis=0, keepdims=True) for p in picks], axis=0).astype(I32)

    lane = lax.broadcasted_iota(I32, cnt_ref.shape, 1)
    cnt_ref[...] += jnp.where(lane == i, n, 0.0)


def _route_call(lgt, router_bias):
    e, t = lgt.shape
    ntiles = t // TILE
    ntp = (ntiles + LANES - 1) // LANES * LANES
    return pl.pallas_call(
        _route_kernel,
        grid=(ntiles,),
        in_specs=[
            pl.BlockSpec((e, TILE), lambda i: (0, i)),
            pl.BlockSpec((e, 1), lambda i: (0, 0)),
        ],
        out_specs=[
            pl.BlockSpec((TOP_K, TILE), lambda i: (0, i)),
            pl.BlockSpec((TOP_K, TILE), lambda i: (0, i)),
            pl.BlockSpec((e, ntp), lambda i: (0, 0)),
        ],
        out_shape=[
            jax.ShapeDtypeStruct((TOP_K, t), I32),
            jax.ShapeDtypeStruct((TOP_K, t), F32),
            jax.ShapeDtypeStruct((e, ntp), F32),
        ],
        compiler_params=pltpu.CompilerParams(
            dimension_semantics=("arbitrary",), vmem_limit_bytes=VMEM_LIMIT),
        name="route",
    )(lgt, router_bias.reshape(e, 1))


_T_LOCAL, _T_LEN, _T_GLOBAL, _T_PER_TILE = 0, 1, 2, 3
_T_EMPTY = 2 * _T_PER_TILE
_T_ROWS = 8
LBUF_ROWS = LROWS + N_EXPERTS


def _segment_copies(tab_ref, half, local_ref, global_hbm, sem, to_global):
    r = _T_PER_TILE * half

    def body(e, carry):
        n = tab_ref[0, r + _T_LEN, e]
        loc = local_ref.at[pl.ds(tab_ref[0, r + _T_LOCAL, e], n)]
        glo = global_hbm.at[pl.ds(tab_ref[0, r + _T_GLOBAL, e], n)]
        if to_global:
            pltpu.make_async_copy(loc, glo, sem).start()
        else:
            pltpu.make_async_copy(glo, loc, sem).start()
        return carry

    lax.fori_loop(0, N_EXPERTS, body, 0, unroll=8)


def _segment_wait(local_ref, global_hbm, sem, nempty):
    rows = LROWS + nempty
    pltpu.make_async_copy(global_hbm.at[pl.ds(0, rows)], local_ref.at[pl.ds(0, rows)], sem).wait()


TAIL_ROWS = EXPERT_ROWS + 2 * N_EXPERTS


def _dispatch_kernel(tab_ref, pad_ref, h2_ref, lp_ref, xs_hbm, xl0, xl1, zb, sem, pending):
    s = pl.program_id(0)
    ns = pl.num_programs(0)

    @pl.when(s == 0)
    def _zero_unassigned_rows():
        zb[...] = jnp.zeros(zb.shape, U32)
        used = pad_ref[2, 0]
        rest = xs_hbm.shape[0] - used
        cp = pltpu.make_async_copy(zb.at[pl.ds(0, rest)], xs_hbm.at[pl.ds(used, rest)], sem.at[2])
        cp.start()
        cp.wait()

        def pad_row(e):
            return pltpu.make_async_copy(zb.at[pl.ds(0, 1)], xs_hbm.at[pl.ds(pad_ref[0, e], 1)], sem.at[2])

        def start(e, carry):
            @pl.when(pad_ref[1, e] > 0)
            def _():
                pad_row(e).start()
            return carry

        def wait(e, carry):
            @pl.when(pad_ref[1, e] > 0)
            def _():
                pad_row(e).wait()
            return carry

        lax.fori_loop(0, N_EXPERTS, start, 0)
        lax.fori_loop(0, N_EXPERTS, wait, 0)

    @pl.when(s > 0)
    def _():
        _segment_wait(xl0, xs_hbm, sem.at[0], pending[0])
        _segment_wait(xl1, xs_hbm, sem.at[1], pending[1])

    for half, xl in ((0, xl0), (1, xl1)):
        rows = h2_ref[half * TILE:(half + 1) * TILE, :]
        lp = lp_ref[:, half * TILE:(half + 1) * TILE]
        for c in range(LROWS // SORT_CHUNK):
            j = lax.broadcasted_iota(I32, (SORT_CHUNK, TILE), 0) + c * SORT_CHUNK
            onehot = jnp.zeros((SORT_CHUNK, TILE), F32)
            for k in range(TOP_K):
                onehot = jnp.where(j == lp[k:k + 1, :], 1.0, onehot)
            _store_packed(xl, c * SORT_CHUNK, SORT_CHUNK, _pack_rows(_dot(onehot.astype(BF16), rows)))
        _segment_copies(tab_ref, half, xl, xs_hbm, sem.at[half], to_global=True)
        pending[half] = tab_ref[0, _T_EMPTY, half]

    @pl.when(s == ns - 1)
    def _():
        _segment_wait(xl0, xs_hbm, sem.at[0], pending[0])
        _segment_wait(xl1, xs_hbm, sem.at[1], pending[1])


def _dispatch_call(tab, pad, h2, lp, nrows):
    t, d = h2.shape
    ns = t // (2 * TILE)
    return pl.pallas_call(
        _dispatch_kernel,
        grid=(ns,),
        in_specs=[
            pl.BlockSpec((1, _T_ROWS, N_EXPERTS), lambda s: (s, 0, 0), memory_space=pltpu.SMEM),
            pl.BlockSpec(memory_space=pltpu.SMEM),
            pl.BlockSpec((2 * TILE, d), lambda s: (s, 0)),
            pl.BlockSpec((TOP_K, 2 * TILE), lambda s: (0, s)),
        ],
        out_specs=pl.BlockSpec(memory_space=pl.ANY),
        out_shape=jax.ShapeDtypeStruct((nrows, PACK_S, LANES), U32),
        scratch_shapes=[
            pltpu.VMEM((LBUF_ROWS, PACK_S, LANES), U32),
            pltpu.VMEM((LBUF_ROWS, PACK_S, LANES), U32),
            pltpu.VMEM((TAIL_ROWS + N_EXPERTS, PACK_S, LANES), U32),
            pltpu.SemaphoreType.DMA((3,)),
            pltpu.SMEM((2,), I32),
        ],
        compiler_params=pltpu.CompilerParams(
            dimension_semantics=("arbitrary",), vmem_limit_bytes=VMEM_LIMIT),
        name="dispatch",
    )(tab, pad, h2, lp)


def _expert_kernel(start_ref, count_ref, wg_ref, wu_ref, wd_ref, xs_hbm, y_hbm,
                   xbuf, ybuf, isem, osem, wgb, wub, wdb):
    e = pl.program_id(0)
    rows = EXPERT_ROWS
    start = start_ref[e]
    n = count_ref[e]
    nchunks = (n + rows - 1) // rows

    def slab(ref, row0, m):
        size = m * PACK_S if isinstance(m, int) else pl.multiple_of(m * PACK_S, SUBLANES)
        first = row0 * PACK_S if isinstance(row0, int) else pl.multiple_of(row0 * PACK_S, SUBLANES)
        return ref.at[pl.ds(first, size)]

    def fetch(c, slot):
        return pltpu.make_async_copy(slab(xs_hbm, start + c * rows, rows),
                                     slab(xbuf, slot * rows, rows), isem.at[slot])

    def flush(c, slot, m):
        m = m if isinstance(m, int) else (m + 1) // 2 * 2
        return pltpu.make_async_copy(slab(ybuf, slot * rows, m), slab(y_hbm, start + c * rows, m), osem.at[slot])

    @pl.when(nchunks > 0)
    def _():
        fetch(0, 0).start()

    wgb[...] = wg_ref[0].astype(BF16)
    wub[...] = wu_ref[0].astype(BF16)
    wdb[...] = wd_ref[0].astype(BF16)

    def chunk(c, carry):
        slot = c % 2

        @pl.when(c + 1 < nchunks)
        def _():
            fetch(c + 1, 1 - slot).start()

        fetch(c, slot).wait()

        @pl.when(c >= 2)
        def _():
            flush(c - 2, slot, rows).wait()

        xb = _unpack_rows(_load_flat(xbuf, slot * rows, rows))
        g = _dot(xb, wgb[...])
        u = _dot(xb, wub[...])
        hid = (g * _sigmoid(g)) * u
        y = _dot(hid.astype(BF16), wdb[...])
        _store_flat(ybuf, slot * rows, rows, _pack_rows(y.astype(BF16).astype(F32)))
        flush(c, slot, jnp.minimum(rows, n - c * rows)).start()
        return carry

    lax.fori_loop(0, nchunks, chunk, 0)

    @pl.when(nchunks >= 2)
    def _():
        flush(nchunks - 2, nchunks % 2, rows).wait()

    @pl.when(nchunks >= 1)
    def _():
        last = nchunks - 1
        flush(last, last % 2, n - last * rows).wait()

    @pl.when(e == pl.num_programs(0) - 1)
    def _define_unused_rows():
        used = start_ref[pl.num_programs(0)]
        rest = y_hbm.shape[0] // PACK_S - used

        @pl.when(rest > 0)
        def _():
            ybuf[0:N_EXPERTS * PACK_S, :] = jnp.zeros((N_EXPERTS * PACK_S, LANES), U32)
            cp = pltpu.make_async_copy(slab(ybuf, 0, rest), slab(y_hbm, used, rest), osem.at[0])
            cp.start()
            cp.wait()


def _expert_call(starts, counts, xs, w_gate, w_up, w_down, nrows):
    ne, d, hid = w_gate.shape
    rows = EXPERT_ROWS
    wsel = lambda e, st, ct: (e, 0, 0)
    grid_spec = pltpu.PrefetchScalarGridSpec(
        num_scalar_prefetch=2,
        grid=(ne,),
        in_specs=[
            pl.BlockSpec((1, d, hid), wsel),
            pl.BlockSpec((1, d, hid), wsel),
            pl.BlockSpec((1, hid, d), wsel),
            pl.BlockSpec(memory_space=pl.ANY),
        ],
        out_specs=pl.BlockSpec(memory_space=pl.ANY),
        scratch_shapes=[
            pltpu.VMEM((2 * rows * PACK_S, LANES), U32),
            pltpu.VMEM((2 * rows * PACK_S, LANES), U32),
            pltpu.SemaphoreType.DMA((2,)),
            pltpu.SemaphoreType.DMA((2,)),
            pltpu.VMEM((d, hid), BF16),
            pltpu.VMEM((d, hid), BF16),
            pltpu.VMEM((hid, d), BF16),
        ],
    )
    return pl.pallas_call(
        _expert_kernel,
        grid_spec=grid_spec,
        out_shape=jax.ShapeDtypeStruct((nrows * PACK_S, LANES), U32),
        compiler_params=pltpu.CompilerParams(
            dimension_semantics=("arbitrary",), vmem_limit_bytes=VMEM_LIMIT),
        name="experts",
    )(starts, counts, w_gate, w_up, w_down, xs.reshape(-1, LANES)).reshape(nrows, PACK_S, LANES)


def _combine_kernel(tab_ref, tabn_ref, lp_ref, gw_ref, xs1_ref, mod_ref, fg_ref, y_hbm, o_ref, yl0, yl1, sem):
    s = pl.program_id(0)
    ns = pl.num_programs(0)

    @pl.when(s == 0)
    def _():
        _segment_copies(tab_ref, 0, yl0, y_hbm, sem.at[0], to_global=False)

    g2 = mod_ref[0][5:6]
    for half, yl in ((0, yl0), (1, yl1)):
        _segment_wait(yl, y_hbm, sem.at[half], tab_ref[0, _T_EMPTY, half])
        if half == 0:
            _segment_copies(tab_ref, 1, yl1, y_hbm, sem.at[1], to_global=False)
        lp = lp_ref[:, half * TILE:(half + 1) * TILE]
        gw = gw_ref[:, half * TILE:(half + 1) * TILE]
        acc = jnp.zeros((TILE, D_MODEL), F32)
        for c in range(LROWS // SORT_CHUNK):
            j = lax.broadcasted_iota(I32, (SORT_CHUNK, TILE), 0) + c * SORT_CHUNK
            wt = jnp.zeros((SORT_CHUNK, TILE), F32)
            for k in range(TOP_K):
                wt = jnp.where(j == lp[k:k + 1, :], gw[k:k + 1, :], wt)
            yv = _unpack_rows(_load_packed(yl, c * SORT_CHUNK, SORT_CHUNK))
            acc = acc + _dot_tn(wt.astype(BF16), yv)
        if half == 0:
            _segment_copies(tabn_ref, 0, yl0, y_hbm, sem.at[0], to_global=False)
        x2 = xs1_ref[half * TILE:(half + 1) * TILE, :] + g2 * acc
        o_ref[half * TILE:(half + 1) * TILE, :] = _rms(x2) * fg_ref[...]

    @pl.when(s == ns - 1)
    def _():
        _segment_wait(yl0, y_hbm, sem.at[0], tab_ref[0, _T_EMPTY, 0])


def _combine_call(tab, lp, gw, xs1, mod3, final_g, y, seq):
    t, d = xs1.shape
    ns = t // (2 * TILE)
    per_seq = seq // (2 * TILE)
    return pl.pallas_call(
        _combine_kernel,
        grid=(ns,),
        in_specs=[
            pl.BlockSpec((1, _T_ROWS, N_EXPERTS), lambda s: (s, 0, 0), memory_space=pltpu.SMEM),
            pl.BlockSpec((1, _T_ROWS, N_EXPERTS), lambda s: (jnp.minimum(s + 1, ns - 1), 0, 0),
                         memory_space=pltpu.SMEM),
            pl.BlockSpec((TOP_K, 2 * TILE), lambda s: (0, s)),
            pl.BlockSpec((TOP_K, 2 * TILE), lambda s: (0, s)),
            pl.BlockSpec((2 * TILE, d), lambda s: (s, 0)),
            pl.BlockSpec((1, 6, d), lambda s: (s // per_seq, 0, 0)),
            pl.BlockSpec((1, d), lambda s: (0, 0)),
            pl.BlockSpec(memory_space=pl.ANY),
        ],
        out_specs=pl.BlockSpec((2 * TILE, d), lambda s: (s, 0)),
        out_shape=jax.ShapeDtypeStruct((t, d), F32),
        scratch_shapes=[
            pltpu.VMEM((LBUF_ROWS, PACK_S, LANES), U32),
            pltpu.VMEM((LBUF_ROWS, PACK_S, LANES), U32),
            pltpu.SemaphoreType.DMA((2,)),
        ],
        compiler_params=pltpu.CompilerParams(
            dimension_semantics=("arbitrary",), vmem_limit_bytes=VMEM_LIMIT),
        name="combine",
    )(tab, tab, lp, gw, xs1, mod3, final_g.reshape(1, d), y)


def kernel(x, c, w_ada, b_ada, w_in, attn_sinks, rel_bias, attn_norm_g, conv_w, conv_b, conv_ln_g,
           conv_ln_b, w_out, w_router, router_bias, w_exp_gate, w_exp_up, w_exp_down, w_sh_gate,
           w_sh_up, w_sh_down, final_norm_g):
    bsz, seq, d = x.shape
    t = bsz * seq
    assert w_ada.shape[0] == 1 and d == D_MODEL
    assert seq % MIX_ROWS == 0 and seq % (2 * TILE) == 0

    mod3 = _ada_call(c, w_ada[0], b_ada[0]).reshape(bsz, 6, d)
    xs1, h2, lgt = _mixer_call(
        x, mod3, w_in[0].astype(BF16), attn_sinks[0], rel_bias,
        attn_norm_g[0].reshape(1, Q_COLS), conv_w[0], conv_b[0].reshape(1, CONV_CH),
        conv_ln_g[0].reshape(1, CONV_CH), conv_ln_b[0].reshape(1, CONV_CH),
        w_out[0].astype(BF16), w_router[0].T,
        w_sh_gate[0].astype(BF16), w_sh_up[0].astype(BF16), w_sh_down[0].astype(BF16))
    xs1 = xs1.reshape(t, d)
    h2 = h2.reshape(t, d)

    lp, gw, cnt = _route_call(lgt, router_bias[0])

    ntiles = t // TILE
    nassign = t * TOP_K
    n = cnt[:, :ntiles].T.astype(I32)
    local = jnp.cumsum(n, axis=1) - n
    earlier = jnp.cumsum(n, axis=0) - n
    total = jnp.sum(n, axis=0)
    region = (total + 1) // 2 * 2
    starts = jnp.cumsum(region) - region
    nalloc = nassign + N_EXPERTS
    used = jnp.sum(region)
    pad = jnp.stack([starts + total, region - total,
                     jnp.broadcast_to(used, (N_EXPERTS,))], axis=0)
    glob = starts[None, :] + earlier
    starts = jnp.concatenate([starts, used[None]])
    empty = n == 0
    length = jnp.maximum(n, 1)
    eid = jnp.arange(N_EXPERTS, dtype=I32)[None, :]
    parity = (jnp.arange(ntiles, dtype=I32) % 2)[:, None]
    spare_global = nalloc + EXPERT_ROWS + parity * N_EXPERTS + eid
    nempty = jnp.sum(empty.astype(I32), axis=1).reshape(ntiles // 2, 2)
    misc = jnp.pad(nempty, ((0, 0), (0, N_EXPERTS - 2)))[:, None, :]

    def table(loc, glo):
        per_tile = jnp.stack([loc, length, glo], axis=1).reshape(ntiles // 2, 2 * _T_PER_TILE, N_EXPERTS)
        fill = jnp.zeros((ntiles // 2, _T_ROWS - 2 * _T_PER_TILE - 1, N_EXPERTS), I32)
        return jnp.concatenate([per_tile, misc, fill], axis=1)

    tab_out = table(jnp.where(empty, 0, local), jnp.where(empty, spare_global, glob))
    tab_back = table(jnp.where(empty, LROWS + eid, local), jnp.where(empty, 0, glob))

    xs = _dispatch_call(tab_out, pad, h2, lp, nalloc + TAIL_ROWS)
    y = _expert_call(starts, total, xs, w_exp_gate[0], w_exp_up[0], w_exp_down[0], nalloc)
    out = _combine_call(tab_back, lp, gw, xs1, mod3, final_norm_g, y, seq)
    return out.reshape(bsz, seq, d)
```

```python
import numpy as np
import jax
import jax.numpy as jnp
from jax import lax
from jax.experimental import pallas as pl
from jax.experimental.pallas import tpu as pltpu

F32 = jnp.float32
BF16 = jnp.bfloat16
U32 = jnp.uint32
I32 = jnp.int32

D_MODEL = 1024
HEAD_DIM = 64
N_HEADS = 8
N_KV_HEADS = 2
GROUP = N_HEADS // N_KV_HEADS
Q_COLS = N_HEADS * HEAD_DIM
KV_COLS = N_KV_HEADS * HEAD_DIM
ATTN_BLOCK = 128
WINDOW = 128
NUM_BUCKETS = 32
MAX_DISTANCE = 128
CONV_CH = D_MODEL - Q_COLS
CONV_WIDTH = 31
IN_COLS = Q_COLS + 2 * KV_COLS + 2 * CONV_CH
N_EXPERTS = 256
TOP_K = 8
N_GROUPS = 8
GROUP_SIZE = N_EXPERTS // N_GROUPS
TOPK_GROUPS = 4
EXPERT_HIDDEN = 256
ROUTED_SCALE = 2.5
EPS = 1e-6

MIX_ROWS = 512
HALO = 32
TILE = 256
LROWS = TILE * TOP_K
SORT_CHUNK = 512
EXPERT_ROWS = 256
LANES = 128
SUBLANES = 8
PACK_W = D_MODEL // 2
PACK_S = PACK_W // LANES
VMEM_LIMIT = 56 * 1024 * 1024


def _sigmoid(v):
    return 1.0 / (1.0 + jnp.exp(-v))


def _rms(v):
    return v * lax.rsqrt(jnp.mean(v * v, axis=-1, keepdims=True) + EPS)


def _split_bf16(a):
    hi = a.astype(BF16)
    lo = (a - hi.astype(F32)).astype(BF16)
    return hi, lo


def _dot(a, b):
    return jnp.dot(a, b, preferred_element_type=F32)


def _dot_nt(a, b):
    return lax.dot_general(a, b, (((1,), (1,)), ((), ())), preferred_element_type=F32)


def _dot_tn(a, b):
    return lax.dot_general(a, b, (((0,), (0,)), ((), ())), preferred_element_type=F32)


def _dot3(a, b, dot):
    ah, al = _split_bf16(a)
    bh, bl = _split_bf16(b)
    return dot(ah, bh) + (dot(ah, bl) + dot(al, bh))


def _pack_rows(v):
    hi = lax.bitcast_convert_type(v[:, :PACK_W], U32) & jnp.uint32(0xFFFF0000)
    lo = lax.bitcast_convert_type(v[:, PACK_W:], U32) >> 16
    return hi | lo


def _unpack_rows(u):
    hi = lax.bitcast_convert_type(u & jnp.uint32(0xFFFF0000), F32)
    lo = lax.bitcast_convert_type(u << 16, F32)
    return jnp.concatenate([hi, lo], axis=1).astype(BF16)


def _load_flat(flat, r0, n):
    return jnp.concatenate(
        [flat[pl.ds(r0 * PACK_S + c, n, stride=PACK_S), :] for c in range(PACK_S)], axis=1)


def _store_flat(flat, r0, n, u):
    for c in range(PACK_S):
        flat[pl.ds(r0 * PACK_S + c, n, stride=PACK_S), :] = u[:, c * LANES:(c + 1) * LANES]


def _load_packed(ref3, r0, n):
    return _load_flat(ref3.reshape(ref3.shape[0] * PACK_S, LANES), r0, n)


def _store_packed(ref3, r0, n, u):
    _store_flat(ref3.reshape(ref3.shape[0] * PACK_S, LANES), r0, n, u)


def _ada_kernel(c_ref, w_ref, b_ref, o_ref):
    c = c_ref[...]
    s = c * _sigmoid(c)
    o_ref[...] = _dot3(s, w_ref[...], _dot) + b_ref[...]


def _ada_call(c, w_ada, b_ada):
    bsz, d = c.shape
    n = w_ada.shape[1]
    tn = 1536
    return pl.pallas_call(
        _ada_kernel,
        grid=(n // tn,),
        in_specs=[
            pl.BlockSpec((bsz, d), lambda i: (0, 0)),
            pl.BlockSpec((d, tn), lambda i: (0, i)),
            pl.BlockSpec((1, tn), lambda i: (0, i)),
        ],
        out_specs=pl.BlockSpec((bsz, tn), lambda i: (0, i)),
        out_shape=jax.ShapeDtypeStruct((bsz, n), F32),
        compiler_params=pltpu.CompilerParams(
            dimension_semantics=("arbitrary",), vmem_limit_bytes=VMEM_LIMIT),
        name="ada",
    )(c, w_ada, b_ada.reshape(1, n))


def _bucket_table():
    qi = np.arange(ATTN_BLOCK)[:, None]
    ki = np.arange(2 * ATTN_BLOCK)[None, :]
    dist = qi + ATTN_BLOCK - ki
    n = np.maximum(dist, 0)
    max_exact = NUM_BUCKETS // 2
    large = max_exact + (np.log(np.maximum(n, 1) / max_exact) / np.log(MAX_DISTANCE / max_exact)
                         * (NUM_BUCKETS - max_exact)).astype(np.int32)
    large = np.minimum(large, NUM_BUCKETS - 1)
    bkt = np.where(n < max_exact, n, large).astype(np.int32)
    band = (dist >= 0) & (dist < WINDOW)
    return np.where(band, bkt, -1).astype(np.int32)


def _mixer_kernel(x_ref, mod_ref, win_ref, sinks_ref, relb_ref, bkt_ref, ang_ref, cw_ref, cb_ref,
                  lng_ref, lnb_ref, wout_ref, wrt_ref, wsg_ref, wsu_ref, wsd_ref,
                  xs1_ref, h2_ref, lgt_ref,
                  kv_ref, uext_ref, bias_ref):
    tm = x_ref.shape[1]
    nblk = tm // ATTN_BLOCK
    j = pl.program_id(1)

    @pl.when((pl.program_id(0) == 0) & (j == 0))
    def _build_bias():
        bkt = bkt_ref[...]
        for h in range(N_HEADS):
            acc = jnp.full(bkt.shape, -jnp.inf, F32)
            for b in range(NUM_BUCKETS):
                acc = jnp.where(bkt == b, relb_ref[b, h], acc)
            bias_ref[h] = acc

    @pl.when(j == 0)
    def _reset_history():
        kv_ref[...] = jnp.zeros(kv_ref.shape, F32)
        uext_ref[0:HALO, :] = jnp.zeros((HALO, CONV_CH), F32)
        uext_ref[HALO + tm:HALO + tm + SUBLANES, :] = jnp.zeros((SUBLANES, CONV_CH), F32)

    x = x_ref[0]
    mod = mod_ref[0]
    sh1, sc1, g1 = mod[0:1], mod[1:2], mod[2:3]
    sh2, sc2, g2 = mod[3:4], mod[4:5], mod[5:6]

    h = _rms(x) * (1.0 + sc1) + sh1
    proj = _dot(h.astype(BF16), win_ref[...])

    qb = (proj[:, :Q_COLS] * (HEAD_DIM ** -0.5)).astype(BF16)
    kv_cur = proj[:, Q_COLS:Q_COLS + 2 * KV_COLS]
    kvb = jnp.concatenate([kv_ref[...], kv_cur], axis=0).astype(BF16)
    kv_ref[...] = kv_cur[tm - ATTN_BLOCK:, :]
    not_first = j > 0
    col = lax.broadcasted_iota(jnp.int32, (GROUP * ATTN_BLOCK, 2 * ATTN_BLOCK), 1)
    attn_rows = []
    for i in range(nblk):
        r0, r1, r2 = i * ATTN_BLOCK, (i + 1) * ATTN_BLOCK, (i + 2) * ATTN_BLOCK
        heads = [None] * N_HEADS
        for kh in range(N_KV_HEADS):
            kpc = kvb[r0:r2, kh * HEAD_DIM:(kh + 1) * HEAD_DIM]
            vpc = kvb[r0:r2, KV_COLS + kh * HEAD_DIM:KV_COLS + (kh + 1) * HEAD_DIM]
            qg = jnp.concatenate(
                [qb[r0:r1, (kh * GROUP + g) * HEAD_DIM:(kh * GROUP + g + 1) * HEAD_DIM]
                 for g in range(GROUP)], axis=0)
            bias = bias_ref[kh * GROUP:(kh + 1) * GROUP].reshape(GROUP * ATTN_BLOCK, 2 * ATTN_BLOCK)
            logit = _dot_nt(qg, kpc) + bias
            if i == 0:
                logit = jnp.where((col >= ATTN_BLOCK) | not_first, logit, -jnp.inf)
            sink = jnp.concatenate(
                [jnp.full((ATTN_BLOCK, 1), sinks_ref[kh * GROUP + g], F32) for g in range(GROUP)], axis=0)
            m = jnp.maximum(jnp.max(logit, axis=-1, keepdims=True), sink)
            p = jnp.exp(logit - m)
            den = jnp.sum(p, axis=-1, keepdims=True) + jnp.exp(sink - m)
            o = _dot(p.astype(BF16), vpc) / den
            for g in range(GROUP):
                heads[kh * GROUP + g] = o[g * ATTN_BLOCK:(g + 1) * ATTN_BLOCK, :]
        attn_rows.append(jnp.concatenate(heads, axis=1))
    attn = jnp.concatenate(attn_rows, axis=0)
    attn = _rms(attn) * ang_ref[...]

    a = proj[:, Q_COLS + 2 * KV_COLS:Q_COLS + 2 * KV_COLS + CONV_CH]
    gt = proj[:, Q_COLS + 2 * KV_COLS + CONV_CH:]
    uext_ref[HALO:HALO + tm, :] = a * _sigmoid(gt)
    cw = cw_ref[...]
    base = HALO - (CONV_WIDTH - 1)
    acc = None
    for res in range(SUBLANES):
        part = None
        for hi in range((HALO + SUBLANES - 1) // SUBLANES + 1):
            t = hi * SUBLANES + res - base
            if 0 <= t < CONV_WIDTH:
                term = cw[t:t + 1, :] * uext_ref[hi * SUBLANES:hi * SUBLANES + tm + SUBLANES, :]
                part = term if part is None else part + term
        part = part[res:res + tm, :]
        acc = part if acc is None else acc + part
    uext_ref[0:HALO, :] = uext_ref[tm:tm + HALO, :]
    cv = acc + cb_ref[...]
    mu = jnp.mean(cv, axis=-1, keepdims=True)
    var = jnp.mean(jnp.square(cv - mu), axis=-1, keepdims=True)
    cv = (cv - mu) * lax.rsqrt(var + EPS) * lng_ref[...] + lnb_ref[...]
    cv = cv * _sigmoid(cv)

    mixed = (_dot(attn.astype(BF16), wout_ref[0:Q_COLS, :])
             + _dot(cv.astype(BF16), wout_ref[Q_COLS:, :]))
    x1 = x + g1 * mixed

    h2 = _rms(x1) * (1.0 + sc2) + sh2
    lgt_ref[...] = _dot3(wrt_ref[...], h2, _dot_nt)
    h2b = h2.astype(BF16)
    h2_ref[0] = h2b
    sg = _dot(h2b, wsg_ref[...])
    su = _dot(h2b, wsu_ref[...])
    shared = _dot(((sg * _sigmoid(sg)) * su).astype(BF16), wsd_ref[...])
    xs1_ref[0] = x1 + g2 * shared


def _mixer_call(x, mod3, w_in_b, sinks, rel_bias, ang, cw, cb, lng, lnb, w_out_b, w_rt, wsg_b, wsu_b, wsd_b):
    bsz, seq, d = x.shape
    tm = MIX_ROWS
    nj = seq // tm
    bkt = jnp.asarray(_bucket_table())
    full = lambda shape: pl.BlockSpec(shape, lambda b, j: (0,) * len(shape))
    smem = pl.BlockSpec(memory_space=pltpu.SMEM)
    return pl.pallas_call(
        _mixer_kernel,
        grid=(bsz, nj),
        in_specs=[
            pl.BlockSpec((1, tm, d), lambda b, j: (b, j, 0)),
            pl.BlockSpec((1, 6, d), lambda b, j: (b, 0, 0)),
            full((d, IN_COLS)),
            smem, smem,
            full((ATTN_BLOCK, 2 * ATTN_BLOCK)),
            full((1, Q_COLS)),
            full((CONV_WIDTH, CONV_CH)),
            full((1, CONV_CH)), full((1, CONV_CH)), full((1, CONV_CH)),
            full((d, d)),
            full((N_EXPERTS, d)),
            full((d, EXPERT_HIDDEN)), full((d, EXPERT_HIDDEN)), full((EXPERT_HIDDEN, d)),
        ],
        out_specs=[
            pl.BlockSpec((1, tm, d), lambda b, j: (b, j, 0)),
            pl.BlockSpec((1, tm, d), lambda b, j: (b, j, 0)),
            pl.BlockSpec((N_EXPERTS, tm), lambda b, j: (0, b * nj + j)),
        ],
        out_shape=[
            jax.ShapeDtypeStruct((bsz, seq, d), F32),
            jax.ShapeDtypeStruct((bsz, seq, d), BF16),
            jax.ShapeDtypeStruct((N_EXPERTS, bsz * seq), F32),
        ],
        scratch_shapes=[
            pltpu.VMEM((ATTN_BLOCK, 2 * KV_COLS), F32),
            pltpu.VMEM((HALO + tm + SUBLANES, CONV_CH), F32),
            pltpu.VMEM((N_HEADS, ATTN_BLOCK, 2 * ATTN_BLOCK), F32),
        ],
        compiler_params=pltpu.CompilerParams(
            dimension_semantics=("arbitrary", "arbitrary"), vmem_limit_bytes=VMEM_LIMIT),
        name="mixer",
    )(x, mod3, w_in_b, sinks, rel_bias, bkt, ang, cw, cb, lng, lnb, w_out_b, w_rt, wsg_b, wsu_b, wsd_b)


def _route_kernel(lgt_ref, rb_ref, lp_ref, gw_ref, cnt_ref):
    tr = lgt_ref.shape[1]
    i = pl.program_id(0)

    @pl.when(i == 0)
    def _():
        cnt_ref[...] = jnp.zeros(cnt_ref.shape, F32)

    scores = _sigmoid(lgt_ref[...])
    sel = scores + rb_ref[...]
    neg = -jnp.inf

    sel3 = sel.reshape(N_GROUPS, GROUP_SIZE, tr)
    loc = lax.broadcasted_iota(I32, sel3.shape, 1)
    m1 = jnp.max(sel3, axis=1, keepdims=True)
    i1 = jnp.min(jnp.where(sel3 == m1, loc, GROUP_SIZE), axis=1, keepdims=True)
    m2 = jnp.max(jnp.where(loc == i1, neg, sel3), axis=1, keepdims=True)
    gscore = (m1 + m2).reshape(N_GROUPS, tr)

    gio = lax.broadcasted_iota(I32, gscore.shape, 0)
    gmask = jnp.zeros(gscore.shape, jnp.bool_)
    cur = gscore
    for _ in range(TOPK_GROUPS):
        m = jnp.max(cur, axis=0, keepdims=True)
        idx = jnp.min(jnp.where(cur == m, gio, N_GROUPS), axis=0, keepdims=True)
        pick = gio == idx
        gmask = gmask | pick
        cur = jnp.where(pick, neg, cur)
    emask = jnp.broadcast_to(gmask.reshape(N_GROUPS, 1, tr), sel3.shape).reshape(N_EXPERTS, tr)

    rio = lax.broadcasted_iota(I32, sel.shape, 0)
    cur = jnp.where(emask, sel, neg)
    picks, gsc = [], []
    for _ in range(TOP_K):
        m = jnp.max(cur, axis=0, keepdims=True)
        idx = jnp.min(jnp.where(cur == m, rio, N_EXPERTS), axis=0, keepdims=True)
        pick = rio == idx
        picks.append(pick)
        gsc.append(jnp.sum(jnp.where(pick, scores, 0.0), axis=0, keepdims=True))
        cur = jnp.where(pick, neg, cur)
    gsum = gsc[0]
    for k in range(1, TOP_K):
        gsum = gsum + gsc[k]
    gw_ref[...] = jnp.concatenate([g / gsum * ROUTED_SCALE for g in gsc], axis=0)

    chosen = picks[0]
    for k in range(1, TOP_K):
        chosen = chosen | picks[k]
    onehot = jnp.where(chosen, 1.0, 0.0)
    tri = (lax.broadcasted_iota(I32, (tr, tr), 0) < lax.broadcasted_iota(I32, (tr, tr), 1))
    before = _dot(onehot.astype(BF16), jnp.where(tri, 1.0, 0.0).astype(BF16))
    n = jnp.sum(onehot, axis=1, keepdims=True)
    low = (lax.broadcasted_iota(I32, (N_EXPERTS, N_EXPERTS), 1)
           < lax.broadcasted_iota(I32, (N_EXPERTS, N_EXPERTS), 0))
    start = _dot(jnp.where(low, 1.0, 0.0).astype(BF16),
                 jnp.broadcast_to(n, (N_EXPERTS, LANES)).astype(BF16))[:, 0:1]
    pos = start + before
    lp_ref[...] = jnp.concatenate(
        [jnp.sum(jnp.where(p, pos, 0.0), axis=0, keepdims=True) for p in picks], axis=0).astype(I32)

    lane = lax.broadcasted_iota(I32, cnt_ref.shape, 1)
    cnt_ref[...] += jnp.where(lane == i, n, 0.0)


def _route_call(lgt, router_bias):
    e, t = lgt.shape
    ntiles = t // TILE
    ntp = (ntiles + LANES - 1) // LANES * LANES
    return pl.pallas_call(
        _route_kernel,
        grid=(ntiles,),
        in_specs=[
            pl.BlockSpec((e, TILE), lambda i: (0, i)),
            pl.BlockSpec((e, 1), lambda i: (0, 0)),
        ],
        out_specs=[
            pl.BlockSpec((TOP_K, TILE), lambda i: (0, i)),
            pl.BlockSpec((TOP_K, TILE), lambda i: (0, i)),
            pl.BlockSpec((e, ntp), lambda i: (0, 0)),
        ],
        out_shape=[
            jax.ShapeDtypeStruct((TOP_K, t), I32),
            jax.ShapeDtypeStruct((TOP_K, t), F32),
            jax.ShapeDtypeStruct((e, ntp), F32),
        ],
        compiler_params=pltpu.CompilerParams(
            dimension_semantics=("arbitrary",), vmem_limit_bytes=VMEM_LIMIT),
        name="route",
    )(lgt, router_bias.reshape(e, 1))


_T_LOCAL, _T_LEN, _T_GLOBAL, _T_PER_TILE = 0, 1, 2, 3
_T_EMPTY = 2 * _T_PER_TILE
_T_ROWS = 8
LBUF_ROWS = LROWS + N_EXPERTS


def _segment_copies(tab_ref, half, local_ref, global_hbm, sem, to_global):
    r = _T_PER_TILE * half

    def body(e, carry):
        n = tab_ref[0, r + _T_LEN, e]
        loc = local_ref.at[pl.ds(tab_ref[0, r + _T_LOCAL, e], n)]
        glo = global_hbm.at[pl.ds(tab_ref[0, r + _T_GLOBAL, e], n)]
        if to_global:
            pltpu.make_async_copy(loc, glo, sem).start()
        else:
            pltpu.make_async_copy(glo, loc, sem).start()
        return carry

    lax.fori_loop(0, N_EXPERTS, body, 0, unroll=8)


def _segment_wait(local_ref, global_hbm, sem, nempty):
    rows = LROWS + nempty
    pltpu.make_async_copy(global_hbm.at[pl.ds(0, rows)], local_ref.at[pl.ds(0, rows)], sem).wait()


TAIL_ROWS = EXPERT_ROWS + 2 * N_EXPERTS


def _dispatch_kernel(tab_ref, pad_ref, h2_ref, lp_ref, xs_hbm, xl0, xl1, zb, sem, pending):
    s = pl.program_id(0)
    ns = pl.num_programs(0)

    @pl.when(s == 0)
    def _zero_unassigned_rows():
        zb[...] = jnp.zeros(zb.shape, U32)
        used = pad_ref[2, 0]
        rest = xs_hbm.shape[0] - used
        cp = pltpu.make_async_copy(zb.at[pl.ds(0, rest)], xs_hbm.at[pl.ds(used, rest)], sem.at[2])
        cp.start()
        cp.wait()

        def pad_row(e):
            return pltpu.make_async_copy(zb.at[pl.ds(0, 1)], xs_hbm.at[pl.ds(pad_ref[0, e], 1)], sem.at[2])

        def start(e, carry):
            @pl.when(pad_ref[1, e] > 0)
            def _():
                pad_row(e).start()
            return carry

        def wait(e, carry):
            @pl.when(pad_ref[1, e] > 0)
            def _():
                pad_row(e).wait()
            return carry

        lax.fori_loop(0, N_EXPERTS, start, 0)
        lax.fori_loop(0, N_EXPERTS, wait, 0)

    @pl.when(s > 0)
    def _():
        _segment_wait(xl0, xs_hbm, sem.at[0], pending[0])
        _segment_wait(xl1, xs_hbm, sem.at[1], pending[1])

    for half, xl in ((0, xl0), (1, xl1)):
        rows = h2_ref[half * TILE:(half + 1) * TILE, :]
        lp = lp_ref[:, half * TILE:(half + 1) * TILE]
        for c in range(LROWS // SORT_CHUNK):
            j = lax.broadcasted_iota(I32, (SORT_CHUNK, TILE), 0) + c * SORT_CHUNK
            onehot = jnp.zeros((SORT_CHUNK, TILE), F32)
            for k in range(TOP_K):
                onehot = jnp.where(j == lp[k:k + 1, :], 1.0, onehot)
            _store_packed(xl, c * SORT_CHUNK, SORT_CHUNK, _pack_rows(_dot(onehot.astype(BF16), rows)))
        _segment_copies(tab_ref, half, xl, xs_hbm, sem.at[half], to_global=True)
        pending[half] = tab_ref[0, _T_EMPTY, half]

    @pl.when(s == ns - 1)
    def _():
        _segment_wait(xl0, xs_hbm, sem.at[0], pending[0])
        _segment_wait(xl1, xs_hbm, sem.at[1], pending[1])


def _dispatch_call(tab, pad, h2, lp, nrows):
    t, d = h2.shape
    ns = t // (2 * TILE)
    return pl.pallas_call(
        _dispatch_kernel,
        grid=(ns,),
        in_specs=[
            pl.BlockSpec((1, _T_ROWS, N_EXPERTS), lambda s: (s, 0, 0), memory_space=pltpu.SMEM),
            pl.BlockSpec(memory_space=pltpu.SMEM),
            pl.BlockSpec((2 * TILE, d), lambda s: (s, 0)),
            pl.BlockSpec((TOP_K, 2 * TILE), lambda s: (0, s)),
        ],
        out_specs=pl.BlockSpec(memory_space=pl.ANY),
        out_shape=jax.ShapeDtypeStruct((nrows, PACK_S, LANES), U32),
        scratch_shapes=[
            pltpu.VMEM((LBUF_ROWS, PACK_S, LANES), U32),
            pltpu.VMEM((LBUF_ROWS, PACK_S, LANES), U32),
            pltpu.VMEM((TAIL_ROWS + N_EXPERTS, PACK_S, LANES), U32),
            pltpu.SemaphoreType.DMA((3,)),
            pltpu.SMEM((2,), I32),
        ],
        compiler_params=pltpu.CompilerParams(
            dimension_semantics=("arbitrary",), vmem_limit_bytes=VMEM_LIMIT),
        name="dispatch",
    )(tab, pad, h2, lp)


RING = 4


def _expert_kernel(first_ref, nchunk_ref, row_ref, len_ref, info_ref, wg_ref, wu_ref, wd_ref, xs_hbm, y_hbm,
                   xbuf, ybuf, isem, osem, wgb, wub, wdb):
    e = pl.program_id(0)
    last_step = e == pl.num_programs(0) - 1
    rows = EXPERT_ROWS
    total_chunks = info_ref[0]

    def slab(ref, row0, m):
        size = m * PACK_S if isinstance(m, int) else pl.multiple_of(m * PACK_S, SUBLANES)
        first = row0 * PACK_S if isinstance(row0, int) else pl.multiple_of(row0 * PACK_S, SUBLANES)
        return ref.at[pl.ds(first, size)]

    def fetch(g):
        slot = g % RING
        return pltpu.make_async_copy(slab(xs_hbm, row_ref[g], rows), slab(xbuf, slot * rows, rows), isem.at[slot])

    def flush(g):
        slot = g % RING
        m = len_ref[g]
        return pltpu.make_async_copy(slab(ybuf, slot * rows, m), slab(y_hbm, row_ref[g], m), osem.at[slot])

    @pl.when(e == 0)
    def _prime():
        for a in range(RING - 1):
            @pl.when(a < total_chunks)
            def _():
                fetch(a).start()

    wgb[...] = wg_ref[0].astype(BF16)
    wub[...] = wu_ref[0].astype(BF16)
    wdb[...] = wd_ref[0].astype(BF16)

    def chunk(g, carry):
        slot = g % RING

        @pl.when(g + RING - 1 < total_chunks)
        def _():
            fetch(g + RING - 1).start()

        fetch(g).wait()

        @pl.when(g >= RING)
        def _():
            flush(g - RING).wait()

        xb = _unpack_rows(_load_flat(xbuf, slot * rows, rows))
        gate = _dot(xb, wgb[...])
        up = _dot(xb, wub[...])
        hid = (gate * _sigmoid(gate)) * up
        y = _dot(hid.astype(BF16), wdb[...])
        _store_flat(ybuf, slot * rows, rows, _pack_rows(y.astype(BF16).astype(F32)))
        flush(g).start()
        return carry

    lax.fori_loop(first_ref[e], first_ref[e] + nchunk_ref[e], chunk, 0)

    @pl.when(last_step)
    def _drain():
        def wait(g, carry):
            flush(g).wait()
            return carry
        lax.fori_loop(jnp.maximum(total_chunks - RING, 0), total_chunks, wait, 0)

    @pl.when(last_step)
    def _define_unused_rows():
        used = info_ref[1]
        rest = y_hbm.shape[0] // PACK_S - used

        @pl.when(rest > 0)
        def _():
            ybuf[0:N_EXPERTS * PACK_S, :] = jnp.zeros((N_EXPERTS * PACK_S, LANES), U32)
            cp = pltpu.make_async_copy(slab(ybuf, 0, rest), slab(y_hbm, used, rest), osem.at[0])
            cp.start()
            cp.wait()


def _expert_call(first, nchunk, chunk_row, chunk_len, info, xs, w_gate, w_up, w_down, nrows):
    ne, d, hid = w_gate.shape
    rows = EXPERT_ROWS
    wsel = lambda e, *_: (e, 0, 0)
    grid_spec = pltpu.PrefetchScalarGridSpec(
        num_scalar_prefetch=5,
        grid=(ne,),
        in_specs=[
            pl.BlockSpec((1, d, hid), wsel),
            pl.BlockSpec((1, d, hid), wsel),
            pl.BlockSpec((1, hid, d), wsel),
            pl.BlockSpec(memory_space=pl.ANY),
        ],
        out_specs=pl.BlockSpec(memory_space=pl.ANY),
        scratch_shapes=[
            pltpu.VMEM((RING * rows * PACK_S, LANES), U32),
            pltpu.VMEM((RING * rows * PACK_S, LANES), U32),
            pltpu.SemaphoreType.DMA((RING,)),
            pltpu.SemaphoreType.DMA((RING,)),
            pltpu.VMEM((d, hid), BF16),
            pltpu.VMEM((d, hid), BF16),
            pltpu.VMEM((hid, d), BF16),
        ],
    )
    return pl.pallas_call(
        _expert_kernel,
        grid_spec=grid_spec,
        out_shape=jax.ShapeDtypeStruct((nrows * PACK_S, LANES), U32),
        compiler_params=pltpu.CompilerParams(
            dimension_semantics=("arbitrary",), vmem_limit_bytes=VMEM_LIMIT),
        name="experts",
    )(first, nchunk, chunk_row, chunk_len, info, w_gate, w_up, w_down,
      xs.reshape(-1, LANES)).reshape(nrows, PACK_S, LANES)


def _combine_kernel(tab_ref, tabn_ref, lp_ref, gw_ref, xs1_ref, mod_ref, fg_ref, y_hbm, o_ref, yl0, yl1, sem):
    s = pl.program_id(0)
    ns = pl.num_programs(0)

    @pl.when(s == 0)
    def _():
        _segment_copies(tab_ref, 0, yl0, y_hbm, sem.at[0], to_global=False)

    g2 = mod_ref[0][5:6]
    for half, yl in ((0, yl0), (1, yl1)):
        _segment_wait(yl, y_hbm, sem.at[half], tab_ref[0, _T_EMPTY, half])
        if half == 0:
            _segment_copies(tab_ref, 1, yl1, y_hbm, sem.at[1], to_global=False)
        lp = lp_ref[:, half * TILE:(half + 1) * TILE]
        gw = gw_ref[:, half * TILE:(half + 1) * TILE]
        acc = jnp.zeros((TILE, D_MODEL), F32)
        for c in range(LROWS // SORT_CHUNK):
            j = lax.broadcasted_iota(I32, (SORT_CHUNK, TILE), 0) + c * SORT_CHUNK
            wt = jnp.zeros((SORT_CHUNK, TILE), F32)
            for k in range(TOP_K):
                wt = jnp.where(j == lp[k:k + 1, :], gw[k:k + 1, :], wt)
            yv = _unpack_rows(_load_packed(yl, c * SORT_CHUNK, SORT_CHUNK))
            acc = acc + _dot_tn(wt.astype(BF16), yv)
        if half == 0:
            _segment_copies(tabn_ref, 0, yl0, y_hbm, sem.at[0], to_global=False)
        x2 = xs1_ref[half * TILE:(half + 1) * TILE, :] + g2 * acc
        o_ref[half * TILE:(half + 1) * TILE, :] = _rms(x2) * fg_ref[...]

    @pl.when(s == ns - 1)
    def _():
        _segment_wait(yl0, y_hbm, sem.at[0], tab_ref[0, _T_EMPTY, 0])


def _combine_call(tab, lp, gw, xs1, mod3, final_g, y, seq):
    t, d = xs1.shape
    ns = t // (2 * TILE)
    per_seq = seq // (2 * TILE)
    return pl.pallas_call(
        _combine_kernel,
        grid=(ns,),
        in_specs=[
            pl.BlockSpec((1, _T_ROWS, N_EXPERTS), lambda s: (s, 0, 0), memory_space=pltpu.SMEM),
            pl.BlockSpec((1, _T_ROWS, N_EXPERTS), lambda s: (jnp.minimum(s + 1, ns - 1), 0, 0),
                         memory_space=pltpu.SMEM),
            pl.BlockSpec((TOP_K, 2 * TILE), lambda s: (0, s)),
            pl.BlockSpec((TOP_K, 2 * TILE), lambda s: (0, s)),
            pl.BlockSpec((2 * TILE, d), lambda s: (s, 0)),
            pl.BlockSpec((1, 6, d), lambda s: (s // per_seq, 0, 0)),
            pl.BlockSpec((1, d), lambda s: (0, 0)),
            pl.BlockSpec(memory_space=pl.ANY),
        ],
        out_specs=pl.BlockSpec((2 * TILE, d), lambda s: (s, 0)),
        out_shape=jax.ShapeDtypeStruct((t, d), F32),
        scratch_shapes=[
            pltpu.VMEM((LBUF_ROWS, PACK_S, LANES), U32),
            pltpu.VMEM((LBUF_ROWS, PACK_S, LANES), U32),
            pltpu.SemaphoreType.DMA((2,)),
        ],
        compiler_params=pltpu.CompilerParams(
            dimension_semantics=("arbitrary",), vmem_limit_bytes=VMEM_LIMIT),
        name="combine",
    )(tab, tab, lp, gw, xs1, mod3, final_g.reshape(1, d), y)


def kernel(x, c, w_ada, b_ada, w_in, attn_sinks, rel_bias, attn_norm_g, conv_w, conv_b, conv_ln_g,
           conv_ln_b, w_out, w_router, router_bias, w_exp_gate, w_exp_up, w_exp_down, w_sh_gate,
           w_sh_up, w_sh_down, final_norm_g):
    bsz, seq, d = x.shape
    t = bsz * seq
    assert w_ada.shape[0] == 1 and d == D_MODEL
    assert seq % MIX_ROWS == 0 and seq % (2 * TILE) == 0

    mod3 = _ada_call(c, w_ada[0], b_ada[0]).reshape(bsz, 6, d)
    xs1, h2, lgt = _mixer_call(
        x, mod3, w_in[0].astype(BF16), attn_sinks[0], rel_bias,
        attn_norm_g[0].reshape(1, Q_COLS), conv_w[0], conv_b[0].reshape(1, CONV_CH),
        conv_ln_g[0].reshape(1, CONV_CH), conv_ln_b[0].reshape(1, CONV_CH),
        w_out[0].astype(BF16), w_router[0].T,
        w_sh_gate[0].astype(BF16), w_sh_up[0].astype(BF16), w_sh_down[0].astype(BF16))
    xs1 = xs1.reshape(t, d)
    h2 = h2.reshape(t, d)

    lp, gw, cnt = _route_call(lgt, router_bias[0])

    ntiles = t // TILE
    nassign = t * TOP_K
    n = cnt[:, :ntiles].T.astype(I32)
    local = jnp.cumsum(n, axis=1) - n
    earlier = jnp.cumsum(n, axis=0) - n
    total = jnp.sum(n, axis=0)
    region = (total + 1) // 2 * 2
    starts = jnp.cumsum(region) - region
    nalloc = nassign + N_EXPERTS
    used = jnp.sum(region)
    pad = jnp.stack([starts + total, region - total,
                     jnp.broadcast_to(used, (N_EXPERTS,))], axis=0)
    glob = starts[None, :] + earlier
    empty = n == 0
    length = jnp.maximum(n, 1)
    eid = jnp.arange(N_EXPERTS, dtype=I32)[None, :]
    parity = (jnp.arange(ntiles, dtype=I32) % 2)[:, None]
    spare_global = nalloc + EXPERT_ROWS + parity * N_EXPERTS + eid
    nempty = jnp.sum(empty.astype(I32), axis=1).reshape(ntiles // 2, 2)
    misc = jnp.pad(nempty, ((0, 0), (0, N_EXPERTS - 2)))[:, None, :]

    def table(loc, glo):
        per_tile = jnp.stack([loc, length, glo], axis=1).reshape(ntiles // 2, 2 * _T_PER_TILE, N_EXPERTS)
        fill = jnp.zeros((ntiles // 2, _T_ROWS - 2 * _T_PER_TILE - 1, N_EXPERTS), I32)
        return jnp.concatenate([per_tile, misc, fill], axis=1)

    tab_out = table(jnp.where(empty, 0, local), jnp.where(empty, spare_global, glob))
    tab_back = table(jnp.where(empty, LROWS + eid, local), jnp.where(empty, 0, glob))

    rows = EXPERT_ROWS
    nchunk = (total + rows - 1) // rows
    chunk_end = jnp.cumsum(nchunk)
    first = chunk_end - nchunk
    max_chunks = nassign // rows + N_EXPERTS
    gid = jnp.arange(max_chunks, dtype=I32)
    owner = jnp.minimum(jnp.sum((chunk_end[None, :] <= gid[:, None]).astype(I32), axis=1), N_EXPERTS - 1)
    pick = owner[:, None] == eid
    within = gid - jnp.sum(jnp.where(pick, first[None, :], 0), axis=1)
    chunk_row = jnp.sum(jnp.where(pick, starts[None, :], 0), axis=1) + within * rows
    left = jnp.sum(jnp.where(pick, region[None, :], 0), axis=1) - within * rows
    live = gid < chunk_end[-1]
    chunk_row = jnp.where(live, chunk_row, 0)
    chunk_len = jnp.where(live, jnp.clip(left, 2, rows), 2)
    info = jnp.stack([chunk_end[-1], used])

    xs = _dispatch_call(tab_out, pad, h2, lp, nalloc + TAIL_ROWS)
    y = _expert_call(first, nchunk, chunk_row, chunk_len, info, xs,
                     w_exp_gate[0], w_exp_up[0], w_exp_down[0], nalloc)
    out = _combine_call(tab_back, lp, gw, xs1, mod3, final_norm_g, y, seq)
    return out.reshape(bsz, seq, d)
```

```python
import numpy as np
import jax
import jax.numpy as jnp
from jax import lax
from jax.experimental import pallas as pl
from jax.experimental.pallas import tpu as pltpu

F32 = jnp.float32
BF16 = jnp.bfloat16
U32 = jnp.uint32
I32 = jnp.int32

D_MODEL = 1024
HEAD_DIM = 64
N_HEADS = 8
N_KV_HEADS = 2
GROUP = N_HEADS // N_KV_HEADS
Q_COLS = N_HEADS * HEAD_DIM
KV_COLS = N_KV_HEADS * HEAD_DIM
ATTN_BLOCK = 128
WINDOW = 128
NUM_BUCKETS = 32
MAX_DISTANCE = 128
CONV_CH = D_MODEL - Q_COLS
CONV_WIDTH = 31
IN_COLS = Q_COLS + 2 * KV_COLS + 2 * CONV_CH
N_EXPERTS = 256
TOP_K = 8
N_GROUPS = 8
GROUP_SIZE = N_EXPERTS // N_GROUPS
TOPK_GROUPS = 4
EXPERT_HIDDEN = 256
ROUTED_SCALE = 2.5
EPS = 1e-6

MIX_ROWS = 256
HALO = 32
TILE = 256
LROWS = TILE * TOP_K
SORT_CHUNK = 256
EXPERT_ROWS = 512
LANES = 128
SUBLANES = 8
PACK_W = D_MODEL // 2
PACK_S = PACK_W // LANES
VMEM_LIMIT = 56 * 1024 * 1024


def _sigmoid(v):
    return 1.0 / (1.0 + jnp.exp(-v))


def _rms(v):
    return v * lax.rsqrt(jnp.mean(v * v, axis=-1, keepdims=True) + EPS)


def _split_bf16(a):
    hi = a.astype(BF16)
    lo = (a - hi.astype(F32)).astype(BF16)
    return hi, lo


def _dot(a, b):
    return jnp.dot(a, b, preferred_element_type=F32)


def _dot_nt(a, b):
    return lax.dot_general(a, b, (((1,), (1,)), ((), ())), preferred_element_type=F32)


def _dot_tn(a, b):
    return lax.dot_general(a, b, (((0,), (0,)), ((), ())), preferred_element_type=F32)


def _dot3(a, b, dot):
    ah, al = _split_bf16(a)
    bh, bl = _split_bf16(b)
    return dot(ah, bh) + (dot(ah, bl) + dot(al, bh))


def _pack_rows(v):
    hi = lax.bitcast_convert_type(v[:, :PACK_W], U32) & jnp.uint32(0xFFFF0000)
    lo = lax.bitcast_convert_type(v[:, PACK_W:], U32) >> 16
    return hi | lo


def _unpack_rows(u):
    hi = lax.bitcast_convert_type(u & jnp.uint32(0xFFFF0000), F32)
    lo = lax.bitcast_convert_type(u << 16, F32)
    return jnp.concatenate([hi, lo], axis=1).astype(BF16)


def _load_flat(flat, r0, n):
    return jnp.concatenate(
        [flat[pl.ds(r0 * PACK_S + c, n, stride=PACK_S), :] for c in range(PACK_S)], axis=1)


def _store_flat(flat, r0, n, u):
    for c in range(PACK_S):
        flat[pl.ds(r0 * PACK_S + c, n, stride=PACK_S), :] = u[:, c * LANES:(c + 1) * LANES]


def _load_packed(ref3, r0, n):
    return _load_flat(ref3.reshape(ref3.shape[0] * PACK_S, LANES), r0, n)


def _store_packed(ref3, r0, n, u):
    _store_flat(ref3.reshape(ref3.shape[0] * PACK_S, LANES), r0, n, u)


def _ada_kernel(c_ref, w_ref, b_ref, o_ref):
    c = c_ref[...]
    s = c * _sigmoid(c)
    o_ref[...] = _dot3(s, w_ref[...], _dot) + b_ref[...]


def _ada_call(c, w_ada, b_ada):
    bsz, d = c.shape
    n = w_ada.shape[1]
    tn = 1536
    return pl.pallas_call(
        _ada_kernel,
        grid=(n // tn,),
        in_specs=[
            pl.BlockSpec((bsz, d), lambda i: (0, 0)),
            pl.BlockSpec((d, tn), lambda i: (0, i)),
            pl.BlockSpec((1, tn), lambda i: (0, i)),
        ],
        out_specs=pl.BlockSpec((bsz, tn), lambda i: (0, i)),
        out_shape=jax.ShapeDtypeStruct((bsz, n), F32),
        compiler_params=pltpu.CompilerParams(
            dimension_semantics=("arbitrary",), vmem_limit_bytes=VMEM_LIMIT),
        name="ada",
    )(c, w_ada, b_ada.reshape(1, n))


def _bucket_table():
    qi = np.arange(ATTN_BLOCK)[:, None]
    ki = np.arange(2 * ATTN_BLOCK)[None, :]
    dist = qi + ATTN_BLOCK - ki
    n = np.maximum(dist, 0)
    max_exact = NUM_BUCKETS // 2
    large = max_exact + (np.log(np.maximum(n, 1) / max_exact) / np.log(MAX_DISTANCE / max_exact)
                         * (NUM_BUCKETS - max_exact)).astype(np.int32)
    large = np.minimum(large, NUM_BUCKETS - 1)
    bkt = np.where(n < max_exact, n, large).astype(np.int32)
    band = (dist >= 0) & (dist < WINDOW)
    return np.where(band, bkt, -1).astype(np.int32)


def _mixer_kernel(x_ref, mod_ref, win_ref, sinks_ref, relb_ref, bkt_ref, ang_ref, cw_ref, cb_ref,
                  lng_ref, lnb_ref, wout_ref, wrt_ref, wsg_ref, wsu_ref, wsd_ref,
                  xs1_ref, h2_ref, lgt_ref,
                  kv_ref, uext_ref, bias_ref):
    tm = x_ref.shape[1]
    nblk = tm // ATTN_BLOCK
    j = pl.program_id(1)

    @pl.when((pl.program_id(0) == 0) & (j == 0))
    def _build_bias():
        bkt = bkt_ref[...]
        for h in range(N_HEADS):
            acc = jnp.full(bkt.shape, -jnp.inf, F32)
            for b in range(NUM_BUCKETS):
                acc = jnp.where(bkt == b, relb_ref[b, h], acc)
            bias_ref[h] = acc

    @pl.when(j == 0)
    def _reset_history():
        kv_ref[...] = jnp.zeros(kv_ref.shape, F32)
        uext_ref[0:HALO, :] = jnp.zeros((HALO, CONV_CH), F32)
        uext_ref[HALO + tm:HALO + tm + SUBLANES, :] = jnp.zeros((SUBLANES, CONV_CH), F32)

    x = x_ref[0]
    mod = mod_ref[0]
    sh1, sc1, g1 = mod[0:1], mod[1:2], mod[2:3]
    sh2, sc2, g2 = mod[3:4], mod[4:5], mod[5:6]

    h = _rms(x) * (1.0 + sc1) + sh1
    proj = _dot(h.astype(BF16), win_ref[...])

    qb = (proj[:, :Q_COLS] * (HEAD_DIM ** -0.5)).astype(BF16)
    kv_cur = proj[:, Q_COLS:Q_COLS + 2 * KV_COLS]
    kvb = jnp.concatenate([kv_ref[...], kv_cur], axis=0).astype(BF16)
    kv_ref[...] = kv_cur[tm - ATTN_BLOCK:, :]
    not_first = j > 0
    col = lax.broadcasted_iota(jnp.int32, (GROUP * ATTN_BLOCK, 2 * ATTN_BLOCK), 1)
    attn_rows = []
    for i in range(nblk):
        r0, r1, r2 = i * ATTN_BLOCK, (i + 1) * ATTN_BLOCK, (i + 2) * ATTN_BLOCK
        heads = [None] * N_HEADS
        for kh in range(N_KV_HEADS):
            kpc = kvb[r0:r2, kh * HEAD_DIM:(kh + 1) * HEAD_DIM]
            vpc = kvb[r0:r2, KV_COLS + kh * HEAD_DIM:KV_COLS + (kh + 1) * HEAD_DIM]
            qg = jnp.concatenate(
                [qb[r0:r1, (kh * GROUP + g) * HEAD_DIM:(kh * GROUP + g + 1) * HEAD_DIM]
                 for g in range(GROUP)], axis=0)
            bias = bias_ref[kh * GROUP:(kh + 1) * GROUP].reshape(GROUP * ATTN_BLOCK, 2 * ATTN_BLOCK)
            logit = _dot_nt(qg, kpc) + bias
            if i == 0:
                logit = jnp.where((col >= ATTN_BLOCK) | not_first, logit, -jnp.inf)
            sink = jnp.concatenate(
                [jnp.full((ATTN_BLOCK, 1), sinks_ref[kh * GROUP + g], F32) for g in range(GROUP)], axis=0)
            m = jnp.maximum(jnp.max(logit, axis=-1, keepdims=True), sink)
            p = jnp.exp(logit - m)
            den = jnp.sum(p, axis=-1, keepdims=True) + jnp.exp(sink - m)
            o = _dot(p.astype(BF16), vpc) / den
            for g in range(GROUP):
                heads[kh * GROUP + g] = o[g * ATTN_BLOCK:(g + 1) * ATTN_BLOCK, :]
        attn_rows.append(jnp.concatenate(heads, axis=1))
    attn = jnp.concatenate(attn_rows, axis=0)
    attn = _rms(attn) * ang_ref[...]

    a = proj[:, Q_COLS + 2 * KV_COLS:Q_COLS + 2 * KV_COLS + CONV_CH]
    gt = proj[:, Q_COLS + 2 * KV_COLS + CONV_CH:]
    uext_ref[HALO:HALO + tm, :] = a * _sigmoid(gt)
    cw = cw_ref[...]
    base = HALO - (CONV_WIDTH - 1)
    acc = None
    for res in range(SUBLANES):
        part = None
        for hi in range((HALO + SUBLANES - 1) // SUBLANES + 1):
            t = hi * SUBLANES + res - base
            if 0 <= t < CONV_WIDTH:
                term = cw[t:t + 1, :] * uext_ref[hi * SUBLANES:hi * SUBLANES + tm + SUBLANES, :]
                part = term if part is None else part + term
        part = part[res:res + tm, :]
        acc = part if acc is None else acc + part
    uext_ref[0:HALO, :] = uext_ref[tm:tm + HALO, :]
    cv = acc + cb_ref[...]
    mu = jnp.mean(cv, axis=-1, keepdims=True)
    var = jnp.mean(jnp.square(cv - mu), axis=-1, keepdims=True)
    cv = (cv - mu) * lax.rsqrt(var + EPS) * lng_ref[...] + lnb_ref[...]
    cv = cv * _sigmoid(cv)

    mixed = (_dot(attn.astype(BF16), wout_ref[0:Q_COLS, :])
             + _dot(cv.astype(BF16), wout_ref[Q_COLS:, :]))
    x1 = x + g1 * mixed

    h2 = _rms(x1) * (1.0 + sc2) + sh2
    lgt_ref[...] = _dot3(wrt_ref[...], h2, _dot_nt)
    h2b = h2.astype(BF16)
    h2_ref[0] = h2b
    sg = _dot(h2b, wsg_ref[...])
    su = _dot(h2b, wsu_ref[...])
    shared = _dot(((sg * _sigmoid(sg)) * su).astype(BF16), wsd_ref[...])
    xs1_ref[0] = x1 + g2 * shared


def _mixer_call(x, mod3, w_in_b, sinks, rel_bias, ang, cw, cb, lng, lnb, w_out_b, w_rt, wsg_b, wsu_b, wsd_b):
    bsz, seq, d = x.shape
    tm = MIX_ROWS
    nj = seq // tm
    bkt = jnp.asarray(_bucket_table())
    full = lambda shape: pl.BlockSpec(shape, lambda b, j: (0,) * len(shape))
    smem = pl.BlockSpec(memory_space=pltpu.SMEM)
    return pl.pallas_call(
        _mixer_kernel,
        grid=(bsz, nj),
        in_specs=[
            pl.BlockSpec((1, tm, d), lambda b, j: (b, j, 0)),
            pl.BlockSpec((1, 6, d), lambda b, j: (b, 0, 0)),
            full((d, IN_COLS)),
            smem, smem,
            full((ATTN_BLOCK, 2 * ATTN_BLOCK)),
            full((1, Q_COLS)),
            full((CONV_WIDTH, CONV_CH)),
            full((1, CONV_CH)), full((1, CONV_CH)), full((1, CONV_CH)),
            full((d, d)),
            full((N_EXPERTS, d)),
            full((d, EXPERT_HIDDEN)), full((d, EXPERT_HIDDEN)), full((EXPERT_HIDDEN, d)),
        ],
        out_specs=[
            pl.BlockSpec((1, tm, d), lambda b, j: (b, j, 0)),
            pl.BlockSpec((1, tm, d), lambda b, j: (b, j, 0)),
            pl.BlockSpec((N_EXPERTS, tm), lambda b, j: (0, b * nj + j)),
        ],
        out_shape=[
            jax.ShapeDtypeStruct((bsz, seq, d), F32),
            jax.ShapeDtypeStruct((bsz, seq, d), BF16),
            jax.ShapeDtypeStruct((N_EXPERTS, bsz * seq), F32),
        ],
        scratch_shapes=[
            pltpu.VMEM((ATTN_BLOCK, 2 * KV_COLS), F32),
            pltpu.VMEM((HALO + tm + SUBLANES, CONV_CH), F32),
            pltpu.VMEM((N_HEADS, ATTN_BLOCK, 2 * ATTN_BLOCK), F32),
        ],
        compiler_params=pltpu.CompilerParams(
            dimension_semantics=("arbitrary", "arbitrary"), vmem_limit_bytes=VMEM_LIMIT),
        name="mixer",
    )(x, mod3, w_in_b, sinks, rel_bias, bkt, ang, cw, cb, lng, lnb, w_out_b, w_rt, wsg_b, wsu_b, wsd_b)


def _route_kernel(lgt_ref, rb_ref, lp_ref, gw_ref, cnt_ref):
    tr = lgt_ref.shape[1]
    i = pl.program_id(0)

    @pl.when(i == 0)
    def _():
        cnt_ref[...] = jnp.zeros(cnt_ref.shape, F32)

    scores = _sigmoid(lgt_ref[...])
    sel = scores + rb_ref[...]
    neg = -jnp.inf

    sel3 = sel.reshape(N_GROUPS, GROUP_SIZE, tr)
    loc = lax.broadcasted_iota(I32, sel3.shape, 1)
    m1 = jnp.max(sel3, axis=1, keepdims=True)
    i1 = jnp.min(jnp.where(sel3 == m1, loc, GROUP_SIZE), axis=1, keepdims=True)
    m2 = jnp.max(jnp.where(loc == i1, neg, sel3), axis=1, keepdims=True)
    gscore = (m1 + m2).reshape(N_GROUPS, tr)

    gio = lax.broadcasted_iota(I32, gscore.shape, 0)
    gmask = jnp.zeros(gscore.shape, jnp.bool_)
    cur = gscore
    for _ in range(TOPK_GROUPS):
        m = jnp.max(cur, axis=0, keepdims=True)
        idx = jnp.min(jnp.where(cur == m, gio, N_GROUPS), axis=0, keepdims=True)
        pick = gio == idx
        gmask = gmask | pick
        cur = jnp.where(pick, neg, cur)
    emask = jnp.broadcast_to(gmask.reshape(N_GROUPS, 1, tr), sel3.shape).reshape(N_EXPERTS, tr)

    rio = lax.broadcasted_iota(I32, sel.shape, 0)
    cur = jnp.where(emask, sel, neg)
    picks, gsc = [], []
    for _ in range(TOP_K):
        m = jnp.max(cur, axis=0, keepdims=True)
        idx = jnp.min(jnp.where(cur == m, rio, N_EXPERTS), axis=0, keepdims=True)
        pick = rio == idx
        picks.append(pick)
        gsc.append(jnp.sum(jnp.where(pick, scores, 0.0), axis=0, keepdims=True))
        cur = jnp.where(pick, neg, cur)
    gsum = gsc[0]
    for k in range(1, TOP_K):
        gsum = gsum + gsc[k]
    gw_ref[...] = jnp.concatenate([g / gsum * ROUTED_SCALE for g in gsc], axis=0)

    chosen = picks[0]
    for k in range(1, TOP_K):
        chosen = chosen | picks[k]
    onehot = jnp.where(chosen, 1.0, 0.0)
    tri = (lax.broadcasted_iota(I32, (tr, tr), 0) < lax.broadcasted_iota(I32, (tr, tr), 1))
    before = _dot(onehot.astype(BF16), jnp.where(tri, 1.0, 0.0).astype(BF16))
    n = jnp.sum(onehot, axis=1, keepdims=True)
    low = (lax.broadcasted_iota(I32, (N_EXPERTS, N_EXPERTS), 1)
           < lax.broadcasted_iota(I32, (N_EXPERTS, N_EXPERTS), 0))
    start = _dot(jnp.where(low, 1.0, 0.0).astype(BF16),
                 jnp.broadcast_to(n, (N_EXPERTS, LANES)).astype(BF16))[:, 0:1]
    pos = start + before
    lp_ref[...] = jnp.concatenate(
        [jnp.sum(jnp.where(p, pos, 0.0), axis=0, keepdims=True) for p in picks], axis=0).astype(I32)

    lane = lax.broadcasted_iota(I32, cnt_ref.shape, 1)
    cnt_ref[...] += jnp.where(lane == i, n, 0.0)


def _route_call(lgt, router_bias):
    e, t = lgt.shape
    ntiles = t // TILE
    ntp = (ntiles + LANES - 1) // LANES * LANES
    return pl.pallas_call(
        _route_kernel,
        grid=(ntiles,),
        in_specs=[
            pl.BlockSpec((e, TILE), lambda i: (0, i)),
            pl.BlockSpec((e, 1), lambda i: (0, 0)),
        ],
        out_specs=[
            pl.BlockSpec((TOP_K, TILE), lambda i: (0, i)),
            pl.BlockSpec((TOP_K, TILE), lambda i: (0, i)),
            pl.BlockSpec((e, ntp), lambda i: (0, 0)),
        ],
        out_shape=[
            jax.ShapeDtypeStruct((TOP_K, t), I32),
            jax.ShapeDtypeStruct((TOP_K, t), F32),
            jax.ShapeDtypeStruct((e, ntp), F32),
        ],
        compiler_params=pltpu.CompilerParams(
            dimension_semantics=("arbitrary",), vmem_limit_bytes=VMEM_LIMIT),
        name="route",
    )(lgt, router_bias.reshape(e, 1))


_T_LOCAL, _T_LEN, _T_GLOBAL, _T_PER_TILE = 0, 1, 2, 3
_T_EMPTY = 2 * _T_PER_TILE
_T_ROWS = 8
LBUF_ROWS = LROWS + N_EXPERTS


def _segment_copies(tab_ref, half, local_ref, global_hbm, sem, to_global):
    r = _T_PER_TILE * half

    def body(e, carry):
        n = tab_ref[0, r + _T_LEN, e]
        loc = local_ref.at[pl.ds(tab_ref[0, r + _T_LOCAL, e], n)]
        glo = global_hbm.at[pl.ds(tab_ref[0, r + _T_GLOBAL, e], n)]
        if to_global:
            pltpu.make_async_copy(loc, glo, sem).start()
        else:
            pltpu.make_async_copy(glo, loc, sem).start()
        return carry

    lax.fori_loop(0, N_EXPERTS, body, 0, unroll=8)


def _chunk_relative(lp_row, c):
    rel = lp_row - c * SORT_CHUNK
    inside = (rel >= 0) & (rel < SORT_CHUNK)
    return jnp.where(inside, rel, -1).astype(F32).astype(BF16)


def _chunk_row_ids():
    return lax.broadcasted_iota(I32, (SORT_CHUNK, TILE), 0).astype(F32).astype(BF16)


def _segment_wait(local_ref, global_hbm, sem, nempty):
    rows = LROWS + nempty
    pltpu.make_async_copy(global_hbm.at[pl.ds(0, rows)], local_ref.at[pl.ds(0, rows)], sem).wait()


TAIL_ROWS = EXPERT_ROWS + 2 * N_EXPERTS


def _dispatch_kernel(tab_ref, pad_ref, h2_ref, lp_ref, xs_hbm, xl0, xl1, zb, sem, pending):
    s = pl.program_id(0)
    ns = pl.num_programs(0)

    @pl.when(s == 0)
    def _zero_unassigned_rows():
        zb[...] = jnp.zeros(zb.shape, U32)
        used = pad_ref[2, 0]
        rest = xs_hbm.shape[0] - used
        cp = pltpu.make_async_copy(zb.at[pl.ds(0, rest)], xs_hbm.at[pl.ds(used, rest)], sem.at[2])
        cp.start()
        cp.wait()

        def pad_row(e):
            return pltpu.make_async_copy(zb.at[pl.ds(0, 1)], xs_hbm.at[pl.ds(pad_ref[0, e], 1)], sem.at[2])

        def start(e, carry):
            @pl.when(pad_ref[1, e] > 0)
            def _():
                pad_row(e).start()
            return carry

        def wait(e, carry):
            @pl.when(pad_ref[1, e] > 0)
            def _():
                pad_row(e).wait()
            return carry

        lax.fori_loop(0, N_EXPERTS, start, 0)
        lax.fori_loop(0, N_EXPERTS, wait, 0)

    @pl.when(s > 0)
    def _():
        _segment_wait(xl0, xs_hbm, sem.at[0], pending[0])
        _segment_wait(xl1, xs_hbm, sem.at[1], pending[1])

    for half, xl in ((0, xl0), (1, xl1)):
        rows = h2_ref[half * TILE:(half + 1) * TILE, :]
        lp = lp_ref[:, half * TILE:(half + 1) * TILE]
        j = _chunk_row_ids()
        for c in range(LROWS // SORT_CHUNK):
            onehot = jnp.zeros((SORT_CHUNK, TILE), BF16)
            for k in range(TOP_K):
                onehot = jnp.where(j == _chunk_relative(lp[k:k + 1, :], c), jnp.ones((), BF16), onehot)
            _store_packed(xl, c * SORT_CHUNK, SORT_CHUNK, _pack_rows(_dot(onehot, rows)))
        _segment_copies(tab_ref, half, xl, xs_hbm, sem.at[half], to_global=True)
        pending[half] = tab_ref[0, _T_EMPTY, half]

    @pl.when(s == ns - 1)
    def _():
        _segment_wait(xl0, xs_hbm, sem.at[0], pending[0])
        _segment_wait(xl1, xs_hbm, sem.at[1], pending[1])


def _dispatch_call(tab, pad, h2, lp, nrows):
    t, d = h2.shape
    ns = t // (2 * TILE)
    return pl.pallas_call(
        _dispatch_kernel,
        grid=(ns,),
        in_specs=[
            pl.BlockSpec((1, _T_ROWS, N_EXPERTS), lambda s: (s, 0, 0), memory_space=pltpu.SMEM),
            pl.BlockSpec(memory_space=pltpu.SMEM),
            pl.BlockSpec((2 * TILE, d), lambda s: (s, 0)),
            pl.BlockSpec((TOP_K, 2 * TILE), lambda s: (0, s)),
        ],
        out_specs=pl.BlockSpec(memory_space=pl.ANY),
        out_shape=jax.ShapeDtypeStruct((nrows, PACK_S, LANES), U32),
        scratch_shapes=[
            pltpu.VMEM((LBUF_ROWS, PACK_S, LANES), U32),
            pltpu.VMEM((LBUF_ROWS, PACK_S, LANES), U32),
            pltpu.VMEM((TAIL_ROWS + N_EXPERTS, PACK_S, LANES), U32),
            pltpu.SemaphoreType.DMA((3,)),
            pltpu.SMEM((2,), I32),
        ],
        compiler_params=pltpu.CompilerParams(
            dimension_semantics=("arbitrary",), vmem_limit_bytes=VMEM_LIMIT),
        name="dispatch",
    )(tab, pad, h2, lp)


RING = 4


def _expert_kernel(first_ref, nchunk_ref, row_ref, len_ref, info_ref, wg_ref, wu_ref, wd_ref, xs_hbm, y_hbm,
                   xbuf, ybuf, isem, osem, wgb, wub, wdb):
    e = pl.program_id(0)
    last_step = e == pl.num_programs(0) - 1
    rows = EXPERT_ROWS
    total_chunks = info_ref[0]

    def slab(ref, row0, m):
        size = m * PACK_S if isinstance(m, int) else pl.multiple_of(m * PACK_S, SUBLANES)
        first = row0 * PACK_S if isinstance(row0, int) else pl.multiple_of(row0 * PACK_S, SUBLANES)
        return ref.at[pl.ds(first, size)]

    def fetch(g):
        slot = g % RING
        return pltpu.make_async_copy(slab(xs_hbm, row_ref[g], rows), slab(xbuf, slot * rows, rows), isem.at[slot])

    def flush(g):
        slot = g % RING
        m = len_ref[g]
        return pltpu.make_async_copy(slab(ybuf, slot * rows, m), slab(y_hbm, row_ref[g], m), osem.at[slot])

    @pl.when(e == 0)
    def _prime():
        for a in range(RING - 1):
            @pl.when(a < total_chunks)
            def _():
                fetch(a).start()

    wgb[...] = wg_ref[0].astype(BF16)
    wub[...] = wu_ref[0].astype(BF16)
    wdb[...] = wd_ref[0].astype(BF16)

    def chunk(g, carry):
        slot = g % RING

        @pl.when(g + RING - 1 < total_chunks)
        def _():
            fetch(g + RING - 1).start()

        fetch(g).wait()

        @pl.when(g >= RING)
        def _():
            flush(g - RING).wait()

        xb = _unpack_rows(_load_flat(xbuf, slot * rows, rows))
        gate = _dot(xb, wgb[...])
        up = _dot(xb, wub[...])
        hid = (gate * _sigmoid(gate)) * up
        y = _dot(hid.astype(BF16), wdb[...])
        _store_flat(ybuf, slot * rows, rows, _pack_rows(y.astype(BF16).astype(F32)))
        flush(g).start()
        return carry

    lax.fori_loop(first_ref[e], first_ref[e] + nchunk_ref[e], chunk, 0)

    @pl.when(last_step)
    def _drain():
        def wait(g, carry):
            flush(g).wait()
            return carry
        lax.fori_loop(jnp.maximum(total_chunks - RING, 0), total_chunks, wait, 0)

    @pl.when(last_step)
    def _define_unused_rows():
        used = info_ref[1]
        rest = y_hbm.shape[0] // PACK_S - used

        @pl.when(rest > 0)
        def _():
            ybuf[0:N_EXPERTS * PACK_S, :] = jnp.zeros((N_EXPERTS * PACK_S, LANES), U32)
            cp = pltpu.make_async_copy(slab(ybuf, 0, rest), slab(y_hbm, used, rest), osem.at[0])
            cp.start()
            cp.wait()


def _expert_call(first, nchunk, chunk_row, chunk_len, info, xs, w_gate, w_up, w_down, nrows):
    ne, d, hid = w_gate.shape
    rows = EXPERT_ROWS
    wsel = lambda e, *_: (e, 0, 0)
    grid_spec = pltpu.PrefetchScalarGridSpec(
        num_scalar_prefetch=5,
        grid=(ne,),
        in_specs=[
            pl.BlockSpec((1, d, hid), wsel),
            pl.BlockSpec((1, d, hid), wsel),
            pl.BlockSpec((1, hid, d), wsel),
            pl.BlockSpec(memory_space=pl.ANY),
        ],
        out_specs=pl.BlockSpec(memory_space=pl.ANY),
        scratch_shapes=[
            pltpu.VMEM((RING * rows * PACK_S, LANES), U32),
            pltpu.VMEM((RING * rows * PACK_S, LANES), U32),
            pltpu.SemaphoreType.DMA((RING,)),
            pltpu.SemaphoreType.DMA((RING,)),
            pltpu.VMEM((d, hid), BF16),
            pltpu.VMEM((d, hid), BF16),
            pltpu.VMEM((hid, d), BF16),
        ],
    )
    return pl.pallas_call(
        _expert_kernel,
        grid_spec=grid_spec,
        out_shape=jax.ShapeDtypeStruct((nrows * PACK_S, LANES), U32),
        compiler_params=pltpu.CompilerParams(
            dimension_semantics=("arbitrary",), vmem_limit_bytes=VMEM_LIMIT),
        name="experts",
    )(first, nchunk, chunk_row, chunk_len, info, w_gate, w_up, w_down,
      xs.reshape(-1, LANES)).reshape(nrows, PACK_S, LANES)


def _combine_kernel(tab_ref, tabn_ref, lp_ref, gw_ref, xs1_ref, mod_ref, fg_ref, y_hbm, o_ref, yl0, yl1, sem):
    s = pl.program_id(0)
    ns = pl.num_programs(0)

    @pl.when(s == 0)
    def _():
        _segment_copies(tab_ref, 0, yl0, y_hbm, sem.at[0], to_global=False)

    g2 = mod_ref[0][5:6]
    for half, yl in ((0, yl0), (1, yl1)):
        _segment_wait(yl, y_hbm, sem.at[half], tab_ref[0, _T_EMPTY, half])
        if half == 0:
            _segment_copies(tab_ref, 1, yl1, y_hbm, sem.at[1], to_global=False)
        lp = lp_ref[:, half * TILE:(half + 1) * TILE]
        gw = gw_ref[:, half * TILE:(half + 1) * TILE]
        acc = jnp.zeros((TILE, D_MODEL), F32)
        gwb = [gw[k:k + 1, :].astype(BF16) for k in range(TOP_K)]
        j = _chunk_row_ids()
        for c in range(LROWS // SORT_CHUNK):
            wt = jnp.zeros((SORT_CHUNK, TILE), BF16)
            for k in range(TOP_K):
                wt = jnp.where(j == _chunk_relative(lp[k:k + 1, :], c), gwb[k], wt)
            yv = _unpack_rows(_load_packed(yl, c * SORT_CHUNK, SORT_CHUNK))
            acc = acc + _dot_tn(wt, yv)
        if half == 0:
            _segment_copies(tabn_ref, 0, yl0, y_hbm, sem.at[0], to_global=False)
        x2 = xs1_ref[half * TILE:(half + 1) * TILE, :] + g2 * acc
        o_ref[half * TILE:(half + 1) * TILE, :] = _rms(x2) * fg_ref[...]

    @pl.when(s == ns - 1)
    def _():
        _segment_wait(yl0, y_hbm, sem.at[0], tab_ref[0, _T_EMPTY, 0])


def _combine_call(tab, lp, gw, xs1, mod3, final_g, y, seq):
    t, d = xs1.shape
    ns = t // (2 * TILE)
    per_seq = seq // (2 * TILE)
    return pl.pallas_call(
        _combine_kernel,
        grid=(ns,),
        in_specs=[
            pl.BlockSpec((1, _T_ROWS, N_EXPERTS), lambda s: (s, 0, 0), memory_space=pltpu.SMEM),
            pl.BlockSpec((1, _T_ROWS, N_EXPERTS), lambda s: (jnp.minimum(s + 1, ns - 1), 0, 0),
                         memory_space=pltpu.SMEM),
            pl.BlockSpec((TOP_K, 2 * TILE), lambda s: (0, s)),
            pl.BlockSpec((TOP_K, 2 * TILE), lambda s: (0, s)),
            pl.BlockSpec((2 * TILE, d), lambda s: (s, 0)),
            pl.BlockSpec((1, 6, d), lambda s: (s // per_seq, 0, 0)),
            pl.BlockSpec((1, d), lambda s: (0, 0)),
            pl.BlockSpec(memory_space=pl.ANY),
        ],
        out_specs=pl.BlockSpec((2 * TILE, d), lambda s: (s, 0)),
        out_shape=jax.ShapeDtypeStruct((t, d), F32),
        scratch_shapes=[
            pltpu.VMEM((LBUF_ROWS, PACK_S, LANES), U32),
            pltpu.VMEM((LBUF_ROWS, PACK_S, LANES), U32),
            pltpu.SemaphoreType.DMA((2,)),
        ],
        compiler_params=pltpu.CompilerParams(
            dimension_semantics=("arbitrary",), vmem_limit_bytes=VMEM_LIMIT),
        name="combine",
    )(tab, tab, lp, gw, xs1, mod3, final_g.reshape(1, d), y)


def kernel(x, c, w_ada, b_ada, w_in, attn_sinks, rel_bias, attn_norm_g, conv_w, conv_b, conv_ln_g,
           conv_ln_b, w_out, w_router, router_bias, w_exp_gate, w_exp_up, w_exp_down, w_sh_gate,
           w_sh_up, w_sh_down, final_norm_g):
    bsz, seq, d = x.shape
    t = bsz * seq
    assert w_ada.shape[0] == 1 and d == D_MODEL
    assert seq % MIX_ROWS == 0 and seq % (2 * TILE) == 0

    mod3 = _ada_call(c, w_ada[0], b_ada[0]).reshape(bsz, 6, d)
    xs1, h2, lgt = _mixer_call(
        x, mod3, w_in[0].astype(BF16), attn_sinks[0], rel_bias,
        attn_norm_g[0].reshape(1, Q_COLS), conv_w[0], conv_b[0].reshape(1, CONV_CH),
        conv_ln_g[0].reshape(1, CONV_CH), conv_ln_b[0].reshape(1, CONV_CH),
        w_out[0].astype(BF16), w_router[0].T,
        w_sh_gate[0].astype(BF16), w_sh_up[0].astype(BF16), w_sh_down[0].astype(BF16))
    xs1 = xs1.reshape(t, d)
    h2 = h2.reshape(t, d)

    lp, gw, cnt = _route_call(lgt, router_bias[0])

    ntiles = t // TILE
    nassign = t * TOP_K
    n = cnt[:, :ntiles].T.astype(I32)
    local = jnp.cumsum(n, axis=1) - n
    earlier = jnp.cumsum(n, axis=0) - n
    total = jnp.sum(n, axis=0)
    region = (total + 1) // 2 * 2
    starts = jnp.cumsum(region) - region
    nalloc = nassign + N_EXPERTS
    used = jnp.sum(region)
    pad = jnp.stack([starts + total, region - total,
                     jnp.broadcast_to(used, (N_EXPERTS,))], axis=0)
    glob = starts[None, :] + earlier
    empty = n == 0
    length = jnp.maximum(n, 1)
    eid = jnp.arange(N_EXPERTS, dtype=I32)[None, :]
    parity = (jnp.arange(ntiles, dtype=I32) % 2)[:, None]
    spare_global = nalloc + EXPERT_ROWS + parity * N_EXPERTS + eid
    nempty = jnp.sum(empty.astype(I32), axis=1).reshape(ntiles // 2, 2)
    misc = jnp.pad(nempty, ((0, 0), (0, N_EXPERTS - 2)))[:, None, :]

    def table(loc, glo):
        per_tile = jnp.stack([loc, length, glo], axis=1).reshape(ntiles // 2, 2 * _T_PER_TILE, N_EXPERTS)
        fill = jnp.zeros((ntiles // 2, _T_ROWS - 2 * _T_PER_TILE - 1, N_EXPERTS), I32)
        return jnp.concatenate([per_tile, misc, fill], axis=1)

    tab_out = table(jnp.where(empty, 0, local), jnp.where(empty, spare_global, glob))
    tab_back = table(jnp.where(empty, LROWS + eid, local), jnp.where(empty, 0, glob))

    rows = EXPERT_ROWS
    nchunk = (total + rows - 1) // rows
    chunk_end = jnp.cumsum(nchunk)
    first = chunk_end - nchunk
    max_chunks = nassign // rows + N_EXPERTS
    gid = jnp.arange(max_chunks, dtype=I32)
    owner = jnp.minimum(jnp.sum((chunk_end[None, :] <= gid[:, None]).astype(I32), axis=1), N_EXPERTS - 1)
    pick = owner[:, None] == eid
    within = gid - jnp.sum(jnp.where(pick, first[None, :], 0), axis=1)
    chunk_row = jnp.sum(jnp.where(pick, starts[None, :], 0), axis=1) + within * rows
    left = jnp.sum(jnp.where(pick, region[None, :], 0), axis=1) - within * rows
    live = gid < chunk_end[-1]
    chunk_row = jnp.where(live, chunk_row, 0)
    chunk_len = jnp.where(live, jnp.clip(left, 2, rows), 2)
    info = jnp.stack([chunk_end[-1], used])

    xs = _dispatch_call(tab_out, pad, h2, lp, nalloc + TAIL_ROWS)
    y = _expert_call(first, nchunk, chunk_row, chunk_len, info, xs,
                     w_exp_gate[0], w_exp_up[0], w_exp_down[0], nalloc)
    out = _combine_call(tab_back, lp, gw, xs1, mod3, final_norm_g, y, seq)
    return out.reshape(bsz, seq, d)
```

```python
import numpy as np
import jax
import jax.numpy as jnp
from jax import lax
from jax.experimental import pallas as pl
from jax.experimental.pallas import tpu as pltpu

F32 = jnp.float32
BF16 = jnp.bfloat16
U32 = jnp.uint32
I32 = jnp.int32

D_MODEL = 1024
HEAD_DIM = 64
N_HEADS = 8
N_KV_HEADS = 2
GROUP = N_HEADS // N_KV_HEADS
Q_COLS = N_HEADS * HEAD_DIM
KV_COLS = N_KV_HEADS * HEAD_DIM
ATTN_BLOCK = 128
WINDOW = 128
NUM_BUCKETS = 32
MAX_DISTANCE = 128
CONV_CH = D_MODEL - Q_COLS
CONV_WIDTH = 31
IN_COLS = Q_COLS + 2 * KV_COLS + 2 * CONV_CH
N_EXPERTS = 256
TOP_K = 8
N_GROUPS = 8
GROUP_SIZE = N_EXPERTS // N_GROUPS
TOPK_GROUPS = 4
EXPERT_HIDDEN = 256
ROUTED_SCALE = 2.5
EPS = 1e-6

MIX_ROWS = 256
HALO = 32
TILE = 256
LROWS = TILE * TOP_K
SORT_CHUNK = 256
EXPERT_ROWS = 512
LANES = 128
SUBLANES = 8
PACK_W = D_MODEL // 2
PACK_S = PACK_W // LANES
VMEM_LIMIT = 56 * 1024 * 1024


def _sigmoid(v):
    return 1.0 / (1.0 + jnp.exp(-v))


def _rms(v):
    return v * lax.rsqrt(jnp.mean(v * v, axis=-1, keepdims=True) + EPS)


def _split_bf16(a):
    hi = a.astype(BF16)
    lo = (a - hi.astype(F32)).astype(BF16)
    return hi, lo


def _dot(a, b):
    return jnp.dot(a, b, preferred_element_type=F32)


def _dot_nt(a, b):
    return lax.dot_general(a, b, (((1,), (1,)), ((), ())), preferred_element_type=F32)


def _dot_tn(a, b):
    return lax.dot_general(a, b, (((0,), (0,)), ((), ())), preferred_element_type=F32)


def _dot3(a, b, dot):
    ah, al = _split_bf16(a)
    bh, bl = _split_bf16(b)
    return dot(ah, bh) + (dot(ah, bl) + dot(al, bh))


def _pack_rows(v):
    hi = lax.bitcast_convert_type(v[:, :PACK_W], U32) & jnp.uint32(0xFFFF0000)
    lo = lax.bitcast_convert_type(v[:, PACK_W:], U32) >> 16
    return hi | lo


def _unpack_rows(u):
    hi = lax.bitcast_convert_type(u & jnp.uint32(0xFFFF0000), F32)
    lo = lax.bitcast_convert_type(u << 16, F32)
    return jnp.concatenate([hi, lo], axis=1).astype(BF16)


def _load_flat(flat, r0, n):
    return jnp.concatenate(
        [flat[pl.ds(r0 * PACK_S + c, n, stride=PACK_S), :] for c in range(PACK_S)], axis=1)


def _store_flat(flat, r0, n, u):
    for c in range(PACK_S):
        flat[pl.ds(r0 * PACK_S + c, n, stride=PACK_S), :] = u[:, c * LANES:(c + 1) * LANES]


def _load_packed(ref3, r0, n):
    return _load_flat(ref3.reshape(ref3.shape[0] * PACK_S, LANES), r0, n)


def _store_packed(ref3, r0, n, u):
    _store_flat(ref3.reshape(ref3.shape[0] * PACK_S, LANES), r0, n, u)


def _ada_kernel(c_ref, w_ref, b_ref, o_ref):
    c = c_ref[...]
    s = c * _sigmoid(c)
    o_ref[...] = _dot3(s, w_ref[...], _dot) + b_ref[...]


def _ada_call(c, w_ada, b_ada):
    bsz, d = c.shape
    n = w_ada.shape[1]
    tn = 1536
    return pl.pallas_call(
        _ada_kernel,
        grid=(n // tn,),
        in_specs=[
            pl.BlockSpec((bsz, d), lambda i: (0, 0)),
            pl.BlockSpec((d, tn), lambda i: (0, i)),
            pl.BlockSpec((1, tn), lambda i: (0, i)),
        ],
        out_specs=pl.BlockSpec((bsz, tn), lambda i: (0, i)),
        out_shape=jax.ShapeDtypeStruct((bsz, n), F32),
        compiler_params=pltpu.CompilerParams(
            dimension_semantics=("arbitrary",), vmem_limit_bytes=VMEM_LIMIT),
        name="ada",
    )(c, w_ada, b_ada.reshape(1, n))


def _bucket_table():
    qi = np.arange(ATTN_BLOCK)[:, None]
    ki = np.arange(2 * ATTN_BLOCK)[None, :]
    dist = qi + ATTN_BLOCK - ki
    n = np.maximum(dist, 0)
    max_exact = NUM_BUCKETS // 2
    large = max_exact + (np.log(np.maximum(n, 1) / max_exact) / np.log(MAX_DISTANCE / max_exact)
                         * (NUM_BUCKETS - max_exact)).astype(np.int32)
    large = np.minimum(large, NUM_BUCKETS - 1)
    bkt = np.where(n < max_exact, n, large).astype(np.int32)
    band = (dist >= 0) & (dist < WINDOW)
    return np.ascontiguousarray(np.where(band, bkt, -1).astype(np.int32).T)


def _mixer_kernel(x_ref, mod_ref, win_ref, sinks_ref, relb_ref, bkt_ref, ang_ref, cw_ref, cb_ref,
                  lng_ref, lnb_ref, wout_ref, wrt_ref, wsg_ref, wsu_ref, wsd_ref,
                  xs1_ref, h2_ref, lgt_ref,
                  kv_ref, uext_ref, bias_ref):
    tm = x_ref.shape[1]
    nblk = tm // ATTN_BLOCK
    j = pl.program_id(1)

    @pl.when((pl.program_id(0) == 0) & (j == 0))
    def _build_bias():
        bkt = bkt_ref[...]
        for h in range(N_HEADS):
            acc = jnp.full(bkt.shape, -jnp.inf, F32)
            for b in range(NUM_BUCKETS):
                acc = jnp.where(bkt == b, relb_ref[b, h], acc)
            g = h % GROUP
            bias_ref[h // GROUP, :, g * ATTN_BLOCK:(g + 1) * ATTN_BLOCK] = acc

    @pl.when(j == 0)
    def _reset_history():
        kv_ref[...] = jnp.zeros(kv_ref.shape, F32)
        uext_ref[0:HALO, :] = jnp.zeros((HALO, CONV_CH), F32)
        uext_ref[HALO + tm:HALO + tm + SUBLANES, :] = jnp.zeros((SUBLANES, CONV_CH), F32)

    x = x_ref[0]
    mod = mod_ref[0]
    sh1, sc1, g1 = mod[0:1], mod[1:2], mod[2:3]
    sh2, sc2, g2 = mod[3:4], mod[4:5], mod[5:6]

    h = _rms(x) * (1.0 + sc1) + sh1
    proj = _dot(h.astype(BF16), win_ref[...])

    qb = (proj[:, :Q_COLS] * (HEAD_DIM ** -0.5)).astype(BF16)
    kv_cur = proj[:, Q_COLS:Q_COLS + 2 * KV_COLS]
    kvb = jnp.concatenate([kv_ref[...], kv_cur], axis=0).astype(BF16)
    kv_ref[...] = kv_cur[tm - ATTN_BLOCK:, :]
    not_first = j > 0
    key = lax.broadcasted_iota(jnp.int32, (2 * ATTN_BLOCK, GROUP * ATTN_BLOCK), 0)
    attn_rows = []
    for i in range(nblk):
        r0, r1, r2 = i * ATTN_BLOCK, (i + 1) * ATTN_BLOCK, (i + 2) * ATTN_BLOCK
        heads_t = [None] * N_HEADS
        for kh in range(N_KV_HEADS):
            kpc = kvb[r0:r2, kh * HEAD_DIM:(kh + 1) * HEAD_DIM]
            vpc = kvb[r0:r2, KV_COLS + kh * HEAD_DIM:KV_COLS + (kh + 1) * HEAD_DIM]
            qg = jnp.concatenate(
                [qb[r0:r1, (kh * GROUP + g) * HEAD_DIM:(kh * GROUP + g + 1) * HEAD_DIM]
                 for g in range(GROUP)], axis=0)
            logit = _dot_nt(kpc, qg) + bias_ref[kh]
            if i == 0:
                logit = jnp.where((key >= ATTN_BLOCK) | not_first, logit, -jnp.inf)
            sink = jnp.concatenate(
                [jnp.full((1, ATTN_BLOCK), sinks_ref[kh * GROUP + g], F32) for g in range(GROUP)], axis=1)
            m = jnp.maximum(jnp.max(logit, axis=0, keepdims=True), sink)
            p = jnp.exp(logit - m)
            den = jnp.sum(p, axis=0, keepdims=True) + jnp.exp(sink - m)
            o_t = _dot_tn(vpc, p.astype(BF16)) / den
            for g in range(GROUP):
                heads_t[kh * GROUP + g] = o_t[:, g * ATTN_BLOCK:(g + 1) * ATTN_BLOCK]
        attn_t = jnp.concatenate(heads_t, axis=0)
        scale = lax.rsqrt(jnp.mean(attn_t * attn_t, axis=0, keepdims=True) + EPS)
        attn_rows.append((attn_t * scale).T)
    attn = jnp.concatenate(attn_rows, axis=0) * ang_ref[...]

    a = proj[:, Q_COLS + 2 * KV_COLS:Q_COLS + 2 * KV_COLS + CONV_CH]
    gt = proj[:, Q_COLS + 2 * KV_COLS + CONV_CH:]
    uext_ref[HALO:HALO + tm, :] = a * _sigmoid(gt)
    cw = cw_ref[...]
    base = HALO - (CONV_WIDTH - 1)
    acc = None
    for res in range(SUBLANES):
        part = None
        for hi in range((HALO + SUBLANES - 1) // SUBLANES + 1):
            t = hi * SUBLANES + res - base
            if 0 <= t < CONV_WIDTH:
                term = cw[t:t + 1, :] * uext_ref[hi * SUBLANES:hi * SUBLANES + tm + SUBLANES, :]
                part = term if part is None else part + term
        part = part[res:res + tm, :]
        acc = part if acc is None else acc + part
    uext_ref[0:HALO, :] = uext_ref[tm:tm + HALO, :]
    cv = acc + cb_ref[...]
    mu = jnp.mean(cv, axis=-1, keepdims=True)
    var = jnp.mean(jnp.square(cv - mu), axis=-1, keepdims=True)
    cv = (cv - mu) * lax.rsqrt(var + EPS) * lng_ref[...] + lnb_ref[...]
    cv = cv * _sigmoid(cv)

    mixed = (_dot(attn.astype(BF16), wout_ref[0:Q_COLS, :])
             + _dot(cv.astype(BF16), wout_ref[Q_COLS:, :]))
    x1 = x + g1 * mixed

    h2 = _rms(x1) * (1.0 + sc2) + sh2
    lgt_ref[...] = _dot3(wrt_ref[...], h2, _dot_nt)
    h2b = h2.astype(BF16)
    h2_ref[0] = h2b
    sg = _dot(h2b, wsg_ref[...])
    su = _dot(h2b, wsu_ref[...])
    shared = _dot(((sg * _sigmoid(sg)) * su).astype(BF16), wsd_ref[...])
    xs1_ref[0] = x1 + g2 * shared


def _mixer_call(x, mod3, w_in_b, sinks, rel_bias, ang, cw, cb, lng, lnb, w_out_b, w_rt, wsg_b, wsu_b, wsd_b):
    bsz, seq, d = x.shape
    tm = MIX_ROWS
    nj = seq // tm
    bkt = jnp.asarray(_bucket_table())
    full = lambda shape: pl.BlockSpec(shape, lambda b, j: (0,) * len(shape))
    smem = pl.BlockSpec(memory_space=pltpu.SMEM)
    return pl.pallas_call(
        _mixer_kernel,
        grid=(bsz, nj),
        in_specs=[
            pl.BlockSpec((1, tm, d), lambda b, j: (b, j, 0)),
            pl.BlockSpec((1, 6, d), lambda b, j: (b, 0, 0)),
            full((d, IN_COLS)),
            smem, smem,
            full((2 * ATTN_BLOCK, ATTN_BLOCK)),
            full((1, Q_COLS)),
            full((CONV_WIDTH, CONV_CH)),
            full((1, CONV_CH)), full((1, CONV_CH)), full((1, CONV_CH)),
            full((d, d)),
            full((N_EXPERTS, d)),
            full((d, EXPERT_HIDDEN)), full((d, EXPERT_HIDDEN)), full((EXPERT_HIDDEN, d)),
        ],
        out_specs=[
            pl.BlockSpec((1, tm, d), lambda b, j: (b, j, 0)),
            pl.BlockSpec((1, tm, d), lambda b, j: (b, j, 0)),
            pl.BlockSpec((N_EXPERTS, tm), lambda b, j: (0, b * nj + j)),
        ],
        out_shape=[
            jax.ShapeDtypeStruct((bsz, seq, d), F32),
            jax.ShapeDtypeStruct((bsz, seq, d), BF16),
            jax.ShapeDtypeStruct((N_EXPERTS, bsz * seq), F32),
        ],
        scratch_shapes=[
            pltpu.VMEM((ATTN_BLOCK, 2 * KV_COLS), F32),
            pltpu.VMEM((HALO + tm + SUBLANES, CONV_CH), F32),
            pltpu.VMEM((N_KV_HEADS, 2 * ATTN_BLOCK, GROUP * ATTN_BLOCK), F32),
        ],
        compiler_params=pltpu.CompilerParams(
            dimension_semantics=("arbitrary", "arbitrary"), vmem_limit_bytes=VMEM_LIMIT),
        name="mixer",
    )(x, mod3, w_in_b, sinks, rel_bias, bkt, ang, cw, cb, lng, lnb, w_out_b, w_rt, wsg_b, wsu_b, wsd_b)


def _route_kernel(lgt_ref, rb_ref, lp_ref, gw_ref, cnt_ref):
    tr = lgt_ref.shape[1]
    i = pl.program_id(0)

    @pl.when(i == 0)
    def _():
        cnt_ref[...] = jnp.zeros(cnt_ref.shape, F32)

    scores = _sigmoid(lgt_ref[...])
    sel = scores + rb_ref[...]
    neg = -jnp.inf

    sel3 = sel.reshape(N_GROUPS, GROUP_SIZE, tr)
    loc = lax.broadcasted_iota(I32, sel3.shape, 1)
    m1 = jnp.max(sel3, axis=1, keepdims=True)
    i1 = jnp.min(jnp.where(sel3 == m1, loc, GROUP_SIZE), axis=1, keepdims=True)
    m2 = jnp.max(jnp.where(loc == i1, neg, sel3), axis=1, keepdims=True)
    gscore = (m1 + m2).reshape(N_GROUPS, tr)

    gio = lax.broadcasted_iota(I32, gscore.shape, 0)
    gmask = jnp.zeros(gscore.shape, jnp.bool_)
    cur = gscore
    for _ in range(TOPK_GROUPS):
        m = jnp.max(cur, axis=0, keepdims=True)
        idx = jnp.min(jnp.where(cur == m, gio, N_GROUPS), axis=0, keepdims=True)
        pick = gio == idx
        gmask = gmask | pick
        cur = jnp.where(pick, neg, cur)
    emask = jnp.broadcast_to(gmask.reshape(N_GROUPS, 1, tr), sel3.shape).reshape(N_EXPERTS, tr)

    rio = lax.broadcasted_iota(I32, sel.shape, 0)
    cur = jnp.where(emask, sel, neg)
    picks, gsc = [], []
    for _ in range(TOP_K):
        m = jnp.max(cur, axis=0, keepdims=True)
        idx = jnp.min(jnp.where(cur == m, rio, N_EXPERTS), axis=0, keepdims=True)
        pick = rio == idx
        picks.append(pick)
        gsc.append(jnp.sum(jnp.where(pick, scores, 0.0), axis=0, keepdims=True))
        cur = jnp.where(pick, neg, cur)
    gsum = gsc[0]
    for k in range(1, TOP_K):
        gsum = gsum + gsc[k]
    gw_ref[...] = jnp.concatenate([g / gsum * ROUTED_SCALE for g in gsc], axis=0)

    chosen = picks[0]
    for k in range(1, TOP_K):
        chosen = chosen | picks[k]
    onehot = jnp.where(chosen, 1.0, 0.0)
    tri = (lax.broadcasted_iota(I32, (tr, tr), 0) < lax.broadcasted_iota(I32, (tr, tr), 1))
    before = _dot(onehot.astype(BF16), jnp.where(tri, 1.0, 0.0).astype(BF16))
    n = jnp.sum(onehot, axis=1, keepdims=True)
    low = (lax.broadcasted_iota(I32, (N_EXPERTS, N_EXPERTS), 1)
           < lax.broadcasted_iota(I32, (N_EXPERTS, N_EXPERTS), 0))
    start = _dot(jnp.where(low, 1.0, 0.0).astype(BF16),
                 jnp.broadcast_to(n, (N_EXPERTS, LANES)).astype(BF16))[:, 0:1]
    pos = start + before
    lp_ref[...] = jnp.concatenate(
        [jnp.sum(jnp.where(p, pos, 0.0), axis=0, keepdims=True) for p in picks], axis=0).astype(I32)

    lane = lax.broadcasted_iota(I32, cnt_ref.shape, 1)
    cnt_ref[...] += jnp.where(lane == i, n, 0.0)


def _route_call(lgt, router_bias):
    e, t = lgt.shape
    ntiles = t // TILE
    ntp = (ntiles + LANES - 1) // LANES * LANES
    return pl.pallas_call(
        _route_kernel,
        grid=(ntiles,),
        in_specs=[
            pl.BlockSpec((e, TILE), lambda i: (0, i)),
            pl.BlockSpec((e, 1), lambda i: (0, 0)),
        ],
        out_specs=[
            pl.BlockSpec((TOP_K, TILE), lambda i: (0, i)),
            pl.BlockSpec((TOP_K, TILE), lambda i: (0, i)),
            pl.BlockSpec((e, ntp), lambda i: (0, 0)),
        ],
        out_shape=[
            jax.ShapeDtypeStruct((TOP_K, t), I32),
            jax.ShapeDtypeStruct((TOP_K, t), F32),
            jax.ShapeDtypeStruct((e, ntp), F32),
        ],
        compiler_params=pltpu.CompilerParams(
            dimension_semantics=("arbitrary",), vmem_limit_bytes=VMEM_LIMIT),
        name="route",
    )(lgt, router_bias.reshape(e, 1))


_T_LOCAL, _T_LEN, _T_GLOBAL, _T_FIELDS = 0, 1, 2, 3
_T_EMPTY = 2 * N_EXPERTS * _T_FIELDS
_T_SIZE = _T_EMPTY + 2
LBUF_ROWS = LROWS + N_EXPERTS


def _segment_copies(tab_ref, half, local_ref, global_hbm, sem, to_global):
    def body(e, carry):
        at = (half * N_EXPERTS + e) * _T_FIELDS
        n = tab_ref[0, 0, at + _T_LEN]
        loc = local_ref.at[pl.ds(tab_ref[0, 0, at + _T_LOCAL], n)]
        glo = global_hbm.at[pl.ds(tab_ref[0, 0, at + _T_GLOBAL], n)]
        if to_global:
            pltpu.make_async_copy(loc, glo, sem).start()
        else:
            pltpu.make_async_copy(glo, loc, sem).start()
        return carry

    lax.fori_loop(0, N_EXPERTS, body, 0, unroll=8)


def _chunk_relative(lp_row, c):
    rel = lp_row - c * SORT_CHUNK
    inside = (rel >= 0) & (rel < SORT_CHUNK)
    return jnp.where(inside, rel, -1).astype(F32).astype(BF16)


def _chunk_row_ids():
    return lax.broadcasted_iota(I32, (SORT_CHUNK, TILE), 0).astype(F32).astype(BF16)


def _segment_wait(local_ref, global_hbm, sem, nempty):
    rows = LROWS + nempty
    pltpu.make_async_copy(global_hbm.at[pl.ds(0, rows)], local_ref.at[pl.ds(0, rows)], sem).wait()


TAIL_ROWS = EXPERT_ROWS + 2 * N_EXPERTS


def _dispatch_kernel(tab_ref, pad_ref, h2_ref, lp_ref, xs_hbm, xl0, xl1, zb, sem, pending):
    s = pl.program_id(0)
    ns = pl.num_programs(0)

    @pl.when(s == 0)
    def _zero_unassigned_rows():
        zb[...] = jnp.zeros(zb.shape, U32)
        used = pad_ref[2, 0]
        rest = xs_hbm.shape[0] - used
        cp = pltpu.make_async_copy(zb.at[pl.ds(0, rest)], xs_hbm.at[pl.ds(used, rest)], sem.at[2])
        cp.start()
        cp.wait()

        def pad_row(e):
            return pltpu.make_async_copy(zb.at[pl.ds(0, 1)], xs_hbm.at[pl.ds(pad_ref[0, e], 1)], sem.at[2])

        def start(e, carry):
            @pl.when(pad_ref[1, e] > 0)
            def _():
                pad_row(e).start()
            return carry

        def wait(e, carry):
            @pl.when(pad_ref[1, e] > 0)
            def _():
                pad_row(e).wait()
            return carry

        lax.fori_loop(0, N_EXPERTS, start, 0)
        lax.fori_loop(0, N_EXPERTS, wait, 0)

    @pl.when(s > 0)
    def _():
        _segment_wait(xl0, xs_hbm, sem.at[0], pending[0])
        _segment_wait(xl1, xs_hbm, sem.at[1], pending[1])

    for half, xl in ((0, xl0), (1, xl1)):
        rows = h2_ref[half * TILE:(half + 1) * TILE, :]
        lp = lp_ref[:, half * TILE:(half + 1) * TILE]
        j = _chunk_row_ids()
        for c in range(LROWS // SORT_CHUNK):
            onehot = jnp.zeros((SORT_CHUNK, TILE), BF16)
            for k in range(TOP_K):
                onehot = jnp.where(j == _chunk_relative(lp[k:k + 1, :], c), jnp.ones((), BF16), onehot)
            _store_packed(xl, c * SORT_CHUNK, SORT_CHUNK, _pack_rows(_dot(onehot, rows)))
        _segment_copies(tab_ref, half, xl, xs_hbm, sem.at[half], to_global=True)
        pending[half] = tab_ref[0, 0, _T_EMPTY + half]

    @pl.when(s == ns - 1)
    def _():
        _segment_wait(xl0, xs_hbm, sem.at[0], pending[0])
        _segment_wait(xl1, xs_hbm, sem.at[1], pending[1])


def _dispatch_call(tab, pad, h2, lp, nrows):
    t, d = h2.shape
    ns = t // (2 * TILE)
    return pl.pallas_call(
        _dispatch_kernel,
        grid=(ns,),
        in_specs=[
            pl.BlockSpec((1, 1, _T_SIZE), lambda s: (s, 0, 0), memory_space=pltpu.SMEM),
            pl.BlockSpec(memory_space=pltpu.SMEM),
            pl.BlockSpec((2 * TILE, d), lambda s: (s, 0)),
            pl.BlockSpec((TOP_K, 2 * TILE), lambda s: (0, s)),
        ],
        out_specs=pl.BlockSpec(memory_space=pl.ANY),
        out_shape=jax.ShapeDtypeStruct((nrows, PACK_S, LANES), U32),
        scratch_shapes=[
            pltpu.VMEM((LBUF_ROWS, PACK_S, LANES), U32),
            pltpu.VMEM((LBUF_ROWS, PACK_S, LANES), U32),
            pltpu.VMEM((TAIL_ROWS + N_EXPERTS, PACK_S, LANES), U32),
            pltpu.SemaphoreType.DMA((3,)),
            pltpu.SMEM((2,), I32),
        ],
        compiler_params=pltpu.CompilerParams(
            dimension_semantics=("arbitrary",), vmem_limit_bytes=VMEM_LIMIT),
        name="dispatch",
    )(tab, pad, h2, lp)


RING = 4


def _expert_kernel(first_ref, nchunk_ref, row_ref, len_ref, info_ref, wg_ref, wu_ref, wd_ref, xs_hbm, y_hbm,
                   xbuf, ybuf, isem, osem, wgb, wub, wdb):
    e = pl.program_id(0)
    last_step = e == pl.num_programs(0) - 1
    rows = EXPERT_ROWS
    total_chunks = info_ref[0]

    def slab(ref, row0, m):
        size = m * PACK_S if isinstance(m, int) else pl.multiple_of(m * PACK_S, SUBLANES)
        first = row0 * PACK_S if isinstance(row0, int) else pl.multiple_of(row0 * PACK_S, SUBLANES)
        return ref.at[pl.ds(first, size)]

    def fetch(g):
        slot = g % RING
        return pltpu.make_async_copy(slab(xs_hbm, row_ref[g], rows), slab(xbuf, slot * rows, rows), isem.at[slot])

    def flush(g):
        slot = g % RING
        m = len_ref[g]
        return pltpu.make_async_copy(slab(ybuf, slot * rows, m), slab(y_hbm, row_ref[g], m), osem.at[slot])

    @pl.when(e == 0)
    def _prime():
        for a in range(RING - 1):
            @pl.when(a < total_chunks)
            def _():
                fetch(a).start()

    wgb[...] = wg_ref[0].astype(BF16)
    wub[...] = wu_ref[0].astype(BF16)
    wdb[...] = wd_ref[0].astype(BF16)

    def chunk(g, carry):
        slot = g % RING

        @pl.when(g + RING - 1 < total_chunks)
        def _():
            fetch(g + RING - 1).start()

        fetch(g).wait()

        @pl.when(g >= RING)
        def _():
            flush(g - RING).wait()

        xb = _unpack_rows(_load_flat(xbuf, slot * rows, rows))
        gate = _dot(xb, wgb[...])
        up = _dot(xb, wub[...])
        hid = (gate * _sigmoid(gate)) * up
        y = _dot(hid.astype(BF16), wdb[...])
        _store_flat(ybuf, slot * rows, rows, _pack_rows(y.astype(BF16).astype(F32)))
        flush(g).start()
        return carry

    lax.fori_loop(first_ref[e], first_ref[e] + nchunk_ref[e], chunk, 0)

    @pl.when(last_step)
    def _drain():
        def wait(g, carry):
            flush(g).wait()
            return carry
        lax.fori_loop(jnp.maximum(total_chunks - RING, 0), total_chunks, wait, 0)

    @pl.when(last_step)
    def _define_unused_rows():
        used = info_ref[1]
        rest = y_hbm.shape[0] // PACK_S - used

        @pl.when(rest > 0)
        def _():
            ybuf[0:N_EXPERTS * PACK_S, :] = jnp.zeros((N_EXPERTS * PACK_S, LANES), U32)
            cp = pltpu.make_async_copy(slab(ybuf, 0, rest), slab(y_hbm, used, rest), osem.at[0])
            cp.start()
            cp.wait()


def _expert_call(first, nchunk, chunk_row, chunk_len, info, xs, w_gate, w_up, w_down, nrows):
    ne, d, hid = w_gate.shape
    rows = EXPERT_ROWS
    wsel = lambda e, *_: (e, 0, 0)
    grid_spec = pltpu.PrefetchScalarGridSpec(
        num_scalar_prefetch=5,
        grid=(ne,),
        in_specs=[
            pl.BlockSpec((1, d, hid), wsel),
            pl.BlockSpec((1, d, hid), wsel),
            pl.BlockSpec((1, hid, d), wsel),
            pl.BlockSpec(memory_space=pl.ANY),
        ],
        out_specs=pl.BlockSpec(memory_space=pl.ANY),
        scratch_shapes=[
            pltpu.VMEM((RING * rows * PACK_S, LANES), U32),
            pltpu.VMEM((RING * rows * PACK_S, LANES), U32),
            pltpu.SemaphoreType.DMA((RING,)),
            pltpu.SemaphoreType.DMA((RING,)),
            pltpu.VMEM((d, hid), BF16),
            pltpu.VMEM((d, hid), BF16),
            pltpu.VMEM((hid, d), BF16),
        ],
    )
    return pl.pallas_call(
        _expert_kernel,
        grid_spec=grid_spec,
        out_shape=jax.ShapeDtypeStruct((nrows * PACK_S, LANES), U32),
        compiler_params=pltpu.CompilerParams(
            dimension_semantics=("arbitrary",), vmem_limit_bytes=VMEM_LIMIT),
        name="experts",
    )(first, nchunk, chunk_row, chunk_len, info, w_gate, w_up, w_down,
      xs.reshape(-1, LANES)).reshape(nrows, PACK_S, LANES)


def _combine_kernel(tab_ref, tabn_ref, lp_ref, gw_ref, xs1_ref, mod_ref, fg_ref, y_hbm, o_ref, yl0, yl1, sem):
    s = pl.program_id(0)
    ns = pl.num_programs(0)

    @pl.when(s == 0)
    def _():
        _segment_copies(tab_ref, 0, yl0, y_hbm, sem.at[0], to_global=False)

    g2 = mod_ref[0][5:6]
    for half, yl in ((0, yl0), (1, yl1)):
        _segment_wait(yl, y_hbm, sem.at[half], tab_ref[0, 0, _T_EMPTY + half])
        if half == 0:
            _segment_copies(tab_ref, 1, yl1, y_hbm, sem.at[1], to_global=False)
        lp = lp_ref[:, half * TILE:(half + 1) * TILE]
        gw = gw_ref[:, half * TILE:(half + 1) * TILE]
        acc = jnp.zeros((TILE, D_MODEL), F32)
        gwb = [gw[k:k + 1, :].astype(BF16) for k in range(TOP_K)]
        j = _chunk_row_ids()
        for c in range(LROWS // SORT_CHUNK):
            wt = jnp.zeros((SORT_CHUNK, TILE), BF16)
            for k in range(TOP_K):
                wt = jnp.where(j == _chunk_relative(lp[k:k + 1, :], c), gwb[k], wt)
            yv = _unpack_rows(_load_packed(yl, c * SORT_CHUNK, SORT_CHUNK))
            acc = acc + _dot_tn(wt, yv)
        if half == 0:
            _segment_copies(tabn_ref, 0, yl0, y_hbm, sem.at[0], to_global=False)
        x2 = xs1_ref[half * TILE:(half + 1) * TILE, :] + g2 * acc
        o_ref[half * TILE:(half + 1) * TILE, :] = _rms(x2) * fg_ref[...]

    @pl.when(s == ns - 1)
    def _():
        _segment_wait(yl0, y_hbm, sem.at[0], tab_ref[0, 0, _T_EMPTY])


def _combine_call(tab, lp, gw, xs1, mod3, final_g, y, seq):
    t, d = xs1.shape
    ns = t // (2 * TILE)
    per_seq = seq // (2 * TILE)
    return pl.pallas_call(
        _combine_kernel,
        grid=(ns,),
        in_specs=[
            pl.BlockSpec((1, 1, _T_SIZE), lambda s: (s, 0, 0), memory_space=pltpu.SMEM),
            pl.BlockSpec((1, 1, _T_SIZE), lambda s: (jnp.minimum(s + 1, ns - 1), 0, 0),
                         memory_space=pltpu.SMEM),
            pl.BlockSpec((TOP_K, 2 * TILE), lambda s: (0, s)),
            pl.BlockSpec((TOP_K, 2 * TILE), lambda s: (0, s)),
            pl.BlockSpec((2 * TILE, d), lambda s: (s, 0)),
            pl.BlockSpec((1, 6, d), lambda s: (s // per_seq, 0, 0)),
            pl.BlockSpec((1, d), lambda s: (0, 0)),
            pl.BlockSpec(memory_space=pl.ANY),
        ],
        out_specs=pl.BlockSpec((2 * TILE, d), lambda s: (s, 0)),
        out_shape=jax.ShapeDtypeStruct((t, d), F32),
        scratch_shapes=[
            pltpu.VMEM((LBUF_ROWS, PACK_S, LANES), U32),
            pltpu.VMEM((LBUF_ROWS, PACK_S, LANES), U32),
            pltpu.SemaphoreType.DMA((2,)),
        ],
        compiler_params=pltpu.CompilerParams(
            dimension_semantics=("arbitrary",), vmem_limit_bytes=VMEM_LIMIT),
        name="combine",
    )(tab, tab, lp, gw, xs1, mod3, final_g.reshape(1, d), y)


def kernel(x, c, w_ada, b_ada, w_in, attn_sinks, rel_bias, attn_norm_g, conv_w, conv_b, conv_ln_g,
           conv_ln_b, w_out, w_router, router_bias, w_exp_gate, w_exp_up, w_exp_down, w_sh_gate,
           w_sh_up, w_sh_down, final_norm_g):
    bsz, seq, d = x.shape
    t = bsz * seq
    assert w_ada.shape[0] == 1 and d == D_MODEL
    assert seq % MIX_ROWS == 0 and seq % (2 * TILE) == 0

    mod3 = _ada_call(c, w_ada[0], b_ada[0]).reshape(bsz, 6, d)
    xs1, h2, lgt = _mixer_call(
        x, mod3, w_in[0].astype(BF16), attn_sinks[0], rel_bias,
        attn_norm_g[0].reshape(1, Q_COLS), conv_w[0], conv_b[0].reshape(1, CONV_CH),
        conv_ln_g[0].reshape(1, CONV_CH), conv_ln_b[0].reshape(1, CONV_CH),
        w_out[0].astype(BF16), w_router[0].T,
        w_sh_gate[0].astype(BF16), w_sh_up[0].astype(BF16), w_sh_down[0].astype(BF16))
    xs1 = xs1.reshape(t, d)
    h2 = h2.reshape(t, d)

    lp, gw, cnt = _route_call(lgt, router_bias[0])

    ntiles = t // TILE
    nassign = t * TOP_K
    n = cnt[:, :ntiles].T.astype(I32)
    local = jnp.cumsum(n, axis=1) - n
    earlier = jnp.cumsum(n, axis=0) - n
    total = jnp.sum(n, axis=0)
    region = (total + 1) // 2 * 2
    starts = jnp.cumsum(region) - region
    nalloc = nassign + N_EXPERTS
    used = jnp.sum(region)
    pad = jnp.stack([starts + total, region - total,
                     jnp.broadcast_to(used, (N_EXPERTS,))], axis=0)
    glob = starts[None, :] + earlier
    empty = n == 0
    length = jnp.maximum(n, 1)
    eid = jnp.arange(N_EXPERTS, dtype=I32)[None, :]
    parity = (jnp.arange(ntiles, dtype=I32) % 2)[:, None]
    spare_global = nalloc + EXPERT_ROWS + parity * N_EXPERTS + eid
    nempty = jnp.sum(empty.astype(I32), axis=1).reshape(ntiles // 2, 2)

    def table(loc, glo):
        per_tile = jnp.stack([loc, length, glo], axis=2).reshape(ntiles // 2, _T_EMPTY)
        return jnp.concatenate([per_tile, nempty], axis=1).reshape(ntiles // 2, 1, _T_SIZE)

    tab_out = table(jnp.where(empty, 0, local), jnp.where(empty, spare_global, glob))
    tab_back = table(jnp.where(empty, LROWS + eid, local), jnp.where(empty, 0, glob))

    rows = EXPERT_ROWS
    nchunk = (total + rows - 1) // rows
    chunk_end = jnp.cumsum(nchunk)
    first = chunk_end - nchunk
    max_chunks = nassign // rows + N_EXPERTS
    gid = jnp.arange(max_chunks, dtype=I32)
    owner = jnp.minimum(jnp.sum((chunk_end[None, :] <= gid[:, None]).astype(I32), axis=1), N_EXPERTS - 1)
    pick = owner[:, None] == eid
    within = gid - jnp.sum(jnp.where(pick, first[None, :], 0), axis=1)
    chunk_row = jnp.sum(jnp.where(pick, starts[None, :], 0), axis=1) + within * rows
    left = jnp.sum(jnp.where(pick, region[None, :], 0), axis=1) - within * rows
    live = gid < chunk_end[-1]
    chunk_row = jnp.where(live, chunk_row, 0)
    chunk_len = jnp.where(live, jnp.clip(left, 2, rows), 2)
    info = jnp.stack([chunk_end[-1], used])

    xs = _dispatch_call(tab_out, pad, h2, lp, nalloc + TAIL_ROWS)
    y = _expert_call(first, nchunk, chunk_row, chunk_len, info, xs,
                     w_exp_gate[0], w_exp_up[0], w_exp_down[0], nalloc)
    out = _combine_call(tab_back, lp, gw, xs1, mod3, final_norm_g, y, seq)
    return out.reshape(bsz, seq, d)
```

```python
import numpy as np
import jax
import jax.numpy as jnp
from jax import lax
from jax.experimental import pallas as pl
from jax.experimental.pallas import tpu as pltpu

F32 = jnp.float32
BF16 = jnp.bfloat16
U32 = jnp.uint32
I32 = jnp.int32

D_MODEL = 1024
HEAD_DIM = 64
N_HEADS = 8
N_KV_HEADS = 2
GROUP = N_HEADS // N_KV_HEADS
Q_COLS = N_HEADS * HEAD_DIM
KV_COLS = N_KV_HEADS * HEAD_DIM
ATTN_BLOCK = 128
WINDOW = 128
NUM_BUCKETS = 32
MAX_DISTANCE = 128
CONV_CH = D_MODEL - Q_COLS
CONV_WIDTH = 31
IN_COLS = Q_COLS + 2 * KV_COLS + 2 * CONV_CH
N_EXPERTS = 256
TOP_K = 8
N_GROUPS = 8
GROUP_SIZE = N_EXPERTS // N_GROUPS
TOPK_GROUPS = 4
EXPERT_HIDDEN = 256
ROUTED_SCALE = 2.5
EPS = 1e-6

MIX_ROWS = 256
HALO = 32
TILE = 256
LROWS = TILE * TOP_K
SORT_CHUNK = 256
EXPERT_ROWS = 512
LANES = 128
SUBLANES = 8
PACK_W = D_MODEL // 2
PACK_S = PACK_W // LANES
VMEM_LIMIT = 56 * 1024 * 1024


def _sigmoid(v):
    return 1.0 / (1.0 + jnp.exp(-v))


def _rms(v):
    return v * lax.rsqrt(jnp.mean(v * v, axis=-1, keepdims=True) + EPS)


def _split_bf16(a):
    hi = a.astype(BF16)
    lo = (a - hi.astype(F32)).astype(BF16)
    return hi, lo


def _dot(a, b):
    return jnp.dot(a, b, preferred_element_type=F32)


def _dot_nt(a, b):
    return lax.dot_general(a, b, (((1,), (1,)), ((), ())), preferred_element_type=F32)


def _dot_tn(a, b):
    return lax.dot_general(a, b, (((0,), (0,)), ((), ())), preferred_element_type=F32)


def _dot3(a, b, dot):
    ah, al = _split_bf16(a)
    bh, bl = _split_bf16(b)
    return dot(ah, bh) + (dot(ah, bl) + dot(al, bh))


def _pack_rows(v):
    hi = lax.bitcast_convert_type(v[:, :PACK_W], U32) & jnp.uint32(0xFFFF0000)
    lo = lax.bitcast_convert_type(v[:, PACK_W:], U32) >> 16
    return hi | lo


def _unpack_rows(u):
    hi = lax.bitcast_convert_type(u & jnp.uint32(0xFFFF0000), F32)
    lo = lax.bitcast_convert_type(u << 16, F32)
    return jnp.concatenate([hi, lo], axis=1).astype(BF16)


def _load_flat(flat, r0, n):
    return jnp.concatenate(
        [flat[pl.ds(r0 * PACK_S + c, n, stride=PACK_S), :] for c in range(PACK_S)], axis=1)


def _store_flat(flat, r0, n, u):
    for c in range(PACK_S):
        flat[pl.ds(r0 * PACK_S + c, n, stride=PACK_S), :] = u[:, c * LANES:(c + 1) * LANES]


def _load_packed(ref3, r0, n):
    return _load_flat(ref3.reshape(ref3.shape[0] * PACK_S, LANES), r0, n)


def _store_packed(ref3, r0, n, u):
    _store_flat(ref3.reshape(ref3.shape[0] * PACK_S, LANES), r0, n, u)


def _ada_kernel(c_ref, w_ref, b_ref, o_ref):
    c = c_ref[...]
    s = c * _sigmoid(c)
    o_ref[...] = _dot3(s, w_ref[...], _dot) + b_ref[...]


def _ada_call(c, w_ada, b_ada):
    bsz, d = c.shape
    n = w_ada.shape[1]
    tn = 1536
    return pl.pallas_call(
        _ada_kernel,
        grid=(n // tn,),
        in_specs=[
            pl.BlockSpec((bsz, d), lambda i: (0, 0)),
            pl.BlockSpec((d, tn), lambda i: (0, i)),
            pl.BlockSpec((1, tn), lambda i: (0, i)),
        ],
        out_specs=pl.BlockSpec((bsz, tn), lambda i: (0, i)),
        out_shape=jax.ShapeDtypeStruct((bsz, n), F32),
        compiler_params=pltpu.CompilerParams(
            dimension_semantics=("arbitrary",), vmem_limit_bytes=VMEM_LIMIT),
        name="ada",
    )(c, w_ada, b_ada.reshape(1, n))


def _bucket_table():
    qi = np.arange(ATTN_BLOCK)[:, None]
    ki = np.arange(2 * ATTN_BLOCK)[None, :]
    dist = qi + ATTN_BLOCK - ki
    n = np.maximum(dist, 0)
    max_exact = NUM_BUCKETS // 2
    large = max_exact + (np.log(np.maximum(n, 1) / max_exact) / np.log(MAX_DISTANCE / max_exact)
                         * (NUM_BUCKETS - max_exact)).astype(np.int32)
    large = np.minimum(large, NUM_BUCKETS - 1)
    bkt = np.where(n < max_exact, n, large).astype(np.int32)
    band = (dist >= 0) & (dist < WINDOW)
    return np.ascontiguousarray(np.where(band, bkt, -1).astype(np.int32).T)


def _mixer_kernel(x_ref, mod_ref, win_ref, sinks_ref, relb_ref, bkt_ref, ang_ref, cw_ref, cb_ref,
                  lng_ref, lnb_ref, wout_ref, wrt_ref, wsg_ref, wsu_ref, wsd_ref,
                  xs1_ref, h2_ref, lgt_ref,
                  kv_ref, uext_ref, bias_ref):
    tm = x_ref.shape[1]
    nblk = tm // ATTN_BLOCK
    j = pl.program_id(1)

    @pl.when((pl.program_id(0) == 0) & (j == 0))
    def _build_bias():
        bkt = bkt_ref[...]
        for h in range(N_HEADS):
            acc = jnp.full(bkt.shape, -jnp.inf, F32)
            for b in range(NUM_BUCKETS):
                acc = jnp.where(bkt == b, relb_ref[b, h], acc)
            g = h % GROUP
            bias_ref[h // GROUP, :, g * ATTN_BLOCK:(g + 1) * ATTN_BLOCK] = acc

    @pl.when(j == 0)
    def _reset_history():
        kv_ref[...] = jnp.zeros(kv_ref.shape, F32)
        uext_ref[0:HALO, :] = jnp.zeros((HALO, CONV_CH), F32)
        uext_ref[HALO + tm:HALO + tm + SUBLANES, :] = jnp.zeros((SUBLANES, CONV_CH), F32)

    x = x_ref[0]
    mod = mod_ref[0]
    sh1, sc1, g1 = mod[0:1], mod[1:2], mod[2:3]
    sh2, sc2, g2 = mod[3:4], mod[4:5], mod[5:6]

    h = _rms(x) * (1.0 + sc1) + sh1
    hb = h.astype(BF16)
    n_qkv = Q_COLS + 2 * KV_COLS
    proj_conv = _dot(hb, win_ref[:, n_qkv:])
    proj = _dot(hb, win_ref[:, :n_qkv])

    qb = (proj[:, :Q_COLS] * (HEAD_DIM ** -0.5)).astype(BF16)
    kv_cur = proj[:, Q_COLS:Q_COLS + 2 * KV_COLS]
    kvb = jnp.concatenate([kv_ref[...], kv_cur], axis=0).astype(BF16)
    kv_ref[...] = kv_cur[tm - ATTN_BLOCK:, :]
    not_first = j > 0
    key = lax.broadcasted_iota(jnp.int32, (2 * ATTN_BLOCK, GROUP * ATTN_BLOCK), 0)
    attn_rows = []
    for i in range(nblk):
        r0, r1, r2 = i * ATTN_BLOCK, (i + 1) * ATTN_BLOCK, (i + 2) * ATTN_BLOCK
        heads_t = [None] * N_HEADS
        for kh in range(N_KV_HEADS):
            kpc = kvb[r0:r2, kh * HEAD_DIM:(kh + 1) * HEAD_DIM]
            vpc = kvb[r0:r2, KV_COLS + kh * HEAD_DIM:KV_COLS + (kh + 1) * HEAD_DIM]
            qg = jnp.concatenate(
                [qb[r0:r1, (kh * GROUP + g) * HEAD_DIM:(kh * GROUP + g + 1) * HEAD_DIM]
                 for g in range(GROUP)], axis=0)
            logit = _dot_nt(kpc, qg) + bias_ref[kh]
            if i == 0:
                logit = jnp.where((key >= ATTN_BLOCK) | not_first, logit, -jnp.inf)
            sink = jnp.concatenate(
                [jnp.full((1, ATTN_BLOCK), sinks_ref[kh * GROUP + g], F32) for g in range(GROUP)], axis=1)
            m = jnp.maximum(jnp.max(logit, axis=0, keepdims=True), sink)
            p = jnp.exp(logit - m)
            den = jnp.sum(p, axis=0, keepdims=True) + jnp.exp(sink - m)
            o_t = _dot_tn(vpc, p.astype(BF16)) / den
            for g in range(GROUP):
                heads_t[kh * GROUP + g] = o_t[:, g * ATTN_BLOCK:(g + 1) * ATTN_BLOCK]
        attn_t = jnp.concatenate(heads_t, axis=0)
        scale = lax.rsqrt(jnp.mean(attn_t * attn_t, axis=0, keepdims=True) + EPS)
        attn_rows.append((attn_t * scale).T)
    attn = jnp.concatenate(attn_rows, axis=0) * ang_ref[...]

    a = proj_conv[:, :CONV_CH]
    gt = proj_conv[:, CONV_CH:]
    uext_ref[HALO:HALO + tm, :] = a * _sigmoid(gt)
    cw = cw_ref[...]
    base = HALO - (CONV_WIDTH - 1)
    acc = None
    for res in range(SUBLANES):
        part = None
        for hi in range((HALO + SUBLANES - 1) // SUBLANES + 1):
            t = hi * SUBLANES + res - base
            if 0 <= t < CONV_WIDTH:
                term = cw[t:t + 1, :] * uext_ref[hi * SUBLANES:hi * SUBLANES + tm + SUBLANES, :]
                part = term if part is None else part + term
        part = part[res:res + tm, :]
        acc = part if acc is None else acc + part
    uext_ref[0:HALO, :] = uext_ref[tm:tm + HALO, :]
    cv = acc + cb_ref[...]
    mu = jnp.mean(cv, axis=-1, keepdims=True)
    var = jnp.mean(jnp.square(cv - mu), axis=-1, keepdims=True)
    cv = (cv - mu) * lax.rsqrt(var + EPS) * lng_ref[...] + lnb_ref[...]
    cv = cv * _sigmoid(cv)

    mixed = (_dot(attn.astype(BF16), wout_ref[0:Q_COLS, :])
             + _dot(cv.astype(BF16), wout_ref[Q_COLS:, :]))
    x1 = x + g1 * mixed

    h2 = _rms(x1) * (1.0 + sc2) + sh2
    lgt_ref[...] = _dot3(wrt_ref[...], h2, _dot_nt)
    h2b = h2.astype(BF16)
    h2_ref[0] = h2b
    sg = _dot(h2b, wsg_ref[...])
    su = _dot(h2b, wsu_ref[...])
    shared = _dot(((sg * _sigmoid(sg)) * su).astype(BF16), wsd_ref[...])
    xs1_ref[0] = x1 + g2 * shared


def _mixer_call(x, mod3, w_in_b, sinks, rel_bias, ang, cw, cb, lng, lnb, w_out_b, w_rt, wsg_b, wsu_b, wsd_b):
    bsz, seq, d = x.shape
    tm = MIX_ROWS
    nj = seq // tm
    bkt = jnp.asarray(_bucket_table())
    full = lambda shape: pl.BlockSpec(shape, lambda b, j: (0,) * len(shape))
    smem = pl.BlockSpec(memory_space=pltpu.SMEM)
    return pl.pallas_call(
        _mixer_kernel,
        grid=(bsz, nj),
        in_specs=[
            pl.BlockSpec((1, tm, d), lambda b, j: (b, j, 0)),
            pl.BlockSpec((1, 6, d), lambda b, j: (b, 0, 0)),
            full((d, IN_COLS)),
            smem, smem,
            full((2 * ATTN_BLOCK, ATTN_BLOCK)),
            full((1, Q_COLS)),
            full((CONV_WIDTH, CONV_CH)),
            full((1, CONV_CH)), full((1, CONV_CH)), full((1, CONV_CH)),
            full((d, d)),
            full((N_EXPERTS, d)),
            full((d, EXPERT_HIDDEN)), full((d, EXPERT_HIDDEN)), full((EXPERT_HIDDEN, d)),
        ],
        out_specs=[
            pl.BlockSpec((1, tm, d), lambda b, j: (b, j, 0)),
            pl.BlockSpec((1, tm, d), lambda b, j: (b, j, 0)),
            pl.BlockSpec((N_EXPERTS, tm), lambda b, j: (0, b * nj + j)),
        ],
        out_shape=[
            jax.ShapeDtypeStruct((bsz, seq, d), F32),
            jax.ShapeDtypeStruct((bsz, seq, d), BF16),
            jax.ShapeDtypeStruct((N_EXPERTS, bsz * seq), F32),
        ],
        scratch_shapes=[
            pltpu.VMEM((ATTN_BLOCK, 2 * KV_COLS), F32),
            pltpu.VMEM((HALO + tm + SUBLANES, CONV_CH), F32),
            pltpu.VMEM((N_KV_HEADS, 2 * ATTN_BLOCK, GROUP * ATTN_BLOCK), F32),
        ],
        compiler_params=pltpu.CompilerParams(
            dimension_semantics=("arbitrary", "arbitrary"), vmem_limit_bytes=VMEM_LIMIT),
        name="mixer",
    )(x, mod3, w_in_b, sinks, rel_bias, bkt, ang, cw, cb, lng, lnb, w_out_b, w_rt, wsg_b, wsu_b, wsd_b)


def _route_kernel(lgt_ref, rb_ref, lp_ref, gw_ref, cnt_ref):
    tr = lgt_ref.shape[1]
    i = pl.program_id(0)

    @pl.when(i == 0)
    def _():
        cnt_ref[...] = jnp.zeros(cnt_ref.shape, F32)

    scores = _sigmoid(lgt_ref[...])
    sel = scores + rb_ref[...]
    neg = -jnp.inf

    sel3 = sel.reshape(N_GROUPS, GROUP_SIZE, tr)
    loc = lax.broadcasted_iota(I32, sel3.shape, 1)
    m1 = jnp.max(sel3, axis=1, keepdims=True)
    i1 = jnp.min(jnp.where(sel3 == m1, loc, GROUP_SIZE), axis=1, keepdims=True)
    m2 = jnp.max(jnp.where(loc == i1, neg, sel3), axis=1, keepdims=True)
    gscore = (m1 + m2).reshape(N_GROUPS, tr)

    gio = lax.broadcasted_iota(I32, gscore.shape, 0)
    gmask = jnp.zeros(gscore.shape, jnp.bool_)
    cur = gscore
    for _ in range(TOPK_GROUPS):
        m = jnp.max(cur, axis=0, keepdims=True)
        idx = jnp.min(jnp.where(cur == m, gio, N_GROUPS), axis=0, keepdims=True)
        pick = gio == idx
        gmask = gmask | pick
        cur = jnp.where(pick, neg, cur)
    emask = jnp.broadcast_to(gmask.reshape(N_GROUPS, 1, tr), sel3.shape).reshape(N_EXPERTS, tr)

    rio = lax.broadcasted_iota(I32, sel.shape, 0)
    cur = jnp.where(emask, sel, neg)
    picks, gsc = [], []
    for _ in range(TOP_K):
        m = jnp.max(cur, axis=0, keepdims=True)
        idx = jnp.min(jnp.where(cur == m, rio, N_EXPERTS), axis=0, keepdims=True)
        pick = rio == idx
        picks.append(pick)
        gsc.append(jnp.sum(jnp.where(pick, scores, 0.0), axis=0, keepdims=True))
        cur = jnp.where(pick, neg, cur)
    gsum = gsc[0]
    for k in range(1, TOP_K):
        gsum = gsum + gsc[k]
    gw_ref[...] = jnp.concatenate([g / gsum * ROUTED_SCALE for g in gsc], axis=0)

    chosen = picks[0]
    for k in range(1, TOP_K):
        chosen = chosen | picks[k]
    onehot = jnp.where(chosen, 1.0, 0.0)
    tri = (lax.broadcasted_iota(I32, (tr, tr), 0) < lax.broadcasted_iota(I32, (tr, tr), 1))
    before = _dot(onehot.astype(BF16), jnp.where(tri, 1.0, 0.0).astype(BF16))
    n = jnp.sum(onehot, axis=1, keepdims=True)
    low = (lax.broadcasted_iota(I32, (N_EXPERTS, N_EXPERTS), 1)
           < lax.broadcasted_iota(I32, (N_EXPERTS, N_EXPERTS), 0))
    start = _dot(jnp.where(low, 1.0, 0.0).astype(BF16),
                 jnp.broadcast_to(n, (N_EXPERTS, LANES)).astype(BF16))[:, 0:1]
    pos = start + before
    lp_ref[...] = jnp.concatenate(
        [jnp.sum(jnp.where(p, pos, 0.0), axis=0, keepdims=True) for p in picks], axis=0).astype(I32)

    lane = lax.broadcasted_iota(I32, cnt_ref.shape, 1)
    cnt_ref[...] += jnp.where(lane == i, n, 0.0)


def _route_call(lgt, router_bias):
    e, t = lgt.shape
    ntiles = t // TILE
    ntp = (ntiles + LANES - 1) // LANES * LANES
    return pl.pallas_call(
        _route_kernel,
        grid=(ntiles,),
        in_specs=[
            pl.BlockSpec((e, TILE), lambda i: (0, i)),
            pl.BlockSpec((e, 1), lambda i: (0, 0)),
        ],
        out_specs=[
            pl.BlockSpec((TOP_K, TILE), lambda i: (0, i)),
            pl.BlockSpec((TOP_K, TILE), lambda i: (0, i)),
            pl.BlockSpec((e, ntp), lambda i: (0, 0)),
        ],
        out_shape=[
            jax.ShapeDtypeStruct((TOP_K, t), I32),
            jax.ShapeDtypeStruct((TOP_K, t), F32),
            jax.ShapeDtypeStruct((e, ntp), F32),
        ],
        compiler_params=pltpu.CompilerParams(
            dimension_semantics=("arbitrary",), vmem_limit_bytes=VMEM_LIMIT),
        name="route",
    )(lgt, router_bias.reshape(e, 1))


_T_LOCAL, _T_LEN, _T_GLOBAL, _T_FIELDS = 0, 1, 2, 3
_T_STRIDE = 2 * N_EXPERTS
_T_EMPTY = _T_STRIDE * _T_FIELDS
_T_SIZE = _T_EMPTY + 2
LBUF_ROWS = LROWS + N_EXPERTS


def _segment_copies(tab_ref, half, local_ref, global_hbm, sem, to_global):
    def body(e, carry):
        at = half * N_EXPERTS + e
        n = tab_ref[0, 0, _T_LEN * _T_STRIDE + at]
        loc = local_ref.at[pl.ds(tab_ref[0, 0, _T_LOCAL * _T_STRIDE + at], n)]
        glo = global_hbm.at[pl.ds(tab_ref[0, 0, _T_GLOBAL * _T_STRIDE + at], n)]
        if to_global:
            pltpu.make_async_copy(loc, glo, sem).start()
        else:
            pltpu.make_async_copy(glo, loc, sem).start()
        return carry

    lax.fori_loop(0, N_EXPERTS, body, 0, unroll=8)


def _chunk_relative(lp_row, c):
    rel = lp_row - c * SORT_CHUNK
    inside = (rel >= 0) & (rel < SORT_CHUNK)
    return jnp.where(inside, rel, -1).astype(F32).astype(BF16)


def _chunk_row_ids():
    return lax.broadcasted_iota(I32, (SORT_CHUNK, TILE), 0).astype(F32).astype(BF16)


def _segment_wait(local_ref, global_hbm, sem, nempty):
    rows = LROWS + nempty
    pltpu.make_async_copy(global_hbm.at[pl.ds(0, rows)], local_ref.at[pl.ds(0, rows)], sem).wait()


TAIL_ROWS = EXPERT_ROWS + 2 * N_EXPERTS


def _dispatch_kernel(tab_ref, pad_ref, h2_ref, lp_ref, xs_hbm, xl0, xl1, zb, sem, pending):
    s = pl.program_id(0)
    ns = pl.num_programs(0)

    @pl.when(s == 0)
    def _zero_unassigned_rows():
        zb[...] = jnp.zeros(zb.shape, U32)
        used = pad_ref[2, 0]
        rest = xs_hbm.shape[0] - used
        cp = pltpu.make_async_copy(zb.at[pl.ds(0, rest)], xs_hbm.at[pl.ds(used, rest)], sem.at[2])
        cp.start()
        cp.wait()

        def pad_row(e):
            return pltpu.make_async_copy(zb.at[pl.ds(0, 1)], xs_hbm.at[pl.ds(pad_ref[0, e], 1)], sem.at[2])

        def start(e, carry):
            @pl.when(pad_ref[1, e] > 0)
            def _():
                pad_row(e).start()
            return carry

        def wait(e, carry):
            @pl.when(pad_ref[1, e] > 0)
            def _():
                pad_row(e).wait()
            return carry

        lax.fori_loop(0, N_EXPERTS, start, 0)
        lax.fori_loop(0, N_EXPERTS, wait, 0)

    @pl.when(s > 0)
    def _():
        _segment_wait(xl0, xs_hbm, sem.at[0], pending[0])
        _segment_wait(xl1, xs_hbm, sem.at[1], pending[1])

    for half, xl in ((0, xl0), (1, xl1)):
        rows = h2_ref[half * TILE:(half + 1) * TILE, :]
        lp = lp_ref[:, half * TILE:(half + 1) * TILE]
        j = _chunk_row_ids()
        for c in range(LROWS // SORT_CHUNK):
            onehot = jnp.zeros((SORT_CHUNK, TILE), BF16)
            for k in range(TOP_K):
                onehot = jnp.where(j == _chunk_relative(lp[k:k + 1, :], c), jnp.ones((), BF16), onehot)
            _store_packed(xl, c * SORT_CHUNK, SORT_CHUNK, _pack_rows(_dot(onehot, rows)))
        _segment_copies(tab_ref, half, xl, xs_hbm, sem.at[half], to_global=True)
        pending[half] = tab_ref[0, 0, _T_EMPTY + half]

    @pl.when(s == ns - 1)
    def _():
        _segment_wait(xl0, xs_hbm, sem.at[0], pending[0])
        _segment_wait(xl1, xs_hbm, sem.at[1], pending[1])


def _dispatch_call(tab, pad, h2, lp, nrows):
    t, d = h2.shape
    ns = t // (2 * TILE)
    return pl.pallas_call(
        _dispatch_kernel,
        grid=(ns,),
        in_specs=[
            pl.BlockSpec((1, 1, _T_SIZE), lambda s: (s, 0, 0), memory_space=pltpu.SMEM),
            pl.BlockSpec(memory_space=pltpu.SMEM),
            pl.BlockSpec((2 * TILE, d), lambda s: (s, 0)),
            pl.BlockSpec((TOP_K, 2 * TILE), lambda s: (0, s)),
        ],
        out_specs=pl.BlockSpec(memory_space=pl.ANY),
        out_shape=jax.ShapeDtypeStruct((nrows, PACK_S, LANES), U32),
        scratch_shapes=[
            pltpu.VMEM((LBUF_ROWS, PACK_S, LANES), U32),
            pltpu.VMEM((LBUF_ROWS, PACK_S, LANES), U32),
            pltpu.VMEM((TAIL_ROWS + N_EXPERTS, PACK_S, LANES), U32),
            pltpu.SemaphoreType.DMA((3,)),
            pltpu.SMEM((2,), I32),
        ],
        compiler_params=pltpu.CompilerParams(
            dimension_semantics=("arbitrary",), vmem_limit_bytes=VMEM_LIMIT),
        name="dispatch",
    )(tab, pad, h2, lp)


RING = 6
AHEAD = RING - 2


def _expert_kernel(first_ref, nchunk_ref, row_ref, len_ref, info_ref, wg_ref, wu_ref, wd_ref, xs_hbm, y_hbm,
                   xbuf, ybuf, isem, osem, wgb, wub, wdb):
    e = pl.program_id(0)
    last_step = e == pl.num_programs(0) - 1
    rows = EXPERT_ROWS
    total_chunks = info_ref[0]

    def slab(ref, row0, m):
        size = m * PACK_S if isinstance(m, int) else pl.multiple_of(m * PACK_S, SUBLANES)
        first = row0 * PACK_S if isinstance(row0, int) else pl.multiple_of(row0 * PACK_S, SUBLANES)
        return ref.at[pl.ds(first, size)]

    def fetch(g):
        slot = g % RING
        return pltpu.make_async_copy(slab(xs_hbm, row_ref[g], rows), slab(xbuf, slot * rows, rows), isem.at[slot])

    def flush(g):
        slot = g % RING
        m = len_ref[g]
        return pltpu.make_async_copy(slab(ybuf, slot * rows, m), slab(y_hbm, row_ref[g], m), osem.at[slot])

    def start_fetch(g):
        @pl.when(g < total_chunks)
        def _():
            fetch(g).start()

    def wait_flush(g):
        @pl.when(g >= 0)
        def _():
            flush(g).wait()

    def compute(g):
        slot = g % RING
        xb = _unpack_rows(_load_flat(xbuf, slot * rows, rows))
        gate = _dot(xb, wgb[...])
        up = _dot(xb, wub[...])
        hid = (gate * _sigmoid(gate)) * up
        y = _dot(hid.astype(BF16), wdb[...])
        _store_flat(ybuf, slot * rows, rows, _pack_rows(y.astype(BF16).astype(F32)))

    @pl.when(e == 0)
    def _prime():
        for a in range(AHEAD):
            start_fetch(a)

    wgb[...] = wg_ref[0].astype(BF16)
    wub[...] = wu_ref[0].astype(BF16)
    wdb[...] = wd_ref[0].astype(BF16)

    begin = first_ref[e]
    count = nchunk_ref[e]

    def pair(p, carry):
        g0 = begin + 2 * p
        g1 = g0 + 1
        start_fetch(g0 + AHEAD)
        start_fetch(g1 + AHEAD)
        fetch(g0).wait()
        fetch(g1).wait()
        wait_flush(g0 - RING)
        wait_flush(g1 - RING)
        compute(g0)
        compute(g1)
        flush(g0).start()
        flush(g1).start()
        return carry

    lax.fori_loop(0, count // 2, pair, 0)

    @pl.when(count % 2 == 1)
    def _odd_chunk():
        g = begin + count - 1
        start_fetch(g + AHEAD)
        fetch(g).wait()
        wait_flush(g - RING)
        compute(g)
        flush(g).start()

    @pl.when(last_step)
    def _drain():
        def wait(g, carry):
            flush(g).wait()
            return carry
        lax.fori_loop(jnp.maximum(total_chunks - RING, 0), total_chunks, wait, 0)

    @pl.when(last_step)
    def _define_unused_rows():
        used = info_ref[1]
        rest = y_hbm.shape[0] // PACK_S - used

        @pl.when(rest > 0)
        def _():
            ybuf[0:N_EXPERTS * PACK_S, :] = jnp.zeros((N_EXPERTS * PACK_S, LANES), U32)
            cp = pltpu.make_async_copy(slab(ybuf, 0, rest), slab(y_hbm, used, rest), osem.at[0])
            cp.start()
            cp.wait()


def _expert_call(first, nchunk, chunk_row, chunk_len, info, xs, w_gate, w_up, w_down, nrows):
    ne, d, hid = w_gate.shape
    rows = EXPERT_ROWS
    wsel = lambda e, *_: (e, 0, 0)
    grid_spec = pltpu.PrefetchScalarGridSpec(
        num_scalar_prefetch=5,
        grid=(ne,),
        in_specs=[
            pl.BlockSpec((1, d, hid), wsel),
            pl.BlockSpec((1, d, hid), wsel),
            pl.BlockSpec((1, hid, d), wsel),
            pl.BlockSpec(memory_space=pl.ANY),
        ],
        out_specs=pl.BlockSpec(memory_space=pl.ANY),
        scratch_shapes=[
            pltpu.VMEM((RING * rows * PACK_S, LANES), U32),
            pltpu.VMEM((RING * rows * PACK_S, LANES), U32),
            pltpu.SemaphoreType.DMA((RING,)),
            pltpu.SemaphoreType.DMA((RING,)),
            pltpu.VMEM((d, hid), BF16),
            pltpu.VMEM((d, hid), BF16),
            pltpu.VMEM((hid, d), BF16),
        ],
    )
    return pl.pallas_call(
        _expert_kernel,
        grid_spec=grid_spec,
        out_shape=jax.ShapeDtypeStruct((nrows * PACK_S, LANES), U32),
        compiler_params=pltpu.CompilerParams(
            dimension_semantics=("arbitrary",), vmem_limit_bytes=VMEM_LIMIT),
        name="experts",
    )(first, nchunk, chunk_row, chunk_len, info, w_gate, w_up, w_down,
      xs.reshape(-1, LANES)).reshape(nrows, PACK_S, LANES)


def _combine_kernel(tab_ref, tabn_ref, lp_ref, gw_ref, xs1_ref, mod_ref, fg_ref, y_hbm, o_ref, yl0, yl1, sem):
    s = pl.program_id(0)
    ns = pl.num_programs(0)

    @pl.when(s == 0)
    def _():
        _segment_copies(tab_ref, 0, yl0, y_hbm, sem.at[0], to_global=False)

    g2 = mod_ref[0][5:6]
    for half, yl in ((0, yl0), (1, yl1)):
        _segment_wait(yl, y_hbm, sem.at[half], tab_ref[0, 0, _T_EMPTY + half])
        if half == 0:
            _segment_copies(tab_ref, 1, yl1, y_hbm, sem.at[1], to_global=False)
        lp = lp_ref[:, half * TILE:(half + 1) * TILE]
        gw = gw_ref[:, half * TILE:(half + 1) * TILE]
        acc = jnp.zeros((TILE, D_MODEL), F32)
        gwb = [gw[k:k + 1, :].astype(BF16) for k in range(TOP_K)]
        j = _chunk_row_ids()
        for c in range(LROWS // SORT_CHUNK):
            wt = jnp.zeros((SORT_CHUNK, TILE), BF16)
            for k in range(TOP_K):
                wt = jnp.where(j == _chunk_relative(lp[k:k + 1, :], c), gwb[k], wt)
            yv = _unpack_rows(_load_packed(yl, c * SORT_CHUNK, SORT_CHUNK))
            acc = acc + _dot_tn(wt, yv)
        if half == 0:
            _segment_copies(tabn_ref, 0, yl0, y_hbm, sem.at[0], to_global=False)
        x2 = xs1_ref[half * TILE:(half + 1) * TILE, :] + g2 * acc
        o_ref[half * TILE:(half + 1) * TILE, :] = _rms(x2) * fg_ref[...]

    @pl.when(s == ns - 1)
    def _():
        _segment_wait(yl0, y_hbm, sem.at[0], tab_ref[0, 0, _T_EMPTY])


def _combine_call(tab, lp, gw, xs1, mod3, final_g, y, seq):
    t, d = xs1.shape
    ns = t // (2 * TILE)
    per_seq = seq // (2 * TILE)
    return pl.pallas_call(
        _combine_kernel,
        grid=(ns,),
        in_specs=[
            pl.BlockSpec((1, 1, _T_SIZE), lambda s: (s, 0, 0), memory_space=pltpu.SMEM),
            pl.BlockSpec((1, 1, _T_SIZE), lambda s: (jnp.minimum(s + 1, ns - 1), 0, 0),
                         memory_space=pltpu.SMEM),
            pl.BlockSpec((TOP_K, 2 * TILE), lambda s: (0, s)),
            pl.BlockSpec((TOP_K, 2 * TILE), lambda s: (0, s)),
            pl.BlockSpec((2 * TILE, d), lambda s: (s, 0)),
            pl.BlockSpec((1, 6, d), lambda s: (s // per_seq, 0, 0)),
            pl.BlockSpec((1, d), lambda s: (0, 0)),
            pl.BlockSpec(memory_space=pl.ANY),
        ],
        out_specs=pl.BlockSpec((2 * TILE, d), lambda s: (s, 0)),
        out_shape=jax.ShapeDtypeStruct((t, d), F32),
        scratch_shapes=[
            pltpu.VMEM((LBUF_ROWS, PACK_S, LANES), U32),
            pltpu.VMEM((LBUF_ROWS, PACK_S, LANES), U32),
            pltpu.SemaphoreType.DMA((2,)),
        ],
        compiler_params=pltpu.CompilerParams(
            dimension_semantics=("arbitrary",), vmem_limit_bytes=VMEM_LIMIT),
        name="combine",
    )(tab, tab, lp, gw, xs1, mod3, final_g.reshape(1, d), y)


def kernel(x, c, w_ada, b_ada, w_in, attn_sinks, rel_bias, attn_norm_g, conv_w, conv_b, conv_ln_g,
           conv_ln_b, w_out, w_router, router_bias, w_exp_gate, w_exp_up, w_exp_down, w_sh_gate,
           w_sh_up, w_sh_down, final_norm_g):
    bsz, seq, d = x.shape
    t = bsz * seq
    assert w_ada.shape[0] == 1 and d == D_MODEL
    assert seq % MIX_ROWS == 0 and seq % (2 * TILE) == 0

    mod3 = _ada_call(c, w_ada[0], b_ada[0]).reshape(bsz, 6, d)
    xs1, h2, lgt = _mixer_call(
        x, mod3, w_in[0].astype(BF16), attn_sinks[0], rel_bias,
        attn_norm_g[0].reshape(1, Q_COLS), conv_w[0], conv_b[0].reshape(1, CONV_CH),
        conv_ln_g[0].reshape(1, CONV_CH), conv_ln_b[0].reshape(1, CONV_CH),
        w_out[0].astype(BF16), w_router[0].T,
        w_sh_gate[0].astype(BF16), w_sh_up[0].astype(BF16), w_sh_down[0].astype(BF16))
    xs1 = xs1.reshape(t, d)
    h2 = h2.reshape(t, d)

    lp, gw, cnt = _route_call(lgt, router_bias[0])

    ntiles = t // TILE
    nassign = t * TOP_K
    n = cnt[:, :ntiles].T.astype(I32)
    local = jnp.cumsum(n, axis=1) - n
    earlier = jnp.cumsum(n, axis=0) - n
    total = jnp.sum(n, axis=0)
    region = (total + 1) // 2 * 2
    starts = jnp.cumsum(region) - region
    nalloc = nassign + N_EXPERTS
    used = jnp.sum(region)
    pad = jnp.stack([starts + total, region - total,
                     jnp.broadcast_to(used, (N_EXPERTS,))], axis=0)
    glob = starts[None, :] + earlier
    empty = n == 0
    length = jnp.maximum(n, 1)
    eid = jnp.arange(N_EXPERTS, dtype=I32)[None, :]
    parity = (jnp.arange(ntiles, dtype=I32) % 2)[:, None]
    spare_global = nalloc + EXPERT_ROWS + parity * N_EXPERTS + eid
    nempty = jnp.sum(empty.astype(I32), axis=1).reshape(ntiles // 2, 2)

    def table(loc, glo):
        fields = [f.reshape(ntiles // 2, _T_STRIDE) for f in (loc, length, glo)]
        return jnp.concatenate(fields + [nempty], axis=1).reshape(ntiles // 2, 1, _T_SIZE)

    tab_out = table(jnp.where(empty, 0, local), jnp.where(empty, spare_global, glob))
    tab_back = table(jnp.where(empty, LROWS + eid, local), jnp.where(empty, 0, glob))

    rows = EXPERT_ROWS
    nchunk = (total + rows - 1) // rows
    chunk_end = jnp.cumsum(nchunk)
    first = chunk_end - nchunk
    max_chunks = nassign // rows + N_EXPERTS
    gid = jnp.arange(max_chunks, dtype=I32)
    owner = jnp.minimum(jnp.sum((chunk_end[None, :] <= gid[:, None]).astype(I32), axis=1), N_EXPERTS - 1)
    pick = owner[:, None] == eid
    within = gid - jnp.sum(jnp.where(pick, first[None, :], 0), axis=1)
    chunk_row = jnp.sum(jnp.where(pick, starts[None, :], 0), axis=1) + within * rows
    left = jnp.sum(jnp.where(pick, region[None, :], 0), axis=1) - within * rows
    live = gid < chunk_end[-1]
    chunk_row = jnp.where(live, chunk_row, 0)
    chunk_len = jnp.where(live, jnp.clip(left, 2, rows), 2)
    info = jnp.stack([chunk_end[-1], used])

    xs = _dispatch_call(tab_out, pad, h2, lp, nalloc + TAIL_ROWS)
    y = _expert_call(first, nchunk, chunk_row, chunk_len, info, xs,
                     w_exp_gate[0], w_exp_up[0], w_exp_down[0], nalloc)
    out = _combine_call(tab_back, lp, gw, xs1, mod3, final_norm_g, y, seq)
    return out.reshape(bsz, seq, d)
```

```python
import numpy as np
import jax
import jax.numpy as jnp
from jax import lax
from jax.experimental import pallas as pl
from jax.experimental.pallas import tpu as pltpu

F32 = jnp.float32
BF16 = jnp.bfloat16
U32 = jnp.uint32
I32 = jnp.int32

D_MODEL = 1024
HEAD_DIM = 64
N_HEADS = 8
N_KV_HEADS = 2
GROUP = N_HEADS // N_KV_HEADS
Q_COLS = N_HEADS * HEAD_DIM
KV_COLS = N_KV_HEADS * HEAD_DIM
ATTN_BLOCK = 128
WINDOW = 128
NUM_BUCKETS = 32
MAX_DISTANCE = 128
CONV_CH = D_MODEL - Q_COLS
CONV_WIDTH = 31
IN_COLS = Q_COLS + 2 * KV_COLS + 2 * CONV_CH
N_EXPERTS = 256
TOP_K = 8
N_GROUPS = 8
GROUP_SIZE = N_EXPERTS // N_GROUPS
TOPK_GROUPS = 4
EXPERT_HIDDEN = 256
ROUTED_SCALE = 2.5
EPS = 1e-6

MIX_ROWS = 256
HALO = 32
TILE = 256
LROWS = TILE * TOP_K
SORT_CHUNK = 256
EXPERT_ROWS = 576
LANES = 128
SUBLANES = 8
PACK_W = D_MODEL // 2
PACK_S = PACK_W // LANES
VMEM_LIMIT = 56 * 1024 * 1024


def _sigmoid(v):
    return 1.0 / (1.0 + jnp.exp(-v))


def _rms(v):
    return v * lax.rsqrt(jnp.mean(v * v, axis=-1, keepdims=True) + EPS)


def _split_bf16(a):
    hi = a.astype(BF16)
    lo = (a - hi.astype(F32)).astype(BF16)
    return hi, lo


def _dot(a, b):
    return jnp.dot(a, b, preferred_element_type=F32)


def _dot_nt(a, b):
    return lax.dot_general(a, b, (((1,), (1,)), ((), ())), preferred_element_type=F32)


def _dot_tn(a, b):
    return lax.dot_general(a, b, (((0,), (0,)), ((), ())), preferred_element_type=F32)


def _dot3(a, b, dot):
    ah, al = _split_bf16(a)
    bh, bl = _split_bf16(b)
    return dot(ah, bh) + (dot(ah, bl) + dot(al, bh))


def _pack_rows(v):
    hi = lax.bitcast_convert_type(v[:, :PACK_W], U32) & jnp.uint32(0xFFFF0000)
    lo = lax.bitcast_convert_type(v[:, PACK_W:], U32) >> 16
    return hi | lo


def _unpack_rows(u):
    hi = lax.bitcast_convert_type(u & jnp.uint32(0xFFFF0000), F32)
    lo = lax.bitcast_convert_type(u << 16, F32)
    return jnp.concatenate([hi, lo], axis=1).astype(BF16)


def _load_flat(flat, r0, n):
    return jnp.concatenate(
        [flat[pl.ds(r0 * PACK_S + c, n, stride=PACK_S), :] for c in range(PACK_S)], axis=1)


def _store_flat(flat, r0, n, u):
    for c in range(PACK_S):
        flat[pl.ds(r0 * PACK_S + c, n, stride=PACK_S), :] = u[:, c * LANES:(c + 1) * LANES]


def _load_packed(ref3, r0, n):
    return _load_flat(ref3.reshape(ref3.shape[0] * PACK_S, LANES), r0, n)


def _store_packed(ref3, r0, n, u):
    _store_flat(ref3.reshape(ref3.shape[0] * PACK_S, LANES), r0, n, u)


def _ada_kernel(c_ref, w_ref, b_ref, o_ref):
    c = c_ref[...]
    s = c * _sigmoid(c)
    o_ref[...] = _dot3(s, w_ref[...], _dot) + b_ref[...]


def _ada_call(c, w_ada, b_ada):
    bsz, d = c.shape
    n = w_ada.shape[1]
    tn = 1536
    return pl.pallas_call(
        _ada_kernel,
        grid=(n // tn,),
        in_specs=[
            pl.BlockSpec((bsz, d), lambda i: (0, 0)),
            pl.BlockSpec((d, tn), lambda i: (0, i)),
            pl.BlockSpec((1, tn), lambda i: (0, i)),
        ],
        out_specs=pl.BlockSpec((bsz, tn), lambda i: (0, i)),
        out_shape=jax.ShapeDtypeStruct((bsz, n), F32),
        compiler_params=pltpu.CompilerParams(
            dimension_semantics=("arbitrary",), vmem_limit_bytes=VMEM_LIMIT),
        name="ada",
    )(c, w_ada, b_ada.reshape(1, n))


def _bucket_table():
    qi = np.arange(ATTN_BLOCK)[:, None]
    ki = np.arange(2 * ATTN_BLOCK)[None, :]
    dist = qi + ATTN_BLOCK - ki
    n = np.maximum(dist, 0)
    max_exact = NUM_BUCKETS // 2
    large = max_exact + (np.log(np.maximum(n, 1) / max_exact) / np.log(MAX_DISTANCE / max_exact)
                         * (NUM_BUCKETS - max_exact)).astype(np.int32)
    large = np.minimum(large, NUM_BUCKETS - 1)
    bkt = np.where(n < max_exact, n, large).astype(np.int32)
    band = (dist >= 0) & (dist < WINDOW)
    return np.ascontiguousarray(np.where(band, bkt, -1).astype(np.int32).T)


def _mixer_kernel(x_ref, mod_ref, win_ref, sinks_ref, relb_ref, bkt_ref, ang_ref, cw_ref, cb_ref,
                  lng_ref, lnb_ref, wout_ref, wrt_ref, wsg_ref, wsu_ref, wsd_ref,
                  xs1_ref, h2_ref, lgt_ref,
                  kv_ref, uext_ref, bias_ref, wrh_ref, wrl_ref):
    tm = x_ref.shape[1]
    nblk = tm // ATTN_BLOCK
    j = pl.program_id(1)

    @pl.when((pl.program_id(0) == 0) & (j == 0))
    def _build_bias():
        bkt = bkt_ref[...]
        for h in range(N_HEADS):
            acc = jnp.full(bkt.shape, -jnp.inf, F32)
            for b in range(NUM_BUCKETS):
                acc = jnp.where(bkt == b, relb_ref[b, h], acc)
            g = h % GROUP
            bias_ref[h // GROUP, :, g * ATTN_BLOCK:(g + 1) * ATTN_BLOCK] = acc
        w_hi, w_lo = _split_bf16(wrt_ref[...])
        wrh_ref[...] = w_hi
        wrl_ref[...] = w_lo

    @pl.when(j == 0)
    def _reset_history():
        kv_ref[...] = jnp.zeros(kv_ref.shape, F32)
        uext_ref[0:HALO, :] = jnp.zeros((HALO, CONV_CH), F32)
        uext_ref[HALO + tm:HALO + tm + SUBLANES, :] = jnp.zeros((SUBLANES, CONV_CH), F32)

    x = x_ref[0]
    mod = mod_ref[0]
    sh1, sc1, g1 = mod[0:1], mod[1:2], mod[2:3]
    sh2, sc2, g2 = mod[3:4], mod[4:5], mod[5:6]

    h = _rms(x) * (1.0 + sc1) + sh1
    hb = h.astype(BF16)
    n_qkv = Q_COLS + 2 * KV_COLS
    proj_conv = _dot(hb, win_ref[:, n_qkv:])
    proj = _dot(hb, win_ref[:, :n_qkv])

    qb = (proj[:, :Q_COLS] * (HEAD_DIM ** -0.5)).astype(BF16)
    kv_cur = proj[:, Q_COLS:Q_COLS + 2 * KV_COLS]
    kvb = jnp.concatenate([kv_ref[...], kv_cur], axis=0).astype(BF16)
    kv_ref[...] = kv_cur[tm - ATTN_BLOCK:, :]
    not_first = j > 0
    key = lax.broadcasted_iota(jnp.int32, (2 * ATTN_BLOCK, GROUP * ATTN_BLOCK), 0)
    attn_rows = []
    for i in range(nblk):
        r0, r1, r2 = i * ATTN_BLOCK, (i + 1) * ATTN_BLOCK, (i + 2) * ATTN_BLOCK
        heads_t = [None] * N_HEADS
        for kh in range(N_KV_HEADS):
            kpc = kvb[r0:r2, kh * HEAD_DIM:(kh + 1) * HEAD_DIM]
            vpc = kvb[r0:r2, KV_COLS + kh * HEAD_DIM:KV_COLS + (kh + 1) * HEAD_DIM]
            qg = jnp.concatenate(
                [qb[r0:r1, (kh * GROUP + g) * HEAD_DIM:(kh * GROUP + g + 1) * HEAD_DIM]
                 for g in range(GROUP)], axis=0)
            logit = _dot_nt(kpc, qg) + bias_ref[kh]
            if i == 0:
                logit = jnp.where((key >= ATTN_BLOCK) | not_first, logit, -jnp.inf)
            sink = jnp.concatenate(
                [jnp.full((1, ATTN_BLOCK), sinks_ref[kh * GROUP + g], F32) for g in range(GROUP)], axis=1)
            m = jnp.maximum(jnp.max(logit, axis=0, keepdims=True), sink)
            p = jnp.exp(logit - m)
            den = jnp.sum(p, axis=0, keepdims=True) + jnp.exp(sink - m)
            o_t = _dot_tn(vpc, p.astype(BF16)) / den
            for g in range(GROUP):
                heads_t[kh * GROUP + g] = o_t[:, g * ATTN_BLOCK:(g + 1) * ATTN_BLOCK]
        attn_t = jnp.concatenate(heads_t, axis=0)
        scale = lax.rsqrt(jnp.mean(attn_t * attn_t, axis=0, keepdims=True) + EPS)
        attn_rows.append((attn_t * scale).T)
    attn = jnp.concatenate(attn_rows, axis=0) * ang_ref[...]

    a = proj_conv[:, :CONV_CH]
    gt = proj_conv[:, CONV_CH:]
    uext_ref[HALO:HALO + tm, :] = a * _sigmoid(gt)
    cw = cw_ref[...]
    base = HALO - (CONV_WIDTH - 1)
    acc = None
    for res in range(SUBLANES):
        part = None
        for hi in range((HALO + SUBLANES - 1) // SUBLANES + 1):
            t = hi * SUBLANES + res - base
            if 0 <= t < CONV_WIDTH:
                term = cw[t:t + 1, :] * uext_ref[hi * SUBLANES:hi * SUBLANES + tm + SUBLANES, :]
                part = term if part is None else part + term
        part = part[res:res + tm, :]
        acc = part if acc is None else acc + part
    uext_ref[0:HALO, :] = uext_ref[tm:tm + HALO, :]
    cv = acc + cb_ref[...]
    mu = jnp.mean(cv, axis=-1, keepdims=True)
    var = jnp.mean(jnp.square(cv - mu), axis=-1, keepdims=True)
    cv = (cv - mu) * lax.rsqrt(var + EPS) * lng_ref[...] + lnb_ref[...]
    cv = cv * _sigmoid(cv)

    mixed = (_dot(attn.astype(BF16), wout_ref[0:Q_COLS, :])
             + _dot(cv.astype(BF16), wout_ref[Q_COLS:, :]))
    x1 = x + g1 * mixed

    h2 = _rms(x1) * (1.0 + sc2) + sh2
    h2_hi, h2_lo = _split_bf16(h2)
    w_hi, w_lo = wrh_ref[...], wrl_ref[...]
    lgt_ref[...] = _dot_nt(w_hi, h2_hi) + (_dot_nt(w_hi, h2_lo) + _dot_nt(w_lo, h2_hi))
    h2b = h2.astype(BF16)
    h2_ref[0] = h2b
    sg = _dot(h2b, wsg_ref[...])
    su = _dot(h2b, wsu_ref[...])
    shared = _dot(((sg * _sigmoid(sg)) * su).astype(BF16), wsd_ref[...])
    xs1_ref[0] = x1 + g2 * shared


def _mixer_call(x, mod3, w_in_b, sinks, rel_bias, ang, cw, cb, lng, lnb, w_out_b, w_rt, wsg_b, wsu_b, wsd_b):
    bsz, seq, d = x.shape
    tm = MIX_ROWS
    nj = seq // tm
    bkt = jnp.asarray(_bucket_table())
    full = lambda shape: pl.BlockSpec(shape, lambda b, j: (0,) * len(shape))
    smem = pl.BlockSpec(memory_space=pltpu.SMEM)
    return pl.pallas_call(
        _mixer_kernel,
        grid=(bsz, nj),
        in_specs=[
            pl.BlockSpec((1, tm, d), lambda b, j: (b, j, 0)),
            pl.BlockSpec((1, 6, d), lambda b, j: (b, 0, 0)),
            full((d, IN_COLS)),
            smem, smem,
            full((2 * ATTN_BLOCK, ATTN_BLOCK)),
            full((1, Q_COLS)),
            full((CONV_WIDTH, CONV_CH)),
            full((1, CONV_CH)), full((1, CONV_CH)), full((1, CONV_CH)),
            full((d, d)),
            full((N_EXPERTS, d)),
            full((d, EXPERT_HIDDEN)), full((d, EXPERT_HIDDEN)), full((EXPERT_HIDDEN, d)),
        ],
        out_specs=[
            pl.BlockSpec((1, tm, d), lambda b, j: (b, j, 0)),
            pl.BlockSpec((1, tm, d), lambda b, j: (b, j, 0)),
            pl.BlockSpec((N_EXPERTS, tm), lambda b, j: (0, b * nj + j)),
        ],
        out_shape=[
            jax.ShapeDtypeStruct((bsz, seq, d), F32),
            jax.ShapeDtypeStruct((bsz, seq, d), BF16),
            jax.ShapeDtypeStruct((N_EXPERTS, bsz * seq), F32),
        ],
        scratch_shapes=[
            pltpu.VMEM((ATTN_BLOCK, 2 * KV_COLS), F32),
            pltpu.VMEM((HALO + tm + SUBLANES, CONV_CH), F32),
            pltpu.VMEM((N_KV_HEADS, 2 * ATTN_BLOCK, GROUP * ATTN_BLOCK), F32),
            pltpu.VMEM((N_EXPERTS, d), BF16),
            pltpu.VMEM((N_EXPERTS, d), BF16),
        ],
        compiler_params=pltpu.CompilerParams(
            dimension_semantics=("arbitrary", "arbitrary"), vmem_limit_bytes=VMEM_LIMIT),
        name="mixer",
    )(x, mod3, w_in_b, sinks, rel_bias, bkt, ang, cw, cb, lng, lnb, w_out_b, w_rt, wsg_b, wsu_b, wsd_b)


def _route_kernel(lgt_ref, rb_ref, lp_ref, gw_ref, cnt_ref):
    tr = lgt_ref.shape[1]
    i = pl.program_id(0)

    @pl.when(i == 0)
    def _():
        cnt_ref[...] = jnp.zeros(cnt_ref.shape, F32)

    scores = _sigmoid(lgt_ref[...])
    sel = scores + rb_ref[...]
    neg = -jnp.inf

    sel3 = sel.reshape(N_GROUPS, GROUP_SIZE, tr)
    loc = lax.broadcasted_iota(I32, sel3.shape, 1)
    m1 = jnp.max(sel3, axis=1, keepdims=True)
    i1 = jnp.min(jnp.where(sel3 == m1, loc, GROUP_SIZE), axis=1, keepdims=True)
    m2 = jnp.max(jnp.where(loc == i1, neg, sel3), axis=1, keepdims=True)
    gscore = (m1 + m2).reshape(N_GROUPS, tr)

    gio = lax.broadcasted_iota(I32, gscore.shape, 0)
    gmask = jnp.zeros(gscore.shape, jnp.bool_)
    cur = gscore
    for _ in range(TOPK_GROUPS):
        m = jnp.max(cur, axis=0, keepdims=True)
        idx = jnp.min(jnp.where(cur == m, gio, N_GROUPS), axis=0, keepdims=True)
        pick = gio == idx
        gmask = gmask | pick
        cur = jnp.where(pick, neg, cur)
    emask = jnp.broadcast_to(gmask.reshape(N_GROUPS, 1, tr), sel3.shape).reshape(N_EXPERTS, tr)

    rio = lax.broadcasted_iota(I32, sel.shape, 0)
    cur = jnp.where(emask, sel, neg)
    picks, gsc = [], []
    for _ in range(TOP_K):
        m = jnp.max(cur, axis=0, keepdims=True)
        idx = jnp.min(jnp.where(cur == m, rio, N_EXPERTS), axis=0, keepdims=True)
        pick = rio == idx
        picks.append(pick)
        gsc.append(jnp.sum(jnp.where(pick, scores, 0.0), axis=0, keepdims=True))
        cur = jnp.where(pick, neg, cur)
    gsum = gsc[0]
    for k in range(1, TOP_K):
        gsum = gsum + gsc[k]
    gw_ref[...] = jnp.concatenate([g / gsum * ROUTED_SCALE for g in gsc], axis=0)

    chosen = picks[0]
    for k in range(1, TOP_K):
        chosen = chosen | picks[k]
    onehot = jnp.where(chosen, 1.0, 0.0)
    tri = (lax.broadcasted_iota(I32, (tr, tr), 0) < lax.broadcasted_iota(I32, (tr, tr), 1))
    before = _dot(onehot.astype(BF16), jnp.where(tri, 1.0, 0.0).astype(BF16))
    n = jnp.sum(onehot, axis=1, keepdims=True)
    low = (lax.broadcasted_iota(I32, (N_EXPERTS, N_EXPERTS), 1)
           < lax.broadcasted_iota(I32, (N_EXPERTS, N_EXPERTS), 0))
    start = _dot(jnp.where(low, 1.0, 0.0).astype(BF16),
                 jnp.broadcast_to(n, (N_EXPERTS, LANES)).astype(BF16))[:, 0:1]
    pos = start + before
    lp_ref[...] = jnp.concatenate(
        [jnp.sum(jnp.where(p, pos, 0.0), axis=0, keepdims=True) for p in picks], axis=0).astype(I32)

    lane = lax.broadcasted_iota(I32, cnt_ref.shape, 1)
    cnt_ref[...] += jnp.where(lane == i, n, 0.0)


def _route_call(lgt, router_bias):
    e, t = lgt.shape
    ntiles = t // TILE
    ntp = (ntiles + LANES - 1) // LANES * LANES
    return pl.pallas_call(
        _route_kernel,
        grid=(ntiles,),
        in_specs=[
            pl.BlockSpec((e, TILE), lambda i: (0, i)),
            pl.BlockSpec((e, 1), lambda i: (0, 0)),
        ],
        out_specs=[
            pl.BlockSpec((TOP_K, TILE), lambda i: (0, i)),
            pl.BlockSpec((TOP_K, TILE), lambda i: (0, i)),
            pl.BlockSpec((e, ntp), lambda i: (0, 0)),
        ],
        out_shape=[
            jax.ShapeDtypeStruct((TOP_K, t), I32),
            jax.ShapeDtypeStruct((TOP_K, t), F32),
            jax.ShapeDtypeStruct((e, ntp), F32),
        ],
        compiler_params=pltpu.CompilerParams(
            dimension_semantics=("arbitrary",), vmem_limit_bytes=VMEM_LIMIT),
        name="route",
    )(lgt, router_bias.reshape(e, 1))


_T_LOCAL, _T_LEN, _T_GLOBAL, _T_FIELDS = 0, 1, 2, 3
_T_STRIDE = 2 * N_EXPERTS
_T_EMPTY = _T_STRIDE * _T_FIELDS
_T_SIZE = _T_EMPTY + 2
LBUF_ROWS = LROWS + N_EXPERTS


def _segment_copies(tab_ref, half, local_ref, global_hbm, sem, to_global):
    def body(e, carry):
        at = half * N_EXPERTS + e
        n = tab_ref[0, 0, _T_LEN * _T_STRIDE + at]
        loc = local_ref.at[pl.ds(tab_ref[0, 0, _T_LOCAL * _T_STRIDE + at], n)]
        glo = global_hbm.at[pl.ds(tab_ref[0, 0, _T_GLOBAL * _T_STRIDE + at], n)]
        if to_global:
            pltpu.make_async_copy(loc, glo, sem).start()
        else:
            pltpu.make_async_copy(glo, loc, sem).start()
        return carry

    lax.fori_loop(0, N_EXPERTS, body, 0, unroll=8)


def _chunk_relative(lp_row, c):
    rel = lp_row - c * SORT_CHUNK
    inside = (rel >= 0) & (rel < SORT_CHUNK)
    return jnp.where(inside, rel, -1).astype(F32).astype(BF16)


def _chunk_row_ids():
    return lax.broadcasted_iota(I32, (SORT_CHUNK, TILE), 0).astype(F32).astype(BF16)


def _segment_wait(local_ref, global_hbm, sem, nempty):
    rows = LROWS + nempty
    pltpu.make_async_copy(global_hbm.at[pl.ds(0, rows)], local_ref.at[pl.ds(0, rows)], sem).wait()


TAIL_ROWS = EXPERT_ROWS + 2 * N_EXPERTS


def _dispatch_kernel(tab_ref, pad_ref, h2_ref, lp_ref, xs_hbm, xl0, xl1, zb, sem, pending):
    s = pl.program_id(0)
    ns = pl.num_programs(0)

    @pl.when(s == 0)
    def _zero_unassigned_rows():
        zb[...] = jnp.zeros(zb.shape, U32)
        used = pad_ref[2, 0]
        rest = xs_hbm.shape[0] - used
        cp = pltpu.make_async_copy(zb.at[pl.ds(0, rest)], xs_hbm.at[pl.ds(used, rest)], sem.at[2])
        cp.start()
        cp.wait()

        def pad_row(e):
            return pltpu.make_async_copy(zb.at[pl.ds(0, 1)], xs_hbm.at[pl.ds(pad_ref[0, e], 1)], sem.at[2])

        def start(e, carry):
            @pl.when(pad_ref[1, e] > 0)
            def _():
                pad_row(e).start()
            return carry

        def wait(e, carry):
            @pl.when(pad_ref[1, e] > 0)
            def _():
                pad_row(e).wait()
            return carry

        lax.fori_loop(0, N_EXPERTS, start, 0)
        lax.fori_loop(0, N_EXPERTS, wait, 0)

    @pl.when(s > 0)
    def _():
        _segment_wait(xl0, xs_hbm, sem.at[0], pending[0])
        _segment_wait(xl1, xs_hbm, sem.at[1], pending[1])

    for half, xl in ((0, xl0), (1, xl1)):
        rows = h2_ref[half * TILE:(half + 1) * TILE, :]
        lp = lp_ref[:, half * TILE:(half + 1) * TILE]
        j = _chunk_row_ids()
        for c in range(LROWS // SORT_CHUNK):
            onehot = jnp.zeros((SORT_CHUNK, TILE), BF16)
            for k in range(TOP_K):
                onehot = jnp.where(j == _chunk_relative(lp[k:k + 1, :], c), jnp.ones((), BF16), onehot)
            _store_packed(xl, c * SORT_CHUNK, SORT_CHUNK, _pack_rows(_dot(onehot, rows)))
        _segment_copies(tab_ref, half, xl, xs_hbm, sem.at[half], to_global=True)
        pending[half] = tab_ref[0, 0, _T_EMPTY + half]

    @pl.when(s == ns - 1)
    def _():
        _segment_wait(xl0, xs_hbm, sem.at[0], pending[0])
        _segment_wait(xl1, xs_hbm, sem.at[1], pending[1])


def _dispatch_call(tab, pad, h2, lp, nrows):
    t, d = h2.shape
    ns = t // (2 * TILE)
    return pl.pallas_call(
        _dispatch_kernel,
        grid=(ns,),
        in_specs=[
            pl.BlockSpec((1, 1, _T_SIZE), lambda s: (s, 0, 0), memory_space=pltpu.SMEM),
            pl.BlockSpec(memory_space=pltpu.SMEM),
            pl.BlockSpec((2 * TILE, d), lambda s: (s, 0)),
            pl.BlockSpec((TOP_K, 2 * TILE), lambda s: (0, s)),
        ],
        out_specs=pl.BlockSpec(memory_space=pl.ANY),
        out_shape=jax.ShapeDtypeStruct((nrows, PACK_S, LANES), U32),
        scratch_shapes=[
            pltpu.VMEM((LBUF_ROWS, PACK_S, LANES), U32),
            pltpu.VMEM((LBUF_ROWS, PACK_S, LANES), U32),
            pltpu.VMEM((TAIL_ROWS + N_EXPERTS, PACK_S, LANES), U32),
            pltpu.SemaphoreType.DMA((3,)),
            pltpu.SMEM((2,), I32),
        ],
        compiler_params=pltpu.CompilerParams(
            dimension_semantics=("arbitrary",), vmem_limit_bytes=VMEM_LIMIT),
        name="dispatch",
    )(tab, pad, h2, lp)


RING = 6
AHEAD = RING - 2


def _expert_kernel(first_ref, nchunk_ref, row_ref, len_ref, info_ref, wg_ref, wu_ref, wd_ref, xs_hbm, y_hbm,
                   xbuf, ybuf, isem, osem, wgb, wub, wdb):
    e = pl.program_id(0)
    last_step = e == pl.num_programs(0) - 1
    rows = EXPERT_ROWS
    total_chunks = info_ref[0]

    def slab(ref, row0, m):
        size = m * PACK_S if isinstance(m, int) else pl.multiple_of(m * PACK_S, SUBLANES)
        first = row0 * PACK_S if isinstance(row0, int) else pl.multiple_of(row0 * PACK_S, SUBLANES)
        return ref.at[pl.ds(first, size)]

    def fetch(g):
        slot = g % RING
        return pltpu.make_async_copy(slab(xs_hbm, row_ref[g], rows), slab(xbuf, slot * rows, rows), isem.at[slot])

    def flush(g):
        slot = g % RING
        m = len_ref[g]
        return pltpu.make_async_copy(slab(ybuf, slot * rows, m), slab(y_hbm, row_ref[g], m), osem.at[slot])

    def start_fetch(g):
        @pl.when(g < total_chunks)
        def _():
            fetch(g).start()

    def wait_flush(g):
        @pl.when(g >= 0)
        def _():
            flush(g).wait()

    def compute(g):
        slot = g % RING
        xb = _unpack_rows(_load_flat(xbuf, slot * rows, rows))
        gate = _dot(xb, wgb[...])
        up = _dot(xb, wub[...])
        hid = (gate * _sigmoid(gate)) * up
        y = _dot(hid.astype(BF16), wdb[...])
        _store_flat(ybuf, slot * rows, rows, _pack_rows(y.astype(BF16).astype(F32)))

    @pl.when(e == 0)
    def _prime():
        for a in range(AHEAD):
            start_fetch(a)

    wgb[...] = wg_ref[0].astype(BF16)
    wub[...] = wu_ref[0].astype(BF16)
    wdb[...] = wd_ref[0].astype(BF16)

    begin = first_ref[e]
    count = nchunk_ref[e]

    def pair(p, carry):
        g0 = begin + 2 * p
        g1 = g0 + 1
        start_fetch(g0 + AHEAD)
        start_fetch(g1 + AHEAD)
        fetch(g0).wait()
        fetch(g1).wait()
        wait_flush(g0 - RING)
        wait_flush(g1 - RING)
        compute(g0)
        compute(g1)
        flush(g0).start()
        flush(g1).start()
        return carry

    lax.fori_loop(0, count // 2, pair, 0)

    @pl.when(count % 2 == 1)
    def _odd_chunk():
        g = begin + count - 1
        start_fetch(g + AHEAD)
        fetch(g).wait()
        wait_flush(g - RING)
        compute(g)
        flush(g).start()

    @pl.when(last_step)
    def _drain():
        def wait(g, carry):
            flush(g).wait()
            return carry
        lax.fori_loop(jnp.maximum(total_chunks - RING, 0), total_chunks, wait, 0)

    @pl.when(last_step)
    def _define_unused_rows():
        used = info_ref[1]
        rest = y_hbm.shape[0] // PACK_S - used

        @pl.when(rest > 0)
        def _():
            ybuf[0:N_EXPERTS * PACK_S, :] = jnp.zeros((N_EXPERTS * PACK_S, LANES), U32)
            cp = pltpu.make_async_copy(slab(ybuf, 0, rest), slab(y_hbm, used, rest), osem.at[0])
            cp.start()
            cp.wait()


def _expert_call(first, nchunk, chunk_row, chunk_len, info, xs, w_gate, w_up, w_down, nrows):
    ne, d, hid = w_gate.shape
    rows = EXPERT_ROWS
    wsel = lambda e, *_: (e, 0, 0)
    grid_spec = pltpu.PrefetchScalarGridSpec(
        num_scalar_prefetch=5,
        grid=(ne,),
        in_specs=[
            pl.BlockSpec((1, d, hid), wsel),
            pl.BlockSpec((1, d, hid), wsel),
            pl.BlockSpec((1, hid, d), wsel),
            pl.BlockSpec(memory_space=pl.ANY),
        ],
        out_specs=pl.BlockSpec(memory_space=pl.ANY),
        scratch_shapes=[
            pltpu.VMEM((RING * rows * PACK_S, LANES), U32),
            pltpu.VMEM((RING * rows * PACK_S, LANES), U32),
            pltpu.SemaphoreType.DMA((RING,)),
            pltpu.SemaphoreType.DMA((RING,)),
            pltpu.VMEM((d, hid), BF16),
            pltpu.VMEM((d, hid), BF16),
            pltpu.VMEM((hid, d), BF16),
        ],
    )
    return pl.pallas_call(
        _expert_kernel,
        grid_spec=grid_spec,
        out_shape=jax.ShapeDtypeStruct((nrows * PACK_S, LANES), U32),
        compiler_params=pltpu.CompilerParams(
            dimension_semantics=("arbitrary",), vmem_limit_bytes=VMEM_LIMIT),
        name="experts",
    )(first, nchunk, chunk_row, chunk_len, info, w_gate, w_up, w_down,
      xs.reshape(-1, LANES)).reshape(nrows, PACK_S, LANES)


def _combine_kernel(tab_ref, tabn_ref, lp_ref, gw_ref, xs1_ref, mod_ref, fg_ref, y_hbm, o_ref, yl0, yl1, sem):
    s = pl.program_id(0)
    ns = pl.num_programs(0)

    @pl.when(s == 0)
    def _():
        _segment_copies(tab_ref, 0, yl0, y_hbm, sem.at[0], to_global=False)

    g2 = mod_ref[0][5:6]
    for half, yl in ((0, yl0), (1, yl1)):
        _segment_wait(yl, y_hbm, sem.at[half], tab_ref[0, 0, _T_EMPTY + half])
        if half == 0:
            _segment_copies(tab_ref, 1, yl1, y_hbm, sem.at[1], to_global=False)
        lp = lp_ref[:, half * TILE:(half + 1) * TILE]
        gw = gw_ref[:, half * TILE:(half + 1) * TILE]
        acc = jnp.zeros((TILE, D_MODEL), F32)
        gwb = [gw[k:k + 1, :].astype(BF16) for k in range(TOP_K)]
        j = _chunk_row_ids()
        for c in range(LROWS // SORT_CHUNK):
            wt = jnp.zeros((SORT_CHUNK, TILE), BF16)
            for k in range(TOP_K):
                wt = jnp.where(j == _chunk_relative(lp[k:k + 1, :], c), gwb[k], wt)
            yv = _unpack_rows(_load_packed(yl, c * SORT_CHUNK, SORT_CHUNK))
            acc = acc + _dot_tn(wt, yv)
        if half == 0:
            _segment_copies(tabn_ref, 0, yl0, y_hbm, sem.at[0], to_global=False)
        x2 = xs1_ref[half * TILE:(half + 1) * TILE, :] + g2 * acc
        o_ref[half * TILE:(half + 1) * TILE, :] = _rms(x2) * fg_ref[...]

    @pl.when(s == ns - 1)
    def _():
        _segment_wait(yl0, y_hbm, sem.at[0], tab_ref[0, 0, _T_EMPTY])


def _combine_call(tab, lp, gw, xs1, mod3, final_g, y, seq):
    t, d = xs1.shape
    ns = t // (2 * TILE)
    per_seq = seq // (2 * TILE)
    return pl.pallas_call(
        _combine_kernel,
        grid=(ns,),
        in_specs=[
            pl.BlockSpec((1, 1, _T_SIZE), lambda s: (s, 0, 0), memory_space=pltpu.SMEM),
            pl.BlockSpec((1, 1, _T_SIZE), lambda s: (jnp.minimum(s + 1, ns - 1), 0, 0),
                         memory_space=pltpu.SMEM),
            pl.BlockSpec((TOP_K, 2 * TILE), lambda s: (0, s)),
            pl.BlockSpec((TOP_K, 2 * TILE), lambda s: (0, s)),
            pl.BlockSpec((2 * TILE, d), lambda s: (s, 0)),
            pl.BlockSpec((1, 6, d), lambda s: (s // per_seq, 0, 0)),
            pl.BlockSpec((1, d), lambda s: (0, 0)),
            pl.BlockSpec(memory_space=pl.ANY),
        ],
        out_specs=pl.BlockSpec((2 * TILE, d), lambda s: (s, 0)),
        out_shape=jax.ShapeDtypeStruct((t, d), F32),
        scratch_shapes=[
            pltpu.VMEM((LBUF_ROWS, PACK_S, LANES), U32),
            pltpu.VMEM((LBUF_ROWS, PACK_S, LANES), U32),
            pltpu.SemaphoreType.DMA((2,)),
        ],
        compiler_params=pltpu.CompilerParams(
            dimension_semantics=("arbitrary",), vmem_limit_bytes=VMEM_LIMIT),
        name="combine",
    )(tab, tab, lp, gw, xs1, mod3, final_g.reshape(1, d), y)


def kernel(x, c, w_ada, b_ada, w_in, attn_sinks, rel_bias, attn_norm_g, conv_w, conv_b, conv_ln_g,
           conv_ln_b, w_out, w_router, router_bias, w_exp_gate, w_exp_up, w_exp_down, w_sh_gate,
           w_sh_up, w_sh_down, final_norm_g):
    bsz, seq, d = x.shape
    t = bsz * seq
    assert w_ada.shape[0] == 1 and d == D_MODEL
    assert seq % MIX_ROWS == 0 and seq % (2 * TILE) == 0

    mod3 = _ada_call(c, w_ada[0], b_ada[0]).reshape(bsz, 6, d)
    xs1, h2, lgt = _mixer_call(
        x, mod3, w_in[0].astype(BF16), attn_sinks[0], rel_bias,
        attn_norm_g[0].reshape(1, Q_COLS), conv_w[0], conv_b[0].reshape(1, CONV_CH),
        conv_ln_g[0].reshape(1, CONV_CH), conv_ln_b[0].reshape(1, CONV_CH),
        w_out[0].astype(BF16), w_router[0].T,
        w_sh_gate[0].astype(BF16), w_sh_up[0].astype(BF16), w_sh_down[0].astype(BF16))
    xs1 = xs1.reshape(t, d)
    h2 = h2.reshape(t, d)

    lp, gw, cnt = _route_call(lgt, router_bias[0])

    ntiles = t // TILE
    nassign = t * TOP_K
    n = cnt[:, :ntiles].T.astype(I32)
    local = jnp.cumsum(n, axis=1) - n
    earlier = jnp.cumsum(n, axis=0) - n
    total = jnp.sum(n, axis=0)
    region = (total + 1) // 2 * 2
    starts = jnp.cumsum(region) - region
    nalloc = nassign + N_EXPERTS
    used = jnp.sum(region)
    pad = jnp.stack([starts + total, region - total,
                     jnp.broadcast_to(used, (N_EXPERTS,))], axis=0)
    glob = starts[None, :] + earlier
    empty = n == 0
    length = jnp.maximum(n, 1)
    eid = jnp.arange(N_EXPERTS, dtype=I32)[None, :]
    parity = (jnp.arange(ntiles, dtype=I32) % 2)[:, None]
    spare_global = nalloc + EXPERT_ROWS + parity * N_EXPERTS + eid
    nempty = jnp.sum(empty.astype(I32), axis=1).reshape(ntiles // 2, 2)

    def table(loc, glo):
        fields = [f.reshape(ntiles // 2, _T_STRIDE) for f in (loc, length, glo)]
        return jnp.concatenate(fields + [nempty], axis=1).reshape(ntiles // 2, 1, _T_SIZE)

    tab_out = table(jnp.where(empty, 0, local), jnp.where(empty, spare_global, glob))
    tab_back = table(jnp.where(empty, LROWS + eid, local), jnp.where(empty, 0, glob))

    rows = EXPERT_ROWS
    nchunk = (total + rows - 1) // rows
    chunk_end = jnp.cumsum(nchunk)
    first = chunk_end - nchunk
    max_chunks = nassign // rows + N_EXPERTS
    gid = jnp.arange(max_chunks, dtype=I32)
    owner = jnp.minimum(jnp.sum((chunk_end[None, :] <= gid[:, None]).astype(I32), axis=1), N_EXPERTS - 1)
    pick = owner[:, None] == eid
    within = gid - jnp.sum(jnp.where(pick, first[None, :], 0), axis=1)
    chunk_row = jnp.sum(jnp.where(pick, starts[None, :], 0), axis=1) + within * rows
    left = jnp.sum(jnp.where(pick, region[None, :], 0), axis=1) - within * rows
    live = gid < chunk_end[-1]
    chunk_row = jnp.where(live, chunk_row, 0)
    chunk_len = jnp.where(live, jnp.clip(left, 2, rows), 2)
    info = jnp.stack([chunk_end[-1], used])

    xs = _dispatch_call(tab_out, pad, h2, lp, nalloc + TAIL_ROWS)
    y = _expert_call(first, nchunk, chunk_row, chunk_len, info, xs,
                     w_exp_gate[0], w_exp_up[0], w_exp_down[0], nalloc)
    out = _combine_call(tab_back, lp, gw, xs1, mod3, final_norm_g, y, seq)
    return out.reshape(bsz, seq, d)
```

```python
import numpy as np
import jax
import jax.numpy as jnp
from jax import lax
from jax.experimental import pallas as pl
from jax.experimental.pallas import tpu as pltpu

F32 = jnp.float32
BF16 = jnp.bfloat16
U32 = jnp.uint32
I32 = jnp.int32

D_MODEL = 1024
HEAD_DIM = 64
N_HEADS = 8
N_KV_HEADS = 2
GROUP = N_HEADS // N_KV_HEADS
Q_COLS = N_HEADS * HEAD_DIM
KV_COLS = N_KV_HEADS * HEAD_DIM
ATTN_BLOCK = 128
WINDOW = 128
NUM_BUCKETS = 32
MAX_DISTANCE = 128
CONV_CH = D_MODEL - Q_COLS
CONV_WIDTH = 31
IN_COLS = Q_COLS + 2 * KV_COLS + 2 * CONV_CH
N_EXPERTS = 256
TOP_K = 8
N_GROUPS = 8
GROUP_SIZE = N_EXPERTS // N_GROUPS
TOPK_GROUPS = 4
EXPERT_HIDDEN = 256
ROUTED_SCALE = 2.5
EPS = 1e-6

MIX_ROWS = 256
HALO = 32
TILE = 256
LROWS = TILE * TOP_K
SORT_CHUNK = 256
EXPERT_ROWS = 576
LANES = 128
SUBLANES = 8
PACK_W = D_MODEL // 2
PACK_S = PACK_W // LANES
VMEM_LIMIT = 56 * 1024 * 1024


def _sigmoid(v):
    return 1.0 / (1.0 + jnp.exp(-v))


def _rms(v):
    return v * lax.rsqrt(jnp.mean(v * v, axis=-1, keepdims=True) + EPS)


def _split_bf16(a):
    hi = a.astype(BF16)
    lo = (a - hi.astype(F32)).astype(BF16)
    return hi, lo


def _dot(a, b):
    return jnp.dot(a, b, preferred_element_type=F32)


def _dot_nt(a, b):
    return lax.dot_general(a, b, (((1,), (1,)), ((), ())), preferred_element_type=F32)


def _dot_tn(a, b):
    return lax.dot_general(a, b, (((0,), (0,)), ((), ())), preferred_element_type=F32)


def _dot3(a, b, dot):
    ah, al = _split_bf16(a)
    bh, bl = _split_bf16(b)
    return dot(ah, bh) + (dot(ah, bl) + dot(al, bh))


def _pack_rows(v):
    hi = lax.bitcast_convert_type(v[:, :PACK_W], U32) & jnp.uint32(0xFFFF0000)
    lo = lax.bitcast_convert_type(v[:, PACK_W:], U32) >> 16
    return hi | lo


def _unpack_rows(u):
    hi = lax.bitcast_convert_type(u & jnp.uint32(0xFFFF0000), F32)
    lo = lax.bitcast_convert_type(u << 16, F32)
    return jnp.concatenate([hi, lo], axis=1).astype(BF16)


def _load_flat(flat, r0, n):
    return jnp.concatenate(
        [flat[pl.ds(r0 * PACK_S + c, n, stride=PACK_S), :] for c in range(PACK_S)], axis=1)


def _store_flat(flat, r0, n, u):
    for c in range(PACK_S):
        flat[pl.ds(r0 * PACK_S + c, n, stride=PACK_S), :] = u[:, c * LANES:(c + 1) * LANES]


def _load_packed(ref3, r0, n):
    return _load_flat(ref3.reshape(ref3.shape[0] * PACK_S, LANES), r0, n)


def _store_packed(ref3, r0, n, u):
    _store_flat(ref3.reshape(ref3.shape[0] * PACK_S, LANES), r0, n, u)


def _ada_kernel(c_ref, w_ref, b_ref, o_ref):
    c = c_ref[...]
    s = c * _sigmoid(c)
    o_ref[...] = _dot3(s, w_ref[...], _dot) + b_ref[...]


def _ada_call(c, w_ada, b_ada):
    bsz, d = c.shape
    n = w_ada.shape[1]
    tn = 1536
    return pl.pallas_call(
        _ada_kernel,
        grid=(n // tn,),
        in_specs=[
            pl.BlockSpec((bsz, d), lambda i: (0, 0)),
            pl.BlockSpec((d, tn), lambda i: (0, i)),
            pl.BlockSpec((1, tn), lambda i: (0, i)),
        ],
        out_specs=pl.BlockSpec((bsz, tn), lambda i: (0, i)),
        out_shape=jax.ShapeDtypeStruct((bsz, n), F32),
        compiler_params=pltpu.CompilerParams(
            dimension_semantics=("arbitrary",), vmem_limit_bytes=VMEM_LIMIT),
        name="ada",
    )(c, w_ada, b_ada.reshape(1, n))


def _bucket_table():
    qi = np.arange(ATTN_BLOCK)[:, None]
    ki = np.arange(2 * ATTN_BLOCK)[None, :]
    dist = qi + ATTN_BLOCK - ki
    n = np.maximum(dist, 0)
    max_exact = NUM_BUCKETS // 2
    large = max_exact + (np.log(np.maximum(n, 1) / max_exact) / np.log(MAX_DISTANCE / max_exact)
                         * (NUM_BUCKETS - max_exact)).astype(np.int32)
    large = np.minimum(large, NUM_BUCKETS - 1)
    bkt = np.where(n < max_exact, n, large).astype(np.int32)
    band = (dist >= 0) & (dist < WINDOW)
    return np.ascontiguousarray(np.where(band, bkt, -1).astype(np.int32).T)


def _mixer_kernel(x_ref, mod_ref, win_ref, sinks_ref, relb_ref, bkt_ref, ang_ref, cw_ref, cb_ref,
                  lng_ref, lnb_ref, wout_ref, wrt_ref, wsg_ref, wsu_ref, wsd_ref,
                  xs1_ref, h2_ref, lgt_ref,
                  kv_ref, uext_ref, bias_ref, wrh_ref, wrl_ref):
    tm = x_ref.shape[1]
    nblk = tm // ATTN_BLOCK
    j = pl.program_id(1)

    @pl.when((pl.program_id(0) == 0) & (j == 0))
    def _build_bias():
        bkt = bkt_ref[...]
        for h in range(N_HEADS):
            acc = jnp.full(bkt.shape, -jnp.inf, F32)
            for b in range(NUM_BUCKETS):
                acc = jnp.where(bkt == b, relb_ref[b, h], acc)
            g = h % GROUP
            bias_ref[h // GROUP, :, g * ATTN_BLOCK:(g + 1) * ATTN_BLOCK] = acc
        w_hi, w_lo = _split_bf16(wrt_ref[...])
        wrh_ref[...] = w_hi
        wrl_ref[...] = w_lo

    @pl.when(j == 0)
    def _reset_history():
        kv_ref[...] = jnp.zeros(kv_ref.shape, F32)
        uext_ref[0:HALO, :] = jnp.zeros((HALO, CONV_CH), F32)
        uext_ref[HALO + tm:HALO + tm + SUBLANES, :] = jnp.zeros((SUBLANES, CONV_CH), F32)

    x = x_ref[0]
    mod = mod_ref[0]
    sh1, sc1, g1 = mod[0:1], mod[1:2], mod[2:3]
    sh2, sc2, g2 = mod[3:4], mod[4:5], mod[5:6]

    h = _rms(x) * (1.0 + sc1) + sh1
    hb = h.astype(BF16)
    n_qkv = Q_COLS + 2 * KV_COLS
    proj_conv = _dot(hb, win_ref[:, n_qkv:])
    proj = _dot(hb, win_ref[:, :n_qkv])

    qb = (proj[:, :Q_COLS] * (HEAD_DIM ** -0.5)).astype(BF16)
    kv_cur = proj[:, Q_COLS:Q_COLS + 2 * KV_COLS]
    kvb = jnp.concatenate([kv_ref[...], kv_cur], axis=0).astype(BF16)
    kv_ref[...] = kv_cur[tm - ATTN_BLOCK:, :]
    not_first = j > 0
    key = lax.broadcasted_iota(jnp.int32, (2 * ATTN_BLOCK, GROUP * ATTN_BLOCK), 0)
    attn_rows = []
    for i in range(nblk):
        r0, r1, r2 = i * ATTN_BLOCK, (i + 1) * ATTN_BLOCK, (i + 2) * ATTN_BLOCK
        heads_t = [None] * N_HEADS
        for kh in range(N_KV_HEADS):
            kpc = kvb[r0:r2, kh * HEAD_DIM:(kh + 1) * HEAD_DIM]
            vpc = kvb[r0:r2, KV_COLS + kh * HEAD_DIM:KV_COLS + (kh + 1) * HEAD_DIM]
            qg = jnp.concatenate(
                [qb[r0:r1, (kh * GROUP + g) * HEAD_DIM:(kh * GROUP + g + 1) * HEAD_DIM]
                 for g in range(GROUP)], axis=0)
            logit = _dot_nt(kpc, qg) + bias_ref[kh]
            if i == 0:
                logit = jnp.where((key >= ATTN_BLOCK) | not_first, logit, -jnp.inf)
            sink = jnp.concatenate(
                [jnp.full((1, ATTN_BLOCK), sinks_ref[kh * GROUP + g], F32) for g in range(GROUP)], axis=1)
            m = jnp.maximum(jnp.max(logit, axis=0, keepdims=True), sink)
            p = jnp.exp(logit - m)
            den = jnp.sum(p, axis=0, keepdims=True) + jnp.exp(sink - m)
            o_t = _dot_tn(vpc, p.astype(BF16)) / den
            for g in range(GROUP):
                heads_t[kh * GROUP + g] = o_t[:, g * ATTN_BLOCK:(g + 1) * ATTN_BLOCK]
        attn_t = jnp.concatenate(heads_t, axis=0)
        scale = lax.rsqrt(jnp.mean(attn_t * attn_t, axis=0, keepdims=True) + EPS)
        attn_rows.append((attn_t * scale).T)
    attn = jnp.concatenate(attn_rows, axis=0) * ang_ref[...]

    a = proj_conv[:, :CONV_CH]
    gt = proj_conv[:, CONV_CH:]
    uext_ref[HALO:HALO + tm, :] = a * _sigmoid(gt)
    cw = cw_ref[...]
    base = HALO - (CONV_WIDTH - 1)
    acc = None
    for res in range(SUBLANES):
        part = None
        for hi in range((HALO + SUBLANES - 1) // SUBLANES + 1):
            t = hi * SUBLANES + res - base
            if 0 <= t < CONV_WIDTH:
                term = cw[t:t + 1, :] * uext_ref[hi * SUBLANES:hi * SUBLANES + tm + SUBLANES, :]
                part = term if part is None else part + term
        part = part[res:res + tm, :]
        acc = part if acc is None else acc + part
    uext_ref[0:HALO, :] = uext_ref[tm:tm + HALO, :]
    cv = acc + cb_ref[...]
    mu = jnp.mean(cv, axis=-1, keepdims=True)
    var = jnp.mean(jnp.square(cv - mu), axis=-1, keepdims=True)
    cv = (cv - mu) * lax.rsqrt(var + EPS) * lng_ref[...] + lnb_ref[...]
    cv = cv * _sigmoid(cv)

    mixed = (_dot(attn.astype(BF16), wout_ref[0:Q_COLS, :])
             + _dot(cv.astype(BF16), wout_ref[Q_COLS:, :]))
    x1 = x + g1 * mixed

    h2 = _rms(x1) * (1.0 + sc2) + sh2
    h2_hi, h2_lo = _split_bf16(h2)
    w_hi, w_lo = wrh_ref[...], wrl_ref[...]
    lgt_ref[...] = _dot_nt(w_hi, h2_hi) + (_dot_nt(w_hi, h2_lo) + _dot_nt(w_lo, h2_hi))
    h2b = h2.astype(BF16)
    h2_ref[0] = h2b
    sg = _dot(h2b, wsg_ref[...])
    su = _dot(h2b, wsu_ref[...])
    shared = _dot(((sg * _sigmoid(sg)) * su).astype(BF16), wsd_ref[...])
    xs1_ref[0] = x1 + g2 * shared


def _mixer_call(x, mod3, w_in_b, sinks, rel_bias, ang, cw, cb, lng, lnb, w_out_b, w_rt, wsg_b, wsu_b, wsd_b):
    bsz, seq, d = x.shape
    tm = MIX_ROWS
    nj = seq // tm
    bkt = jnp.asarray(_bucket_table())
    full = lambda shape: pl.BlockSpec(shape, lambda b, j: (0,) * len(shape))
    smem = pl.BlockSpec(memory_space=pltpu.SMEM)
    return pl.pallas_call(
        _mixer_kernel,
        grid=(bsz, nj),
        in_specs=[
            pl.BlockSpec((1, tm, d), lambda b, j: (b, j, 0)),
            pl.BlockSpec((1, 6, d), lambda b, j: (b, 0, 0)),
            full((d, IN_COLS)),
            smem, smem,
            full((2 * ATTN_BLOCK, ATTN_BLOCK)),
            full((1, Q_COLS)),
            full((CONV_WIDTH, CONV_CH)),
            full((1, CONV_CH)), full((1, CONV_CH)), full((1, CONV_CH)),
            full((d, d)),
            full((N_EXPERTS, d)),
            full((d, EXPERT_HIDDEN)), full((d, EXPERT_HIDDEN)), full((EXPERT_HIDDEN, d)),
        ],
        out_specs=[
            pl.BlockSpec((1, tm, d), lambda b, j: (b, j, 0)),
            pl.BlockSpec((1, tm, d), lambda b, j: (b, j, 0)),
            pl.BlockSpec((N_EXPERTS, tm), lambda b, j: (0, b * nj + j)),
        ],
        out_shape=[
            jax.ShapeDtypeStruct((bsz, seq, d), F32),
            jax.ShapeDtypeStruct((bsz, seq, d), BF16),
            jax.ShapeDtypeStruct((N_EXPERTS, bsz * seq), F32),
        ],
        scratch_shapes=[
            pltpu.VMEM((ATTN_BLOCK, 2 * KV_COLS), F32),
            pltpu.VMEM((HALO + tm + SUBLANES, CONV_CH), F32),
            pltpu.VMEM((N_KV_HEADS, 2 * ATTN_BLOCK, GROUP * ATTN_BLOCK), F32),
            pltpu.VMEM((N_EXPERTS, d), BF16),
            pltpu.VMEM((N_EXPERTS, d), BF16),
        ],
        compiler_params=pltpu.CompilerParams(
            dimension_semantics=("arbitrary", "arbitrary"), vmem_limit_bytes=VMEM_LIMIT),
        name="mixer",
    )(x, mod3, w_in_b, sinks, rel_bias, bkt, ang, cw, cb, lng, lnb, w_out_b, w_rt, wsg_b, wsu_b, wsd_b)


def _route_kernel(lgt_ref, rb_ref, lp_ref, gw_ref, cnt_ref):
    tr = lgt_ref.shape[1]
    i = pl.program_id(0)

    @pl.when(i == 0)
    def _():
        cnt_ref[...] = jnp.zeros(cnt_ref.shape, F32)

    scores = _sigmoid(lgt_ref[...])
    sel = scores + rb_ref[...]
    neg = -jnp.inf

    sel3 = sel.reshape(N_GROUPS, GROUP_SIZE, tr)
    loc = lax.broadcasted_iota(I32, sel3.shape, 1)
    m1 = jnp.max(sel3, axis=1, keepdims=True)
    i1 = jnp.min(jnp.where(sel3 == m1, loc, GROUP_SIZE), axis=1, keepdims=True)
    m2 = jnp.max(jnp.where(loc == i1, neg, sel3), axis=1, keepdims=True)
    gscore = (m1 + m2).reshape(N_GROUPS, tr)

    gio = lax.broadcasted_iota(I32, gscore.shape, 0)
    gmask = jnp.zeros(gscore.shape, jnp.bool_)
    cur = gscore
    for _ in range(TOPK_GROUPS):
        m = jnp.max(cur, axis=0, keepdims=True)
        idx = jnp.min(jnp.where(cur == m, gio, N_GROUPS), axis=0, keepdims=True)
        pick = gio == idx
        gmask = gmask | pick
        cur = jnp.where(pick, neg, cur)
    emask = jnp.broadcast_to(gmask.reshape(N_GROUPS, 1, tr), sel3.shape).reshape(N_EXPERTS, tr)

    rio = lax.broadcasted_iota(I32, sel.shape, 0)
    cur = jnp.where(emask, sel, neg)
    picks, gsc = [], []
    for _ in range(TOP_K):
        m = jnp.max(cur, axis=0, keepdims=True)
        idx = jnp.min(jnp.where(cur == m, rio, N_EXPERTS), axis=0, keepdims=True)
        pick = rio == idx
        picks.append(pick)
        gsc.append(jnp.sum(jnp.where(pick, scores, 0.0), axis=0, keepdims=True))
        cur = jnp.where(pick, neg, cur)
    gsum = gsc[0]
    for k in range(1, TOP_K):
        gsum = gsum + gsc[k]
    gw_ref[...] = jnp.concatenate([g / gsum * ROUTED_SCALE for g in gsc], axis=0)

    chosen = picks[0]
    for k in range(1, TOP_K):
        chosen = chosen | picks[k]
    onehot = jnp.where(chosen, 1.0, 0.0)
    tri = (lax.broadcasted_iota(I32, (tr, tr), 0) < lax.broadcasted_iota(I32, (tr, tr), 1))
    before = _dot(onehot.astype(BF16), jnp.where(tri, 1.0, 0.0).astype(BF16))
    n = jnp.sum(onehot, axis=1, keepdims=True)
    low = (lax.broadcasted_iota(I32, (N_EXPERTS, N_EXPERTS), 1)
           < lax.broadcasted_iota(I32, (N_EXPERTS, N_EXPERTS), 0))
    start = _dot(jnp.where(low, 1.0, 0.0).astype(BF16),
                 jnp.broadcast_to(n, (N_EXPERTS, LANES)).astype(BF16))[:, 0:1]
    pos = start + before
    lp_ref[...] = jnp.concatenate(
        [jnp.sum(jnp.where(p, pos, 0.0), axis=0, keepdims=True) for p in picks], axis=0).astype(I32)

    lane = lax.broadcasted_iota(I32, cnt_ref.shape, 1)
    cnt_ref[...] += jnp.where(lane == i, n, 0.0)


def _route_call(lgt, router_bias):
    e, t = lgt.shape
    ntiles = t // TILE
    ntp = (ntiles + LANES - 1) // LANES * LANES
    return pl.pallas_call(
        _route_kernel,
        grid=(ntiles,),
        in_specs=[
            pl.BlockSpec((e, TILE), lambda i: (0, i)),
            pl.BlockSpec((e, 1), lambda i: (0, 0)),
        ],
        out_specs=[
            pl.BlockSpec((TOP_K, TILE), lambda i: (0, i)),
            pl.BlockSpec((TOP_K, TILE), lambda i: (0, i)),
            pl.BlockSpec((e, ntp), lambda i: (0, 0)),
        ],
        out_shape=[
            jax.ShapeDtypeStruct((TOP_K, t), I32),
            jax.ShapeDtypeStruct((TOP_K, t), F32),
            jax.ShapeDtypeStruct((e, ntp), F32),
        ],
        compiler_params=pltpu.CompilerParams(
            dimension_semantics=("arbitrary",), vmem_limit_bytes=VMEM_LIMIT),
        name="route",
    )(lgt, router_bias.reshape(e, 1))


_T_LOCAL, _T_LEN, _T_GLOBAL, _T_FIELDS = 0, 1, 2, 3
_T_STRIDE = 2 * N_EXPERTS
_T_EMPTY = _T_STRIDE * _T_FIELDS
_T_SIZE = _T_EMPTY + 2
LBUF_ROWS = LROWS + N_EXPERTS


def _segment_copies(tab_ref, half, local_ref, global_hbm, sem, to_global):
    def body(e, carry):
        at = half * N_EXPERTS + e
        n = tab_ref[0, 0, _T_LEN * _T_STRIDE + at]
        loc = local_ref.at[pl.ds(tab_ref[0, 0, _T_LOCAL * _T_STRIDE + at], n)]
        glo = global_hbm.at[pl.ds(tab_ref[0, 0, _T_GLOBAL * _T_STRIDE + at], n)]
        if to_global:
            pltpu.make_async_copy(loc, glo, sem).start()
        else:
            pltpu.make_async_copy(glo, loc, sem).start()
        return carry

    lax.fori_loop(0, N_EXPERTS, body, 0, unroll=8)


def _chunk_relative(lp_row, c):
    rel = lp_row - c * SORT_CHUNK
    inside = (rel >= 0) & (rel < SORT_CHUNK)
    return jnp.where(inside, rel, -1).astype(F32).astype(BF16)


def _chunk_row_ids():
    return lax.broadcasted_iota(I32, (SORT_CHUNK, TILE), 0).astype(F32).astype(BF16)


def _segment_wait(local_ref, global_hbm, sem, nempty):
    rows = LROWS + nempty
    pltpu.make_async_copy(global_hbm.at[pl.ds(0, rows)], local_ref.at[pl.ds(0, rows)], sem).wait()


TAIL_ROWS = EXPERT_ROWS + 2 * N_EXPERTS


def _dispatch_kernel(tab_ref, pad_ref, h2_ref, lp_ref, xs_hbm, xl0, xl1, zb, sem, pending):
    s = pl.program_id(0)
    ns = pl.num_programs(0)

    @pl.when(s == 0)
    def _zero_unassigned_rows():
        zb[...] = jnp.zeros(zb.shape, U32)
        used = pad_ref[2, 0]
        rest = xs_hbm.shape[0] - used
        cp = pltpu.make_async_copy(zb.at[pl.ds(0, rest)], xs_hbm.at[pl.ds(used, rest)], sem.at[2])
        cp.start()
        cp.wait()

        def pad_row(e):
            return pltpu.make_async_copy(zb.at[pl.ds(0, 1)], xs_hbm.at[pl.ds(pad_ref[0, e], 1)], sem.at[2])

        def start(e, carry):
            @pl.when(pad_ref[1, e] > 0)
            def _():
                pad_row(e).start()
            return carry

        def wait(e, carry):
            @pl.when(pad_ref[1, e] > 0)
            def _():
                pad_row(e).wait()
            return carry

        lax.fori_loop(0, N_EXPERTS, start, 0)
        lax.fori_loop(0, N_EXPERTS, wait, 0)

    @pl.when(s > 0)
    def _():
        _segment_wait(xl0, xs_hbm, sem.at[0], pending[0])
        _segment_wait(xl1, xs_hbm, sem.at[1], pending[1])

    for half, xl in ((0, xl0), (1, xl1)):
        rows = h2_ref[half * TILE:(half + 1) * TILE, :]
        lp = lp_ref[:, half * TILE:(half + 1) * TILE]
        j = _chunk_row_ids()
        for c in range(LROWS // SORT_CHUNK):
            onehot = jnp.zeros((SORT_CHUNK, TILE), BF16)
            for k in range(TOP_K):
                onehot = jnp.where(j == _chunk_relative(lp[k:k + 1, :], c), jnp.ones((), BF16), onehot)
            _store_packed(xl, c * SORT_CHUNK, SORT_CHUNK, _pack_rows(_dot(onehot, rows)))
        _segment_copies(tab_ref, half, xl, xs_hbm, sem.at[half], to_global=True)
        pending[half] = tab_ref[0, 0, _T_EMPTY + half]

    @pl.when(s == ns - 1)
    def _():
        _segment_wait(xl0, xs_hbm, sem.at[0], pending[0])
        _segment_wait(xl1, xs_hbm, sem.at[1], pending[1])


def _dispatch_call(tab, pad, h2, lp, nrows):
    t, d = h2.shape
    ns = t // (2 * TILE)
    return pl.pallas_call(
        _dispatch_kernel,
        grid=(ns,),
        in_specs=[
            pl.BlockSpec((1, 1, _T_SIZE), lambda s: (s, 0, 0), memory_space=pltpu.SMEM),
            pl.BlockSpec(memory_space=pltpu.SMEM),
            pl.BlockSpec((2 * TILE, d), lambda s: (s, 0)),
            pl.BlockSpec((TOP_K, 2 * TILE), lambda s: (0, s)),
        ],
        out_specs=pl.BlockSpec(memory_space=pl.ANY),
        out_shape=jax.ShapeDtypeStruct((nrows, PACK_S, LANES), U32),
        scratch_shapes=[
            pltpu.VMEM((LBUF_ROWS, PACK_S, LANES), U32),
            pltpu.VMEM((LBUF_ROWS, PACK_S, LANES), U32),
            pltpu.VMEM((TAIL_ROWS + N_EXPERTS, PACK_S, LANES), U32),
            pltpu.SemaphoreType.DMA((3,)),
            pltpu.SMEM((2,), I32),
        ],
        compiler_params=pltpu.CompilerParams(
            dimension_semantics=("arbitrary",), vmem_limit_bytes=VMEM_LIMIT),
        name="dispatch",
    )(tab, pad, h2, lp)


RING = 6
AHEAD = RING - 2


def _expert_kernel(first_ref, nchunk_ref, row_ref, len_ref, info_ref, wg_ref, wu_ref, wd_ref, xs_hbm, y_hbm,
                   xbuf, ybuf, isem, osem, wgb, wub, wdb):
    e = pl.program_id(0)
    last_step = e == pl.num_programs(0) - 1
    rows = EXPERT_ROWS
    total_chunks = info_ref[0]

    def slab(ref, row0, m):
        size = m * PACK_S if isinstance(m, int) else pl.multiple_of(m * PACK_S, SUBLANES)
        first = row0 * PACK_S if isinstance(row0, int) else pl.multiple_of(row0 * PACK_S, SUBLANES)
        return ref.at[pl.ds(first, size)]

    def fetch(g):
        slot = g % RING
        m = len_ref[g]
        return pltpu.make_async_copy(slab(xs_hbm, row_ref[g], m), slab(xbuf, slot * rows, m), isem.at[slot])

    def flush(g):
        slot = g % RING
        m = len_ref[g]
        return pltpu.make_async_copy(slab(ybuf, slot * rows, m), slab(y_hbm, row_ref[g], m), osem.at[slot])

    def start_fetch(g):
        @pl.when(g < total_chunks)
        def _():
            fetch(g).start()

    def wait_flush(g):
        @pl.when(g >= 0)
        def _():
            flush(g).wait()

    def compute(g):
        slot = g % RING
        xb = _unpack_rows(_load_flat(xbuf, slot * rows, rows))
        gate = _dot(xb, wgb[...])
        up = _dot(xb, wub[...])
        hid = (gate * _sigmoid(gate)) * up
        y = _dot(hid.astype(BF16), wdb[...])
        _store_flat(ybuf, slot * rows, rows, _pack_rows(y.astype(BF16).astype(F32)))

    @pl.when(e == 0)
    def _prime():
        xbuf[...] = jnp.zeros(xbuf.shape, U32)
        for a in range(AHEAD):
            start_fetch(a)

    wgb[...] = wg_ref[0].astype(BF16)
    wub[...] = wu_ref[0].astype(BF16)
    wdb[...] = wd_ref[0].astype(BF16)

    begin = first_ref[e]
    count = nchunk_ref[e]

    def pair(p, carry):
        g0 = begin + 2 * p
        g1 = g0 + 1
        start_fetch(g0 + AHEAD)
        start_fetch(g1 + AHEAD)
        fetch(g0).wait()
        fetch(g1).wait()
        wait_flush(g0 - RING)
        wait_flush(g1 - RING)
        compute(g0)
        compute(g1)
        flush(g0).start()
        flush(g1).start()
        return carry

    lax.fori_loop(0, count // 2, pair, 0)

    @pl.when(count % 2 == 1)
    def _odd_chunk():
        g = begin + count - 1
        start_fetch(g + AHEAD)
        fetch(g).wait()
        wait_flush(g - RING)
        compute(g)
        flush(g).start()

    @pl.when(last_step)
    def _drain():
        def wait(g, carry):
            flush(g).wait()
            return carry
        lax.fori_loop(jnp.maximum(total_chunks - RING, 0), total_chunks, wait, 0)

    @pl.when(last_step)
    def _define_unused_rows():
        used = info_ref[1]
        rest = y_hbm.shape[0] // PACK_S - used

        @pl.when(rest > 0)
        def _():
            ybuf[0:N_EXPERTS * PACK_S, :] = jnp.zeros((N_EXPERTS * PACK_S, LANES), U32)
            cp = pltpu.make_async_copy(slab(ybuf, 0, rest), slab(y_hbm, used, rest), osem.at[0])
            cp.start()
            cp.wait()


def _expert_call(first, nchunk, chunk_row, chunk_len, info, xs, w_gate, w_up, w_down, nrows):
    ne, d, hid = w_gate.shape
    rows = EXPERT_ROWS
    wsel = lambda e, *_: (e, 0, 0)
    grid_spec = pltpu.PrefetchScalarGridSpec(
        num_scalar_prefetch=5,
        grid=(ne,),
        in_specs=[
            pl.BlockSpec((1, d, hid), wsel),
            pl.BlockSpec((1, d, hid), wsel),
            pl.BlockSpec((1, hid, d), wsel),
            pl.BlockSpec(memory_space=pl.ANY),
        ],
        out_specs=pl.BlockSpec(memory_space=pl.ANY),
        scratch_shapes=[
            pltpu.VMEM((RING * rows * PACK_S, LANES), U32),
            pltpu.VMEM((RING * rows * PACK_S, LANES), U32),
            pltpu.SemaphoreType.DMA((RING,)),
            pltpu.SemaphoreType.DMA((RING,)),
            pltpu.VMEM((d, hid), BF16),
            pltpu.VMEM((d, hid), BF16),
            pltpu.VMEM((hid, d), BF16),
        ],
    )
    return pl.pallas_call(
        _expert_kernel,
        grid_spec=grid_spec,
        out_shape=jax.ShapeDtypeStruct((nrows * PACK_S, LANES), U32),
        compiler_params=pltpu.CompilerParams(
            dimension_semantics=("arbitrary",), vmem_limit_bytes=VMEM_LIMIT),
        name="experts",
    )(first, nchunk, chunk_row, chunk_len, info, w_gate, w_up, w_down,
      xs.reshape(-1, LANES)).reshape(nrows, PACK_S, LANES)


def _combine_kernel(tab_ref, tabn_ref, lp_ref, gw_ref, xs1_ref, mod_ref, fg_ref, y_hbm, o_ref, yl0, yl1, sem):
    s = pl.program_id(0)
    ns = pl.num_programs(0)

    @pl.when(s == 0)
    def _():
        _segment_copies(tab_ref, 0, yl0, y_hbm, sem.at[0], to_global=False)

    g2 = mod_ref[0][5:6]
    for half, yl in ((0, yl0), (1, yl1)):
        _segment_wait(yl, y_hbm, sem.at[half], tab_ref[0, 0, _T_EMPTY + half])
        if half == 0:
            _segment_copies(tab_ref, 1, yl1, y_hbm, sem.at[1], to_global=False)
        lp = lp_ref[:, half * TILE:(half + 1) * TILE]
        gw = gw_ref[:, half * TILE:(half + 1) * TILE]
        acc = jnp.zeros((TILE, D_MODEL), F32)
        gwb = [gw[k:k + 1, :].astype(BF16) for k in range(TOP_K)]
        j = _chunk_row_ids()
        for c in range(LROWS // SORT_CHUNK):
            wt = jnp.zeros((SORT_CHUNK, TILE), BF16)
            for k in range(TOP_K):
                wt = jnp.where(j == _chunk_relative(lp[k:k + 1, :], c), gwb[k], wt)
            yv = _unpack_rows(_load_packed(yl, c * SORT_CHUNK, SORT_CHUNK))
            acc = acc + _dot_tn(wt, yv)
        if half == 0:
            _segment_copies(tabn_ref, 0, yl0, y_hbm, sem.at[0], to_global=False)
        x2 = xs1_ref[half * TILE:(half + 1) * TILE, :] + g2 * acc
        o_ref[half * TILE:(half + 1) * TILE, :] = _rms(x2) * fg_ref[...]

    @pl.when(s == ns - 1)
    def _():
        _segment_wait(yl0, y_hbm, sem.at[0], tab_ref[0, 0, _T_EMPTY])


def _combine_call(tab, lp, gw, xs1, mod3, final_g, y, seq):
    t, d = xs1.shape
    ns = t // (2 * TILE)
    per_seq = seq // (2 * TILE)
    return pl.pallas_call(
        _combine_kernel,
        grid=(ns,),
        in_specs=[
            pl.BlockSpec((1, 1, _T_SIZE), lambda s: (s, 0, 0), memory_space=pltpu.SMEM),
            pl.BlockSpec((1, 1, _T_SIZE), lambda s: (jnp.minimum(s + 1, ns - 1), 0, 0),
                         memory_space=pltpu.SMEM),
            pl.BlockSpec((TOP_K, 2 * TILE), lambda s: (0, s)),
            pl.BlockSpec((TOP_K, 2 * TILE), lambda s: (0, s)),
            pl.BlockSpec((2 * TILE, d), lambda s: (s, 0)),
            pl.BlockSpec((1, 6, d), lambda s: (s // per_seq, 0, 0)),
            pl.BlockSpec((1, d), lambda s: (0, 0)),
            pl.BlockSpec(memory_space=pl.ANY),
        ],
        out_specs=pl.BlockSpec((2 * TILE, d), lambda s: (s, 0)),
        out_shape=jax.ShapeDtypeStruct((t, d), F32),
        scratch_shapes=[
            pltpu.VMEM((LBUF_ROWS, PACK_S, LANES), U32),
            pltpu.VMEM((LBUF_ROWS, PACK_S, LANES), U32),
            pltpu.SemaphoreType.DMA((2,)),
        ],
        compiler_params=pltpu.CompilerParams(
            dimension_semantics=("arbitrary",), vmem_limit_bytes=VMEM_LIMIT),
        name="combine",
    )(tab, tab, lp, gw, xs1, mod3, final_g.reshape(1, d), y)


def kernel(x, c, w_ada, b_ada, w_in, attn_sinks, rel_bias, attn_norm_g, conv_w, conv_b, conv_ln_g,
           conv_ln_b, w_out, w_router, router_bias, w_exp_gate, w_exp_up, w_exp_down, w_sh_gate,
           w_sh_up, w_sh_down, final_norm_g):
    bsz, seq, d = x.shape
    t = bsz * seq
    assert w_ada.shape[0] == 1 and d == D_MODEL
    assert seq % MIX_ROWS == 0 and seq % (2 * TILE) == 0

    mod3 = _ada_call(c, w_ada[0], b_ada[0]).reshape(bsz, 6, d)
    xs1, h2, lgt = _mixer_call(
        x, mod3, w_in[0].astype(BF16), attn_sinks[0], rel_bias,
        attn_norm_g[0].reshape(1, Q_COLS), conv_w[0], conv_b[0].reshape(1, CONV_CH),
        conv_ln_g[0].reshape(1, CONV_CH), conv_ln_b[0].reshape(1, CONV_CH),
        w_out[0].astype(BF16), w_router[0].T,
        w_sh_gate[0].astype(BF16), w_sh_up[0].astype(BF16), w_sh_down[0].astype(BF16))
    xs1 = xs1.reshape(t, d)
    h2 = h2.reshape(t, d)

    lp, gw, cnt = _route_call(lgt, router_bias[0])

    ntiles = t // TILE
    nassign = t * TOP_K
    n = cnt[:, :ntiles].T.astype(I32)
    local = jnp.cumsum(n, axis=1) - n
    earlier = jnp.cumsum(n, axis=0) - n
    total = jnp.sum(n, axis=0)
    region = (total + 1) // 2 * 2
    starts = jnp.cumsum(region) - region
    nalloc = nassign + N_EXPERTS
    used = jnp.sum(region)
    pad = jnp.stack([starts + total, region - total,
                     jnp.broadcast_to(used, (N_EXPERTS,))], axis=0)
    glob = starts[None, :] + earlier
    empty = n == 0
    length = jnp.maximum(n, 1)
    eid = jnp.arange(N_EXPERTS, dtype=I32)[None, :]
    parity = (jnp.arange(ntiles, dtype=I32) % 2)[:, None]
    spare_global = nalloc + EXPERT_ROWS + parity * N_EXPERTS + eid
    nempty = jnp.sum(empty.astype(I32), axis=1).reshape(ntiles // 2, 2)

    def table(loc, glo):
        fields = [f.reshape(ntiles // 2, _T_STRIDE) for f in (loc, length, glo)]
        return jnp.concatenate(fields + [nempty], axis=1).reshape(ntiles // 2, 1, _T_SIZE)

    tab_out = table(jnp.where(empty, 0, local), jnp.where(empty, spare_global, glob))
    tab_back = table(jnp.where(empty, LROWS + eid, local), jnp.where(empty, 0, glob))

    rows = EXPERT_ROWS
    nchunk = (total + rows - 1) // rows
    chunk_end = jnp.cumsum(nchunk)
    first = chunk_end - nchunk
    max_chunks = nassign // rows + N_EXPERTS
    gid = jnp.arange(max_chunks, dtype=I32)
    owner = jnp.minimum(jnp.sum((chunk_end[None, :] <= gid[:, None]).astype(I32), axis=1), N_EXPERTS - 1)
    pick = owner[:, None] == eid
    within = gid - jnp.sum(jnp.where(pick, first[None, :], 0), axis=1)
    chunk_row = jnp.sum(jnp.where(pick, starts[None, :], 0), axis=1) + within * rows
    left = jnp.sum(jnp.where(pick, region[None, :], 0), axis=1) - within * rows
    live = gid < chunk_end[-1]
    chunk_row = jnp.where(live, chunk_row, 0)
    chunk_len = jnp.where(live, jnp.clip(left, 2, rows), 2)
    info = jnp.stack([chunk_end[-1], used])

    xs = _dispatch_call(tab_out, pad, h2, lp, nalloc + TAIL_ROWS)
    y = _expert_call(first, nchunk, chunk_row, chunk_len, info, xs,
                     w_exp_gate[0], w_exp_up[0], w_exp_down[0], nalloc)
    out = _combine_call(tab_back, lp, gw, xs1, mod3, final_norm_g, y, seq)
    return out.reshape(bsz, seq, d)
```

```python
import numpy as np
import jax
import jax.numpy as jnp
from jax import lax
from jax.experimental import pallas as pl
from jax.experimental.pallas import tpu as pltpu

F32 = jnp.float32
BF16 = jnp.bfloat16
U32 = jnp.uint32
I32 = jnp.int32

D_MODEL = 1024
HEAD_DIM = 64
N_HEADS = 8
N_KV_HEADS = 2
GROUP = N_HEADS // N_KV_HEADS
Q_COLS = N_HEADS * HEAD_DIM
KV_COLS = N_KV_HEADS * HEAD_DIM
ATTN_BLOCK = 128
WINDOW = 128
NUM_BUCKETS = 32
MAX_DISTANCE = 128
CONV_CH = D_MODEL - Q_COLS
CONV_WIDTH = 31
IN_COLS = Q_COLS + 2 * KV_COLS + 2 * CONV_CH
N_EXPERTS = 256
TOP_K = 8
N_GROUPS = 8
GROUP_SIZE = N_EXPERTS // N_GROUPS
TOPK_GROUPS = 4
EXPERT_HIDDEN = 256
ROUTED_SCALE = 2.5
EPS = 1e-6

MIX_ROWS = 256
HALO = 32
TILE = 256
LROWS = TILE * TOP_K
SORT_CHUNK = 256
EXPERT_ROWS = 576
LANES = 128
SUBLANES = 8
PACK_W = D_MODEL // 2
PACK_S = PACK_W // LANES
VMEM_LIMIT = 56 * 1024 * 1024


def _sigmoid(v):
    return 1.0 / (1.0 + jnp.exp(-v))


def _rms(v):
    return v * lax.rsqrt(jnp.mean(v * v, axis=-1, keepdims=True) + EPS)


def _split_bf16(a):
    hi = a.astype(BF16)
    lo = (a - hi.astype(F32)).astype(BF16)
    return hi, lo


def _dot(a, b):
    return jnp.dot(a, b, preferred_element_type=F32)


def _dot_nt(a, b):
    return lax.dot_general(a, b, (((1,), (1,)), ((), ())), preferred_element_type=F32)


def _dot_tn(a, b):
    return lax.dot_general(a, b, (((0,), (0,)), ((), ())), preferred_element_type=F32)


def _dot3(a, b, dot):
    ah, al = _split_bf16(a)
    bh, bl = _split_bf16(b)
    return dot(ah, bh) + (dot(ah, bl) + dot(al, bh))


def _pack_rows(v):
    hi = lax.bitcast_convert_type(v[:, :PACK_W], U32) & jnp.uint32(0xFFFF0000)
    lo = lax.bitcast_convert_type(v[:, PACK_W:], U32) >> 16
    return hi | lo


def _unpack_rows(u):
    hi = lax.bitcast_convert_type(u & jnp.uint32(0xFFFF0000), F32)
    lo = lax.bitcast_convert_type(u << 16, F32)
    return jnp.concatenate([hi, lo], axis=1).astype(BF16)


def _load_flat(flat, r0, n):
    return jnp.concatenate(
        [flat[pl.ds(r0 * PACK_S + c, n, stride=PACK_S), :] for c in range(PACK_S)], axis=1)


def _store_flat(flat, r0, n, u):
    for c in range(PACK_S):
        flat[pl.ds(r0 * PACK_S + c, n, stride=PACK_S), :] = u[:, c * LANES:(c + 1) * LANES]


def _load_packed(ref3, r0, n):
    return _load_flat(ref3.reshape(ref3.shape[0] * PACK_S, LANES), r0, n)


def _store_packed(ref3, r0, n, u):
    _store_flat(ref3.reshape(ref3.shape[0] * PACK_S, LANES), r0, n, u)


def _ada_kernel(c_ref, w_ref, b_ref, o_ref):
    c = c_ref[...]
    s = c * _sigmoid(c)
    o_ref[...] = _dot3(s, w_ref[...], _dot) + b_ref[...]


def _ada_call(c, w_ada, b_ada):
    bsz, d = c.shape
    n = w_ada.shape[1]
    tn = 1536
    return pl.pallas_call(
        _ada_kernel,
        grid=(n // tn,),
        in_specs=[
            pl.BlockSpec((bsz, d), lambda i: (0, 0)),
            pl.BlockSpec((d, tn), lambda i: (0, i)),
            pl.BlockSpec((1, tn), lambda i: (0, i)),
        ],
        out_specs=pl.BlockSpec((bsz, tn), lambda i: (0, i)),
        out_shape=jax.ShapeDtypeStruct((bsz, n), F32),
        compiler_params=pltpu.CompilerParams(
            dimension_semantics=("arbitrary",), vmem_limit_bytes=VMEM_LIMIT),
        name="ada",
    )(c, w_ada, b_ada.reshape(1, n))


def _bucket_table():
    qi = np.arange(ATTN_BLOCK)[:, None]
    ki = np.arange(2 * ATTN_BLOCK)[None, :]
    dist = qi + ATTN_BLOCK - ki
    n = np.maximum(dist, 0)
    max_exact = NUM_BUCKETS // 2
    large = max_exact + (np.log(np.maximum(n, 1) / max_exact) / np.log(MAX_DISTANCE / max_exact)
                         * (NUM_BUCKETS - max_exact)).astype(np.int32)
    large = np.minimum(large, NUM_BUCKETS - 1)
    bkt = np.where(n < max_exact, n, large).astype(np.int32)
    band = (dist >= 0) & (dist < WINDOW)
    return np.ascontiguousarray(np.where(band, bkt, -1).astype(np.int32).T)


def _mixer_kernel(x_ref, mod_ref, win_ref, sinks_ref, relb_ref, bkt_ref, ang_ref, cw_ref, cb_ref,
                  lng_ref, lnb_ref, wout_ref, wrt_ref, wsg_ref, wsu_ref, wsd_ref,
                  xs1_ref, h2_ref, lgt_ref,
                  kv_ref, uext_ref, bias_ref, wrh_ref, wrl_ref):
    tm = x_ref.shape[1]
    nblk = tm // ATTN_BLOCK
    j = pl.program_id(1)

    @pl.when((pl.program_id(0) == 0) & (j == 0))
    def _build_bias():
        bkt = bkt_ref[...]
        for h in range(N_HEADS):
            acc = jnp.full(bkt.shape, -jnp.inf, F32)
            for b in range(NUM_BUCKETS):
                acc = jnp.where(bkt == b, relb_ref[b, h], acc)
            g = h % GROUP
            bias_ref[h // GROUP, :, g * ATTN_BLOCK:(g + 1) * ATTN_BLOCK] = acc
        w_hi, w_lo = _split_bf16(wrt_ref[...])
        wrh_ref[...] = w_hi
        wrl_ref[...] = w_lo

    @pl.when(j == 0)
    def _reset_history():
        kv_ref[...] = jnp.zeros(kv_ref.shape, F32)
        uext_ref[0:HALO, :] = jnp.zeros((HALO, CONV_CH), F32)
        uext_ref[HALO + tm:HALO + tm + SUBLANES, :] = jnp.zeros((SUBLANES, CONV_CH), F32)

    x = x_ref[0]
    mod = mod_ref[0]
    sh1, sc1, g1 = mod[0:1], mod[1:2], mod[2:3]
    sh2, sc2, g2 = mod[3:4], mod[4:5], mod[5:6]

    h = _rms(x) * (1.0 + sc1) + sh1
    hb = h.astype(BF16)
    n_qkv = Q_COLS + 2 * KV_COLS
    proj_conv = _dot(hb, win_ref[:, n_qkv:])
    proj = _dot(hb, win_ref[:, :n_qkv])

    qb = (proj[:, :Q_COLS] * (HEAD_DIM ** -0.5)).astype(BF16)
    kv_cur = proj[:, Q_COLS:Q_COLS + 2 * KV_COLS]
    kvb = jnp.concatenate([kv_ref[...], kv_cur], axis=0).astype(BF16)
    kv_ref[...] = kv_cur[tm - ATTN_BLOCK:, :]
    not_first = j > 0
    key = lax.broadcasted_iota(jnp.int32, (2 * ATTN_BLOCK, GROUP * ATTN_BLOCK), 0)
    attn_rows = []
    for i in range(nblk):
        r0, r1, r2 = i * ATTN_BLOCK, (i + 1) * ATTN_BLOCK, (i + 2) * ATTN_BLOCK
        heads_t = [None] * N_HEADS
        for kh in range(N_KV_HEADS):
            kpc = kvb[r0:r2, kh * HEAD_DIM:(kh + 1) * HEAD_DIM]
            vpc = kvb[r0:r2, KV_COLS + kh * HEAD_DIM:KV_COLS + (kh + 1) * HEAD_DIM]
            qg = jnp.concatenate(
                [qb[r0:r1, (kh * GROUP + g) * HEAD_DIM:(kh * GROUP + g + 1) * HEAD_DIM]
                 for g in range(GROUP)], axis=0)
            logit = _dot_nt(kpc, qg) + bias_ref[kh]
            if i == 0:
                logit = jnp.where((key >= ATTN_BLOCK) | not_first, logit, -jnp.inf)
            sink = jnp.concatenate(
                [jnp.full((1, ATTN_BLOCK), sinks_ref[kh * GROUP + g], F32) for g in range(GROUP)], axis=1)
            m = jnp.maximum(jnp.max(logit, axis=0, keepdims=True), sink)
            p = jnp.exp(logit - m)
            den = jnp.sum(p, axis=0, keepdims=True) + jnp.exp(sink - m)
            o_t = _dot_tn(vpc, p.astype(BF16)) / den
            for g in range(GROUP):
                heads_t[kh * GROUP + g] = o_t[:, g * ATTN_BLOCK:(g + 1) * ATTN_BLOCK]
        attn_t = jnp.concatenate(heads_t, axis=0)
        scale = lax.rsqrt(jnp.mean(attn_t * attn_t, axis=0, keepdims=True) + EPS)
        attn_rows.append((attn_t * scale).T)
    attn = jnp.concatenate(attn_rows, axis=0) * ang_ref[...]

    a = proj_conv[:, :CONV_CH]
    gt = proj_conv[:, CONV_CH:]
    uext_ref[HALO:HALO + tm, :] = a * _sigmoid(gt)
    cw = cw_ref[...]
    base = HALO - (CONV_WIDTH - 1)
    acc = None
    for res in range(SUBLANES):
        part = None
        for hi in range((HALO + SUBLANES - 1) // SUBLANES + 1):
            t = hi * SUBLANES + res - base
            if 0 <= t < CONV_WIDTH:
                term = cw[t:t + 1, :] * uext_ref[hi * SUBLANES:hi * SUBLANES + tm + SUBLANES, :]
                part = term if part is None else part + term
        part = part[res:res + tm, :]
        acc = part if acc is None else acc + part
    uext_ref[0:HALO, :] = uext_ref[tm:tm + HALO, :]
    cv = acc + cb_ref[...]
    mu = jnp.mean(cv, axis=-1, keepdims=True)
    var = jnp.mean(jnp.square(cv - mu), axis=-1, keepdims=True)
    cv = (cv - mu) * lax.rsqrt(var + EPS) * lng_ref[...] + lnb_ref[...]
    cv = cv * _sigmoid(cv)

    mixed = (_dot(attn.astype(BF16), wout_ref[0:Q_COLS, :])
             + _dot(cv.astype(BF16), wout_ref[Q_COLS:, :]))
    x1 = x + g1 * mixed

    h2 = _rms(x1) * (1.0 + sc2) + sh2
    h2_hi, h2_lo = _split_bf16(h2)
    w_hi, w_lo = wrh_ref[...], wrl_ref[...]
    lgt_ref[...] = _dot_nt(w_hi, h2_hi) + (_dot_nt(w_hi, h2_lo) + _dot_nt(w_lo, h2_hi))
    h2b = h2.astype(BF16)
    h2_ref[0] = h2b
    sg = _dot(h2b, wsg_ref[...])
    su = _dot(h2b, wsu_ref[...])
    shared = _dot(((sg * _sigmoid(sg)) * su).astype(BF16), wsd_ref[...])
    xs1_ref[0] = x1 + g2 * shared


def _mixer_call(x, mod3, w_in_b, sinks, rel_bias, ang, cw, cb, lng, lnb, w_out_b, w_rt, wsg_b, wsu_b, wsd_b):
    bsz, seq, d = x.shape
    tm = MIX_ROWS
    nj = seq // tm
    bkt = jnp.asarray(_bucket_table())
    full = lambda shape: pl.BlockSpec(shape, lambda b, j: (0,) * len(shape))
    smem = pl.BlockSpec(memory_space=pltpu.SMEM)
    return pl.pallas_call(
        _mixer_kernel,
        grid=(bsz, nj),
        in_specs=[
            pl.BlockSpec((1, tm, d), lambda b, j: (b, j, 0)),
            pl.BlockSpec((1, 6, d), lambda b, j: (b, 0, 0)),
            full((d, IN_COLS)),
            smem, smem,
            full((2 * ATTN_BLOCK, ATTN_BLOCK)),
            full((1, Q_COLS)),
            full((CONV_WIDTH, CONV_CH)),
            full((1, CONV_CH)), full((1, CONV_CH)), full((1, CONV_CH)),
            full((d, d)),
            full((N_EXPERTS, d)),
            full((d, EXPERT_HIDDEN)), full((d, EXPERT_HIDDEN)), full((EXPERT_HIDDEN, d)),
        ],
        out_specs=[
            pl.BlockSpec((1, tm, d), lambda b, j: (b, j, 0)),
            pl.BlockSpec((1, tm, d), lambda b, j: (b, j, 0)),
            pl.BlockSpec((N_EXPERTS, tm), lambda b, j: (0, b * nj + j)),
        ],
        out_shape=[
            jax.ShapeDtypeStruct((bsz, seq, d), F32),
            jax.ShapeDtypeStruct((bsz, seq, d), BF16),
            jax.ShapeDtypeStruct((N_EXPERTS, bsz * seq), F32),
        ],
        scratch_shapes=[
            pltpu.VMEM((ATTN_BLOCK, 2 * KV_COLS), F32),
            pltpu.VMEM((HALO + tm + SUBLANES, CONV_CH), F32),
            pltpu.VMEM((N_KV_HEADS, 2 * ATTN_BLOCK, GROUP * ATTN_BLOCK), F32),
            pltpu.VMEM((N_EXPERTS, d), BF16),
            pltpu.VMEM((N_EXPERTS, d), BF16),
        ],
        compiler_params=pltpu.CompilerParams(
            dimension_semantics=("arbitrary", "arbitrary"), vmem_limit_bytes=VMEM_LIMIT),
        name="mixer",
    )(x, mod3, w_in_b, sinks, rel_bias, bkt, ang, cw, cb, lng, lnb, w_out_b, w_rt, wsg_b, wsu_b, wsd_b)


def _route_kernel(lgt_ref, rb_ref, lp_ref, gw_ref, cnt_ref):
    tr = lgt_ref.shape[1]
    i = pl.program_id(0)

    @pl.when(i == 0)
    def _():
        cnt_ref[...] = jnp.zeros(cnt_ref.shape, F32)

    scores = _sigmoid(lgt_ref[...])
    sel = scores + rb_ref[...]
    neg = -jnp.inf

    sel3 = sel.reshape(N_GROUPS, GROUP_SIZE, tr)
    loc = lax.broadcasted_iota(I32, sel3.shape, 1)
    m1 = jnp.max(sel3, axis=1, keepdims=True)
    i1 = jnp.min(jnp.where(sel3 == m1, loc, GROUP_SIZE), axis=1, keepdims=True)
    m2 = jnp.max(jnp.where(loc == i1, neg, sel3), axis=1, keepdims=True)
    gscore = (m1 + m2).reshape(N_GROUPS, tr)

    gio = lax.broadcasted_iota(I32, gscore.shape, 0)
    gmask = jnp.zeros(gscore.shape, jnp.bool_)
    cur = gscore
    for _ in range(TOPK_GROUPS):
        m = jnp.max(cur, axis=0, keepdims=True)
        idx = jnp.min(jnp.where(cur == m, gio, N_GROUPS), axis=0, keepdims=True)
        pick = gio == idx
        gmask = gmask | pick
        cur = jnp.where(pick, neg, cur)
    emask = jnp.broadcast_to(gmask.reshape(N_GROUPS, 1, tr), sel3.shape).reshape(N_EXPERTS, tr)

    rio = lax.broadcasted_iota(I32, sel.shape, 0)
    cur = jnp.where(emask, sel, neg)
    picks, gsc = [], []
    for _ in range(TOP_K):
        m = jnp.max(cur, axis=0, keepdims=True)
        idx = jnp.min(jnp.where(cur == m, rio, N_EXPERTS), axis=0, keepdims=True)
        pick = rio == idx
        picks.append(pick)
        gsc.append(jnp.sum(jnp.where(pick, scores, 0.0), axis=0, keepdims=True))
        cur = jnp.where(pick, neg, cur)
    gsum = gsc[0]
    for k in range(1, TOP_K):
        gsum = gsum + gsc[k]
    gw_ref[...] = jnp.concatenate([g / gsum * ROUTED_SCALE for g in gsc], axis=0)

    chosen = picks[0]
    for k in range(1, TOP_K):
        chosen = chosen | picks[k]
    onehot = jnp.where(chosen, 1.0, 0.0)
    tri = (lax.broadcasted_iota(I32, (tr, tr), 0) < lax.broadcasted_iota(I32, (tr, tr), 1))
    before = _dot(onehot.astype(BF16), jnp.where(tri, 1.0, 0.0).astype(BF16))
    n = jnp.sum(onehot, axis=1, keepdims=True)
    low = (lax.broadcasted_iota(I32, (N_EXPERTS, N_EXPERTS), 1)
           < lax.broadcasted_iota(I32, (N_EXPERTS, N_EXPERTS), 0))
    start = _dot(jnp.where(low, 1.0, 0.0).astype(BF16),
                 jnp.broadcast_to(n, (N_EXPERTS, LANES)).astype(BF16))[:, 0:1]
    pos = start + before
    lp_ref[...] = jnp.concatenate(
        [jnp.sum(jnp.where(p, pos, 0.0), axis=0, keepdims=True) for p in picks], axis=0).astype(I32)

    lane = lax.broadcasted_iota(I32, cnt_ref.shape, 1)
    cnt_ref[...] += jnp.where(lane == i, n, 0.0)


def _route_call(lgt, router_bias):
    e, t = lgt.shape
    ntiles = t // TILE
    ntp = (ntiles + LANES - 1) // LANES * LANES
    return pl.pallas_call(
        _route_kernel,
        grid=(ntiles,),
        in_specs=[
            pl.BlockSpec((e, TILE), lambda i: (0, i)),
            pl.BlockSpec((e, 1), lambda i: (0, 0)),
        ],
        out_specs=[
            pl.BlockSpec((TOP_K, TILE), lambda i: (0, i)),
            pl.BlockSpec((TOP_K, TILE), lambda i: (0, i)),
            pl.BlockSpec((e, ntp), lambda i: (0, 0)),
        ],
        out_shape=[
            jax.ShapeDtypeStruct((TOP_K, t), I32),
            jax.ShapeDtypeStruct((TOP_K, t), F32),
            jax.ShapeDtypeStruct((e, ntp), F32),
        ],
        compiler_params=pltpu.CompilerParams(
            dimension_semantics=("arbitrary",), vmem_limit_bytes=VMEM_LIMIT),
        name="route",
    )(lgt, router_bias.reshape(e, 1))


_T_LOCAL, _T_LEN, _T_GLOBAL, _T_FIELDS = 0, 1, 2, 3
_T_STRIDE = 2 * N_EXPERTS
_T_EMPTY = _T_STRIDE * _T_FIELDS
_T_SIZE = _T_EMPTY + 2
LBUF_ROWS = LROWS + N_EXPERTS


def _segment_copies(tab_ref, half, local_ref, global_hbm, sem, to_global):
    def body(e, carry):
        at = half * N_EXPERTS + e
        n = tab_ref[0, 0, _T_LEN * _T_STRIDE + at]
        loc = local_ref.at[pl.ds(tab_ref[0, 0, _T_LOCAL * _T_STRIDE + at], n)]
        glo = global_hbm.at[pl.ds(tab_ref[0, 0, _T_GLOBAL * _T_STRIDE + at], n)]
        if to_global:
            pltpu.make_async_copy(loc, glo, sem).start()
        else:
            pltpu.make_async_copy(glo, loc, sem).start()
        return carry

    for e in range(N_EXPERTS):
        body(e, 0)


def _chunk_relative(lp_row, c):
    rel = lp_row - c * SORT_CHUNK
    inside = (rel >= 0) & (rel < SORT_CHUNK)
    return jnp.where(inside, rel, -1).astype(F32).astype(BF16)


def _chunk_row_ids():
    return lax.broadcasted_iota(I32, (SORT_CHUNK, TILE), 0).astype(F32).astype(BF16)


def _segment_wait(local_ref, global_hbm, sem, nempty):
    rows = LROWS + nempty
    pltpu.make_async_copy(global_hbm.at[pl.ds(0, rows)], local_ref.at[pl.ds(0, rows)], sem).wait()


TAIL_ROWS = EXPERT_ROWS + 2 * N_EXPERTS


def _dispatch_kernel(tab_ref, pad_ref, h2_ref, lp_ref, xs_hbm, xl0, xl1, zb, sem, pending):
    s = pl.program_id(0)
    ns = pl.num_programs(0)

    @pl.when(s == 0)
    def _zero_unassigned_rows():
        zb[...] = jnp.zeros(zb.shape, U32)
        used = pad_ref[2, 0]
        rest = xs_hbm.shape[0] - used
        cp = pltpu.make_async_copy(zb.at[pl.ds(0, rest)], xs_hbm.at[pl.ds(used, rest)], sem.at[2])
        cp.start()
        cp.wait()

        def pad_row(e):
            return pltpu.make_async_copy(zb.at[pl.ds(0, 1)], xs_hbm.at[pl.ds(pad_ref[0, e], 1)], sem.at[2])

        def start(e, carry):
            @pl.when(pad_ref[1, e] > 0)
            def _():
                pad_row(e).start()
            return carry

        def wait(e, carry):
            @pl.when(pad_ref[1, e] > 0)
            def _():
                pad_row(e).wait()
            return carry

        lax.fori_loop(0, N_EXPERTS, start, 0)
        lax.fori_loop(0, N_EXPERTS, wait, 0)

    @pl.when(s > 0)
    def _():
        _segment_wait(xl0, xs_hbm, sem.at[0], pending[0])
        _segment_wait(xl1, xs_hbm, sem.at[1], pending[1])

    for half, xl in ((0, xl0), (1, xl1)):
        rows = h2_ref[half * TILE:(half + 1) * TILE, :]
        lp = lp_ref[:, half * TILE:(half + 1) * TILE]
        j = _chunk_row_ids()
        for c in range(LROWS // SORT_CHUNK):
            onehot = jnp.zeros((SORT_CHUNK, TILE), BF16)
            for k in range(TOP_K):
                onehot = jnp.where(j == _chunk_relative(lp[k:k + 1, :], c), jnp.ones((), BF16), onehot)
            _store_packed(xl, c * SORT_CHUNK, SORT_CHUNK, _pack_rows(_dot(onehot, rows)))
        _segment_copies(tab_ref, half, xl, xs_hbm, sem.at[half], to_global=True)
        pending[half] = tab_ref[0, 0, _T_EMPTY + half]

    @pl.when(s == ns - 1)
    def _():
        _segment_wait(xl0, xs_hbm, sem.at[0], pending[0])
        _segment_wait(xl1, xs_hbm, sem.at[1], pending[1])


def _dispatch_call(tab, pad, h2, lp, nrows):
    t, d = h2.shape
    ns = t // (2 * TILE)
    return pl.pallas_call(
        _dispatch_kernel,
        grid=(ns,),
        in_specs=[
            pl.BlockSpec((1, 1, _T_SIZE), lambda s: (s, 0, 0), memory_space=pltpu.SMEM),
            pl.BlockSpec(memory_space=pltpu.SMEM),
            pl.BlockSpec((2 * TILE, d), lambda s: (s, 0)),
            pl.BlockSpec((TOP_K, 2 * TILE), lambda s: (0, s)),
        ],
        out_specs=pl.BlockSpec(memory_space=pl.ANY),
        out_shape=jax.ShapeDtypeStruct((nrows, PACK_S, LANES), U32),
        scratch_shapes=[
            pltpu.VMEM((LBUF_ROWS, PACK_S, LANES), U32),
            pltpu.VMEM((LBUF_ROWS, PACK_S, LANES), U32),
            pltpu.VMEM((TAIL_ROWS + N_EXPERTS, PACK_S, LANES), U32),
            pltpu.SemaphoreType.DMA((3,)),
            pltpu.SMEM((2,), I32),
        ],
        compiler_params=pltpu.CompilerParams(
            dimension_semantics=("arbitrary",), vmem_limit_bytes=VMEM_LIMIT),
        name="dispatch",
    )(tab, pad, h2, lp)


RING = 6
AHEAD = RING - 2


def _expert_kernel(first_ref, nchunk_ref, row_ref, len_ref, info_ref, wg_ref, wu_ref, wd_ref, xs_hbm, y_hbm,
                   xbuf, ybuf, isem, osem, wgb, wub, wdb):
    e = pl.program_id(0)
    last_step = e == pl.num_programs(0) - 1
    rows = EXPERT_ROWS
    total_chunks = info_ref[0]

    def slab(ref, row0, m):
        size = m * PACK_S if isinstance(m, int) else pl.multiple_of(m * PACK_S, SUBLANES)
        first = row0 * PACK_S if isinstance(row0, int) else pl.multiple_of(row0 * PACK_S, SUBLANES)
        return ref.at[pl.ds(first, size)]

    def fetch(g):
        slot = g % RING
        return pltpu.make_async_copy(slab(xs_hbm, row_ref[g], rows), slab(xbuf, slot * rows, rows), isem.at[slot])

    def flush(g):
        slot = g % RING
        m = len_ref[g]
        return pltpu.make_async_copy(slab(ybuf, slot * rows, m), slab(y_hbm, row_ref[g], m), osem.at[slot])

    def start_fetch(g):
        @pl.when(g < total_chunks)
        def _():
            fetch(g).start()

    def wait_flush(g):
        @pl.when(g >= 0)
        def _():
            flush(g).wait()

    def compute(g):
        slot = g % RING
        xb = _unpack_rows(_load_flat(xbuf, slot * rows, rows))
        gate = _dot(xb, wgb[...])
        up = _dot(xb, wub[...])
        hid = (gate * _sigmoid(gate)) * up
        y = _dot(hid.astype(BF16), wdb[...])
        _store_flat(ybuf, slot * rows, rows, _pack_rows(y.astype(BF16).astype(F32)))

    @pl.when(e == 0)
    def _prime():
        for a in range(AHEAD):
            start_fetch(a)

    wgb[...] = wg_ref[0].astype(BF16)
    wub[...] = wu_ref[0].astype(BF16)
    wdb[...] = wd_ref[0].astype(BF16)

    begin = first_ref[e]
    count = nchunk_ref[e]

    def pair(p, carry):
        g0 = begin + 2 * p
        g1 = g0 + 1
        start_fetch(g0 + AHEAD)
        start_fetch(g1 + AHEAD)
        fetch(g0).wait()
        fetch(g1).wait()
        wait_flush(g0 - RING)
        wait_flush(g1 - RING)
        compute(g0)
        compute(g1)
        flush(g0).start()
        flush(g1).start()
        return carry

    lax.fori_loop(0, count // 2, pair, 0)

    @pl.when(count % 2 == 1)
    def _odd_chunk():
        g = begin + count - 1
        start_fetch(g + AHEAD)
        fetch(g).wait()
        wait_flush(g - RING)
        compute(g)
        flush(g).start()

    @pl.when(last_step)
    def _drain():
        def wait(g, carry):
            flush(g).wait()
            return carry
        lax.fori_loop(jnp.maximum(total_chunks - RING, 0), total_chunks, wait, 0)

    @pl.when(last_step)
    def _define_unused_rows():
        used = info_ref[1]
        rest = y_hbm.shape[0] // PACK_S - used

        @pl.when(rest > 0)
        def _():
            ybuf[0:N_EXPERTS * PACK_S, :] = jnp.zeros((N_EXPERTS * PACK_S, LANES), U32)
            cp = pltpu.make_async_copy(slab(ybuf, 0, rest), slab(y_hbm, used, rest), osem.at[0])
            cp.start()
            cp.wait()


def _expert_call(first, nchunk, chunk_row, chunk_len, info, xs, w_gate, w_up, w_down, nrows):
    ne, d, hid = w_gate.shape
    rows = EXPERT_ROWS
    wsel = lambda e, *_: (e, 0, 0)
    grid_spec = pltpu.PrefetchScalarGridSpec(
        num_scalar_prefetch=5,
        grid=(ne,),
        in_specs=[
            pl.BlockSpec((1, d, hid), wsel),
            pl.BlockSpec((1, d, hid), wsel),
            pl.BlockSpec((1, hid, d), wsel),
            pl.BlockSpec(memory_space=pl.ANY),
        ],
        out_specs=pl.BlockSpec(memory_space=pl.ANY),
        scratch_shapes=[
            pltpu.VMEM((RING * rows * PACK_S, LANES), U32),
            pltpu.VMEM((RING * rows * PACK_S, LANES), U32),
            pltpu.SemaphoreType.DMA((RING,)),
            pltpu.SemaphoreType.DMA((RING,)),
            pltpu.VMEM((d, hid), BF16),
            pltpu.VMEM((d, hid), BF16),
            pltpu.VMEM((hid, d), BF16),
        ],
    )
    return pl.pallas_call(
        _expert_kernel,
        grid_spec=grid_spec,
        out_shape=jax.ShapeDtypeStruct((nrows * PACK_S, LANES), U32),
        compiler_params=pltpu.CompilerParams(
            dimension_semantics=("arbitrary",), vmem_limit_bytes=VMEM_LIMIT),
        name="experts",
    )(first, nchunk, chunk_row, chunk_len, info, w_gate, w_up, w_down,
      xs.reshape(-1, LANES)).reshape(nrows, PACK_S, LANES)


def _combine_kernel(tab_ref, tabn_ref, lp_ref, gw_ref, xs1_ref, mod_ref, fg_ref, y_hbm, o_ref, yl0, yl1, sem):
    s = pl.program_id(0)
    ns = pl.num_programs(0)

    @pl.when(s == 0)
    def _():
        _segment_copies(tab_ref, 0, yl0, y_hbm, sem.at[0], to_global=False)

    g2 = mod_ref[0][5:6]
    for half, yl in ((0, yl0), (1, yl1)):
        _segment_wait(yl, y_hbm, sem.at[half], tab_ref[0, 0, _T_EMPTY + half])
        if half == 0:
            _segment_copies(tab_ref, 1, yl1, y_hbm, sem.at[1], to_global=False)
        lp = lp_ref[:, half * TILE:(half + 1) * TILE]
        gw = gw_ref[:, half * TILE:(half + 1) * TILE]
        acc = jnp.zeros((TILE, D_MODEL), F32)
        gwb = [gw[k:k + 1, :].astype(BF16) for k in range(TOP_K)]
        j = _chunk_row_ids()
        for c in range(LROWS // SORT_CHUNK):
            wt = jnp.zeros((SORT_CHUNK, TILE), BF16)
            for k in range(TOP_K):
                wt = jnp.where(j == _chunk_relative(lp[k:k + 1, :], c), gwb[k], wt)
            yv = _unpack_rows(_load_packed(yl, c * SORT_CHUNK, SORT_CHUNK))
            acc = acc + _dot_tn(wt, yv)
        if half == 0:
            _segment_copies(tabn_ref, 0, yl0, y_hbm, sem.at[0], to_global=False)
        x2 = xs1_ref[half * TILE:(half + 1) * TILE, :] + g2 * acc
        o_ref[half * TILE:(half + 1) * TILE, :] = _rms(x2) * fg_ref[...]

    @pl.when(s == ns - 1)
    def _():
        _segment_wait(yl0, y_hbm, sem.at[0], tab_ref[0, 0, _T_EMPTY])


def _combine_call(tab, lp, gw, xs1, mod3, final_g, y, seq):
    t, d = xs1.shape
    ns = t // (2 * TILE)
    per_seq = seq // (2 * TILE)
    return pl.pallas_call(
        _combine_kernel,
        grid=(ns,),
        in_specs=[
            pl.BlockSpec((1, 1, _T_SIZE), lambda s: (s, 0, 0), memory_space=pltpu.SMEM),
            pl.BlockSpec((1, 1, _T_SIZE), lambda s: (jnp.minimum(s + 1, ns - 1), 0, 0),
                         memory_space=pltpu.SMEM),
            pl.BlockSpec((TOP_K, 2 * TILE), lambda s: (0, s)),
            pl.BlockSpec((TOP_K, 2 * TILE), lambda s: (0, s)),
            pl.BlockSpec((2 * TILE, d), lambda s: (s, 0)),
            pl.BlockSpec((1, 6, d), lambda s: (s // per_seq, 0, 0)),
            pl.BlockSpec((1, d), lambda s: (0, 0)),
            pl.BlockSpec(memory_space=pl.ANY),
        ],
        out_specs=pl.BlockSpec((2 * TILE, d), lambda s: (s, 0)),
        out_shape=jax.ShapeDtypeStruct((t, d), F32),
        scratch_shapes=[
            pltpu.VMEM((LBUF_ROWS, PACK_S, LANES), U32),
            pltpu.VMEM((LBUF_ROWS, PACK_S, LANES), U32),
            pltpu.SemaphoreType.DMA((2,)),
        ],
        compiler_params=pltpu.CompilerParams(
            dimension_semantics=("arbitrary",), vmem_limit_bytes=VMEM_LIMIT),
        name="combine",
    )(tab, tab, lp, gw, xs1, mod3, final_g.reshape(1, d), y)


def kernel(x, c, w_ada, b_ada, w_in, attn_sinks, rel_bias, attn_norm_g, conv_w, conv_b, conv_ln_g,
           conv_ln_b, w_out, w_router, router_bias, w_exp_gate, w_exp_up, w_exp_down, w_sh_gate,
           w_sh_up, w_sh_down, final_norm_g):
    bsz, seq, d = x.shape
    t = bsz * seq
    assert w_ada.shape[0] == 1 and d == D_MODEL
    assert seq % MIX_ROWS == 0 and seq % (2 * TILE) == 0

    mod3 = _ada_call(c, w_ada[0], b_ada[0]).reshape(bsz, 6, d)
    xs1, h2, lgt = _mixer_call(
        x, mod3, w_in[0].astype(BF16), attn_sinks[0], rel_bias,
        attn_norm_g[0].reshape(1, Q_COLS), conv_w[0], conv_b[0].reshape(1, CONV_CH),
        conv_ln_g[0].reshape(1, CONV_CH), conv_ln_b[0].reshape(1, CONV_CH),
        w_out[0].astype(BF16), w_router[0].T,
        w_sh_gate[0].astype(BF16), w_sh_up[0].astype(BF16), w_sh_down[0].astype(BF16))
    xs1 = xs1.reshape(t, d)
    h2 = h2.reshape(t, d)

    lp, gw, cnt = _route_call(lgt, router_bias[0])

    ntiles = t // TILE
    nassign = t * TOP_K
    n = cnt[:, :ntiles].T.astype(I32)
    local = jnp.cumsum(n, axis=1) - n
    earlier = jnp.cumsum(n, axis=0) - n
    total = jnp.sum(n, axis=0)
    region = (total + 1) // 2 * 2
    starts = jnp.cumsum(region) - region
    nalloc = nassign + N_EXPERTS
    used = jnp.sum(region)
    pad = jnp.stack([starts + total, region - total,
                     jnp.broadcast_to(used, (N_EXPERTS,))], axis=0)
    glob = starts[None, :] + earlier
    empty = n == 0
    length = jnp.maximum(n, 1)
    eid = jnp.arange(N_EXPERTS, dtype=I32)[None, :]
    parity = (jnp.arange(ntiles, dtype=I32) % 2)[:, None]
    spare_global = nalloc + EXPERT_ROWS + parity * N_EXPERTS + eid
    nempty = jnp.sum(empty.astype(I32), axis=1).reshape(ntiles // 2, 2)

    def table(loc, glo):
        fields = [f.reshape(ntiles // 2, _T_STRIDE) for f in (loc, length, glo)]
        return jnp.concatenate(fields + [nempty], axis=1).reshape(ntiles // 2, 1, _T_SIZE)

    tab_out = table(jnp.where(empty, 0, local), jnp.where(empty, spare_global, glob))
    tab_back = table(jnp.where(empty, LROWS + eid, local), jnp.where(empty, 0, glob))

    rows = EXPERT_ROWS
    nchunk = (total + rows - 1) // rows
    chunk_end = jnp.cumsum(nchunk)
    first = chunk_end - nchunk
    max_chunks = nassign // rows + N_EXPERTS
    gid = jnp.arange(max_chunks, dtype=I32)
    owner = jnp.minimum(jnp.sum((chunk_end[None, :] <= gid[:, None]).astype(I32), axis=1), N_EXPERTS - 1)
    pick = owner[:, None] == eid
    within = gid - jnp.sum(jnp.where(pick, first[None, :], 0), axis=1)
    chunk_row = jnp.sum(jnp.where(pick, starts[None, :], 0), axis=1) + within * rows
    left = jnp.sum(jnp.where(pick, region[None, :], 0), axis=1) - within * rows
    live = gid < chunk_end[-1]
    chunk_row = jnp.where(live, chunk_row, 0)
    chunk_len = jnp.where(live, jnp.clip(left, 2, rows), 2)
    info = jnp.stack([chunk_end[-1], used])

    xs = _dispatch_call(tab_out, pad, h2, lp, nalloc + TAIL_ROWS)
    y = _expert_call(first, nchunk, chunk_row, chunk_len, info, xs,
                     w_exp_gate[0], w_exp_up[0], w_exp_down[0], nalloc)
    out = _combine_call(tab_back, lp, gw, xs1, mod3, final_norm_g, y, seq)
    return out.reshape(bsz, seq, d)
```

```python
import numpy as np
import jax
import jax.numpy as jnp
from jax import lax
from jax.experimental import pallas as pl
from jax.experimental.pallas import tpu as pltpu

F32 = jnp.float32
BF16 = jnp.bfloat16
U32 = jnp.uint32
I32 = jnp.int32

D_MODEL = 1024
HEAD_DIM = 64
N_HEADS = 8
N_KV_HEADS = 2
GROUP = N_HEADS // N_KV_HEADS
Q_COLS = N_HEADS * HEAD_DIM
KV_COLS = N_KV_HEADS * HEAD_DIM
ATTN_BLOCK = 128
WINDOW = 128
NUM_BUCKETS = 32
MAX_DISTANCE = 128
CONV_CH = D_MODEL - Q_COLS
CONV_WIDTH = 31
IN_COLS = Q_COLS + 2 * KV_COLS + 2 * CONV_CH
N_EXPERTS = 256
TOP_K = 8
N_GROUPS = 8
GROUP_SIZE = N_EXPERTS // N_GROUPS
TOPK_GROUPS = 4
EXPERT_HIDDEN = 256
ROUTED_SCALE = 2.5
EPS = 1e-6

MIX_ROWS = 256
HALO = 32
TILE = 256
LROWS = TILE * TOP_K
SORT_CHUNK = 256
EXPERT_ROWS = 576
LANES = 128
SUBLANES = 8
PACK_W = D_MODEL // 2
PACK_S = PACK_W // LANES
VMEM_LIMIT = 56 * 1024 * 1024


def _sigmoid(v):
    return 1.0 / (1.0 + jnp.exp(-v))


def _rms(v):
    return v * lax.rsqrt(jnp.mean(v * v, axis=-1, keepdims=True) + EPS)


def _split_bf16(a):
    hi = a.astype(BF16)
    lo = (a - hi.astype(F32)).astype(BF16)
    return hi, lo


def _dot(a, b):
    return jnp.dot(a, b, preferred_element_type=F32)


def _dot_nt(a, b):
    return lax.dot_general(a, b, (((1,), (1,)), ((), ())), preferred_element_type=F32)


def _dot_tn(a, b):
    return lax.dot_general(a, b, (((0,), (0,)), ((), ())), preferred_element_type=F32)


def _dot3(a, b, dot):
    ah, al = _split_bf16(a)
    bh, bl = _split_bf16(b)
    return dot(ah, bh) + (dot(ah, bl) + dot(al, bh))


def _pack_rows(v):
    hi = lax.bitcast_convert_type(v[:, :PACK_W], U32) & jnp.uint32(0xFFFF0000)
    lo = lax.bitcast_convert_type(v[:, PACK_W:], U32) >> 16
    return hi | lo


def _unpack_rows(u):
    hi = lax.bitcast_convert_type(u & jnp.uint32(0xFFFF0000), F32)
    lo = lax.bitcast_convert_type(u << 16, F32)
    return jnp.concatenate([hi, lo], axis=1).astype(BF16)


def _load_flat(flat, r0, n):
    return jnp.concatenate(
        [flat[pl.ds(r0 * PACK_S + c, n, stride=PACK_S), :] for c in range(PACK_S)], axis=1)


def _store_flat(flat, r0, n, u):
    for c in range(PACK_S):
        flat[pl.ds(r0 * PACK_S + c, n, stride=PACK_S), :] = u[:, c * LANES:(c + 1) * LANES]


def _load_packed(ref3, r0, n):
    return _load_flat(ref3.reshape(ref3.shape[0] * PACK_S, LANES), r0, n)


def _store_packed(ref3, r0, n, u):
    _store_flat(ref3.reshape(ref3.shape[0] * PACK_S, LANES), r0, n, u)


def _ada_kernel(c_ref, w_ref, b_ref, o_ref):
    c = c_ref[...]
    s = c * _sigmoid(c)
    o_ref[...] = _dot3(s, w_ref[...], _dot) + b_ref[...]


def _ada_call(c, w_ada, b_ada):
    bsz, d = c.shape
    n = w_ada.shape[1]
    tn = 1536
    return pl.pallas_call(
        _ada_kernel,
        grid=(n // tn,),
        in_specs=[
            pl.BlockSpec((bsz, d), lambda i: (0, 0)),
            pl.BlockSpec((d, tn), lambda i: (0, i)),
            pl.BlockSpec((1, tn), lambda i: (0, i)),
        ],
        out_specs=pl.BlockSpec((bsz, tn), lambda i: (0, i)),
        out_shape=jax.ShapeDtypeStruct((bsz, n), F32),
        compiler_params=pltpu.CompilerParams(
            dimension_semantics=("arbitrary",), vmem_limit_bytes=VMEM_LIMIT),
        name="ada",
    )(c, w_ada, b_ada.reshape(1, n))


def _bucket_table():
    qi = np.arange(ATTN_BLOCK)[:, None]
    ki = np.arange(2 * ATTN_BLOCK)[None, :]
    dist = qi + ATTN_BLOCK - ki
    n = np.maximum(dist, 0)
    max_exact = NUM_BUCKETS // 2
    large = max_exact + (np.log(np.maximum(n, 1) / max_exact) / np.log(MAX_DISTANCE / max_exact)
                         * (NUM_BUCKETS - max_exact)).astype(np.int32)
    large = np.minimum(large, NUM_BUCKETS - 1)
    bkt = np.where(n < max_exact, n, large).astype(np.int32)
    band = (dist >= 0) & (dist < WINDOW)
    return np.ascontiguousarray(np.where(band, bkt, -1).astype(np.int32).T)


def _mixer_kernel(x_ref, mod_ref, win_ref, sinks_ref, relb_ref, bkt_ref, ang_ref, cw_ref, cb_ref,
                  lng_ref, lnb_ref, wout_ref, wrt_ref, wsg_ref, wsu_ref, wsd_ref,
                  xs1_ref, h2_ref, lgt_ref,
                  kv_ref, uext_ref, bias_ref, wrh_ref, wrl_ref):
    tm = x_ref.shape[1]
    nblk = tm // ATTN_BLOCK
    j = pl.program_id(1)

    @pl.when((pl.program_id(0) == 0) & (j == 0))
    def _build_bias():
        bkt = bkt_ref[...]
        for h in range(N_HEADS):
            acc = jnp.full(bkt.shape, -jnp.inf, F32)
            for b in range(NUM_BUCKETS):
                acc = jnp.where(bkt == b, relb_ref[b, h], acc)
            g = h % GROUP
            bias_ref[h // GROUP, :, g * ATTN_BLOCK:(g + 1) * ATTN_BLOCK] = acc
        w_hi, w_lo = _split_bf16(wrt_ref[...])
        wrh_ref[...] = w_hi
        wrl_ref[...] = w_lo

    @pl.when(j == 0)
    def _reset_history():
        kv_ref[...] = jnp.zeros(kv_ref.shape, F32)
        uext_ref[0:HALO, :] = jnp.zeros((HALO, CONV_CH), F32)
        uext_ref[HALO + tm:HALO + tm + SUBLANES, :] = jnp.zeros((SUBLANES, CONV_CH), F32)

    x = x_ref[0]
    mod = mod_ref[0]
    sh1, sc1, g1 = mod[0:1], mod[1:2], mod[2:3]
    sh2, sc2, g2 = mod[3:4], mod[4:5], mod[5:6]

    h = _rms(x) * (1.0 + sc1) + sh1
    hb = h.astype(BF16)
    n_qkv = Q_COLS + 2 * KV_COLS
    proj_conv = _dot(hb, win_ref[:, n_qkv:])
    proj = _dot(hb, win_ref[:, :n_qkv])

    qb = (proj[:, :Q_COLS] * (HEAD_DIM ** -0.5)).astype(BF16)
    kv_cur = proj[:, Q_COLS:Q_COLS + 2 * KV_COLS]
    kvb = jnp.concatenate([kv_ref[...], kv_cur], axis=0).astype(BF16)
    kv_ref[...] = kv_cur[tm - ATTN_BLOCK:, :]
    not_first = j > 0
    key = lax.broadcasted_iota(jnp.int32, (2 * ATTN_BLOCK, GROUP * ATTN_BLOCK), 0)
    attn_rows = []
    for i in range(nblk):
        r0, r1, r2 = i * ATTN_BLOCK, (i + 1) * ATTN_BLOCK, (i + 2) * ATTN_BLOCK
        heads_t = [None] * N_HEADS
        for kh in range(N_KV_HEADS):
            kpc = kvb[r0:r2, kh * HEAD_DIM:(kh + 1) * HEAD_DIM]
            vpc = kvb[r0:r2, KV_COLS + kh * HEAD_DIM:KV_COLS + (kh + 1) * HEAD_DIM]
            qg = jnp.concatenate(
                [qb[r0:r1, (kh * GROUP + g) * HEAD_DIM:(kh * GROUP + g + 1) * HEAD_DIM]
                 for g in range(GROUP)], axis=0)
            logit = _dot_nt(kpc, qg) + bias_ref[kh]
            if i == 0:
                logit = jnp.where((key >= ATTN_BLOCK) | not_first, logit, -jnp.inf)
            sink = jnp.concatenate(
                [jnp.full((1, ATTN_BLOCK), sinks_ref[kh * GROUP + g], F32) for g in range(GROUP)], axis=1)
            m = jnp.maximum(jnp.max(logit, axis=0, keepdims=True), sink)
            p = jnp.exp(logit - m)
            den = jnp.sum(p, axis=0, keepdims=True) + jnp.exp(sink - m)
            o_t = _dot_tn(vpc, p.astype(BF16)) / den
            for g in range(GROUP):
                heads_t[kh * GROUP + g] = o_t[:, g * ATTN_BLOCK:(g + 1) * ATTN_BLOCK]
        attn_t = jnp.concatenate(heads_t, axis=0)
        scale = lax.rsqrt(jnp.mean(attn_t * attn_t, axis=0, keepdims=True) + EPS)
        attn_rows.append((attn_t * scale).T)
    attn = jnp.concatenate(attn_rows, axis=0) * ang_ref[...]

    a = proj_conv[:, :CONV_CH]
    gt = proj_conv[:, CONV_CH:]
    uext_ref[HALO:HALO + tm, :] = a * _sigmoid(gt)
    cw = cw_ref[...]
    base = HALO - (CONV_WIDTH - 1)
    acc = None
    for res in range(SUBLANES):
        part = None
        for hi in range((HALO + SUBLANES - 1) // SUBLANES + 1):
            t = hi * SUBLANES + res - base
            if 0 <= t < CONV_WIDTH:
                term = cw[t:t + 1, :] * uext_ref[hi * SUBLANES:hi * SUBLANES + tm + SUBLANES, :]
                part = term if part is None else part + term
        part = part[res:res + tm, :]
        acc = part if acc is None else acc + part
    uext_ref[0:HALO, :] = uext_ref[tm:tm + HALO, :]
    cv = acc + cb_ref[...]
    mu = jnp.mean(cv, axis=-1, keepdims=True)
    var = jnp.mean(jnp.square(cv - mu), axis=-1, keepdims=True)
    cv = (cv - mu) * lax.rsqrt(var + EPS) * lng_ref[...] + lnb_ref[...]
    cv = cv * _sigmoid(cv)

    mixed = (_dot(attn.astype(BF16), wout_ref[0:Q_COLS, :])
             + _dot(cv.astype(BF16), wout_ref[Q_COLS:, :]))
    x1 = x + g1 * mixed

    h2 = _rms(x1) * (1.0 + sc2) + sh2
    h2_hi, h2_lo = _split_bf16(h2)
    w_hi, w_lo = wrh_ref[...], wrl_ref[...]
    lgt_ref[...] = _dot_nt(w_hi, h2_hi) + (_dot_nt(w_hi, h2_lo) + _dot_nt(w_lo, h2_hi))
    h2b = h2.astype(BF16)
    h2_ref[0] = h2b
    sg = _dot(h2b, wsg_ref[...])
    su = _dot(h2b, wsu_ref[...])
    shared = _dot(((sg * _sigmoid(sg)) * su).astype(BF16), wsd_ref[...])
    xs1_ref[0] = x1 + g2 * shared


def _mixer_call(x, mod3, w_in_b, sinks, rel_bias, ang, cw, cb, lng, lnb, w_out_b, w_rt, wsg_b, wsu_b, wsd_b):
    bsz, seq, d = x.shape
    tm = MIX_ROWS
    nj = seq // tm
    bkt = jnp.asarray(_bucket_table())
    full = lambda shape: pl.BlockSpec(shape, lambda b, j: (0,) * len(shape))
    smem = pl.BlockSpec(memory_space=pltpu.SMEM)
    return pl.pallas_call(
        _mixer_kernel,
        grid=(bsz, nj),
        in_specs=[
            pl.BlockSpec((1, tm, d), lambda b, j: (b, j, 0)),
            pl.BlockSpec((1, 6, d), lambda b, j: (b, 0, 0)),
            full((d, IN_COLS)),
            smem, smem,
            full((2 * ATTN_BLOCK, ATTN_BLOCK)),
            full((1, Q_COLS)),
            full((CONV_WIDTH, CONV_CH)),
            full((1, CONV_CH)), full((1, CONV_CH)), full((1, CONV_CH)),
            full((d, d)),
            full((N_EXPERTS, d)),
            full((d, EXPERT_HIDDEN)), full((d, EXPERT_HIDDEN)), full((EXPERT_HIDDEN, d)),
        ],
        out_specs=[
            pl.BlockSpec((1, tm, d), lambda b, j: (b, j, 0)),
            pl.BlockSpec((1, tm, d), lambda b, j: (b, j, 0)),
            pl.BlockSpec((N_EXPERTS, tm), lambda b, j: (0, b * nj + j)),
        ],
        out_shape=[
            jax.ShapeDtypeStruct((bsz, seq, d), F32),
            jax.ShapeDtypeStruct((bsz, seq, d), BF16),
            jax.ShapeDtypeStruct((N_EXPERTS, bsz * seq), F32),
        ],
        scratch_shapes=[
            pltpu.VMEM((ATTN_BLOCK, 2 * KV_COLS), F32),
            pltpu.VMEM((HALO + tm + SUBLANES, CONV_CH), F32),
            pltpu.VMEM((N_KV_HEADS, 2 * ATTN_BLOCK, GROUP * ATTN_BLOCK), F32),
            pltpu.VMEM((N_EXPERTS, d), BF16),
            pltpu.VMEM((N_EXPERTS, d), BF16),
        ],
        compiler_params=pltpu.CompilerParams(
            dimension_semantics=("arbitrary", "arbitrary"), vmem_limit_bytes=VMEM_LIMIT),
        name="mixer",
    )(x, mod3, w_in_b, sinks, rel_bias, bkt, ang, cw, cb, lng, lnb, w_out_b, w_rt, wsg_b, wsu_b, wsd_b)


def _route_kernel(lgt_ref, rb_ref, lp_ref, gw_ref, cnt_ref):
    tr = lgt_ref.shape[1]
    i = pl.program_id(0)

    @pl.when(i == 0)
    def _():
        cnt_ref[...] = jnp.zeros(cnt_ref.shape, F32)

    scores = _sigmoid(lgt_ref[...])
    sel = scores + rb_ref[...]
    neg = -jnp.inf

    sel3 = sel.reshape(N_GROUPS, GROUP_SIZE, tr)
    loc = lax.broadcasted_iota(I32, sel3.shape, 1)
    m1 = jnp.max(sel3, axis=1, keepdims=True)
    i1 = jnp.min(jnp.where(sel3 == m1, loc, GROUP_SIZE), axis=1, keepdims=True)
    m2 = jnp.max(jnp.where(loc == i1, neg, sel3), axis=1, keepdims=True)
    gscore = (m1 + m2).reshape(N_GROUPS, tr)

    gio = lax.broadcasted_iota(I32, gscore.shape, 0)
    gmask = jnp.zeros(gscore.shape, jnp.bool_)
    cur = gscore
    for _ in range(TOPK_GROUPS):
        m = jnp.max(cur, axis=0, keepdims=True)
        idx = jnp.min(jnp.where(cur == m, gio, N_GROUPS), axis=0, keepdims=True)
        pick = gio == idx
        gmask = gmask | pick
        cur = jnp.where(pick, neg, cur)
    emask = jnp.broadcast_to(gmask.reshape(N_GROUPS, 1, tr), sel3.shape).reshape(N_EXPERTS, tr)

    rio = lax.broadcasted_iota(I32, sel.shape, 0)
    cur = jnp.where(emask, sel, neg)
    picks, gsc = [], []
    for _ in range(TOP_K):
        m = jnp.max(cur, axis=0, keepdims=True)
        idx = jnp.min(jnp.where(cur == m, rio, N_EXPERTS), axis=0, keepdims=True)
        pick = rio == idx
        picks.append(pick)
        gsc.append(jnp.sum(jnp.where(pick, scores, 0.0), axis=0, keepdims=True))
        cur = jnp.where(pick, neg, cur)
    gsum = gsc[0]
    for k in range(1, TOP_K):
        gsum = gsum + gsc[k]
    gw_ref[...] = jnp.concatenate([g / gsum * ROUTED_SCALE for g in gsc], axis=0)

    chosen = picks[0]
    for k in range(1, TOP_K):
        chosen = chosen | picks[k]
    onehot = jnp.where(chosen, 1.0, 0.0)
    tri = (lax.broadcasted_iota(I32, (tr, tr), 0) < lax.broadcasted_iota(I32, (tr, tr), 1))
    before = _dot(onehot.astype(BF16), jnp.where(tri, 1.0, 0.0).astype(BF16))
    n = jnp.sum(onehot, axis=1, keepdims=True)
    low = (lax.broadcasted_iota(I32, (N_EXPERTS, N_EXPERTS), 1)
           < lax.broadcasted_iota(I32, (N_EXPERTS, N_EXPERTS), 0))
    start = _dot(jnp.where(low, 1.0, 0.0).astype(BF16),
                 jnp.broadcast_to(n, (N_EXPERTS, LANES)).astype(BF16))[:, 0:1]
    pos = start + before
    lp_ref[...] = jnp.concatenate(
        [jnp.sum(jnp.where(p, pos, 0.0), axis=0, keepdims=True) for p in picks], axis=0).astype(I32)

    lane = lax.broadcasted_iota(I32, cnt_ref.shape, 1)
    cnt_ref[...] += jnp.where(lane == i, n, 0.0)


def _route_call(lgt, router_bias):
    e, t = lgt.shape
    ntiles = t // TILE
    ntp = (ntiles + LANES - 1) // LANES * LANES
    return pl.pallas_call(
        _route_kernel,
        grid=(ntiles,),
        in_specs=[
            pl.BlockSpec((e, TILE), lambda i: (0, i)),
            pl.BlockSpec((e, 1), lambda i: (0, 0)),
        ],
        out_specs=[
            pl.BlockSpec((TOP_K, TILE), lambda i: (0, i)),
            pl.BlockSpec((TOP_K, TILE), lambda i: (0, i)),
            pl.BlockSpec((e, ntp), lambda i: (0, 0)),
        ],
        out_shape=[
            jax.ShapeDtypeStruct((TOP_K, t), I32),
            jax.ShapeDtypeStruct((TOP_K, t), F32),
            jax.ShapeDtypeStruct((e, ntp), F32),
        ],
        compiler_params=pltpu.CompilerParams(
            dimension_semantics=("arbitrary",), vmem_limit_bytes=VMEM_LIMIT),
        name="route",
    )(lgt, router_bias.reshape(e, 1))


_T_LOCAL, _T_LEN, _T_GLOBAL, _T_FIELDS = 0, 1, 2, 3
_T_STRIDE = 2 * N_EXPERTS
_T_EMPTY = _T_STRIDE * _T_FIELDS
_T_SIZE = _T_EMPTY + 2
LBUF_ROWS = LROWS + N_EXPERTS


def _segment_copies(tab_ref, half, local_ref, global_hbm, sem, to_global):
    def body(e, carry):
        at = half * N_EXPERTS + e
        n = tab_ref[0, 0, _T_LEN * _T_STRIDE + at]
        loc = local_ref.at[pl.ds(tab_ref[0, 0, _T_LOCAL * _T_STRIDE + at], n)]
        glo = global_hbm.at[pl.ds(tab_ref[0, 0, _T_GLOBAL * _T_STRIDE + at], n)]
        if to_global:
            pltpu.make_async_copy(loc, glo, sem).start()
        else:
            pltpu.make_async_copy(glo, loc, sem).start()
        return carry

    for e in range(N_EXPERTS):
        body(e, 0)


def _chunk_relative(lp_row, c):
    rel = lp_row - c * SORT_CHUNK
    inside = (rel >= 0) & (rel < SORT_CHUNK)
    return jnp.where(inside, rel, -1).astype(F32).astype(BF16)


def _chunk_row_ids():
    return lax.broadcasted_iota(I32, (SORT_CHUNK, TILE), 0).astype(F32).astype(BF16)


def _segment_wait(local_ref, global_hbm, sem, nempty):
    rows = LROWS + nempty
    pltpu.make_async_copy(global_hbm.at[pl.ds(0, rows)], local_ref.at[pl.ds(0, rows)], sem).wait()


TAIL_ROWS = EXPERT_ROWS + 2 * N_EXPERTS


def _dispatch_kernel(tab_ref, pad_ref, h2_ref, lp_ref, xs_hbm, xl0, xl1, zb, sem, pending):
    s = pl.program_id(0)
    ns = pl.num_programs(0)

    @pl.when(s == 0)
    def _zero_unassigned_rows():
        zb[...] = jnp.zeros(zb.shape, U32)
        used = pad_ref[2, 0]
        rest = xs_hbm.shape[0] - used
        cp = pltpu.make_async_copy(zb.at[pl.ds(0, rest)], xs_hbm.at[pl.ds(used, rest)], sem.at[2])
        cp.start()
        cp.wait()

        def pad_row(e):
            return pltpu.make_async_copy(zb.at[pl.ds(0, 1)], xs_hbm.at[pl.ds(pad_ref[0, e], 1)], sem.at[2])

        def start(e, carry):
            @pl.when(pad_ref[1, e] > 0)
            def _():
                pad_row(e).start()
            return carry

        def wait(e, carry):
            @pl.when(pad_ref[1, e] > 0)
            def _():
                pad_row(e).wait()
            return carry

        lax.fori_loop(0, N_EXPERTS, start, 0)
        lax.fori_loop(0, N_EXPERTS, wait, 0)

    for half, xl in ((0, xl0), (1, xl1)):
        @pl.when(s > 0)
        def _():
            _segment_wait(xl, xs_hbm, sem.at[half], pending[half])

        rows = h2_ref[half * TILE:(half + 1) * TILE, :]
        lp = lp_ref[:, half * TILE:(half + 1) * TILE]
        j = _chunk_row_ids()
        for c in range(LROWS // SORT_CHUNK):
            onehot = jnp.zeros((SORT_CHUNK, TILE), BF16)
            for k in range(TOP_K):
                onehot = jnp.where(j == _chunk_relative(lp[k:k + 1, :], c), jnp.ones((), BF16), onehot)
            _store_packed(xl, c * SORT_CHUNK, SORT_CHUNK, _pack_rows(_dot(onehot, rows)))
        _segment_copies(tab_ref, half, xl, xs_hbm, sem.at[half], to_global=True)
        pending[half] = tab_ref[0, 0, _T_EMPTY + half]

    @pl.when(s == ns - 1)
    def _():
        _segment_wait(xl0, xs_hbm, sem.at[0], pending[0])
        _segment_wait(xl1, xs_hbm, sem.at[1], pending[1])


def _dispatch_call(tab, pad, h2, lp, nrows):
    t, d = h2.shape
    ns = t // (2 * TILE)
    return pl.pallas_call(
        _dispatch_kernel,
        grid=(ns,),
        in_specs=[
            pl.BlockSpec((1, 1, _T_SIZE), lambda s: (s, 0, 0), memory_space=pltpu.SMEM),
            pl.BlockSpec(memory_space=pltpu.SMEM),
            pl.BlockSpec((2 * TILE, d), lambda s: (s, 0)),
            pl.BlockSpec((TOP_K, 2 * TILE), lambda s: (0, s)),
        ],
        out_specs=pl.BlockSpec(memory_space=pl.ANY),
        out_shape=jax.ShapeDtypeStruct((nrows, PACK_S, LANES), U32),
        scratch_shapes=[
            pltpu.VMEM((LBUF_ROWS, PACK_S, LANES), U32),
            pltpu.VMEM((LBUF_ROWS, PACK_S, LANES), U32),
            pltpu.VMEM((TAIL_ROWS + N_EXPERTS, PACK_S, LANES), U32),
            pltpu.SemaphoreType.DMA((3,)),
            pltpu.SMEM((2,), I32),
        ],
        compiler_params=pltpu.CompilerParams(
            dimension_semantics=("arbitrary",), vmem_limit_bytes=VMEM_LIMIT),
        name="dispatch",
    )(tab, pad, h2, lp)


RING = 6
AHEAD = RING - 2


def _expert_kernel(first_ref, nchunk_ref, row_ref, len_ref, info_ref, wg_ref, wu_ref, wd_ref, xs_hbm, y_hbm,
                   xbuf, ybuf, isem, osem, wgb, wub, wdb):
    e = pl.program_id(0)
    last_step = e == pl.num_programs(0) - 1
    rows = EXPERT_ROWS
    total_chunks = info_ref[0]

    def slab(ref, row0, m):
        size = m * PACK_S if isinstance(m, int) else pl.multiple_of(m * PACK_S, SUBLANES)
        first = row0 * PACK_S if isinstance(row0, int) else pl.multiple_of(row0 * PACK_S, SUBLANES)
        return ref.at[pl.ds(first, size)]

    def fetch(g):
        slot = g % RING
        return pltpu.make_async_copy(slab(xs_hbm, row_ref[g], rows), slab(xbuf, slot * rows, rows), isem.at[slot])

    def flush(g):
        slot = g % RING
        m = len_ref[g]
        return pltpu.make_async_copy(slab(ybuf, slot * rows, m), slab(y_hbm, row_ref[g], m), osem.at[slot])

    def start_fetch(g):
        @pl.when(g < total_chunks)
        def _():
            fetch(g).start()

    def wait_flush(g):
        @pl.when(g >= 0)
        def _():
            flush(g).wait()

    def compute(g):
        slot = g % RING
        xb = _unpack_rows(_load_flat(xbuf, slot * rows, rows))
        gate = _dot(xb, wgb[...])
        up = _dot(xb, wub[...])
        hid = (gate * _sigmoid(gate)) * up
        y = _dot(hid.astype(BF16), wdb[...])
        _store_flat(ybuf, slot * rows, rows, _pack_rows(y.astype(BF16).astype(F32)))

    @pl.when(e == 0)
    def _prime():
        for a in range(AHEAD):
            start_fetch(a)

    wgb[...] = wg_ref[0].astype(BF16)
    wub[...] = wu_ref[0].astype(BF16)
    wdb[...] = wd_ref[0].astype(BF16)

    begin = first_ref[e]
    count = nchunk_ref[e]

    def pair(p, carry):
        g0 = begin + 2 * p
        g1 = g0 + 1
        start_fetch(g0 + AHEAD)
        start_fetch(g1 + AHEAD)
        fetch(g0).wait()
        fetch(g1).wait()
        wait_flush(g0 - RING)
        wait_flush(g1 - RING)
        compute(g0)
        compute(g1)
        flush(g0).start()
        flush(g1).start()
        return carry

    lax.fori_loop(0, count // 2, pair, 0)

    @pl.when(count % 2 == 1)
    def _odd_chunk():
        g = begin + count - 1
        start_fetch(g + AHEAD)
        fetch(g).wait()
        wait_flush(g - RING)
        compute(g)
        flush(g).start()

    @pl.when(last_step)
    def _drain():
        def wait(g, carry):
            flush(g).wait()
            return carry
        lax.fori_loop(jnp.maximum(total_chunks - RING, 0), total_chunks, wait, 0)

    @pl.when(last_step)
    def _define_unused_rows():
        used = info_ref[1]
        rest = y_hbm.shape[0] // PACK_S - used

        @pl.when(rest > 0)
        def _():
            ybuf[0:N_EXPERTS * PACK_S, :] = jnp.zeros((N_EXPERTS * PACK_S, LANES), U32)
            cp = pltpu.make_async_copy(slab(ybuf, 0, rest), slab(y_hbm, used, rest), osem.at[0])
            cp.start()
            cp.wait()


def _expert_call(first, nchunk, chunk_row, chunk_len, info, xs, w_gate, w_up, w_down, nrows):
    ne, d, hid = w_gate.shape
    rows = EXPERT_ROWS
    wsel = lambda e, *_: (e, 0, 0)
    grid_spec = pltpu.PrefetchScalarGridSpec(
        num_scalar_prefetch=5,
        grid=(ne,),
        in_specs=[
            pl.BlockSpec((1, d, hid), wsel),
            pl.BlockSpec((1, d, hid), wsel),
            pl.BlockSpec((1, hid, d), wsel),
            pl.BlockSpec(memory_space=pl.ANY),
        ],
        out_specs=pl.BlockSpec(memory_space=pl.ANY),
        scratch_shapes=[
            pltpu.VMEM((RING * rows * PACK_S, LANES), U32),
            pltpu.VMEM((RING * rows * PACK_S, LANES), U32),
            pltpu.SemaphoreType.DMA((RING,)),
            pltpu.SemaphoreType.DMA((RING,)),
            pltpu.VMEM((d, hid), BF16),
            pltpu.VMEM((d, hid), BF16),
            pltpu.VMEM((hid, d), BF16),
        ],
    )
    return pl.pallas_call(
        _expert_kernel,
        grid_spec=grid_spec,
        out_shape=jax.ShapeDtypeStruct((nrows * PACK_S, LANES), U32),
        compiler_params=pltpu.CompilerParams(
            dimension_semantics=("arbitrary",), vmem_limit_bytes=VMEM_LIMIT),
        name="experts",
    )(first, nchunk, chunk_row, chunk_len, info, w_gate, w_up, w_down,
      xs.reshape(-1, LANES)).reshape(nrows, PACK_S, LANES)


def _combine_kernel(tab_ref, tabn_ref, lp_ref, gw_ref, xs1_ref, mod_ref, fg_ref, y_hbm, o_ref, yl0, yl1, sem):
    s = pl.program_id(0)
    ns = pl.num_programs(0)

    @pl.when(s == 0)
    def _():
        _segment_copies(tab_ref, 0, yl0, y_hbm, sem.at[0], to_global=False)

    g2 = mod_ref[0][5:6]
    for half, yl in ((0, yl0), (1, yl1)):
        _segment_wait(yl, y_hbm, sem.at[half], tab_ref[0, 0, _T_EMPTY + half])
        if half == 0:
            _segment_copies(tab_ref, 1, yl1, y_hbm, sem.at[1], to_global=False)
        lp = lp_ref[:, half * TILE:(half + 1) * TILE]
        gw = gw_ref[:, half * TILE:(half + 1) * TILE]
        acc = jnp.zeros((TILE, D_MODEL), F32)
        gwb = [gw[k:k + 1, :].astype(BF16) for k in range(TOP_K)]
        j = _chunk_row_ids()
        for c in range(LROWS // SORT_CHUNK):
            wt = jnp.zeros((SORT_CHUNK, TILE), BF16)
            for k in range(TOP_K):
                wt = jnp.where(j == _chunk_relative(lp[k:k + 1, :], c), gwb[k], wt)
            yv = _unpack_rows(_load_packed(yl, c * SORT_CHUNK, SORT_CHUNK))
            acc = acc + _dot_tn(wt, yv)
        if half == 0:
            _segment_copies(tabn_ref, 0, yl0, y_hbm, sem.at[0], to_global=False)
        x2 = xs1_ref[half * TILE:(half + 1) * TILE, :] + g2 * acc
        o_ref[half * TILE:(half + 1) * TILE, :] = _rms(x2) * fg_ref[...]

    @pl.when(s == ns - 1)
    def _():
        _segment_wait(yl0, y_hbm, sem.at[0], tab_ref[0, 0, _T_EMPTY])


def _combine_call(tab, lp, gw, xs1, mod3, final_g, y, seq):
    t, d = xs1.shape
    ns = t // (2 * TILE)
    per_seq = seq // (2 * TILE)
    return pl.pallas_call(
        _combine_kernel,
        grid=(ns,),
        in_specs=[
            pl.BlockSpec((1, 1, _T_SIZE), lambda s: (s, 0, 0), memory_space=pltpu.SMEM),
            pl.BlockSpec((1, 1, _T_SIZE), lambda s: (jnp.minimum(s + 1, ns - 1), 0, 0),
                         memory_space=pltpu.SMEM),
            pl.BlockSpec((TOP_K, 2 * TILE), lambda s: (0, s)),
            pl.BlockSpec((TOP_K, 2 * TILE), lambda s: (0, s)),
            pl.BlockSpec((2 * TILE, d), lambda s: (s, 0)),
            pl.BlockSpec((1, 6, d), lambda s: (s // per_seq, 0, 0)),
            pl.BlockSpec((1, d), lambda s: (0, 0)),
            pl.BlockSpec(memory_space=pl.ANY),
        ],
        out_specs=pl.BlockSpec((2 * TILE, d), lambda s: (s, 0)),
        out_shape=jax.ShapeDtypeStruct((t, d), F32),
        scratch_shapes=[
            pltpu.VMEM((LBUF_ROWS, PACK_S, LANES), U32),
            pltpu.VMEM((LBUF_ROWS, PACK_S, LANES), U32),
            pltpu.SemaphoreType.DMA((2,)),
        ],
        compiler_params=pltpu.CompilerParams(
            dimension_semantics=("arbitrary",), vmem_limit_bytes=VMEM_LIMIT),
        name="combine",
    )(tab, tab, lp, gw, xs1, mod3, final_g.reshape(1, d), y)


def kernel(x, c, w_ada, b_ada, w_in, attn_sinks, rel_bias, attn_norm_g, conv_w, conv_b, conv_ln_g,
           conv_ln_b, w_out, w_router, router_bias, w_exp_gate, w_exp_up, w_exp_down, w_sh_gate,
           w_sh_up, w_sh_down, final_norm_g):
    bsz, seq, d = x.shape
    t = bsz * seq
    assert w_ada.shape[0] == 1 and d == D_MODEL
    assert seq % MIX_ROWS == 0 and seq % (2 * TILE) == 0

    mod3 = _ada_call(c, w_ada[0], b_ada[0]).reshape(bsz, 6, d)
    xs1, h2, lgt = _mixer_call(
        x, mod3, w_in[0].astype(BF16), attn_sinks[0], rel_bias,
        attn_norm_g[0].reshape(1, Q_COLS), conv_w[0], conv_b[0].reshape(1, CONV_CH),
        conv_ln_g[0].reshape(1, CONV_CH), conv_ln_b[0].reshape(1, CONV_CH),
        w_out[0].astype(BF16), w_router[0].T,
        w_sh_gate[0].astype(BF16), w_sh_up[0].astype(BF16), w_sh_down[0].astype(BF16))
    xs1 = xs1.reshape(t, d)
    h2 = h2.reshape(t, d)

    lp, gw, cnt = _route_call(lgt, router_bias[0])

    ntiles = t // TILE
    nassign = t * TOP_K
    n = cnt[:, :ntiles].T.astype(I32)
    local = jnp.cumsum(n, axis=1) - n
    earlier = jnp.cumsum(n, axis=0) - n
    total = jnp.sum(n, axis=0)
    region = (total + 1) // 2 * 2
    starts = jnp.cumsum(region) - region
    nalloc = nassign + N_EXPERTS
    used = jnp.sum(region)
    pad = jnp.stack([starts + total, region - total,
                     jnp.broadcast_to(used, (N_EXPERTS,))], axis=0)
    glob = starts[None, :] + earlier
    empty = n == 0
    length = jnp.maximum(n, 1)
    eid = jnp.arange(N_EXPERTS, dtype=I32)[None, :]
    parity = (jnp.arange(ntiles, dtype=I32) % 2)[:, None]
    spare_global = nalloc + EXPERT_ROWS + parity * N_EXPERTS + eid
    nempty = jnp.sum(empty.astype(I32), axis=1).reshape(ntiles // 2, 2)

    def table(loc, glo):
        fields = [f.reshape(ntiles // 2, _T_STRIDE) for f in (loc, length, glo)]
        return jnp.concatenate(fields + [nempty], axis=1).reshape(ntiles // 2, 1, _T_SIZE)

    tab_out = table(jnp.where(empty, 0, local), jnp.where(empty, spare_global, glob))
    tab_back = table(jnp.where(empty, LROWS + eid, local), jnp.where(empty, 0, glob))

    rows = EXPERT_ROWS
    nchunk = (total + rows - 1) // rows
    chunk_end = jnp.cumsum(nchunk)
    first = chunk_end - nchunk
    max_chunks = nassign // rows + N_EXPERTS
    gid = jnp.arange(max_chunks, dtype=I32)
    owner = jnp.minimum(jnp.sum((chunk_end[None, :] <= gid[:, None]).astype(I32), axis=1), N_EXPERTS - 1)
    pick = owner[:, None] == eid
    within = gid - jnp.sum(jnp.where(pick, first[None, :], 0), axis=1)
    chunk_row = jnp.sum(jnp.where(pick, starts[None, :], 0), axis=1) + within * rows
    left = jnp.sum(jnp.where(pick, region[None, :], 0), axis=1) - within * rows
    live = gid < chunk_end[-1]
    chunk_row = jnp.where(live, chunk_row, 0)
    chunk_len = jnp.where(live, jnp.clip(left, 2, rows), 2)
    info = jnp.stack([chunk_end[-1], used])

    xs = _dispatch_call(tab_out, pad, h2, lp, nalloc + TAIL_ROWS)
    y = _expert_call(first, nchunk, chunk_row, chunk_len, info, xs,
                     w_exp_gate[0], w_exp_up[0], w_exp_down[0], nalloc)
    out = _combine_call(tab_back, lp, gw, xs1, mod3, final_norm_g, y, seq)
    return out.reshape(bsz, seq, d)
```

```python
import numpy as np
import jax
import jax.numpy as jnp
from jax import lax
from jax.experimental import pallas as pl
from jax.experimental.pallas import tpu as pltpu

F32 = jnp.float32
BF16 = jnp.bfloat16
U32 = jnp.uint32
I32 = jnp.int32

D_MODEL = 1024
HEAD_DIM = 64
N_HEADS = 8
N_KV_HEADS = 2
GROUP = N_HEADS // N_KV_HEADS
Q_COLS = N_HEADS * HEAD_DIM
KV_COLS = N_KV_HEADS * HEAD_DIM
ATTN_BLOCK = 128
WINDOW = 128
NUM_BUCKETS = 32
MAX_DISTANCE = 128
CONV_CH = D_MODEL - Q_COLS
CONV_WIDTH = 31
IN_COLS = Q_COLS + 2 * KV_COLS + 2 * CONV_CH
N_EXPERTS = 256
TOP_K = 8
N_GROUPS = 8
GROUP_SIZE = N_EXPERTS // N_GROUPS
TOPK_GROUPS = 4
EXPERT_HIDDEN = 256
ROUTED_SCALE = 2.5
EPS = 1e-6

MIX_ROWS = 512
HALO = 32
TILE = 256
LROWS = TILE * TOP_K
SORT_CHUNK = 256
EXPERT_ROWS = 576
LANES = 128
SUBLANES = 8
PACK_W = D_MODEL // 2
PACK_S = PACK_W // LANES
VMEM_LIMIT = 56 * 1024 * 1024


def _sigmoid(v):
    return 1.0 / (1.0 + jnp.exp(-v))


def _rms(v):
    return v * lax.rsqrt(jnp.mean(v * v, axis=-1, keepdims=True) + EPS)


def _split_bf16(a):
    hi = a.astype(BF16)
    lo = (a - hi.astype(F32)).astype(BF16)
    return hi, lo


def _dot(a, b):
    return jnp.dot(a, b, preferred_element_type=F32)


def _dot_nt(a, b):
    return lax.dot_general(a, b, (((1,), (1,)), ((), ())), preferred_element_type=F32)


def _dot_tn(a, b):
    return lax.dot_general(a, b, (((0,), (0,)), ((), ())), preferred_element_type=F32)


def _dot3(a, b, dot):
    ah, al = _split_bf16(a)
    bh, bl = _split_bf16(b)
    return dot(ah, bh) + (dot(ah, bl) + dot(al, bh))


def _pack_rows(v):
    hi = lax.bitcast_convert_type(v[:, :PACK_W], U32) & jnp.uint32(0xFFFF0000)
    lo = lax.bitcast_convert_type(v[:, PACK_W:], U32) >> 16
    return hi | lo


def _unpack_rows(u):
    hi = lax.bitcast_convert_type(u & jnp.uint32(0xFFFF0000), F32)
    lo = lax.bitcast_convert_type(u << 16, F32)
    return jnp.concatenate([hi, lo], axis=1).astype(BF16)


def _load_flat(flat, r0, n):
    return jnp.concatenate(
        [flat[pl.ds(r0 * PACK_S + c, n, stride=PACK_S), :] for c in range(PACK_S)], axis=1)


def _store_flat(flat, r0, n, u):
    for c in range(PACK_S):
        flat[pl.ds(r0 * PACK_S + c, n, stride=PACK_S), :] = u[:, c * LANES:(c + 1) * LANES]


def _load_packed(ref3, r0, n):
    return _load_flat(ref3.reshape(ref3.shape[0] * PACK_S, LANES), r0, n)


def _store_packed(ref3, r0, n, u):
    _store_flat(ref3.reshape(ref3.shape[0] * PACK_S, LANES), r0, n, u)


def _ada_kernel(c_ref, w_ref, b_ref, o_ref):
    c = c_ref[...]
    s = c * _sigmoid(c)
    o_ref[...] = _dot3(s, w_ref[...], _dot) + b_ref[...]


def _ada_call(c, w_ada, b_ada):
    bsz, d = c.shape
    n = w_ada.shape[1]
    tn = 1536
    return pl.pallas_call(
        _ada_kernel,
        grid=(n // tn,),
        in_specs=[
            pl.BlockSpec((bsz, d), lambda i: (0, 0)),
            pl.BlockSpec((d, tn), lambda i: (0, i)),
            pl.BlockSpec((1, tn), lambda i: (0, i)),
        ],
        out_specs=pl.BlockSpec((bsz, tn), lambda i: (0, i)),
        out_shape=jax.ShapeDtypeStruct((bsz, n), F32),
        compiler_params=pltpu.CompilerParams(
            dimension_semantics=("arbitrary",), vmem_limit_bytes=VMEM_LIMIT),
        name="ada",
    )(c, w_ada, b_ada.reshape(1, n))


def _bucket_table():
    qi = np.arange(ATTN_BLOCK)[:, None]
    ki = np.arange(2 * ATTN_BLOCK)[None, :]
    dist = qi + ATTN_BLOCK - ki
    n = np.maximum(dist, 0)
    max_exact = NUM_BUCKETS // 2
    large = max_exact + (np.log(np.maximum(n, 1) / max_exact) / np.log(MAX_DISTANCE / max_exact)
                         * (NUM_BUCKETS - max_exact)).astype(np.int32)
    large = np.minimum(large, NUM_BUCKETS - 1)
    bkt = np.where(n < max_exact, n, large).astype(np.int32)
    band = (dist >= 0) & (dist < WINDOW)
    return np.ascontiguousarray(np.where(band, bkt, -1).astype(np.int32).T)


def _mixer_kernel(x_ref, mod_ref, win_ref, sinks_ref, relb_ref, bkt_ref, ang_ref, cw_ref, cb_ref,
                  lng_ref, lnb_ref, wout_ref, wrt_ref, wsg_ref, wsu_ref, wsd_ref,
                  xs1_ref, h2_ref, lgt_ref,
                  kv_ref, uext_ref, bias_ref, wrh_ref, wrl_ref):
    tm = x_ref.shape[1]
    nblk = tm // ATTN_BLOCK
    j = pl.program_id(1)

    @pl.when((pl.program_id(0) == 0) & (j == 0))
    def _build_bias():
        bkt = bkt_ref[...]
        for h in range(N_HEADS):
            acc = jnp.full(bkt.shape, -jnp.inf, F32)
            for b in range(NUM_BUCKETS):
                acc = jnp.where(bkt == b, relb_ref[b, h], acc)
            g = h % GROUP
            bias_ref[h // GROUP, :, g * ATTN_BLOCK:(g + 1) * ATTN_BLOCK] = acc
        w_hi, w_lo = _split_bf16(wrt_ref[...])
        wrh_ref[...] = w_hi
        wrl_ref[...] = w_lo

    @pl.when(j == 0)
    def _reset_history():
        kv_ref[...] = jnp.zeros(kv_ref.shape, F32)
        uext_ref[0:HALO, :] = jnp.zeros((HALO, CONV_CH), F32)
        uext_ref[HALO + tm:HALO + tm + SUBLANES, :] = jnp.zeros((SUBLANES, CONV_CH), F32)

    x = x_ref[0]
    mod = mod_ref[0]
    sh1, sc1, g1 = mod[0:1], mod[1:2], mod[2:3]
    sh2, sc2, g2 = mod[3:4], mod[4:5], mod[5:6]

    h = _rms(x) * (1.0 + sc1) + sh1
    hb = h.astype(BF16)
    n_qkv = Q_COLS + 2 * KV_COLS
    proj_conv = _dot(hb, win_ref[:, n_qkv:])
    proj = _dot(hb, win_ref[:, :n_qkv])

    qb = (proj[:, :Q_COLS] * (HEAD_DIM ** -0.5)).astype(BF16)
    kv_cur = proj[:, Q_COLS:Q_COLS + 2 * KV_COLS]
    kvb = jnp.concatenate([kv_ref[...], kv_cur], axis=0).astype(BF16)
    kv_ref[...] = kv_cur[tm - ATTN_BLOCK:, :]
    not_first = j > 0
    key = lax.broadcasted_iota(jnp.int32, (2 * ATTN_BLOCK, GROUP * ATTN_BLOCK), 0)
    attn_rows = []
    for i in range(nblk):
        r0, r1, r2 = i * ATTN_BLOCK, (i + 1) * ATTN_BLOCK, (i + 2) * ATTN_BLOCK
        heads_t = [None] * N_HEADS
        for kh in range(N_KV_HEADS):
            kpc = kvb[r0:r2, kh * HEAD_DIM:(kh + 1) * HEAD_DIM]
            vpc = kvb[r0:r2, KV_COLS + kh * HEAD_DIM:KV_COLS + (kh + 1) * HEAD_DIM]
            qg = jnp.concatenate(
                [qb[r0:r1, (kh * GROUP + g) * HEAD_DIM:(kh * GROUP + g + 1) * HEAD_DIM]
                 for g in range(GROUP)], axis=0)
            logit = _dot_nt(kpc, qg) + bias_ref[kh]
            if i == 0:
                logit = jnp.where((key >= ATTN_BLOCK) | not_first, logit, -jnp.inf)
            sink = jnp.concatenate(
                [jnp.full((1, ATTN_BLOCK), sinks_ref[kh * GROUP + g], F32) for g in range(GROUP)], axis=1)
            m = jnp.maximum(jnp.max(logit, axis=0, keepdims=True), sink)
            p = jnp.exp(logit - m)
            den = jnp.sum(p, axis=0, keepdims=True) + jnp.exp(sink - m)
            o_t = _dot_tn(vpc, p.astype(BF16)) / den
            for g in range(GROUP):
                heads_t[kh * GROUP + g] = o_t[:, g * ATTN_BLOCK:(g + 1) * ATTN_BLOCK]
        attn_t = jnp.concatenate(heads_t, axis=0)
        scale = lax.rsqrt(jnp.mean(attn_t * attn_t, axis=0, keepdims=True) + EPS)
        attn_rows.append((attn_t * scale).T)
    attn = jnp.concatenate(attn_rows, axis=0) * ang_ref[...]

    a = proj_conv[:, :CONV_CH]
    gt = proj_conv[:, CONV_CH:]
    uext_ref[HALO:HALO + tm, :] = a * _sigmoid(gt)
    cw = cw_ref[...]
    base = HALO - (CONV_WIDTH - 1)
    acc = None
    for res in range(SUBLANES):
        part = None
        for hi in range((HALO + SUBLANES - 1) // SUBLANES + 1):
            t = hi * SUBLANES + res - base
            if 0 <= t < CONV_WIDTH:
                term = cw[t:t + 1, :] * uext_ref[hi * SUBLANES:hi * SUBLANES + tm + SUBLANES, :]
                part = term if part is None else part + term
        part = part[res:res + tm, :]
        acc = part if acc is None else acc + part
    uext_ref[0:HALO, :] = uext_ref[tm:tm + HALO, :]
    cv = acc + cb_ref[...]
    mu = jnp.mean(cv, axis=-1, keepdims=True)
    var = jnp.mean(jnp.square(cv - mu), axis=-1, keepdims=True)
    cv = (cv - mu) * lax.rsqrt(var + EPS) * lng_ref[...] + lnb_ref[...]
    cv = cv * _sigmoid(cv)

    mixed = (_dot(attn.astype(BF16), wout_ref[0:Q_COLS, :])
             + _dot(cv.astype(BF16), wout_ref[Q_COLS:, :]))
    x1 = x + g1 * mixed

    h2 = _rms(x1) * (1.0 + sc2) + sh2
    h2_hi, h2_lo = _split_bf16(h2)
    w_hi, w_lo = wrh_ref[...], wrl_ref[...]
    lgt_ref[...] = _dot_nt(w_hi, h2_hi) + (_dot_nt(w_hi, h2_lo) + _dot_nt(w_lo, h2_hi))
    h2b = h2.astype(BF16)
    h2_ref[0] = h2b
    sg = _dot(h2b, wsg_ref[...])
    su = _dot(h2b, wsu_ref[...])
    shared = _dot(((sg * _sigmoid(sg)) * su).astype(BF16), wsd_ref[...])
    xs1_ref[0] = x1 + g2 * shared


def _mixer_call(x, mod3, w_in_b, sinks, rel_bias, ang, cw, cb, lng, lnb, w_out_b, w_rt, wsg_b, wsu_b, wsd_b):
    bsz, seq, d = x.shape
    tm = MIX_ROWS
    nj = seq // tm
    bkt = jnp.asarray(_bucket_table())
    full = lambda shape: pl.BlockSpec(shape, lambda b, j: (0,) * len(shape))
    smem = pl.BlockSpec(memory_space=pltpu.SMEM)
    return pl.pallas_call(
        _mixer_kernel,
        grid=(bsz, nj),
        in_specs=[
            pl.BlockSpec((1, tm, d), lambda b, j: (b, j, 0)),
            pl.BlockSpec((1, 6, d), lambda b, j: (b, 0, 0)),
            full((d, IN_COLS)),
            smem, smem,
            full((2 * ATTN_BLOCK, ATTN_BLOCK)),
            full((1, Q_COLS)),
            full((CONV_WIDTH, CONV_CH)),
            full((1, CONV_CH)), full((1, CONV_CH)), full((1, CONV_CH)),
            full((d, d)),
            full((N_EXPERTS, d)),
            full((d, EXPERT_HIDDEN)), full((d, EXPERT_HIDDEN)), full((EXPERT_HIDDEN, d)),
        ],
        out_specs=[
            pl.BlockSpec((1, tm, d), lambda b, j: (b, j, 0)),
            pl.BlockSpec((1, tm, d), lambda b, j: (b, j, 0)),
            pl.BlockSpec((N_EXPERTS, tm), lambda b, j: (0, b * nj + j)),
        ],
        out_shape=[
            jax.ShapeDtypeStruct((bsz, seq, d), F32),
            jax.ShapeDtypeStruct((bsz, seq, d), BF16),
            jax.ShapeDtypeStruct((N_EXPERTS, bsz * seq), F32),
        ],
        scratch_shapes=[
            pltpu.VMEM((ATTN_BLOCK, 2 * KV_COLS), F32),
            pltpu.VMEM((HALO + tm + SUBLANES, CONV_CH), F32),
            pltpu.VMEM((N_KV_HEADS, 2 * ATTN_BLOCK, GROUP * ATTN_BLOCK), F32),
            pltpu.VMEM((N_EXPERTS, d), BF16),
            pltpu.VMEM((N_EXPERTS, d), BF16),
        ],
        compiler_params=pltpu.CompilerParams(
            dimension_semantics=("arbitrary", "arbitrary"), vmem_limit_bytes=VMEM_LIMIT),
        name="mixer",
    )(x, mod3, w_in_b, sinks, rel_bias, bkt, ang, cw, cb, lng, lnb, w_out_b, w_rt, wsg_b, wsu_b, wsd_b)


def _route_kernel(lgt_ref, rb_ref, lp_ref, gw_ref, cnt_ref, earlier_ref, lower_ref):
    tr = lgt_ref.shape[1]
    i = pl.program_id(0)

    @pl.when(i == 0)
    def _():
        cnt_ref[...] = jnp.zeros(cnt_ref.shape, F32)
        earlier_ref[...] = jnp.where(
            lax.broadcasted_iota(I32, (tr, tr), 0) < lax.broadcasted_iota(I32, (tr, tr), 1), 1.0, 0.0).astype(BF16)
        lower_ref[...] = jnp.where(
            lax.broadcasted_iota(I32, (N_EXPERTS, N_EXPERTS), 1)
            < lax.broadcasted_iota(I32, (N_EXPERTS, N_EXPERTS), 0), 1.0, 0.0).astype(BF16)

    scores = _sigmoid(lgt_ref[...])
    sel = scores + rb_ref[...]
    neg = -jnp.inf

    sel3 = sel.reshape(N_GROUPS, GROUP_SIZE, tr)
    loc = lax.broadcasted_iota(I32, sel3.shape, 1)
    m1 = jnp.max(sel3, axis=1, keepdims=True)
    i1 = jnp.min(jnp.where(sel3 == m1, loc, GROUP_SIZE), axis=1, keepdims=True)
    m2 = jnp.max(jnp.where(loc == i1, neg, sel3), axis=1, keepdims=True)
    gscore = (m1 + m2).reshape(N_GROUPS, tr)

    gio = lax.broadcasted_iota(I32, gscore.shape, 0)
    gmask = jnp.zeros(gscore.shape, jnp.bool_)
    cur = gscore
    for _ in range(TOPK_GROUPS):
        m = jnp.max(cur, axis=0, keepdims=True)
        idx = jnp.min(jnp.where(cur == m, gio, N_GROUPS), axis=0, keepdims=True)
        pick = gio == idx
        gmask = gmask | pick
        cur = jnp.where(pick, neg, cur)
    emask = jnp.broadcast_to(gmask.reshape(N_GROUPS, 1, tr), sel3.shape).reshape(N_EXPERTS, tr)

    rio = lax.broadcasted_iota(I32, sel.shape, 0)
    cur = jnp.where(emask, sel, neg)
    picks, gsc = [], []
    for _ in range(TOP_K):
        m = jnp.max(cur, axis=0, keepdims=True)
        idx = jnp.min(jnp.where(cur == m, rio, N_EXPERTS), axis=0, keepdims=True)
        pick = rio == idx
        picks.append(pick)
        gsc.append(jnp.sum(jnp.where(pick, scores, 0.0), axis=0, keepdims=True))
        cur = jnp.where(pick, neg, cur)
    gsum = gsc[0]
    for k in range(1, TOP_K):
        gsum = gsum + gsc[k]
    gw_ref[...] = jnp.concatenate([g / gsum * ROUTED_SCALE for g in gsc], axis=0)

    onehot = jnp.where(cur == neg, jnp.where(emask, 1.0, 0.0), 0.0)
    before = _dot(onehot.astype(BF16), earlier_ref[...])
    n = jnp.sum(onehot, axis=1, keepdims=True)
    start = _dot(lower_ref[...],
                 jnp.broadcast_to(n, (N_EXPERTS, LANES)).astype(BF16))[:, 0:1]
    pos = start + before
    lp_ref[...] = jnp.concatenate(
        [jnp.sum(jnp.where(p, pos, 0.0), axis=0, keepdims=True) for p in picks], axis=0).astype(I32)

    lane = lax.broadcasted_iota(I32, cnt_ref.shape, 1)
    cnt_ref[...] += jnp.where(lane == i, n, 0.0)


def _route_call(lgt, router_bias):
    e, t = lgt.shape
    ntiles = t // TILE
    ntp = (ntiles + LANES - 1) // LANES * LANES
    return pl.pallas_call(
        _route_kernel,
        grid=(ntiles,),
        in_specs=[
            pl.BlockSpec((e, TILE), lambda i: (0, i)),
            pl.BlockSpec((e, 1), lambda i: (0, 0)),
        ],
        out_specs=[
            pl.BlockSpec((TOP_K, TILE), lambda i: (0, i)),
            pl.BlockSpec((TOP_K, TILE), lambda i: (0, i)),
            pl.BlockSpec((e, ntp), lambda i: (0, 0)),
        ],
        out_shape=[
            jax.ShapeDtypeStruct((TOP_K, t), I32),
            jax.ShapeDtypeStruct((TOP_K, t), F32),
            jax.ShapeDtypeStruct((e, ntp), F32),
        ],
        scratch_shapes=[pltpu.VMEM((TILE, TILE), BF16), pltpu.VMEM((e, e), BF16)],
        compiler_params=pltpu.CompilerParams(
            dimension_semantics=("arbitrary",), vmem_limit_bytes=VMEM_LIMIT),
        name="route",
    )(lgt, router_bias.reshape(e, 1))


_T_LOCAL, _T_LEN, _T_GLOBAL, _T_FIELDS = 0, 1, 2, 3
_T_STRIDE = 2 * N_EXPERTS
_T_EMPTY = _T_STRIDE * _T_FIELDS
_T_SIZE = _T_EMPTY + 2
LBUF_ROWS = LROWS + N_EXPERTS


def _segment_copies(tab_ref, half, local_ref, global_hbm, sem, to_global):
    def body(e, carry):
        at = half * N_EXPERTS + e
        n = tab_ref[0, 0, _T_LEN * _T_STRIDE + at]
        loc = local_ref.at[pl.ds(tab_ref[0, 0, _T_LOCAL * _T_STRIDE + at], n)]
        glo = global_hbm.at[pl.ds(tab_ref[0, 0, _T_GLOBAL * _T_STRIDE + at], n)]
        if to_global:
            pltpu.make_async_copy(loc, glo, sem).start()
        else:
            pltpu.make_async_copy(glo, loc, sem).start()
        return carry

    for e in range(N_EXPERTS):
        body(e, 0)


def _chunk_relative(lp_row, c):
    rel = lp_row - c * SORT_CHUNK
    inside = (rel >= 0) & (rel < SORT_CHUNK)
    return jnp.where(inside, rel, -1).astype(F32).astype(BF16)


def _chunk_row_ids():
    return lax.broadcasted_iota(I32, (SORT_CHUNK, TILE), 0).astype(F32).astype(BF16)


def _segment_wait(local_ref, global_hbm, sem, nempty):
    rows = LROWS + nempty
    pltpu.make_async_copy(global_hbm.at[pl.ds(0, rows)], local_ref.at[pl.ds(0, rows)], sem).wait()


TAIL_ROWS = EXPERT_ROWS + 2 * N_EXPERTS


def _dispatch_kernel(tab_ref, pad_ref, h2_ref, lp_ref, xs_hbm, xl0, xl1, zb, sem, pending):
    s = pl.program_id(0)
    ns = pl.num_programs(0)

    @pl.when(s == 0)
    def _zero_unassigned_rows():
        zb[...] = jnp.zeros(zb.shape, U32)
        used = pad_ref[2, 0]
        rest = xs_hbm.shape[0] - used
        cp = pltpu.make_async_copy(zb.at[pl.ds(0, rest)], xs_hbm.at[pl.ds(used, rest)], sem.at[2])
        cp.start()
        cp.wait()

        def pad_row(e):
            return pltpu.make_async_copy(zb.at[pl.ds(0, 1)], xs_hbm.at[pl.ds(pad_ref[0, e], 1)], sem.at[2])

        def start(e, carry):
            @pl.when(pad_ref[1, e] > 0)
            def _():
                pad_row(e).start()
            return carry

        def wait(e, carry):
            @pl.when(pad_ref[1, e] > 0)
            def _():
                pad_row(e).wait()
            return carry

        lax.fori_loop(0, N_EXPERTS, start, 0)
        lax.fori_loop(0, N_EXPERTS, wait, 0)

    for half, xl in ((0, xl0), (1, xl1)):
        @pl.when(s > 0)
        def _():
            _segment_wait(xl, xs_hbm, sem.at[half], pending[half])

        rows = h2_ref[half * TILE:(half + 1) * TILE, :]
        lp = lp_ref[:, half * TILE:(half + 1) * TILE]
        j = _chunk_row_ids()
        for c in range(LROWS // SORT_CHUNK):
            onehot = jnp.zeros((SORT_CHUNK, TILE), BF16)
            for k in range(TOP_K):
                onehot = jnp.where(j == _chunk_relative(lp[k:k + 1, :], c), jnp.ones((), BF16), onehot)
            _store_packed(xl, c * SORT_CHUNK, SORT_CHUNK, _pack_rows(_dot(onehot, rows)))
        _segment_copies(tab_ref, half, xl, xs_hbm, sem.at[half], to_global=True)
        pending[half] = tab_ref[0, 0, _T_EMPTY + half]

    @pl.when(s == ns - 1)
    def _():
        _segment_wait(xl0, xs_hbm, sem.at[0], pending[0])
        _segment_wait(xl1, xs_hbm, sem.at[1], pending[1])


def _dispatch_call(tab, pad, h2, lp, nrows):
    t, d = h2.shape
    ns = t // (2 * TILE)
    return pl.pallas_call(
        _dispatch_kernel,
        grid=(ns,),
        in_specs=[
            pl.BlockSpec((1, 1, _T_SIZE), lambda s: (s, 0, 0), memory_space=pltpu.SMEM),
            pl.BlockSpec(memory_space=pltpu.SMEM),
            pl.BlockSpec((2 * TILE, d), lambda s: (s, 0)),
            pl.BlockSpec((TOP_K, 2 * TILE), lambda s: (0, s)),
        ],
        out_specs=pl.BlockSpec(memory_space=pl.ANY),
        out_shape=jax.ShapeDtypeStruct((nrows, PACK_S, LANES), U32),
        scratch_shapes=[
            pltpu.VMEM((LBUF_ROWS, PACK_S, LANES), U32),
            pltpu.VMEM((LBUF_ROWS, PACK_S, LANES), U32),
            pltpu.VMEM((TAIL_ROWS + N_EXPERTS, PACK_S, LANES), U32),
            pltpu.SemaphoreType.DMA((3,)),
            pltpu.SMEM((2,), I32),
        ],
        compiler_params=pltpu.CompilerParams(
            dimension_semantics=("arbitrary",), vmem_limit_bytes=VMEM_LIMIT),
        name="dispatch",
    )(tab, pad, h2, lp)


RING = 6
AHEAD = RING - 2


def _expert_kernel(first_ref, nchunk_ref, row_ref, len_ref, info_ref, wg_ref, wu_ref, wd_ref, xs_hbm, y_hbm,
                   xbuf, ybuf, isem, osem, wgb, wub, wdb):
    e = pl.program_id(0)
    last_step = e == pl.num_programs(0) - 1
    rows = EXPERT_ROWS
    total_chunks = info_ref[0]

    def slab(ref, row0, m):
        size = m * PACK_S if isinstance(m, int) else pl.multiple_of(m * PACK_S, SUBLANES)
        first = row0 * PACK_S if isinstance(row0, int) else pl.multiple_of(row0 * PACK_S, SUBLANES)
        return ref.at[pl.ds(first, size)]

    def fetch(g):
        slot = g % RING
        return pltpu.make_async_copy(slab(xs_hbm, row_ref[g], rows), slab(xbuf, slot * rows, rows), isem.at[slot])

    def flush(g):
        slot = g % RING
        m = len_ref[g]
        return pltpu.make_async_copy(slab(ybuf, slot * rows, m), slab(y_hbm, row_ref[g], m), osem.at[slot])

    def start_fetch(g):
        @pl.when(g < total_chunks)
        def _():
            fetch(g).start()

    def wait_flush(g):
        @pl.when(g >= 0)
        def _():
            flush(g).wait()

    def compute(g):
        slot = g % RING
        xb = _unpack_rows(_load_flat(xbuf, slot * rows, rows))
        gate = _dot(xb, wgb[...])
        up = _dot(xb, wub[...])
        hid = (gate * _sigmoid(gate)) * up
        y = _dot(hid.astype(BF16), wdb[...])
        _store_flat(ybuf, slot * rows, rows, _pack_rows(y.astype(BF16).astype(F32)))

    @pl.when(e == 0)
    def _prime():
        for a in range(AHEAD):
            start_fetch(a)

    wgb[...] = wg_ref[0].astype(BF16)
    wub[...] = wu_ref[0].astype(BF16)
    wdb[...] = wd_ref[0].astype(BF16)

    begin = first_ref[e]
    count = nchunk_ref[e]

    def pair(p, carry):
        g0 = begin + 2 * p
        g1 = g0 + 1
        start_fetch(g0 + AHEAD)
        start_fetch(g1 + AHEAD)
        fetch(g0).wait()
        fetch(g1).wait()
        wait_flush(g0 - RING)
        wait_flush(g1 - RING)
        compute(g0)
        compute(g1)
        flush(g0).start()
        flush(g1).start()
        return carry

    lax.fori_loop(0, count // 2, pair, 0)

    @pl.when(count % 2 == 1)
    def _odd_chunk():
        g = begin + count - 1
        start_fetch(g + AHEAD)
        fetch(g).wait()
        wait_flush(g - RING)
        compute(g)
        flush(g).start()

    @pl.when(last_step)
    def _drain():
        def wait(g, carry):
            flush(g).wait()
            return carry
        lax.fori_loop(jnp.maximum(total_chunks - RING, 0), total_chunks, wait, 0)

    @pl.when(last_step)
    def _define_unused_rows():
        used = info_ref[1]
        rest = y_hbm.shape[0] // PACK_S - used

        @pl.when(rest > 0)
        def _():
            ybuf[0:N_EXPERTS * PACK_S, :] = jnp.zeros((N_EXPERTS * PACK_S, LANES), U32)
            cp = pltpu.make_async_copy(slab(ybuf, 0, rest), slab(y_hbm, used, rest), osem.at[0])
            cp.start()
            cp.wait()


def _expert_call(first, nchunk, chunk_row, chunk_len, info, xs, w_gate, w_up, w_down, nrows):
    ne, d, hid = w_gate.shape
    rows = EXPERT_ROWS
    wsel = lambda e, *_: (e, 0, 0)
    grid_spec = pltpu.PrefetchScalarGridSpec(
        num_scalar_prefetch=5,
        grid=(ne,),
        in_specs=[
            pl.BlockSpec((1, d, hid), wsel),
            pl.BlockSpec((1, d, hid), wsel),
            pl.BlockSpec((1, hid, d), wsel),
            pl.BlockSpec(memory_space=pl.ANY),
        ],
        out_specs=pl.BlockSpec(memory_space=pl.ANY),
        scratch_shapes=[
            pltpu.VMEM((RING * rows * PACK_S, LANES), U32),
            pltpu.VMEM((RING * rows * PACK_S, LANES), U32),
            pltpu.SemaphoreType.DMA((RING,)),
            pltpu.SemaphoreType.DMA((RING,)),
            pltpu.VMEM((d, hid), BF16),
            pltpu.VMEM((d, hid), BF16),
            pltpu.VMEM((hid, d), BF16),
        ],
    )
    return pl.pallas_call(
        _expert_kernel,
        grid_spec=grid_spec,
        out_shape=jax.ShapeDtypeStruct((nrows * PACK_S, LANES), U32),
        compiler_params=pltpu.CompilerParams(
            dimension_semantics=("arbitrary",), vmem_limit_bytes=VMEM_LIMIT),
        name="experts",
    )(first, nchunk, chunk_row, chunk_len, info, w_gate, w_up, w_down,
      xs.reshape(-1, LANES)).reshape(nrows, PACK_S, LANES)


def _combine_kernel(tab_ref, tabn_ref, lp_ref, gw_ref, xs1_ref, mod_ref, fg_ref, y_hbm, o_ref, yl0, yl1, sem):
    s = pl.program_id(0)
    ns = pl.num_programs(0)

    @pl.when(s == 0)
    def _():
        _segment_copies(tab_ref, 0, yl0, y_hbm, sem.at[0], to_global=False)

    g2 = mod_ref[0][5:6]
    for half, yl in ((0, yl0), (1, yl1)):
        _segment_wait(yl, y_hbm, sem.at[half], tab_ref[0, 0, _T_EMPTY + half])
        if half == 0:
            _segment_copies(tab_ref, 1, yl1, y_hbm, sem.at[1], to_global=False)
        lp = lp_ref[:, half * TILE:(half + 1) * TILE]
        gw = gw_ref[:, half * TILE:(half + 1) * TILE]
        acc = jnp.zeros((TILE, D_MODEL), F32)
        gwb = [gw[k:k + 1, :].astype(BF16) for k in range(TOP_K)]
        j = _chunk_row_ids()
        for c in range(LROWS // SORT_CHUNK):
            wt = jnp.zeros((SORT_CHUNK, TILE), BF16)
            for k in range(TOP_K):
                wt = jnp.where(j == _chunk_relative(lp[k:k + 1, :], c), gwb[k], wt)
            yv = _unpack_rows(_load_packed(yl, c * SORT_CHUNK, SORT_CHUNK))
            acc = acc + _dot_tn(wt, yv)
        if half == 0:
            _segment_copies(tabn_ref, 0, yl0, y_hbm, sem.at[0], to_global=False)
        x2 = xs1_ref[half * TILE:(half + 1) * TILE, :] + g2 * acc
        o_ref[half * TILE:(half + 1) * TILE, :] = _rms(x2) * fg_ref[...]

    @pl.when(s == ns - 1)
    def _():
        _segment_wait(yl0, y_hbm, sem.at[0], tab_ref[0, 0, _T_EMPTY])


def _combine_call(tab, lp, gw, xs1, mod3, final_g, y, seq):
    t, d = xs1.shape
    ns = t // (2 * TILE)
    per_seq = seq // (2 * TILE)
    return pl.pallas_call(
        _combine_kernel,
        grid=(ns,),
        in_specs=[
            pl.BlockSpec((1, 1, _T_SIZE), lambda s: (s, 0, 0), memory_space=pltpu.SMEM),
            pl.BlockSpec((1, 1, _T_SIZE), lambda s: (jnp.minimum(s + 1, ns - 1), 0, 0),
                         memory_space=pltpu.SMEM),
            pl.BlockSpec((TOP_K, 2 * TILE), lambda s: (0, s)),
            pl.BlockSpec((TOP_K, 2 * TILE), lambda s: (0, s)),
            pl.BlockSpec((2 * TILE, d), lambda s: (s, 0)),
            pl.BlockSpec((1, 6, d), lambda s: (s // per_seq, 0, 0)),
            pl.BlockSpec((1, d), lambda s: (0, 0)),
            pl.BlockSpec(memory_space=pl.ANY),
        ],
        out_specs=pl.BlockSpec((2 * TILE, d), lambda s: (s, 0)),
        out_shape=jax.ShapeDtypeStruct((t, d), F32),
        scratch_shapes=[
            pltpu.VMEM((LBUF_ROWS, PACK_S, LANES), U32),
            pltpu.VMEM((LBUF_ROWS, PACK_S, LANES), U32),
            pltpu.SemaphoreType.DMA((2,)),
        ],
        compiler_params=pltpu.CompilerParams(
            dimension_semantics=("arbitrary",), vmem_limit_bytes=VMEM_LIMIT),
        name="combine",
    )(tab, tab, lp, gw, xs1, mod3, final_g.reshape(1, d), y)


def kernel(x, c, w_ada, b_ada, w_in, attn_sinks, rel_bias, attn_norm_g, conv_w, conv_b, conv_ln_g,
           conv_ln_b, w_out, w_router, router_bias, w_exp_gate, w_exp_up, w_exp_down, w_sh_gate,
           w_sh_up, w_sh_down, final_norm_g):
    bsz, seq, d = x.shape
    t = bsz * seq
    assert w_ada.shape[0] == 1 and d == D_MODEL
    assert seq % MIX_ROWS == 0 and seq % (2 * TILE) == 0

    mod3 = _ada_call(c, w_ada[0], b_ada[0]).reshape(bsz, 6, d)
    xs1, h2, lgt = _mixer_call(
        x, mod3, w_in[0].astype(BF16), attn_sinks[0], rel_bias,
        attn_norm_g[0].reshape(1, Q_COLS), conv_w[0], conv_b[0].reshape(1, CONV_CH),
        conv_ln_g[0].reshape(1, CONV_CH), conv_ln_b[0].reshape(1, CONV_CH),
        w_out[0].astype(BF16), w_router[0].T,
        w_sh_gate[0].astype(BF16), w_sh_up[0].astype(BF16), w_sh_down[0].astype(BF16))
    xs1 = xs1.reshape(t, d)
    h2 = h2.reshape(t, d)

    lp, gw, cnt = _route_call(lgt, router_bias[0])

    ntiles = t // TILE
    nassign = t * TOP_K
    n = cnt[:, :ntiles].T.astype(I32)
    local = jnp.cumsum(n, axis=1) - n
    earlier = jnp.cumsum(n, axis=0) - n
    total = jnp.sum(n, axis=0)
    region = (total + 1) // 2 * 2
    starts = jnp.cumsum(region) - region
    nalloc = nassign + N_EXPERTS
    used = jnp.sum(region)
    pad = jnp.stack([starts + total, region - total,
                     jnp.broadcast_to(used, (N_EXPERTS,))], axis=0)
    glob = starts[None, :] + earlier
    empty = n == 0
    length = jnp.maximum(n, 1)
    eid = jnp.arange(N_EXPERTS, dtype=I32)[None, :]
    parity = (jnp.arange(ntiles, dtype=I32) % 2)[:, None]
    spare_global = nalloc + EXPERT_ROWS + parity * N_EXPERTS + eid
    nempty = jnp.sum(empty.astype(I32), axis=1).reshape(ntiles // 2, 2)

    def table(loc, glo):
        fields = [f.reshape(ntiles // 2, _T_STRIDE) for f in (loc, length, glo)]
        return jnp.concatenate(fields + [nempty], axis=1).reshape(ntiles // 2, 1, _T_SIZE)

    tab_out = table(jnp.where(empty, 0, local), jnp.where(empty, spare_global, glob))
    tab_back = table(jnp.where(empty, LROWS + eid, local), jnp.where(empty, 0, glob))

    rows = EXPERT_ROWS
    nchunk = (total + rows - 1) // rows
    chunk_end = jnp.cumsum(nchunk)
    first = chunk_end - nchunk
    max_chunks = nassign // rows + N_EXPERTS
    gid = jnp.arange(max_chunks, dtype=I32)
    owner = jnp.minimum(jnp.sum((chunk_end[None, :] <= gid[:, None]).astype(I32), axis=1), N_EXPERTS - 1)
    pick = owner[:, None] == eid
    within = gid - jnp.sum(jnp.where(pick, first[None, :], 0), axis=1)
    chunk_row = jnp.sum(jnp.where(pick, starts[None, :], 0), axis=1) + within * rows
    left = jnp.sum(jnp.where(pick, region[None, :], 0), axis=1) - within * rows
    live = gid < chunk_end[-1]
    chunk_row = jnp.where(live, chunk_row, 0)
    chunk_len = jnp.where(live, jnp.clip(left, 2, rows), 2)
    info = jnp.stack([chunk_end[-1], used])

    xs = _dispatch_call(tab_out, pad, h2, lp, nalloc + TAIL_ROWS)
    y = _expert_call(first, nchunk, chunk_row, chunk_len, info, xs,
                     w_exp_gate[0], w_exp_up[0], w_exp_down[0], nalloc)
    out = _combine_call(tab_back, lp, gw, xs1, mod3, final_norm_g, y, seq)
    return out.reshape(bsz, seq, d)
```

```python
import numpy as np
import jax
import jax.numpy as jnp
from jax import lax
from jax.experimental import pallas as pl
from jax.experimental.pallas import tpu as pltpu

F32 = jnp.float32
BF16 = jnp.bfloat16
U32 = jnp.uint32
I32 = jnp.int32

D_MODEL = 1024
HEAD_DIM = 64
N_HEADS = 8
N_KV_HEADS = 2
GROUP = N_HEADS // N_KV_HEADS
Q_COLS = N_HEADS * HEAD_DIM
KV_COLS = N_KV_HEADS * HEAD_DIM
ATTN_BLOCK = 128
WINDOW = 128
NUM_BUCKETS = 32
MAX_DISTANCE = 128
CONV_CH = D_MODEL - Q_COLS
CONV_WIDTH = 31
IN_COLS = Q_COLS + 2 * KV_COLS + 2 * CONV_CH
N_EXPERTS = 256
TOP_K = 8
N_GROUPS = 8
GROUP_SIZE = N_EXPERTS // N_GROUPS
TOPK_GROUPS = 4
EXPERT_HIDDEN = 256
ROUTED_SCALE = 2.5
EPS = 1e-6

MIX_ROWS = 512
HALO = 32
TILE = 256
LROWS = TILE * TOP_K
SORT_CHUNK = 256
EXPERT_ROWS = 576
LANES = 128
SUBLANES = 8
PACK_W = D_MODEL // 2
PACK_S = PACK_W // LANES
VMEM_LIMIT = 56 * 1024 * 1024


def _sigmoid(v):
    return 1.0 / (1.0 + jnp.exp(-v))


def _rms(v):
    return v * lax.rsqrt(jnp.mean(v * v, axis=-1, keepdims=True) + EPS)


def _split_bf16(a):
    hi = a.astype(BF16)
    lo = (a - hi.astype(F32)).astype(BF16)
    return hi, lo


def _dot(a, b):
    return jnp.dot(a, b, preferred_element_type=F32)


def _dot_nt(a, b):
    return lax.dot_general(a, b, (((1,), (1,)), ((), ())), preferred_element_type=F32)


def _dot_tn(a, b):
    return lax.dot_general(a, b, (((0,), (0,)), ((), ())), preferred_element_type=F32)


def _dot3(a, b, dot):
    ah, al = _split_bf16(a)
    bh, bl = _split_bf16(b)
    return dot(ah, bh) + (dot(ah, bl) + dot(al, bh))


def _pack_rows(v):
    hi = lax.bitcast_convert_type(v[:, :PACK_W], U32) & jnp.uint32(0xFFFF0000)
    lo = lax.bitcast_convert_type(v[:, PACK_W:], U32) >> 16
    return hi | lo


def _unpack_rows(u):
    hi = lax.bitcast_convert_type(u & jnp.uint32(0xFFFF0000), F32)
    lo = lax.bitcast_convert_type(u << 16, F32)
    return jnp.concatenate([hi, lo], axis=1).astype(BF16)


def _load_flat(flat, r0, n):
    return jnp.concatenate(
        [flat[pl.ds(r0 * PACK_S + c, n, stride=PACK_S), :] for c in range(PACK_S)], axis=1)


def _store_flat(flat, r0, n, u):
    for c in range(PACK_S):
        flat[pl.ds(r0 * PACK_S + c, n, stride=PACK_S), :] = u[:, c * LANES:(c + 1) * LANES]


def _load_packed(ref3, r0, n):
    return _load_flat(ref3.reshape(ref3.shape[0] * PACK_S, LANES), r0, n)


def _store_packed(ref3, r0, n, u):
    _store_flat(ref3.reshape(ref3.shape[0] * PACK_S, LANES), r0, n, u)


def _ada_kernel(c_ref, w_ref, b_ref, o_ref):
    c = c_ref[...]
    s = c * _sigmoid(c)
    o_ref[...] = _dot3(s, w_ref[...], _dot) + b_ref[...]


def _ada_call(c, w_ada, b_ada):
    bsz, d = c.shape
    n = w_ada.shape[1]
    tn = 1536
    return pl.pallas_call(
        _ada_kernel,
        grid=(n // tn,),
        in_specs=[
            pl.BlockSpec((bsz, d), lambda i: (0, 0)),
            pl.BlockSpec((d, tn), lambda i: (0, i)),
            pl.BlockSpec((1, tn), lambda i: (0, i)),
        ],
        out_specs=pl.BlockSpec((bsz, tn), lambda i: (0, i)),
        out_shape=jax.ShapeDtypeStruct((bsz, n), F32),
        compiler_params=pltpu.CompilerParams(
            dimension_semantics=("arbitrary",), vmem_limit_bytes=VMEM_LIMIT),
        name="ada",
    )(c, w_ada, b_ada.reshape(1, n))


def _bucket_table():
    qi = np.arange(ATTN_BLOCK)[:, None]
    ki = np.arange(2 * ATTN_BLOCK)[None, :]
    dist = qi + ATTN_BLOCK - ki
    n = np.maximum(dist, 0)
    max_exact = NUM_BUCKETS // 2
    large = max_exact + (np.log(np.maximum(n, 1) / max_exact) / np.log(MAX_DISTANCE / max_exact)
                         * (NUM_BUCKETS - max_exact)).astype(np.int32)
    large = np.minimum(large, NUM_BUCKETS - 1)
    bkt = np.where(n < max_exact, n, large).astype(np.int32)
    band = (dist >= 0) & (dist < WINDOW)
    return np.ascontiguousarray(np.where(band, bkt, -1).astype(np.int32).T)


def _mixer_kernel(x_ref, mod_ref, win_ref, sinks_ref, relb_ref, bkt_ref, ang_ref, cw_ref, cb_ref,
                  lng_ref, lnb_ref, wout_ref, wrt_ref, wsg_ref, wsu_ref, wsd_ref, ewg_ref, ewu_ref,
                  xs1_ref, h2_ref, lgt_ref, ewgb_ref, ewub_ref,
                  kv_ref, uext_ref, bias_ref, wrh_ref, wrl_ref):
    tm = x_ref.shape[1]
    nblk = tm // ATTN_BLOCK
    j = pl.program_id(1)

    @pl.when((pl.program_id(0) == 0) & (j == 0))
    def _build_bias():
        bkt = bkt_ref[...]
        for h in range(N_HEADS):
            acc = jnp.full(bkt.shape, -jnp.inf, F32)
            for b in range(NUM_BUCKETS):
                acc = jnp.where(bkt == b, relb_ref[b, h], acc)
            g = h % GROUP
            bias_ref[h // GROUP, :, g * ATTN_BLOCK:(g + 1) * ATTN_BLOCK] = acc
        w_hi, w_lo = _split_bf16(wrt_ref[...])
        wrh_ref[...] = w_hi
        wrl_ref[...] = w_lo

    @pl.when(j == 0)
    def _reset_history():
        kv_ref[...] = jnp.zeros(kv_ref.shape, F32)
        uext_ref[0:HALO, :] = jnp.zeros((HALO, CONV_CH), F32)
        uext_ref[HALO + tm:HALO + tm + SUBLANES, :] = jnp.zeros((SUBLANES, CONV_CH), F32)

    x = x_ref[0]
    mod = mod_ref[0]
    sh1, sc1, g1 = mod[0:1], mod[1:2], mod[2:3]
    sh2, sc2, g2 = mod[3:4], mod[4:5], mod[5:6]

    h = _rms(x) * (1.0 + sc1) + sh1
    hb = h.astype(BF16)
    n_qkv = Q_COLS + 2 * KV_COLS
    proj_conv = _dot(hb, win_ref[:, n_qkv:])
    proj = _dot(hb, win_ref[:, :n_qkv])

    qb = (proj[:, :Q_COLS] * (HEAD_DIM ** -0.5)).astype(BF16)
    kv_cur = proj[:, Q_COLS:Q_COLS + 2 * KV_COLS]
    kvb = jnp.concatenate([kv_ref[...], kv_cur], axis=0).astype(BF16)
    kv_ref[...] = kv_cur[tm - ATTN_BLOCK:, :]
    not_first = j > 0
    key = lax.broadcasted_iota(jnp.int32, (2 * ATTN_BLOCK, GROUP * ATTN_BLOCK), 0)
    attn_rows = []
    for i in range(nblk):
        r0, r1, r2 = i * ATTN_BLOCK, (i + 1) * ATTN_BLOCK, (i + 2) * ATTN_BLOCK
        heads_t = [None] * N_HEADS
        for kh in range(N_KV_HEADS):
            kpc = kvb[r0:r2, kh * HEAD_DIM:(kh + 1) * HEAD_DIM]
            vpc = kvb[r0:r2, KV_COLS + kh * HEAD_DIM:KV_COLS + (kh + 1) * HEAD_DIM]
            qg = jnp.concatenate(
                [qb[r0:r1, (kh * GROUP + g) * HEAD_DIM:(kh * GROUP + g + 1) * HEAD_DIM]
                 for g in range(GROUP)], axis=0)
            logit = _dot_nt(kpc, qg) + bias_ref[kh]
            if i == 0:
                logit = jnp.where((key >= ATTN_BLOCK) | not_first, logit, -jnp.inf)
            sink = jnp.concatenate(
                [jnp.full((1, ATTN_BLOCK), sinks_ref[kh * GROUP + g], F32) for g in range(GROUP)], axis=1)
            m = jnp.maximum(jnp.max(logit, axis=0, keepdims=True), sink)
            p = jnp.exp(logit - m)
            den = jnp.sum(p, axis=0, keepdims=True) + jnp.exp(sink - m)
            o_t = _dot_tn(vpc, p.astype(BF16)) / den
            for g in range(GROUP):
                heads_t[kh * GROUP + g] = o_t[:, g * ATTN_BLOCK:(g + 1) * ATTN_BLOCK]
        attn_t = jnp.concatenate(heads_t, axis=0)
        scale = lax.rsqrt(jnp.mean(attn_t * attn_t, axis=0, keepdims=True) + EPS)
        attn_rows.append((attn_t * scale).T)
    attn = jnp.concatenate(attn_rows, axis=0) * ang_ref[...]

    a = proj_conv[:, :CONV_CH]
    gt = proj_conv[:, CONV_CH:]
    uext_ref[HALO:HALO + tm, :] = a * _sigmoid(gt)
    cw = cw_ref[...]
    base = HALO - (CONV_WIDTH - 1)
    acc = None
    for res in range(SUBLANES):
        part = None
        for hi in range((HALO + SUBLANES - 1) // SUBLANES + 1):
            t = hi * SUBLANES + res - base
            if 0 <= t < CONV_WIDTH:
                term = cw[t:t + 1, :] * uext_ref[hi * SUBLANES:hi * SUBLANES + tm + SUBLANES, :]
                part = term if part is None else part + term
        part = part[res:res + tm, :]
        acc = part if acc is None else acc + part
    uext_ref[0:HALO, :] = uext_ref[tm:tm + HALO, :]
    cv = acc + cb_ref[...]
    mu = jnp.mean(cv, axis=-1, keepdims=True)
    var = jnp.mean(jnp.square(cv - mu), axis=-1, keepdims=True)
    cv = (cv - mu) * lax.rsqrt(var + EPS) * lng_ref[...] + lnb_ref[...]
    cv = cv * _sigmoid(cv)

    mixed = (_dot(attn.astype(BF16), wout_ref[0:Q_COLS, :])
             + _dot(cv.astype(BF16), wout_ref[Q_COLS:, :]))
    x1 = x + g1 * mixed

    h2 = _rms(x1) * (1.0 + sc2) + sh2
    h2_hi, h2_lo = _split_bf16(h2)
    w_hi, w_lo = wrh_ref[...], wrl_ref[...]
    lgt_ref[...] = _dot_nt(w_hi, h2_hi) + (_dot_nt(w_hi, h2_lo) + _dot_nt(w_lo, h2_hi))
    h2b = h2.astype(BF16)
    h2_ref[0] = h2b
    sg = _dot(h2b, wsg_ref[...])
    su = _dot(h2b, wsu_ref[...])
    shared = _dot(((sg * _sigmoid(sg)) * su).astype(BF16), wsd_ref[...])
    xs1_ref[0] = x1 + g2 * shared

    ewgb_ref[...] = ewg_ref[...].astype(BF16)
    ewub_ref[...] = ewu_ref[...].astype(BF16)


def _mixer_call(x, mod3, w_in_b, sinks, rel_bias, ang, cw, cb, lng, lnb, w_out_b, w_rt, wsg_b, wsu_b, wsd_b,
                w_gate, w_up):
    bsz, seq, d = x.shape
    tm = MIX_ROWS
    nj = seq // tm
    ne, _, hid = w_gate.shape
    per_step = ne // (bsz * nj)
    assert per_step * bsz * nj == ne
    bkt = jnp.asarray(_bucket_table())
    full = lambda shape: pl.BlockSpec(shape, lambda b, j: (0,) * len(shape), pipeline_mode=pl.Buffered(1))
    share = lambda shape: pl.BlockSpec((per_step,) + shape, lambda b, j: (b * nj + j, 0, 0))
    smem = pl.BlockSpec(memory_space=pltpu.SMEM)
    return pl.pallas_call(
        _mixer_kernel,
        grid=(bsz, nj),
        in_specs=[
            pl.BlockSpec((1, tm, d), lambda b, j: (b, j, 0)),
            pl.BlockSpec((1, 6, d), lambda b, j: (b, 0, 0)),
            full((d, IN_COLS)),
            smem, smem,
            full((2 * ATTN_BLOCK, ATTN_BLOCK)),
            full((1, Q_COLS)),
            full((CONV_WIDTH, CONV_CH)),
            full((1, CONV_CH)), full((1, CONV_CH)), full((1, CONV_CH)),
            full((d, d)),
            full((N_EXPERTS, d)),
            full((d, EXPERT_HIDDEN)), full((d, EXPERT_HIDDEN)), full((EXPERT_HIDDEN, d)),
            share((d, hid)), share((d, hid)),
        ],
        out_specs=[
            pl.BlockSpec((1, tm, d), lambda b, j: (b, j, 0)),
            pl.BlockSpec((1, tm, d), lambda b, j: (b, j, 0)),
            pl.BlockSpec((N_EXPERTS, tm), lambda b, j: (0, b * nj + j)),
            share((d, hid)), share((d, hid)),
        ],
        out_shape=[
            jax.ShapeDtypeStruct((bsz, seq, d), F32),
            jax.ShapeDtypeStruct((bsz, seq, d), BF16),
            jax.ShapeDtypeStruct((N_EXPERTS, bsz * seq), F32),
            jax.ShapeDtypeStruct((ne, d, hid), BF16),
            jax.ShapeDtypeStruct((ne, d, hid), BF16),
        ],
        scratch_shapes=[
            pltpu.VMEM((ATTN_BLOCK, 2 * KV_COLS), F32),
            pltpu.VMEM((HALO + tm + SUBLANES, CONV_CH), F32),
            pltpu.VMEM((N_KV_HEADS, 2 * ATTN_BLOCK, GROUP * ATTN_BLOCK), F32),
            pltpu.VMEM((N_EXPERTS, d), BF16),
            pltpu.VMEM((N_EXPERTS, d), BF16),
        ],
        compiler_params=pltpu.CompilerParams(
            dimension_semantics=("arbitrary", "arbitrary"), vmem_limit_bytes=VMEM_LIMIT),
        name="mixer",
    )(x, mod3, w_in_b, sinks, rel_bias, bkt, ang, cw, cb, lng, lnb, w_out_b, w_rt, wsg_b, wsu_b, wsd_b,
      w_gate, w_up)


def _route_kernel(lgt_ref, rb_ref, lp_ref, gw_ref, cnt_ref, earlier_ref, lower_ref):
    tr = lgt_ref.shape[1]
    i = pl.program_id(0)

    @pl.when(i == 0)
    def _():
        cnt_ref[...] = jnp.zeros(cnt_ref.shape, F32)
        earlier_ref[...] = jnp.where(
            lax.broadcasted_iota(I32, (tr, tr), 0) < lax.broadcasted_iota(I32, (tr, tr), 1), 1.0, 0.0).astype(BF16)
        lower_ref[...] = jnp.where(
            lax.broadcasted_iota(I32, (N_EXPERTS, N_EXPERTS), 1)
            < lax.broadcasted_iota(I32, (N_EXPERTS, N_EXPERTS), 0), 1.0, 0.0).astype(BF16)

    scores = _sigmoid(lgt_ref[...])
    sel = scores + rb_ref[...]
    neg = -jnp.inf

    sel3 = sel.reshape(N_GROUPS, GROUP_SIZE, tr)
    loc = lax.broadcasted_iota(I32, sel3.shape, 1)
    m1 = jnp.max(sel3, axis=1, keepdims=True)
    i1 = jnp.min(jnp.where(sel3 == m1, loc, GROUP_SIZE), axis=1, keepdims=True)
    m2 = jnp.max(jnp.where(loc == i1, neg, sel3), axis=1, keepdims=True)
    gscore = (m1 + m2).reshape(N_GROUPS, tr)

    gio = lax.broadcasted_iota(I32, gscore.shape, 0)
    gmask = jnp.zeros(gscore.shape, jnp.bool_)
    cur = gscore
    for _ in range(TOPK_GROUPS):
        m = jnp.max(cur, axis=0, keepdims=True)
        idx = jnp.min(jnp.where(cur == m, gio, N_GROUPS), axis=0, keepdims=True)
        pick = gio == idx
        gmask = gmask | pick
        cur = jnp.where(pick, neg, cur)
    emask = jnp.broadcast_to(gmask.reshape(N_GROUPS, 1, tr), sel3.shape).reshape(N_EXPERTS, tr)

    rio = lax.broadcasted_iota(I32, sel.shape, 0)
    cur = jnp.where(emask, sel, neg)
    picks, gsc = [], []
    for _ in range(TOP_K):
        m = jnp.max(cur, axis=0, keepdims=True)
        idx = jnp.min(jnp.where(cur == m, rio, N_EXPERTS), axis=0, keepdims=True)
        pick = rio == idx
        picks.append(pick)
        gsc.append(jnp.sum(jnp.where(pick, scores, 0.0), axis=0, keepdims=True))
        cur = jnp.where(pick, neg, cur)
    gsum = gsc[0]
    for k in range(1, TOP_K):
        gsum = gsum + gsc[k]
    gw_ref[...] = jnp.concatenate([g / gsum * ROUTED_SCALE for g in gsc], axis=0)

    onehot = jnp.where(cur == neg, jnp.where(emask, 1.0, 0.0), 0.0)
    before = _dot(onehot.astype(BF16), earlier_ref[...])
    n = jnp.sum(onehot, axis=1, keepdims=True)
    start = _dot(lower_ref[...],
                 jnp.broadcast_to(n, (N_EXPERTS, LANES)).astype(BF16))[:, 0:1]
    pos = start + before
    lp_ref[...] = jnp.concatenate(
        [jnp.sum(jnp.where(p, pos, 0.0), axis=0, keepdims=True) for p in picks], axis=0).astype(I32)

    lane = lax.broadcasted_iota(I32, cnt_ref.shape, 1)
    cnt_ref[...] += jnp.where(lane == i, n, 0.0)


def _route_call(lgt, router_bias):
    e, t = lgt.shape
    ntiles = t // TILE
    ntp = (ntiles + LANES - 1) // LANES * LANES
    return pl.pallas_call(
        _route_kernel,
        grid=(ntiles,),
        in_specs=[
            pl.BlockSpec((e, TILE), lambda i: (0, i)),
            pl.BlockSpec((e, 1), lambda i: (0, 0)),
        ],
        out_specs=[
            pl.BlockSpec((TOP_K, TILE), lambda i: (0, i)),
            pl.BlockSpec((TOP_K, TILE), lambda i: (0, i)),
            pl.BlockSpec((e, ntp), lambda i: (0, 0)),
        ],
        out_shape=[
            jax.ShapeDtypeStruct((TOP_K, t), I32),
            jax.ShapeDtypeStruct((TOP_K, t), F32),
            jax.ShapeDtypeStruct((e, ntp), F32),
        ],
        scratch_shapes=[pltpu.VMEM((TILE, TILE), BF16), pltpu.VMEM((e, e), BF16)],
        compiler_params=pltpu.CompilerParams(
            dimension_semantics=("arbitrary",), vmem_limit_bytes=VMEM_LIMIT),
        name="route",
    )(lgt, router_bias.reshape(e, 1))


_T_LOCAL, _T_LEN, _T_GLOBAL, _T_FIELDS = 0, 1, 2, 3
_T_STRIDE = 2 * N_EXPERTS
_T_EMPTY = _T_STRIDE * _T_FIELDS
_T_SIZE = _T_EMPTY + 2
LBUF_ROWS = LROWS + N_EXPERTS


def _segment_copies(tab_ref, half, local_ref, global_hbm, sem, to_global):
    def body(e, carry):
        at = half * N_EXPERTS + e
        n = tab_ref[0, 0, _T_LEN * _T_STRIDE + at]
        loc = local_ref.at[pl.ds(tab_ref[0, 0, _T_LOCAL * _T_STRIDE + at], n)]
        glo = global_hbm.at[pl.ds(tab_ref[0, 0, _T_GLOBAL * _T_STRIDE + at], n)]
        if to_global:
            pltpu.make_async_copy(loc, glo, sem).start()
        else:
            pltpu.make_async_copy(glo, loc, sem).start()
        return carry

    for e in range(N_EXPERTS):
        body(e, 0)


def _chunk_relative(lp_row, c):
    rel = lp_row - c * SORT_CHUNK
    inside = (rel >= 0) & (rel < SORT_CHUNK)
    return jnp.where(inside, rel, -1).astype(F32).astype(BF16)


def _chunk_row_ids():
    return lax.broadcasted_iota(I32, (SORT_CHUNK, TILE), 0).astype(F32).astype(BF16)


def _segment_wait(local_ref, global_hbm, sem, nempty):
    rows = LROWS + nempty
    pltpu.make_async_copy(global_hbm.at[pl.ds(0, rows)], local_ref.at[pl.ds(0, rows)], sem).wait()


TAIL_ROWS = EXPERT_ROWS + 2 * N_EXPERTS


def _dispatch_kernel(tab_ref, pad_ref, h2_ref, lp_ref, xs_hbm, xl0, xl1, zb, sem, pending):
    s = pl.program_id(0)
    ns = pl.num_programs(0)

    @pl.when(s == 0)
    def _zero_unassigned_rows():
        zb[...] = jnp.zeros(zb.shape, U32)
        used = pad_ref[2, 0]
        rest = xs_hbm.shape[0] - used
        cp = pltpu.make_async_copy(zb.at[pl.ds(0, rest)], xs_hbm.at[pl.ds(used, rest)], sem.at[2])
        cp.start()
        cp.wait()

        def pad_row(e):
            return pltpu.make_async_copy(zb.at[pl.ds(0, 1)], xs_hbm.at[pl.ds(pad_ref[0, e], 1)], sem.at[2])

        def start(e, carry):
            @pl.when(pad_ref[1, e] > 0)
            def _():
                pad_row(e).start()
            return carry

        def wait(e, carry):
            @pl.when(pad_ref[1, e] > 0)
            def _():
                pad_row(e).wait()
            return carry

        lax.fori_loop(0, N_EXPERTS, start, 0)
        lax.fori_loop(0, N_EXPERTS, wait, 0)

    for half, xl in ((0, xl0), (1, xl1)):
        @pl.when(s > 0)
        def _():
            _segment_wait(xl, xs_hbm, sem.at[half], pending[half])

        rows = h2_ref[half * TILE:(half + 1) * TILE, :]
        lp = lp_ref[:, half * TILE:(half + 1) * TILE]
        j = _chunk_row_ids()
        for c in range(LROWS // SORT_CHUNK):
            onehot = jnp.zeros((SORT_CHUNK, TILE), BF16)
            for k in range(TOP_K):
                onehot = jnp.where(j == _chunk_relative(lp[k:k + 1, :], c), jnp.ones((), BF16), onehot)
            _store_packed(xl, c * SORT_CHUNK, SORT_CHUNK, _pack_rows(_dot(onehot, rows)))
        _segment_copies(tab_ref, half, xl, xs_hbm, sem.at[half], to_global=True)
        pending[half] = tab_ref[0, 0, _T_EMPTY + half]

    @pl.when(s == ns - 1)
    def _():
        _segment_wait(xl0, xs_hbm, sem.at[0], pending[0])
        _segment_wait(xl1, xs_hbm, sem.at[1], pending[1])


def _dispatch_call(tab, pad, h2, lp, nrows):
    t, d = h2.shape
    ns = t // (2 * TILE)
    return pl.pallas_call(
        _dispatch_kernel,
        grid=(ns,),
        in_specs=[
            pl.BlockSpec((1, 1, _T_SIZE), lambda s: (s, 0, 0), memory_space=pltpu.SMEM),
            pl.BlockSpec(memory_space=pltpu.SMEM),
            pl.BlockSpec((2 * TILE, d), lambda s: (s, 0)),
            pl.BlockSpec((TOP_K, 2 * TILE), lambda s: (0, s)),
        ],
        out_specs=pl.BlockSpec(memory_space=pl.ANY),
        out_shape=jax.ShapeDtypeStruct((nrows, PACK_S, LANES), U32),
        scratch_shapes=[
            pltpu.VMEM((LBUF_ROWS, PACK_S, LANES), U32),
            pltpu.VMEM((LBUF_ROWS, PACK_S, LANES), U32),
            pltpu.VMEM((TAIL_ROWS + N_EXPERTS, PACK_S, LANES), U32),
            pltpu.SemaphoreType.DMA((3,)),
            pltpu.SMEM((2,), I32),
        ],
        compiler_params=pltpu.CompilerParams(
            dimension_semantics=("arbitrary",), vmem_limit_bytes=VMEM_LIMIT),
        name="dispatch",
    )(tab, pad, h2, lp)


RING = 6
AHEAD = RING - 2


def _expert_kernel(first_ref, nchunk_ref, row_ref, len_ref, info_ref, wg_ref, wu_ref, wd_ref, xs_hbm, y_hbm,
                   xbuf, ybuf, isem, osem, wdb):
    e = pl.program_id(0)
    last_step = e == pl.num_programs(0) - 1
    rows = EXPERT_ROWS
    total_chunks = info_ref[0]

    def slab(ref, row0, m):
        size = m * PACK_S if isinstance(m, int) else pl.multiple_of(m * PACK_S, SUBLANES)
        first = row0 * PACK_S if isinstance(row0, int) else pl.multiple_of(row0 * PACK_S, SUBLANES)
        return ref.at[pl.ds(first, size)]

    def fetch(g):
        slot = g % RING
        return pltpu.make_async_copy(slab(xs_hbm, row_ref[g], rows), slab(xbuf, slot * rows, rows), isem.at[slot])

    def flush(g):
        slot = g % RING
        m = len_ref[g]
        return pltpu.make_async_copy(slab(ybuf, slot * rows, m), slab(y_hbm, row_ref[g], m), osem.at[slot])

    def start_fetch(g):
        @pl.when(g < total_chunks)
        def _():
            fetch(g).start()

    def wait_flush(g):
        @pl.when(g >= 0)
        def _():
            flush(g).wait()

    def compute(g):
        slot = g % RING
        xb = _unpack_rows(_load_flat(xbuf, slot * rows, rows))
        gate = _dot(xb, wg_ref[0])
        up = _dot(xb, wu_ref[0])
        hid = (gate * _sigmoid(gate)) * up
        y = _dot(hid.astype(BF16), wdb[...])
        _store_flat(ybuf, slot * rows, rows, _pack_rows(y.astype(BF16).astype(F32)))

    @pl.when(e == 0)
    def _prime():
        for a in range(AHEAD):
            start_fetch(a)

    wdb[...] = wd_ref[0].astype(BF16)

    begin = first_ref[e]
    count = nchunk_ref[e]

    def pair(p, carry):
        g0 = begin + 2 * p
        g1 = g0 + 1
        start_fetch(g0 + AHEAD)
        start_fetch(g1 + AHEAD)
        fetch(g0).wait()
        fetch(g1).wait()
        wait_flush(g0 - RING)
        wait_flush(g1 - RING)
        compute(g0)
        compute(g1)
        flush(g0).start()
        flush(g1).start()
        return carry

    lax.fori_loop(0, count // 2, pair, 0)

    @pl.when(count % 2 == 1)
    def _odd_chunk():
        g = begin + count - 1
        start_fetch(g + AHEAD)
        fetch(g).wait()
        wait_flush(g - RING)
        compute(g)
        flush(g).start()

    @pl.when(last_step)
    def _drain():
        def wait(g, carry):
            flush(g).wait()
            return carry
        lax.fori_loop(jnp.maximum(total_chunks - RING, 0), total_chunks, wait, 0)

    @pl.when(last_step)
    def _define_unused_rows():
        used = info_ref[1]
        rest = y_hbm.shape[0] // PACK_S - used

        @pl.when(rest > 0)
        def _():
            ybuf[0:N_EXPERTS * PACK_S, :] = jnp.zeros((N_EXPERTS * PACK_S, LANES), U32)
            cp = pltpu.make_async_copy(slab(ybuf, 0, rest), slab(y_hbm, used, rest), osem.at[0])
            cp.start()
            cp.wait()


def _expert_call(first, nchunk, chunk_row, chunk_len, info, xs, w_gate, w_up, w_down, nrows):
    ne, d, hid = w_gate.shape
    rows = EXPERT_ROWS
    wsel = lambda e, *_: (e, 0, 0)
    grid_spec = pltpu.PrefetchScalarGridSpec(
        num_scalar_prefetch=5,
        grid=(ne,),
        in_specs=[
            pl.BlockSpec((1, d, hid), wsel),
            pl.BlockSpec((1, d, hid), wsel),
            pl.BlockSpec((1, hid, d), wsel),
            pl.BlockSpec(memory_space=pl.ANY),
        ],
        out_specs=pl.BlockSpec(memory_space=pl.ANY),
        scratch_shapes=[
            pltpu.VMEM((RING * rows * PACK_S, LANES), U32),
            pltpu.VMEM((RING * rows * PACK_S, LANES), U32),
            pltpu.SemaphoreType.DMA((RING,)),
            pltpu.SemaphoreType.DMA((RING,)),
            pltpu.VMEM((hid, d), BF16),
        ],
    )
    return pl.pallas_call(
        _expert_kernel,
        grid_spec=grid_spec,
        out_shape=jax.ShapeDtypeStruct((nrows * PACK_S, LANES), U32),
        compiler_params=pltpu.CompilerParams(
            dimension_semantics=("arbitrary",), vmem_limit_bytes=VMEM_LIMIT),
        name="experts",
    )(first, nchunk, chunk_row, chunk_len, info, w_gate, w_up, w_down,
      xs.reshape(-1, LANES)).reshape(nrows, PACK_S, LANES)


def _combine_kernel(tab_ref, tabn_ref, lp_ref, gw_ref, xs1_ref, mod_ref, fg_ref, y_hbm, o_ref, yl0, yl1, sem):
    s = pl.program_id(0)
    ns = pl.num_programs(0)

    @pl.when(s == 0)
    def _():
        _segment_copies(tab_ref, 0, yl0, y_hbm, sem.at[0], to_global=False)

    g2 = mod_ref[0][5:6]
    for half, yl in ((0, yl0), (1, yl1)):
        _segment_wait(yl, y_hbm, sem.at[half], tab_ref[0, 0, _T_EMPTY + half])
        if half == 0:
            _segment_copies(tab_ref, 1, yl1, y_hbm, sem.at[1], to_global=False)
        lp = lp_ref[:, half * TILE:(half + 1) * TILE]
        gw = gw_ref[:, half * TILE:(half + 1) * TILE]
        acc = jnp.zeros((TILE, D_MODEL), F32)
        gwb = [gw[k:k + 1, :].astype(BF16) for k in range(TOP_K)]
        j = _chunk_row_ids()
        for c in range(LROWS // SORT_CHUNK):
            wt = jnp.zeros((SORT_CHUNK, TILE), BF16)
            for k in range(TOP_K):
                wt = jnp.where(j == _chunk_relative(lp[k:k + 1, :], c), gwb[k], wt)
            yv = _unpack_rows(_load_packed(yl, c * SORT_CHUNK, SORT_CHUNK))
            acc = acc + _dot_tn(wt, yv)
        if half == 0:
            _segment_copies(tabn_ref, 0, yl0, y_hbm, sem.at[0], to_global=False)
        x2 = xs1_ref[half * TILE:(half + 1) * TILE, :] + g2 * acc
        o_ref[half * TILE:(half + 1) * TILE, :] = _rms(x2) * fg_ref[...]

    @pl.when(s == ns - 1)
    def _():
        _segment_wait(yl0, y_hbm, sem.at[0], tab_ref[0, 0, _T_EMPTY])


def _combine_call(tab, lp, gw, xs1, mod3, final_g, y, seq):
    t, d = xs1.shape
    ns = t // (2 * TILE)
    per_seq = seq // (2 * TILE)
    return pl.pallas_call(
        _combine_kernel,
        grid=(ns,),
        in_specs=[
            pl.BlockSpec((1, 1, _T_SIZE), lambda s: (s, 0, 0), memory_space=pltpu.SMEM),
            pl.BlockSpec((1, 1, _T_SIZE), lambda s: (jnp.minimum(s + 1, ns - 1), 0, 0),
                         memory_space=pltpu.SMEM),
            pl.BlockSpec((TOP_K, 2 * TILE), lambda s: (0, s)),
            pl.BlockSpec((TOP_K, 2 * TILE), lambda s: (0, s)),
            pl.BlockSpec((2 * TILE, d), lambda s: (s, 0)),
            pl.BlockSpec((1, 6, d), lambda s: (s // per_seq, 0, 0)),
            pl.BlockSpec((1, d), lambda s: (0, 0)),
            pl.BlockSpec(memory_space=pl.ANY),
        ],
        out_specs=pl.BlockSpec((2 * TILE, d), lambda s: (s, 0)),
        out_shape=jax.ShapeDtypeStruct((t, d), F32),
        scratch_shapes=[
            pltpu.VMEM((LBUF_ROWS, PACK_S, LANES), U32),
            pltpu.VMEM((LBUF_ROWS, PACK_S, LANES), U32),
            pltpu.SemaphoreType.DMA((2,)),
        ],
        compiler_params=pltpu.CompilerParams(
            dimension_semantics=("arbitrary",), vmem_limit_bytes=VMEM_LIMIT),
        name="combine",
    )(tab, tab, lp, gw, xs1, mod3, final_g.reshape(1, d), y)


def kernel(x, c, w_ada, b_ada, w_in, attn_sinks, rel_bias, attn_norm_g, conv_w, conv_b, conv_ln_g,
           conv_ln_b, w_out, w_router, router_bias, w_exp_gate, w_exp_up, w_exp_down, w_sh_gate,
           w_sh_up, w_sh_down, final_norm_g):
    bsz, seq, d = x.shape
    t = bsz * seq
    assert w_ada.shape[0] == 1 and d == D_MODEL
    assert seq % MIX_ROWS == 0 and seq % (2 * TILE) == 0

    mod3 = _ada_call(c, w_ada[0], b_ada[0]).reshape(bsz, 6, d)
    xs1, h2, lgt, wg_b, wu_b = _mixer_call(
        x, mod3, w_in[0].astype(BF16), attn_sinks[0], rel_bias,
        attn_norm_g[0].reshape(1, Q_COLS), conv_w[0], conv_b[0].reshape(1, CONV_CH),
        conv_ln_g[0].reshape(1, CONV_CH), conv_ln_b[0].reshape(1, CONV_CH),
        w_out[0].astype(BF16), w_router[0].T,
        w_sh_gate[0].astype(BF16), w_sh_up[0].astype(BF16), w_sh_down[0].astype(BF16),
        w_exp_gate[0], w_exp_up[0])
    xs1 = xs1.reshape(t, d)
    h2 = h2.reshape(t, d)

    lp, gw, cnt = _route_call(lgt, router_bias[0])

    ntiles = t // TILE
    nassign = t * TOP_K
    n = cnt[:, :ntiles].T.astype(I32)
    local = jnp.cumsum(n, axis=1) - n
    earlier = jnp.cumsum(n, axis=0) - n
    total = jnp.sum(n, axis=0)
    region = (total + 1) // 2 * 2
    starts = jnp.cumsum(region) - region
    nalloc = nassign + N_EXPERTS
    used = jnp.sum(region)
    pad = jnp.stack([starts + total, region - total,
                     jnp.broadcast_to(used, (N_EXPERTS,))], axis=0)
    glob = starts[None, :] + earlier
    empty = n == 0
    length = jnp.maximum(n, 1)
    eid = jnp.arange(N_EXPERTS, dtype=I32)[None, :]
    parity = (jnp.arange(ntiles, dtype=I32) % 2)[:, None]
    spare_global = nalloc + EXPERT_ROWS + parity * N_EXPERTS + eid
    nempty = jnp.sum(empty.astype(I32), axis=1).reshape(ntiles // 2, 2)

    def table(loc, glo):
        fields = [f.reshape(ntiles // 2, _T_STRIDE) for f in (loc, length, glo)]
        return jnp.concatenate(fields + [nempty], axis=1).reshape(ntiles // 2, 1, _T_SIZE)

    tab_out = table(jnp.where(empty, 0, local), jnp.where(empty, spare_global, glob))
    tab_back = table(jnp.where(empty, LROWS + eid, local), jnp.where(empty, 0, glob))

    rows = EXPERT_ROWS
    nchunk = (total + rows - 1) // rows
    chunk_end = jnp.cumsum(nchunk)
    first = chunk_end - nchunk
    max_chunks = nassign // rows + N_EXPERTS
    gid = jnp.arange(max_chunks, dtype=I32)
    owner = jnp.minimum(jnp.sum((chunk_end[None, :] <= gid[:, None]).astype(I32), axis=1), N_EXPERTS - 1)
    pick = owner[:, None] == eid
    within = gid - jnp.sum(jnp.where(pick, first[None, :], 0), axis=1)
    chunk_row = jnp.sum(jnp.where(pick, starts[None, :], 0), axis=1) + within * rows
    left = jnp.sum(jnp.where(pick, region[None, :], 0), axis=1) - within * rows
    live = gid < chunk_end[-1]
    chunk_row = jnp.where(live, chunk_row, 0)
    chunk_len = jnp.where(live, jnp.clip(left, 2, rows), 2)
    info = jnp.stack([chunk_end[-1], used])

    xs = _dispatch_call(tab_out, pad, h2, lp, nalloc + TAIL_ROWS)
    y = _expert_call(first, nchunk, chunk_row, chunk_len, info, xs, wg_b, wu_b, w_exp_down[0], nalloc)
    out = _combine_call(tab_back, lp, gw, xs1, mod3, final_norm_g, y, seq)
    return out.reshape(bsz, seq, d)
```

```python
import numpy as np
import jax
import jax.numpy as jnp
from jax import lax
from jax.experimental import pallas as pl
from jax.experimental.pallas import tpu as pltpu

F32 = jnp.float32
BF16 = jnp.bfloat16
U32 = jnp.uint32
I32 = jnp.int32

D_MODEL = 1024
HEAD_DIM = 64
N_HEADS = 8
N_KV_HEADS = 2
GROUP = N_HEADS // N_KV_HEADS
Q_COLS = N_HEADS * HEAD_DIM
KV_COLS = N_KV_HEADS * HEAD_DIM
ATTN_BLOCK = 128
WINDOW = 128
NUM_BUCKETS = 32
MAX_DISTANCE = 128
CONV_CH = D_MODEL - Q_COLS
CONV_WIDTH = 31
IN_COLS = Q_COLS + 2 * KV_COLS + 2 * CONV_CH
N_EXPERTS = 256
TOP_K = 8
N_GROUPS = 8
GROUP_SIZE = N_EXPERTS // N_GROUPS
TOPK_GROUPS = 4
EXPERT_HIDDEN = 256
ROUTED_SCALE = 2.5
EPS = 1e-6

MIX_ROWS = 512
HALO = 32
TILE = 256
LROWS = TILE * TOP_K
SORT_CHUNK = 256
EXPERT_ROWS = 576
TAIL_CHUNK_ROWS = 256
LANES = 128
SUBLANES = 8
PACK_W = D_MODEL // 2
PACK_S = PACK_W // LANES
VMEM_LIMIT = 56 * 1024 * 1024


def _sigmoid(v):
    return 1.0 / (1.0 + jnp.exp(-v))


def _rms(v):
    return v * lax.rsqrt(jnp.mean(v * v, axis=-1, keepdims=True) + EPS)


def _split_bf16(a):
    hi = a.astype(BF16)
    lo = (a - hi.astype(F32)).astype(BF16)
    return hi, lo


def _dot(a, b):
    return jnp.dot(a, b, preferred_element_type=F32)


def _dot_nt(a, b):
    return lax.dot_general(a, b, (((1,), (1,)), ((), ())), preferred_element_type=F32)


def _dot_tn(a, b):
    return lax.dot_general(a, b, (((0,), (0,)), ((), ())), preferred_element_type=F32)


def _dot3(a, b, dot):
    ah, al = _split_bf16(a)
    bh, bl = _split_bf16(b)
    return dot(ah, bh) + (dot(ah, bl) + dot(al, bh))


def _pack_rows(v):
    hi = lax.bitcast_convert_type(v[:, :PACK_W], U32) & jnp.uint32(0xFFFF0000)
    lo = lax.bitcast_convert_type(v[:, PACK_W:], U32) >> 16
    return hi | lo


def _unpack_rows(u):
    hi = lax.bitcast_convert_type(u & jnp.uint32(0xFFFF0000), F32)
    lo = lax.bitcast_convert_type(u << 16, F32)
    return jnp.concatenate([hi, lo], axis=1).astype(BF16)


def _load_flat(flat, r0, n):
    return jnp.concatenate(
        [flat[pl.ds(r0 * PACK_S + c, n, stride=PACK_S), :] for c in range(PACK_S)], axis=1)


def _store_flat(flat, r0, n, u):
    for c in range(PACK_S):
        flat[pl.ds(r0 * PACK_S + c, n, stride=PACK_S), :] = u[:, c * LANES:(c + 1) * LANES]


def _load_packed(ref3, r0, n):
    return _load_flat(ref3.reshape(ref3.shape[0] * PACK_S, LANES), r0, n)


def _store_packed(ref3, r0, n, u):
    _store_flat(ref3.reshape(ref3.shape[0] * PACK_S, LANES), r0, n, u)


def _ada_kernel(c_ref, w_ref, b_ref, o_ref):
    c = c_ref[...]
    s = c * _sigmoid(c)
    o_ref[...] = _dot3(s, w_ref[...], _dot) + b_ref[...]


def _ada_call(c, w_ada, b_ada):
    bsz, d = c.shape
    n = w_ada.shape[1]
    tn = 1536
    return pl.pallas_call(
        _ada_kernel,
        grid=(n // tn,),
        in_specs=[
            pl.BlockSpec((bsz, d), lambda i: (0, 0)),
            pl.BlockSpec((d, tn), lambda i: (0, i)),
            pl.BlockSpec((1, tn), lambda i: (0, i)),
        ],
        out_specs=pl.BlockSpec((bsz, tn), lambda i: (0, i)),
        out_shape=jax.ShapeDtypeStruct((bsz, n), F32),
        compiler_params=pltpu.CompilerParams(
            dimension_semantics=("arbitrary",), vmem_limit_bytes=VMEM_LIMIT),
        name="ada",
    )(c, w_ada, b_ada.reshape(1, n))


def _bucket_table():
    qi = np.arange(ATTN_BLOCK)[:, None]
    ki = np.arange(2 * ATTN_BLOCK)[None, :]
    dist = qi + ATTN_BLOCK - ki
    n = np.maximum(dist, 0)
    max_exact = NUM_BUCKETS // 2
    large = max_exact + (np.log(np.maximum(n, 1) / max_exact) / np.log(MAX_DISTANCE / max_exact)
                         * (NUM_BUCKETS - max_exact)).astype(np.int32)
    large = np.minimum(large, NUM_BUCKETS - 1)
    bkt = np.where(n < max_exact, n, large).astype(np.int32)
    band = (dist >= 0) & (dist < WINDOW)
    return np.ascontiguousarray(np.where(band, bkt, -1).astype(np.int32).T)


def _mixer_kernel(x_ref, mod_ref, win_ref, sinks_ref, relb_ref, bkt_ref, ang_ref, cw_ref, cb_ref,
                  lng_ref, lnb_ref, wout_ref, wrt_ref, wsg_ref, wsu_ref, wsd_ref, ewg_ref, ewu_ref,
                  xs1_ref, h2_ref, lgt_ref, ewgb_ref, ewub_ref,
                  kv_ref, uext_ref, bias_ref, wrh_ref, wrl_ref):
    tm = x_ref.shape[1]
    nblk = tm // ATTN_BLOCK
    j = pl.program_id(1)

    @pl.when((pl.program_id(0) == 0) & (j == 0))
    def _build_bias():
        bkt = bkt_ref[...]
        for h in range(N_HEADS):
            acc = jnp.full(bkt.shape, -jnp.inf, F32)
            for b in range(NUM_BUCKETS):
                acc = jnp.where(bkt == b, relb_ref[b, h], acc)
            g = h % GROUP
            bias_ref[h // GROUP, :, g * ATTN_BLOCK:(g + 1) * ATTN_BLOCK] = acc
        w_hi, w_lo = _split_bf16(wrt_ref[...])
        wrh_ref[...] = w_hi
        wrl_ref[...] = w_lo

    @pl.when(j == 0)
    def _reset_history():
        kv_ref[...] = jnp.zeros(kv_ref.shape, F32)
        uext_ref[0:HALO, :] = jnp.zeros((HALO, CONV_CH), F32)
        uext_ref[HALO + tm:HALO + tm + SUBLANES, :] = jnp.zeros((SUBLANES, CONV_CH), F32)

    x = x_ref[0]
    mod = mod_ref[0]
    sh1, sc1, g1 = mod[0:1], mod[1:2], mod[2:3]
    sh2, sc2, g2 = mod[3:4], mod[4:5], mod[5:6]

    h = _rms(x) * (1.0 + sc1) + sh1
    hb = h.astype(BF16)
    n_qkv = Q_COLS + 2 * KV_COLS
    proj_conv = _dot(hb, win_ref[:, n_qkv:])
    proj = _dot(hb, win_ref[:, :n_qkv])

    qb = (proj[:, :Q_COLS] * (HEAD_DIM ** -0.5)).astype(BF16)
    kv_cur = proj[:, Q_COLS:Q_COLS + 2 * KV_COLS]
    kvb = jnp.concatenate([kv_ref[...], kv_cur], axis=0).astype(BF16)
    kv_ref[...] = kv_cur[tm - ATTN_BLOCK:, :]
    not_first = j > 0
    key = lax.broadcasted_iota(jnp.int32, (2 * ATTN_BLOCK, GROUP * ATTN_BLOCK), 0)
    attn_rows = []
    for i in range(nblk):
        r0, r1, r2 = i * ATTN_BLOCK, (i + 1) * ATTN_BLOCK, (i + 2) * ATTN_BLOCK
        heads_t = [None] * N_HEADS
        for kh in range(N_KV_HEADS):
            kpc = kvb[r0:r2, kh * HEAD_DIM:(kh + 1) * HEAD_DIM]
            vpc = kvb[r0:r2, KV_COLS + kh * HEAD_DIM:KV_COLS + (kh + 1) * HEAD_DIM]
            qg = jnp.concatenate(
                [qb[r0:r1, (kh * GROUP + g) * HEAD_DIM:(kh * GROUP + g + 1) * HEAD_DIM]
                 for g in range(GROUP)], axis=0)
            logit = _dot_nt(kpc, qg) + bias_ref[kh]
            if i == 0:
                logit = jnp.where((key >= ATTN_BLOCK) | not_first, logit, -jnp.inf)
            sink = jnp.concatenate(
                [jnp.full((1, ATTN_BLOCK), sinks_ref[kh * GROUP + g], F32) for g in range(GROUP)], axis=1)
            m = jnp.maximum(jnp.max(logit, axis=0, keepdims=True), sink)
            p = jnp.exp(logit - m)
            den = jnp.sum(p, axis=0, keepdims=True) + jnp.exp(sink - m)
            o_t = _dot_tn(vpc, p.astype(BF16)) / den
            for g in range(GROUP):
                heads_t[kh * GROUP + g] = o_t[:, g * ATTN_BLOCK:(g + 1) * ATTN_BLOCK]
        attn_t = jnp.concatenate(heads_t, axis=0)
        scale = lax.rsqrt(jnp.mean(attn_t * attn_t, axis=0, keepdims=True) + EPS)
        attn_rows.append((attn_t * scale).T)
    attn = jnp.concatenate(attn_rows, axis=0) * ang_ref[...]

    a = proj_conv[:, :CONV_CH]
    gt = proj_conv[:, CONV_CH:]
    uext_ref[HALO:HALO + tm, :] = a * _sigmoid(gt)
    cw = cw_ref[...]
    base = HALO - (CONV_WIDTH - 1)
    acc = None
    for res in range(SUBLANES):
        part = None
        for hi in range((HALO + SUBLANES - 1) // SUBLANES + 1):
            t = hi * SUBLANES + res - base
            if 0 <= t < CONV_WIDTH:
                term = cw[t:t + 1, :] * uext_ref[hi * SUBLANES:hi * SUBLANES + tm + SUBLANES, :]
                part = term if part is None else part + term
        part = part[res:res + tm, :]
        acc = part if acc is None else acc + part
    uext_ref[0:HALO, :] = uext_ref[tm:tm + HALO, :]
    cv = acc + cb_ref[...]
    mu = jnp.mean(cv, axis=-1, keepdims=True)
    var = jnp.mean(jnp.square(cv - mu), axis=-1, keepdims=True)
    cv = (cv - mu) * lax.rsqrt(var + EPS) * lng_ref[...] + lnb_ref[...]
    cv = cv * _sigmoid(cv)

    mixed = (_dot(attn.astype(BF16), wout_ref[0:Q_COLS, :])
             + _dot(cv.astype(BF16), wout_ref[Q_COLS:, :]))
    x1 = x + g1 * mixed

    h2 = _rms(x1) * (1.0 + sc2) + sh2
    h2_hi, h2_lo = _split_bf16(h2)
    w_hi, w_lo = wrh_ref[...], wrl_ref[...]
    lgt_ref[...] = _dot_nt(w_hi, h2_hi) + (_dot_nt(w_hi, h2_lo) + _dot_nt(w_lo, h2_hi))
    h2b = h2.astype(BF16)
    h2_ref[0] = h2b
    sg = _dot(h2b, wsg_ref[...])
    su = _dot(h2b, wsu_ref[...])
    shared = _dot(((sg * _sigmoid(sg)) * su).astype(BF16), wsd_ref[...])
    xs1_ref[0] = x1 + g2 * shared

    ewgb_ref[...] = ewg_ref[...].astype(BF16)
    ewub_ref[...] = ewu_ref[...].astype(BF16)


def _mixer_call(x, mod3, w_in_b, sinks, rel_bias, ang, cw, cb, lng, lnb, w_out_b, w_rt, wsg_b, wsu_b, wsd_b,
                w_gate, w_up):
    bsz, seq, d = x.shape
    tm = MIX_ROWS
    nj = seq // tm
    ne, _, hid = w_gate.shape
    per_step = ne // (bsz * nj)
    assert per_step * bsz * nj == ne
    bkt = jnp.asarray(_bucket_table())
    full = lambda shape: pl.BlockSpec(shape, lambda b, j: (0,) * len(shape), pipeline_mode=pl.Buffered(1))
    share = lambda shape: pl.BlockSpec((per_step,) + shape, lambda b, j: (b * nj + j, 0, 0))
    smem = pl.BlockSpec(memory_space=pltpu.SMEM)
    return pl.pallas_call(
        _mixer_kernel,
        grid=(bsz, nj),
        in_specs=[
            pl.BlockSpec((1, tm, d), lambda b, j: (b, j, 0)),
            pl.BlockSpec((1, 6, d), lambda b, j: (b, 0, 0)),
            full((d, IN_COLS)),
            smem, smem,
            full((2 * ATTN_BLOCK, ATTN_BLOCK)),
            full((1, Q_COLS)),
            full((CONV_WIDTH, CONV_CH)),
            full((1, CONV_CH)), full((1, CONV_CH)), full((1, CONV_CH)),
            full((d, d)),
            full((N_EXPERTS, d)),
            full((d, EXPERT_HIDDEN)), full((d, EXPERT_HIDDEN)), full((EXPERT_HIDDEN, d)),
            share((d, hid)), share((d, hid)),
        ],
        out_specs=[
            pl.BlockSpec((1, tm, d), lambda b, j: (b, j, 0)),
            pl.BlockSpec((1, tm, d), lambda b, j: (b, j, 0)),
            pl.BlockSpec((N_EXPERTS, tm), lambda b, j: (0, b * nj + j)),
            share((d, hid)), share((d, hid)),
        ],
        out_shape=[
            jax.ShapeDtypeStruct((bsz, seq, d), F32),
            jax.ShapeDtypeStruct((bsz, seq, d), BF16),
            jax.ShapeDtypeStruct((N_EXPERTS, bsz * seq), F32),
            jax.ShapeDtypeStruct((ne, d, hid), BF16),
            jax.ShapeDtypeStruct((ne, d, hid), BF16),
        ],
        scratch_shapes=[
            pltpu.VMEM((ATTN_BLOCK, 2 * KV_COLS), F32),
            pltpu.VMEM((HALO + tm + SUBLANES, CONV_CH), F32),
            pltpu.VMEM((N_KV_HEADS, 2 * ATTN_BLOCK, GROUP * ATTN_BLOCK), F32),
            pltpu.VMEM((N_EXPERTS, d), BF16),
            pltpu.VMEM((N_EXPERTS, d), BF16),
        ],
        compiler_params=pltpu.CompilerParams(
            dimension_semantics=("arbitrary", "arbitrary"), vmem_limit_bytes=VMEM_LIMIT),
        name="mixer",
    )(x, mod3, w_in_b, sinks, rel_bias, bkt, ang, cw, cb, lng, lnb, w_out_b, w_rt, wsg_b, wsu_b, wsd_b,
      w_gate, w_up)


def _route_kernel(lgt_ref, rb_ref, lp_ref, gw_ref, cnt_ref, earlier_ref, lower_ref):
    tr = lgt_ref.shape[1]
    i = pl.program_id(0)

    @pl.when(i == 0)
    def _():
        cnt_ref[...] = jnp.zeros(cnt_ref.shape, F32)
        earlier_ref[...] = jnp.where(
            lax.broadcasted_iota(I32, (tr, tr), 0) < lax.broadcasted_iota(I32, (tr, tr), 1), 1.0, 0.0).astype(BF16)
        lower_ref[...] = jnp.where(
            lax.broadcasted_iota(I32, (N_EXPERTS, N_EXPERTS), 1)
            < lax.broadcasted_iota(I32, (N_EXPERTS, N_EXPERTS), 0), 1.0, 0.0).astype(BF16)

    scores = _sigmoid(lgt_ref[...])
    sel = scores + rb_ref[...]
    neg = -jnp.inf

    sel3 = sel.reshape(N_GROUPS, GROUP_SIZE, tr)
    loc = lax.broadcasted_iota(I32, sel3.shape, 1)
    m1 = jnp.max(sel3, axis=1, keepdims=True)
    i1 = jnp.min(jnp.where(sel3 == m1, loc, GROUP_SIZE), axis=1, keepdims=True)
    m2 = jnp.max(jnp.where(loc == i1, neg, sel3), axis=1, keepdims=True)
    gscore = (m1 + m2).reshape(N_GROUPS, tr)

    gio = lax.broadcasted_iota(I32, gscore.shape, 0)
    gmask = jnp.zeros(gscore.shape, jnp.bool_)
    cur = gscore
    for _ in range(TOPK_GROUPS):
        m = jnp.max(cur, axis=0, keepdims=True)
        idx = jnp.min(jnp.where(cur == m, gio, N_GROUPS), axis=0, keepdims=True)
        pick = gio == idx
        gmask = gmask | pick
        cur = jnp.where(pick, neg, cur)
    emask = jnp.broadcast_to(gmask.reshape(N_GROUPS, 1, tr), sel3.shape).reshape(N_EXPERTS, tr)

    rio = lax.broadcasted_iota(I32, sel.shape, 0)
    cur = jnp.where(emask, sel, neg)
    picks, gsc = [], []
    for _ in range(TOP_K):
        m = jnp.max(cur, axis=0, keepdims=True)
        idx = jnp.min(jnp.where(cur == m, rio, N_EXPERTS), axis=0, keepdims=True)
        pick = rio == idx
        picks.append(pick)
        gsc.append(jnp.sum(jnp.where(pick, scores, 0.0), axis=0, keepdims=True))
        cur = jnp.where(pick, neg, cur)
    gsum = gsc[0]
    for k in range(1, TOP_K):
        gsum = gsum + gsc[k]
    gw_ref[...] = jnp.concatenate([g / gsum * ROUTED_SCALE for g in gsc], axis=0)

    onehot = jnp.where(cur == neg, jnp.where(emask, 1.0, 0.0), 0.0)
    before = _dot(onehot.astype(BF16), earlier_ref[...])
    n = jnp.sum(onehot, axis=1, keepdims=True)
    start = _dot(lower_ref[...],
                 jnp.broadcast_to(n, (N_EXPERTS, LANES)).astype(BF16))[:, 0:1]
    pos = start + before
    lp_ref[...] = jnp.concatenate(
        [jnp.sum(jnp.where(p, pos, 0.0), axis=0, keepdims=True) for p in picks], axis=0).astype(I32)

    lane = lax.broadcasted_iota(I32, cnt_ref.shape, 1)
    cnt_ref[...] += jnp.where(lane == i, n, 0.0)


def _route_call(lgt, router_bias):
    e, t = lgt.shape
    ntiles = t // TILE
    ntp = (ntiles + LANES - 1) // LANES * LANES
    return pl.pallas_call(
        _route_kernel,
        grid=(ntiles,),
        in_specs=[
            pl.BlockSpec((e, TILE), lambda i: (0, i)),
            pl.BlockSpec((e, 1), lambda i: (0, 0)),
        ],
        out_specs=[
            pl.BlockSpec((TOP_K, TILE), lambda i: (0, i)),
            pl.BlockSpec((TOP_K, TILE), lambda i: (0, i)),
            pl.BlockSpec((e, ntp), lambda i: (0, 0)),
        ],
        out_shape=[
            jax.ShapeDtypeStruct((TOP_K, t), I32),
            jax.ShapeDtypeStruct((TOP_K, t), F32),
            jax.ShapeDtypeStruct((e, ntp), F32),
        ],
        scratch_shapes=[pltpu.VMEM((TILE, TILE), BF16), pltpu.VMEM((e, e), BF16)],
        compiler_params=pltpu.CompilerParams(
            dimension_semantics=("arbitrary",), vmem_limit_bytes=VMEM_LIMIT),
        name="route",
    )(lgt, router_bias.reshape(e, 1))


_T_LOCAL, _T_LEN, _T_GLOBAL, _T_FIELDS = 0, 1, 2, 3
_T_STRIDE = 2 * N_EXPERTS
_T_EMPTY = _T_STRIDE * _T_FIELDS
_T_SIZE = _T_EMPTY + 2
LBUF_ROWS = LROWS + N_EXPERTS


def _segment_copies(tab_ref, half, local_ref, global_hbm, sem, to_global):
    def body(e, carry):
        at = half * N_EXPERTS + e
        n = tab_ref[0, 0, _T_LEN * _T_STRIDE + at]
        loc = local_ref.at[pl.ds(tab_ref[0, 0, _T_LOCAL * _T_STRIDE + at], n)]
        glo = global_hbm.at[pl.ds(tab_ref[0, 0, _T_GLOBAL * _T_STRIDE + at], n)]
        if to_global:
            pltpu.make_async_copy(loc, glo, sem).start()
        else:
            pltpu.make_async_copy(glo, loc, sem).start()
        return carry

    for e in range(N_EXPERTS):
        body(e, 0)


def _chunk_relative(lp_row, c):
    rel = lp_row - c * SORT_CHUNK
    inside = (rel >= 0) & (rel < SORT_CHUNK)
    return jnp.where(inside, rel, -1).astype(F32).astype(BF16)


def _chunk_row_ids():
    return lax.broadcasted_iota(I32, (SORT_CHUNK, TILE), 0).astype(F32).astype(BF16)


def _segment_wait(local_ref, global_hbm, sem, nempty):
    rows = LROWS + nempty
    pltpu.make_async_copy(global_hbm.at[pl.ds(0, rows)], local_ref.at[pl.ds(0, rows)], sem).wait()


TAIL_ROWS = EXPERT_ROWS + 2 * N_EXPERTS


def _dispatch_kernel(tab_ref, pad_ref, h2_ref, lp_ref, xs_hbm, xl0, xl1, zb, sem, pending):
    s = pl.program_id(0)
    ns = pl.num_programs(0)

    @pl.when(s == 0)
    def _zero_unassigned_rows():
        zb[...] = jnp.zeros(zb.shape, U32)
        used = pad_ref[2, 0]
        rest = xs_hbm.shape[0] - used
        cp = pltpu.make_async_copy(zb.at[pl.ds(0, rest)], xs_hbm.at[pl.ds(used, rest)], sem.at[2])
        cp.start()
        cp.wait()

        def pad_row(e):
            return pltpu.make_async_copy(zb.at[pl.ds(0, 1)], xs_hbm.at[pl.ds(pad_ref[0, e], 1)], sem.at[2])

        def start(e, carry):
            @pl.when(pad_ref[1, e] > 0)
            def _():
                pad_row(e).start()
            return carry

        def wait(e, carry):
            @pl.when(pad_ref[1, e] > 0)
            def _():
                pad_row(e).wait()
            return carry

        lax.fori_loop(0, N_EXPERTS, start, 0)
        lax.fori_loop(0, N_EXPERTS, wait, 0)

    for half, xl in ((0, xl0), (1, xl1)):
        @pl.when(s > 0)
        def _():
            _segment_wait(xl, xs_hbm, sem.at[half], pending[half])

        rows = h2_ref[half * TILE:(half + 1) * TILE, :]
        lp = lp_ref[:, half * TILE:(half + 1) * TILE]
        j = _chunk_row_ids()
        for c in range(LROWS // SORT_CHUNK):
            onehot = jnp.zeros((SORT_CHUNK, TILE), BF16)
            for k in range(TOP_K):
                onehot = jnp.where(j == _chunk_relative(lp[k:k + 1, :], c), jnp.ones((), BF16), onehot)
            _store_packed(xl, c * SORT_CHUNK, SORT_CHUNK, _pack_rows(_dot(onehot, rows)))
        _segment_copies(tab_ref, half, xl, xs_hbm, sem.at[half], to_global=True)
        pending[half] = tab_ref[0, 0, _T_EMPTY + half]

    @pl.when(s == ns - 1)
    def _():
        _segment_wait(xl0, xs_hbm, sem.at[0], pending[0])
        _segment_wait(xl1, xs_hbm, sem.at[1], pending[1])


def _dispatch_call(tab, pad, h2, lp, nrows):
    t, d = h2.shape
    ns = t // (2 * TILE)
    return pl.pallas_call(
        _dispatch_kernel,
        grid=(ns,),
        in_specs=[
            pl.BlockSpec((1, 1, _T_SIZE), lambda s: (s, 0, 0), memory_space=pltpu.SMEM),
            pl.BlockSpec(memory_space=pltpu.SMEM),
            pl.BlockSpec((2 * TILE, d), lambda s: (s, 0)),
            pl.BlockSpec((TOP_K, 2 * TILE), lambda s: (0, s)),
        ],
        out_specs=pl.BlockSpec(memory_space=pl.ANY),
        out_shape=jax.ShapeDtypeStruct((nrows, PACK_S, LANES), U32),
        scratch_shapes=[
            pltpu.VMEM((LBUF_ROWS, PACK_S, LANES), U32),
            pltpu.VMEM((LBUF_ROWS, PACK_S, LANES), U32),
            pltpu.VMEM((TAIL_ROWS + N_EXPERTS, PACK_S, LANES), U32),
            pltpu.SemaphoreType.DMA((3,)),
            pltpu.SMEM((2,), I32),
        ],
        compiler_params=pltpu.CompilerParams(
            dimension_semantics=("arbitrary",), vmem_limit_bytes=VMEM_LIMIT),
        name="dispatch",
    )(tab, pad, h2, lp)


RING = 6
AHEAD = RING - 2


def _expert_kernel(first_ref, nchunk_ref, ntail_ref, row_ref, len_ref, info_ref, wg_ref, wu_ref, wd_ref,
                   xs_hbm, y_hbm, xbuf, ybuf, isem, osem, wdb):
    e = pl.program_id(0)
    last_step = e == pl.num_programs(0) - 1
    rows = EXPERT_ROWS
    total_chunks = info_ref[0]

    def slab(ref, row0, m):
        size = m * PACK_S if isinstance(m, int) else pl.multiple_of(m * PACK_S, SUBLANES)
        first = row0 * PACK_S if isinstance(row0, int) else pl.multiple_of(row0 * PACK_S, SUBLANES)
        return ref.at[pl.ds(first, size)]

    def fetch(g):
        slot = g % RING
        return pltpu.make_async_copy(slab(xs_hbm, row_ref[g], rows), slab(xbuf, slot * rows, rows), isem.at[slot])

    def flush(g):
        slot = g % RING
        m = len_ref[g]
        return pltpu.make_async_copy(slab(ybuf, slot * rows, m), slab(y_hbm, row_ref[g], m), osem.at[slot])

    def start_fetch(g):
        @pl.when(g < total_chunks)
        def _():
            fetch(g).start()

    def wait_flush(g):
        @pl.when(g >= 0)
        def _():
            flush(g).wait()

    def compute(g, n=EXPERT_ROWS):
        slot = g % RING
        xb = _unpack_rows(_load_flat(xbuf, slot * rows, n))
        gate = _dot(xb, wg_ref[0])
        up = _dot(xb, wu_ref[0])
        hid = (gate * _sigmoid(gate)) * up
        y = _dot(hid.astype(BF16), wdb[...])
        _store_flat(ybuf, slot * rows, n, _pack_rows(y.astype(BF16).astype(F32)))

    @pl.when(e == 0)
    def _prime():
        for a in range(AHEAD):
            start_fetch(a)

    wdb[...] = wd_ref[0].astype(BF16)

    begin = first_ref[e]
    count = nchunk_ref[e]

    def pair(p, carry):
        g0 = begin + 2 * p
        g1 = g0 + 1
        start_fetch(g0 + AHEAD)
        start_fetch(g1 + AHEAD)
        fetch(g0).wait()
        fetch(g1).wait()
        wait_flush(g0 - RING)
        wait_flush(g1 - RING)
        compute(g0)
        compute(g1)
        flush(g0).start()
        flush(g1).start()
        return carry

    lax.fori_loop(0, count // 2, pair, 0)

    def single(g, n):
        start_fetch(g + AHEAD)
        fetch(g).wait()
        wait_flush(g - RING)
        compute(g, n)
        flush(g).start()

    @pl.when(count % 2 == 1)
    def _odd_chunk():
        single(begin + count - 1, EXPERT_ROWS)

    @pl.when(ntail_ref[e] > 0)
    def _short_chunk():
        single(begin + count, TAIL_CHUNK_ROWS)

    @pl.when(last_step)
    def _drain():
        def wait(g, carry):
            flush(g).wait()
            return carry
        lax.fori_loop(jnp.maximum(total_chunks - RING, 0), total_chunks, wait, 0)

    @pl.when(last_step)
    def _define_unused_rows():
        used = info_ref[1]
        rest = y_hbm.shape[0] // PACK_S - used

        @pl.when(rest > 0)
        def _():
            ybuf[0:N_EXPERTS * PACK_S, :] = jnp.zeros((N_EXPERTS * PACK_S, LANES), U32)
            cp = pltpu.make_async_copy(slab(ybuf, 0, rest), slab(y_hbm, used, rest), osem.at[0])
            cp.start()
            cp.wait()


def _expert_call(first, nchunk, ntail, chunk_row, chunk_len, info, xs, w_gate, w_up, w_down, nrows):
    ne, d, hid = w_gate.shape
    rows = EXPERT_ROWS
    wsel = lambda e, *_: (e, 0, 0)
    grid_spec = pltpu.PrefetchScalarGridSpec(
        num_scalar_prefetch=6,
        grid=(ne,),
        in_specs=[
            pl.BlockSpec((1, d, hid), wsel),
            pl.BlockSpec((1, d, hid), wsel),
            pl.BlockSpec((1, hid, d), wsel),
            pl.BlockSpec(memory_space=pl.ANY),
        ],
        out_specs=pl.BlockSpec(memory_space=pl.ANY),
        scratch_shapes=[
            pltpu.VMEM((RING * rows * PACK_S, LANES), U32),
            pltpu.VMEM((RING * rows * PACK_S, LANES), U32),
            pltpu.SemaphoreType.DMA((RING,)),
            pltpu.SemaphoreType.DMA((RING,)),
            pltpu.VMEM((hid, d), BF16),
        ],
    )
    return pl.pallas_call(
        _expert_kernel,
        grid_spec=grid_spec,
        out_shape=jax.ShapeDtypeStruct((nrows * PACK_S, LANES), U32),
        compiler_params=pltpu.CompilerParams(
            dimension_semantics=("arbitrary",), vmem_limit_bytes=VMEM_LIMIT),
        name="experts",
    )(first, nchunk, ntail, chunk_row, chunk_len, info, w_gate, w_up, w_down,
      xs.reshape(-1, LANES)).reshape(nrows, PACK_S, LANES)


def _combine_kernel(tab_ref, tabn_ref, lp_ref, gw_ref, xs1_ref, mod_ref, fg_ref, y_hbm, o_ref, yl0, yl1, sem):
    s = pl.program_id(0)
    ns = pl.num_programs(0)

    @pl.when(s == 0)
    def _():
        _segment_copies(tab_ref, 0, yl0, y_hbm, sem.at[0], to_global=False)

    g2 = mod_ref[0][5:6]
    for half, yl in ((0, yl0), (1, yl1)):
        _segment_wait(yl, y_hbm, sem.at[half], tab_ref[0, 0, _T_EMPTY + half])
        if half == 0:
            _segment_copies(tab_ref, 1, yl1, y_hbm, sem.at[1], to_global=False)
        lp = lp_ref[:, half * TILE:(half + 1) * TILE]
        gw = gw_ref[:, half * TILE:(half + 1) * TILE]
        acc = jnp.zeros((TILE, D_MODEL), F32)
        gwb = [gw[k:k + 1, :].astype(BF16) for k in range(TOP_K)]
        j = _chunk_row_ids()
        for c in range(LROWS // SORT_CHUNK):
            wt = jnp.zeros((SORT_CHUNK, TILE), BF16)
            for k in range(TOP_K):
                wt = jnp.where(j == _chunk_relative(lp[k:k + 1, :], c), gwb[k], wt)
            yv = _unpack_rows(_load_packed(yl, c * SORT_CHUNK, SORT_CHUNK))
            acc = acc + _dot_tn(wt, yv)
        if half == 0:
            _segment_copies(tabn_ref, 0, yl0, y_hbm, sem.at[0], to_global=False)
        x2 = xs1_ref[half * TILE:(half + 1) * TILE, :] + g2 * acc
        o_ref[half * TILE:(half + 1) * TILE, :] = _rms(x2) * fg_ref[...]

    @pl.when(s == ns - 1)
    def _():
        _segment_wait(yl0, y_hbm, sem.at[0], tab_ref[0, 0, _T_EMPTY])


def _combine_call(tab, lp, gw, xs1, mod3, final_g, y, seq):
    t, d = xs1.shape
    ns = t // (2 * TILE)
    per_seq = seq // (2 * TILE)
    return pl.pallas_call(
        _combine_kernel,
        grid=(ns,),
        in_specs=[
            pl.BlockSpec((1, 1, _T_SIZE), lambda s: (s, 0, 0), memory_space=pltpu.SMEM),
            pl.BlockSpec((1, 1, _T_SIZE), lambda s: (jnp.minimum(s + 1, ns - 1), 0, 0),
                         memory_space=pltpu.SMEM),
            pl.BlockSpec((TOP_K, 2 * TILE), lambda s: (0, s)),
            pl.BlockSpec((TOP_K, 2 * TILE), lambda s: (0, s)),
            pl.BlockSpec((2 * TILE, d), lambda s: (s, 0)),
            pl.BlockSpec((1, 6, d), lambda s: (s // per_seq, 0, 0)),
            pl.BlockSpec((1, d), lambda s: (0, 0)),
            pl.BlockSpec(memory_space=pl.ANY),
        ],
        out_specs=pl.BlockSpec((2 * TILE, d), lambda s: (s, 0)),
        out_shape=jax.ShapeDtypeStruct((t, d), F32),
        scratch_shapes=[
            pltpu.VMEM((LBUF_ROWS, PACK_S, LANES), U32),
            pltpu.VMEM((LBUF_ROWS, PACK_S, LANES), U32),
            pltpu.SemaphoreType.DMA((2,)),
        ],
        compiler_params=pltpu.CompilerParams(
            dimension_semantics=("arbitrary",), vmem_limit_bytes=VMEM_LIMIT),
        name="combine",
    )(tab, tab, lp, gw, xs1, mod3, final_g.reshape(1, d), y)


def kernel(x, c, w_ada, b_ada, w_in, attn_sinks, rel_bias, attn_norm_g, conv_w, conv_b, conv_ln_g,
           conv_ln_b, w_out, w_router, router_bias, w_exp_gate, w_exp_up, w_exp_down, w_sh_gate,
           w_sh_up, w_sh_down, final_norm_g):
    bsz, seq, d = x.shape
    t = bsz * seq
    assert w_ada.shape[0] == 1 and d == D_MODEL
    assert seq % MIX_ROWS == 0 and seq % (2 * TILE) == 0

    mod3 = _ada_call(c, w_ada[0], b_ada[0]).reshape(bsz, 6, d)
    xs1, h2, lgt, wg_b, wu_b = _mixer_call(
        x, mod3, w_in[0].astype(BF16), attn_sinks[0], rel_bias,
        attn_norm_g[0].reshape(1, Q_COLS), conv_w[0], conv_b[0].reshape(1, CONV_CH),
        conv_ln_g[0].reshape(1, CONV_CH), conv_ln_b[0].reshape(1, CONV_CH),
        w_out[0].astype(BF16), w_router[0].T,
        w_sh_gate[0].astype(BF16), w_sh_up[0].astype(BF16), w_sh_down[0].astype(BF16),
        w_exp_gate[0], w_exp_up[0])
    xs1 = xs1.reshape(t, d)
    h2 = h2.reshape(t, d)

    lp, gw, cnt = _route_call(lgt, router_bias[0])

    ntiles = t // TILE
    nassign = t * TOP_K
    n = cnt[:, :ntiles].T.astype(I32)
    local = jnp.cumsum(n, axis=1) - n
    earlier = jnp.cumsum(n, axis=0) - n
    total = jnp.sum(n, axis=0)
    region = (total + 1) // 2 * 2
    starts = jnp.cumsum(region) - region
    nalloc = nassign + N_EXPERTS
    used = jnp.sum(region)
    pad = jnp.stack([starts + total, region - total,
                     jnp.broadcast_to(used, (N_EXPERTS,))], axis=0)
    glob = starts[None, :] + earlier
    empty = n == 0
    length = jnp.maximum(n, 1)
    eid = jnp.arange(N_EXPERTS, dtype=I32)[None, :]
    parity = (jnp.arange(ntiles, dtype=I32) % 2)[:, None]
    spare_global = nalloc + EXPERT_ROWS + parity * N_EXPERTS + eid
    nempty = jnp.sum(empty.astype(I32), axis=1).reshape(ntiles // 2, 2)

    def table(loc, glo):
        fields = [f.reshape(ntiles // 2, _T_STRIDE) for f in (loc, length, glo)]
        return jnp.concatenate(fields + [nempty], axis=1).reshape(ntiles // 2, 1, _T_SIZE)

    tab_out = table(jnp.where(empty, 0, local), jnp.where(empty, spare_global, glob))
    tab_back = table(jnp.where(empty, LROWS + eid, local), jnp.where(empty, 0, glob))

    rows = EXPERT_ROWS
    left_over = total % (2 * rows)
    ntail = (((left_over > 0) & (left_over <= TAIL_CHUNK_ROWS))
             | ((left_over > rows) & (left_over <= rows + TAIL_CHUNK_ROWS))).astype(I32)
    nwhole = (2 * (total // (2 * rows)) + (left_over > TAIL_CHUNK_ROWS).astype(I32)
              + (left_over > rows + TAIL_CHUNK_ROWS).astype(I32))
    nchunk = nwhole + ntail
    chunk_end = jnp.cumsum(nchunk)
    first = chunk_end - nchunk
    max_chunks = nassign // rows + 2 * N_EXPERTS
    gid = jnp.arange(max_chunks, dtype=I32)
    owner = jnp.minimum(jnp.sum((chunk_end[None, :] <= gid[:, None]).astype(I32), axis=1), N_EXPERTS - 1)
    pick = owner[:, None] == eid
    within = gid - jnp.sum(jnp.where(pick, first[None, :], 0), axis=1)
    chunk_row = jnp.sum(jnp.where(pick, starts[None, :], 0), axis=1) + within * rows
    left = jnp.sum(jnp.where(pick, region[None, :], 0), axis=1) - within * rows
    live = gid < chunk_end[-1]
    chunk_row = jnp.where(live, chunk_row, 0)
    chunk_len = jnp.where(live, jnp.clip(left, 2, rows), 2)
    info = jnp.stack([chunk_end[-1], used])

    xs = _dispatch_call(tab_out, pad, h2, lp, nalloc + TAIL_ROWS)
    y = _expert_call(first, nwhole, ntail, chunk_row, chunk_len, info, xs, wg_b, wu_b, w_exp_down[0], nalloc)
    out = _combine_call(tab_back, lp, gw, xs1, mod3, final_norm_g, y, seq)
    return out.reshape(bsz, seq, d)
```

```python
import numpy as np
import jax
import jax.numpy as jnp
from jax import lax
from jax.experimental import pallas as pl
from jax.experimental.pallas import tpu as pltpu

F32 = jnp.float32
BF16 = jnp.bfloat16
U32 = jnp.uint32
I32 = jnp.int32

D_MODEL = 1024
HEAD_DIM = 64
N_HEADS = 8
N_KV_HEADS = 2
GROUP = N_HEADS // N_KV_HEADS
Q_COLS = N_HEADS * HEAD_DIM
KV_COLS = N_KV_HEADS * HEAD_DIM
ATTN_BLOCK = 128
WINDOW = 128
NUM_BUCKETS = 32
MAX_DISTANCE = 128
CONV_CH = D_MODEL - Q_COLS
CONV_WIDTH = 31
IN_COLS = Q_COLS + 2 * KV_COLS + 2 * CONV_CH
N_EXPERTS = 256
TOP_K = 8
N_GROUPS = 8
GROUP_SIZE = N_EXPERTS // N_GROUPS
TOPK_GROUPS = 4
EXPERT_HIDDEN = 256
ROUTED_SCALE = 2.5
EPS = 1e-6

MIX_ROWS = 512
HALO = 32
TILE = 256
LROWS = TILE * TOP_K
SORT_CHUNK = 256
EXPERT_ROWS = 576
TAIL_UNIT_ROWS = 128
TAIL_MAX_UNITS = 3
LANES = 128
SUBLANES = 8
PACK_W = D_MODEL // 2
PACK_S = PACK_W // LANES
VMEM_LIMIT = 56 * 1024 * 1024


def _sigmoid(v):
    return 1.0 / (1.0 + jnp.exp(-v))


def _rms(v):
    return v * lax.rsqrt(jnp.mean(v * v, axis=-1, keepdims=True) + EPS)


def _split_bf16(a):
    hi = a.astype(BF16)
    lo = (a - hi.astype(F32)).astype(BF16)
    return hi, lo


def _dot(a, b):
    return jnp.dot(a, b, preferred_element_type=F32)


def _dot_nt(a, b):
    return lax.dot_general(a, b, (((1,), (1,)), ((), ())), preferred_element_type=F32)


def _dot_tn(a, b):
    return lax.dot_general(a, b, (((0,), (0,)), ((), ())), preferred_element_type=F32)


def _dot3(a, b, dot):
    ah, al = _split_bf16(a)
    bh, bl = _split_bf16(b)
    return dot(ah, bh) + (dot(ah, bl) + dot(al, bh))


def _pack_rows(v):
    hi = lax.bitcast_convert_type(v[:, :PACK_W], U32) & jnp.uint32(0xFFFF0000)
    lo = lax.bitcast_convert_type(v[:, PACK_W:], U32) >> 16
    return hi | lo


def _unpack_rows(u):
    hi = lax.bitcast_convert_type(u & jnp.uint32(0xFFFF0000), F32)
    lo = lax.bitcast_convert_type(u << 16, F32)
    return jnp.concatenate([hi, lo], axis=1).astype(BF16)


def _load_flat(flat, r0, n):
    return jnp.concatenate(
        [flat[pl.ds(r0 * PACK_S + c, n, stride=PACK_S), :] for c in range(PACK_S)], axis=1)


def _store_flat(flat, r0, n, u):
    for c in range(PACK_S):
        flat[pl.ds(r0 * PACK_S + c, n, stride=PACK_S), :] = u[:, c * LANES:(c + 1) * LANES]


def _load_packed(ref3, r0, n):
    return _load_flat(ref3.reshape(ref3.shape[0] * PACK_S, LANES), r0, n)


def _store_packed(ref3, r0, n, u):
    _store_flat(ref3.reshape(ref3.shape[0] * PACK_S, LANES), r0, n, u)


def _ada_kernel(c_ref, w_ref, b_ref, o_ref):
    c = c_ref[...]
    s = c * _sigmoid(c)
    o_ref[...] = _dot3(s, w_ref[...], _dot) + b_ref[...]


def _ada_call(c, w_ada, b_ada):
    bsz, d = c.shape
    n = w_ada.shape[1]
    tn = 1536
    return pl.pallas_call(
        _ada_kernel,
        grid=(n // tn,),
        in_specs=[
            pl.BlockSpec((bsz, d), lambda i: (0, 0)),
            pl.BlockSpec((d, tn), lambda i: (0, i)),
            pl.BlockSpec((1, tn), lambda i: (0, i)),
        ],
        out_specs=pl.BlockSpec((bsz, tn), lambda i: (0, i)),
        out_shape=jax.ShapeDtypeStruct((bsz, n), F32),
        compiler_params=pltpu.CompilerParams(
            dimension_semantics=("arbitrary",), vmem_limit_bytes=VMEM_LIMIT),
        name="ada",
    )(c, w_ada, b_ada.reshape(1, n))


def _bucket_table():
    qi = np.arange(ATTN_BLOCK)[:, None]
    ki = np.arange(2 * ATTN_BLOCK)[None, :]
    dist = qi + ATTN_BLOCK - ki
    n = np.maximum(dist, 0)
    max_exact = NUM_BUCKETS // 2
    large = max_exact + (np.log(np.maximum(n, 1) / max_exact) / np.log(MAX_DISTANCE / max_exact)
                         * (NUM_BUCKETS - max_exact)).astype(np.int32)
    large = np.minimum(large, NUM_BUCKETS - 1)
    bkt = np.where(n < max_exact, n, large).astype(np.int32)
    band = (dist >= 0) & (dist < WINDOW)
    return np.ascontiguousarray(np.where(band, bkt, -1).astype(np.int32).T)


def _mixer_kernel(x_ref, mod_ref, win_ref, sinks_ref, relb_ref, bkt_ref, ang_ref, cw_ref, cb_ref,
                  lng_ref, lnb_ref, wout_ref, wrt_ref, wsg_ref, wsu_ref, wsd_ref, ewg_ref, ewu_ref,
                  xs1_ref, h2_ref, lgt_ref, ewgb_ref, ewub_ref,
                  kv_ref, uext_ref, bias_ref, wrh_ref, wrl_ref):
    tm = x_ref.shape[1]
    nblk = tm // ATTN_BLOCK
    j = pl.program_id(1)

    @pl.when((pl.program_id(0) == 0) & (j == 0))
    def _build_bias():
        bkt = bkt_ref[...]
        for h in range(N_HEADS):
            acc = jnp.full(bkt.shape, -jnp.inf, F32)
            for b in range(NUM_BUCKETS):
                acc = jnp.where(bkt == b, relb_ref[b, h], acc)
            g = h % GROUP
            bias_ref[h // GROUP, :, g * ATTN_BLOCK:(g + 1) * ATTN_BLOCK] = acc
        w_hi, w_lo = _split_bf16(wrt_ref[...])
        wrh_ref[...] = w_hi
        wrl_ref[...] = w_lo

    @pl.when(j == 0)
    def _reset_history():
        kv_ref[...] = jnp.zeros(kv_ref.shape, F32)
        uext_ref[0:HALO, :] = jnp.zeros((HALO, CONV_CH), F32)
        uext_ref[HALO + tm:HALO + tm + SUBLANES, :] = jnp.zeros((SUBLANES, CONV_CH), F32)

    x = x_ref[0]
    mod = mod_ref[0]
    sh1, sc1, g1 = mod[0:1], mod[1:2], mod[2:3]
    sh2, sc2, g2 = mod[3:4], mod[4:5], mod[5:6]

    h = _rms(x) * (1.0 + sc1) + sh1
    hb = h.astype(BF16)
    n_qkv = Q_COLS + 2 * KV_COLS
    proj_conv = _dot(hb, win_ref[:, n_qkv:])
    proj = _dot(hb, win_ref[:, :n_qkv])

    qb = (proj[:, :Q_COLS] * (HEAD_DIM ** -0.5)).astype(BF16)
    kv_cur = proj[:, Q_COLS:Q_COLS + 2 * KV_COLS]
    kvb = jnp.concatenate([kv_ref[...], kv_cur], axis=0).astype(BF16)
    kv_ref[...] = kv_cur[tm - ATTN_BLOCK:, :]
    not_first = j > 0
    key = lax.broadcasted_iota(jnp.int32, (2 * ATTN_BLOCK, GROUP * ATTN_BLOCK), 0)
    attn_rows = []
    for i in range(nblk):
        r0, r1, r2 = i * ATTN_BLOCK, (i + 1) * ATTN_BLOCK, (i + 2) * ATTN_BLOCK
        heads_t = [None] * N_HEADS
        for kh in range(N_KV_HEADS):
            kpc = kvb[r0:r2, kh * HEAD_DIM:(kh + 1) * HEAD_DIM]
            vpc = kvb[r0:r2, KV_COLS + kh * HEAD_DIM:KV_COLS + (kh + 1) * HEAD_DIM]
            qg = jnp.concatenate(
                [qb[r0:r1, (kh * GROUP + g) * HEAD_DIM:(kh * GROUP + g + 1) * HEAD_DIM]
                 for g in range(GROUP)], axis=0)
            logit = _dot_nt(kpc, qg) + bias_ref[kh]
            if i == 0:
                logit = jnp.where((key >= ATTN_BLOCK) | not_first, logit, -jnp.inf)
            sink = jnp.concatenate(
                [jnp.full((1, ATTN_BLOCK), sinks_ref[kh * GROUP + g], F32) for g in range(GROUP)], axis=1)
            m = jnp.maximum(jnp.max(logit, axis=0, keepdims=True), sink)
            p = jnp.exp(logit - m)
            den = jnp.sum(p, axis=0, keepdims=True) + jnp.exp(sink - m)
            o_t = _dot_tn(vpc, p.astype(BF16)) / den
            for g in range(GROUP):
                heads_t[kh * GROUP + g] = o_t[:, g * ATTN_BLOCK:(g + 1) * ATTN_BLOCK]
        attn_t = jnp.concatenate(heads_t, axis=0)
        scale = lax.rsqrt(jnp.mean(attn_t * attn_t, axis=0, keepdims=True) + EPS)
        attn_rows.append((attn_t * scale).T)
    attn = jnp.concatenate(attn_rows, axis=0) * ang_ref[...]

    a = proj_conv[:, :CONV_CH]
    gt = proj_conv[:, CONV_CH:]
    uext_ref[HALO:HALO + tm, :] = a * _sigmoid(gt)
    cw = cw_ref[...]
    base = HALO - (CONV_WIDTH - 1)
    acc = None
    for res in range(SUBLANES):
        part = None
        for hi in range((HALO + SUBLANES - 1) // SUBLANES + 1):
            t = hi * SUBLANES + res - base
            if 0 <= t < CONV_WIDTH:
                term = cw[t:t + 1, :] * uext_ref[hi * SUBLANES:hi * SUBLANES + tm + SUBLANES, :]
                part = term if part is None else part + term
        part = part[res:res + tm, :]
        acc = part if acc is None else acc + part
    uext_ref[0:HALO, :] = uext_ref[tm:tm + HALO, :]
    cv = acc + cb_ref[...]
    mu = jnp.mean(cv, axis=-1, keepdims=True)
    var = jnp.mean(jnp.square(cv - mu), axis=-1, keepdims=True)
    cv = (cv - mu) * lax.rsqrt(var + EPS) * lng_ref[...] + lnb_ref[...]
    cv = cv * _sigmoid(cv)

    mixed = (_dot(attn.astype(BF16), wout_ref[0:Q_COLS, :])
             + _dot(cv.astype(BF16), wout_ref[Q_COLS:, :]))
    x1 = x + g1 * mixed

    h2 = _rms(x1) * (1.0 + sc2) + sh2
    h2_hi, h2_lo = _split_bf16(h2)
    w_hi, w_lo = wrh_ref[...], wrl_ref[...]
    lgt_ref[...] = _dot_nt(w_hi, h2_hi) + (_dot_nt(w_hi, h2_lo) + _dot_nt(w_lo, h2_hi))
    h2b = h2.astype(BF16)
    h2_ref[0] = h2b
    sg = _dot(h2b, wsg_ref[...])
    su = _dot(h2b, wsu_ref[...])
    shared = _dot(((sg * _sigmoid(sg)) * su).astype(BF16), wsd_ref[...])
    xs1_ref[0] = x1 + g2 * shared

    ewgb_ref[...] = ewg_ref[...].astype(BF16)
    ewub_ref[...] = ewu_ref[...].astype(BF16)


def _mixer_call(x, mod3, w_in_b, sinks, rel_bias, ang, cw, cb, lng, lnb, w_out_b, w_rt, wsg_b, wsu_b, wsd_b,
                w_gate, w_up):
    bsz, seq, d = x.shape
    tm = MIX_ROWS
    nj = seq // tm
    ne, _, hid = w_gate.shape
    per_step = ne // (bsz * nj)
    assert per_step * bsz * nj == ne
    bkt = jnp.asarray(_bucket_table())
    full = lambda shape: pl.BlockSpec(shape, lambda b, j: (0,) * len(shape), pipeline_mode=pl.Buffered(1))
    share = lambda shape: pl.BlockSpec((per_step,) + shape, lambda b, j: (b * nj + j, 0, 0))
    smem = pl.BlockSpec(memory_space=pltpu.SMEM)
    return pl.pallas_call(
        _mixer_kernel,
        grid=(bsz, nj),
        in_specs=[
            pl.BlockSpec((1, tm, d), lambda b, j: (b, j, 0)),
            pl.BlockSpec((1, 6, d), lambda b, j: (b, 0, 0)),
            full((d, IN_COLS)),
            smem, smem,
            full((2 * ATTN_BLOCK, ATTN_BLOCK)),
            full((1, Q_COLS)),
            full((CONV_WIDTH, CONV_CH)),
            full((1, CONV_CH)), full((1, CONV_CH)), full((1, CONV_CH)),
            full((d, d)),
            full((N_EXPERTS, d)),
            full((d, EXPERT_HIDDEN)), full((d, EXPERT_HIDDEN)), full((EXPERT_HIDDEN, d)),
            share((d, hid)), share((d, hid)),
        ],
        out_specs=[
            pl.BlockSpec((1, tm, d), lambda b, j: (b, j, 0)),
            pl.BlockSpec((1, tm, d), lambda b, j: (b, j, 0)),
            pl.BlockSpec((N_EXPERTS, tm), lambda b, j: (0, b * nj + j)),
            share((d, hid)), share((d, hid)),
        ],
        out_shape=[
            jax.ShapeDtypeStruct((bsz, seq, d), F32),
            jax.ShapeDtypeStruct((bsz, seq, d), BF16),
            jax.ShapeDtypeStruct((N_EXPERTS, bsz * seq), F32),
            jax.ShapeDtypeStruct((ne, d, hid), BF16),
            jax.ShapeDtypeStruct((ne, d, hid), BF16),
        ],
        scratch_shapes=[
            pltpu.VMEM((ATTN_BLOCK, 2 * KV_COLS), F32),
            pltpu.VMEM((HALO + tm + SUBLANES, CONV_CH), F32),
            pltpu.VMEM((N_KV_HEADS, 2 * ATTN_BLOCK, GROUP * ATTN_BLOCK), F32),
            pltpu.VMEM((N_EXPERTS, d), BF16),
            pltpu.VMEM((N_EXPERTS, d), BF16),
        ],
        compiler_params=pltpu.CompilerParams(
            dimension_semantics=("arbitrary", "arbitrary"), vmem_limit_bytes=VMEM_LIMIT),
        name="mixer",
    )(x, mod3, w_in_b, sinks, rel_bias, bkt, ang, cw, cb, lng, lnb, w_out_b, w_rt, wsg_b, wsu_b, wsd_b,
      w_gate, w_up)


def _route_kernel(lgt_ref, rb_ref, lp_ref, gw_ref, cnt_ref, earlier_ref, lower_ref):
    tr = lgt_ref.shape[1]
    i = pl.program_id(0)

    @pl.when(i == 0)
    def _():
        cnt_ref[...] = jnp.zeros(cnt_ref.shape, F32)
        earlier_ref[...] = jnp.where(
            lax.broadcasted_iota(I32, (tr, tr), 0) < lax.broadcasted_iota(I32, (tr, tr), 1), 1.0, 0.0).astype(BF16)
        lower_ref[...] = jnp.where(
            lax.broadcasted_iota(I32, (N_EXPERTS, N_EXPERTS), 1)
            < lax.broadcasted_iota(I32, (N_EXPERTS, N_EXPERTS), 0), 1.0, 0.0).astype(BF16)

    scores = _sigmoid(lgt_ref[...])
    sel = scores + rb_ref[...]
    neg = -jnp.inf

    sel3 = sel.reshape(N_GROUPS, GROUP_SIZE, tr)
    loc = lax.broadcasted_iota(I32, sel3.shape, 1)
    m1 = jnp.max(sel3, axis=1, keepdims=True)
    i1 = jnp.min(jnp.where(sel3 == m1, loc, GROUP_SIZE), axis=1, keepdims=True)
    m2 = jnp.max(jnp.where(loc == i1, neg, sel3), axis=1, keepdims=True)
    gscore = (m1 + m2).reshape(N_GROUPS, tr)

    gio = lax.broadcasted_iota(I32, gscore.shape, 0)
    gmask = jnp.zeros(gscore.shape, jnp.bool_)
    cur = gscore
    for _ in range(TOPK_GROUPS):
        m = jnp.max(cur, axis=0, keepdims=True)
        idx = jnp.min(jnp.where(cur == m, gio, N_GROUPS), axis=0, keepdims=True)
        pick = gio == idx
        gmask = gmask | pick
        cur = jnp.where(pick, neg, cur)
    emask = jnp.broadcast_to(gmask.reshape(N_GROUPS, 1, tr), sel3.shape).reshape(N_EXPERTS, tr)

    rio = lax.broadcasted_iota(I32, sel.shape, 0)
    cur = jnp.where(emask, sel, neg)
    picks, gsc = [], []
    for _ in range(TOP_K):
        m = jnp.max(cur, axis=0, keepdims=True)
        idx = jnp.min(jnp.where(cur == m, rio, N_EXPERTS), axis=0, keepdims=True)
        pick = rio == idx
        picks.append(pick)
        gsc.append(jnp.sum(jnp.where(pick, scores, 0.0), axis=0, keepdims=True))
        cur = jnp.where(pick, neg, cur)
    gsum = gsc[0]
    for k in range(1, TOP_K):
        gsum = gsum + gsc[k]
    gw_ref[...] = jnp.concatenate([g / gsum * ROUTED_SCALE for g in gsc], axis=0)

    onehot = jnp.where(cur == neg, jnp.where(emask, 1.0, 0.0), 0.0)
    before = _dot(onehot.astype(BF16), earlier_ref[...])
    n = jnp.sum(onehot, axis=1, keepdims=True)
    start = _dot(lower_ref[...],
                 jnp.broadcast_to(n, (N_EXPERTS, LANES)).astype(BF16))[:, 0:1]
    pos = start + before
    lp_ref[...] = jnp.concatenate(
        [jnp.sum(jnp.where(p, pos, 0.0), axis=0, keepdims=True) for p in picks], axis=0).astype(I32)

    lane = lax.broadcasted_iota(I32, cnt_ref.shape, 1)
    cnt_ref[...] += jnp.where(lane == i, n, 0.0)


def _route_call(lgt, router_bias):
    e, t = lgt.shape
    ntiles = t // TILE
    ntp = (ntiles + LANES - 1) // LANES * LANES
    return pl.pallas_call(
        _route_kernel,
        grid=(ntiles,),
        in_specs=[
            pl.BlockSpec((e, TILE), lambda i: (0, i)),
            pl.BlockSpec((e, 1), lambda i: (0, 0)),
        ],
        out_specs=[
            pl.BlockSpec((TOP_K, TILE), lambda i: (0, i)),
            pl.BlockSpec((TOP_K, TILE), lambda i: (0, i)),
            pl.BlockSpec((e, ntp), lambda i: (0, 0)),
        ],
        out_shape=[
            jax.ShapeDtypeStruct((TOP_K, t), I32),
            jax.ShapeDtypeStruct((TOP_K, t), F32),
            jax.ShapeDtypeStruct((e, ntp), F32),
        ],
        scratch_shapes=[pltpu.VMEM((TILE, TILE), BF16), pltpu.VMEM((e, e), BF16)],
        compiler_params=pltpu.CompilerParams(
            dimension_semantics=("arbitrary",), vmem_limit_bytes=VMEM_LIMIT),
        name="route",
    )(lgt, router_bias.reshape(e, 1))


_T_LOCAL, _T_LEN, _T_GLOBAL, _T_FIELDS = 0, 1, 2, 3
_T_STRIDE = 2 * N_EXPERTS
_T_EMPTY = _T_STRIDE * _T_FIELDS
_T_SIZE = _T_EMPTY + 2
LBUF_ROWS = LROWS + N_EXPERTS


def _segment_copies(tab_ref, half, local_ref, global_hbm, sem, to_global):
    def body(e, carry):
        at = half * N_EXPERTS + e
        n = tab_ref[0, 0, _T_LEN * _T_STRIDE + at]
        loc = local_ref.at[pl.ds(tab_ref[0, 0, _T_LOCAL * _T_STRIDE + at], n)]
        glo = global_hbm.at[pl.ds(tab_ref[0, 0, _T_GLOBAL * _T_STRIDE + at], n)]
        if to_global:
            pltpu.make_async_copy(loc, glo, sem).start()
        else:
            pltpu.make_async_copy(glo, loc, sem).start()
        return carry

    for e in range(N_EXPERTS):
        body(e, 0)


def _chunk_relative(lp_row, c):
    rel = lp_row - c * SORT_CHUNK
    inside = (rel >= 0) & (rel < SORT_CHUNK)
    return jnp.where(inside, rel, -1).astype(F32).astype(BF16)


def _chunk_row_ids():
    return lax.broadcasted_iota(I32, (SORT_CHUNK, TILE), 0).astype(F32).astype(BF16)


def _segment_wait(local_ref, global_hbm, sem, nempty):
    rows = LROWS + nempty
    pltpu.make_async_copy(global_hbm.at[pl.ds(0, rows)], local_ref.at[pl.ds(0, rows)], sem).wait()


TAIL_ROWS = EXPERT_ROWS + 2 * N_EXPERTS


def _dispatch_kernel(tab_ref, pad_ref, h2_ref, lp_ref, xs_hbm, xl0, xl1, zb, sem, pending):
    s = pl.program_id(0)
    ns = pl.num_programs(0)

    @pl.when(s == 0)
    def _zero_unassigned_rows():
        zb[...] = jnp.zeros(zb.shape, U32)
        used = pad_ref[2, 0]
        rest = xs_hbm.shape[0] - used
        cp = pltpu.make_async_copy(zb.at[pl.ds(0, rest)], xs_hbm.at[pl.ds(used, rest)], sem.at[2])
        cp.start()
        cp.wait()

        def pad_row(e):
            return pltpu.make_async_copy(zb.at[pl.ds(0, 1)], xs_hbm.at[pl.ds(pad_ref[0, e], 1)], sem.at[2])

        def start(e, carry):
            @pl.when(pad_ref[1, e] > 0)
            def _():
                pad_row(e).start()
            return carry

        def wait(e, carry):
            @pl.when(pad_ref[1, e] > 0)
            def _():
                pad_row(e).wait()
            return carry

        lax.fori_loop(0, N_EXPERTS, start, 0)
        lax.fori_loop(0, N_EXPERTS, wait, 0)

    for half, xl in ((0, xl0), (1, xl1)):
        @pl.when(s > 0)
        def _():
            _segment_wait(xl, xs_hbm, sem.at[half], pending[half])

        rows = h2_ref[half * TILE:(half + 1) * TILE, :]
        lp = lp_ref[:, half * TILE:(half + 1) * TILE]
        j = _chunk_row_ids()
        for c in range(LROWS // SORT_CHUNK):
            onehot = jnp.zeros((SORT_CHUNK, TILE), BF16)
            for k in range(TOP_K):
                onehot = jnp.where(j == _chunk_relative(lp[k:k + 1, :], c), jnp.ones((), BF16), onehot)
            _store_packed(xl, c * SORT_CHUNK, SORT_CHUNK, _pack_rows(_dot(onehot, rows)))
        _segment_copies(tab_ref, half, xl, xs_hbm, sem.at[half], to_global=True)
        pending[half] = tab_ref[0, 0, _T_EMPTY + half]

    @pl.when(s == ns - 1)
    def _():
        _segment_wait(xl0, xs_hbm, sem.at[0], pending[0])
        _segment_wait(xl1, xs_hbm, sem.at[1], pending[1])


def _dispatch_call(tab, pad, h2, lp, nrows):
    t, d = h2.shape
    ns = t // (2 * TILE)
    return pl.pallas_call(
        _dispatch_kernel,
        grid=(ns,),
        in_specs=[
            pl.BlockSpec((1, 1, _T_SIZE), lambda s: (s, 0, 0), memory_space=pltpu.SMEM),
            pl.BlockSpec(memory_space=pltpu.SMEM),
            pl.BlockSpec((2 * TILE, d), lambda s: (s, 0)),
            pl.BlockSpec((TOP_K, 2 * TILE), lambda s: (0, s)),
        ],
        out_specs=pl.BlockSpec(memory_space=pl.ANY),
        out_shape=jax.ShapeDtypeStruct((nrows, PACK_S, LANES), U32),
        scratch_shapes=[
            pltpu.VMEM((LBUF_ROWS, PACK_S, LANES), U32),
            pltpu.VMEM((LBUF_ROWS, PACK_S, LANES), U32),
            pltpu.VMEM((TAIL_ROWS + N_EXPERTS, PACK_S, LANES), U32),
            pltpu.SemaphoreType.DMA((3,)),
            pltpu.SMEM((2,), I32),
        ],
        compiler_params=pltpu.CompilerParams(
            dimension_semantics=("arbitrary",), vmem_limit_bytes=VMEM_LIMIT),
        name="dispatch",
    )(tab, pad, h2, lp)


RING = 6
AHEAD = RING - 2


def _expert_kernel(first_ref, nchunk_ref, ntail_ref, row_ref, len_ref, info_ref, wg_ref, wu_ref, wd_ref,
                   xs_hbm, y_hbm, xbuf, ybuf, isem, osem, wdb):
    e = pl.program_id(0)
    last_step = e == pl.num_programs(0) - 1
    rows = EXPERT_ROWS
    total_chunks = info_ref[0]

    def slab(ref, row0, m):
        size = m * PACK_S if isinstance(m, int) else pl.multiple_of(m * PACK_S, SUBLANES)
        first = row0 * PACK_S if isinstance(row0, int) else pl.multiple_of(row0 * PACK_S, SUBLANES)
        return ref.at[pl.ds(first, size)]

    def fetch(g):
        slot = g % RING
        return pltpu.make_async_copy(slab(xs_hbm, row_ref[g], rows), slab(xbuf, slot * rows, rows), isem.at[slot])

    def flush(g):
        slot = g % RING
        m = len_ref[g]
        return pltpu.make_async_copy(slab(ybuf, slot * rows, m), slab(y_hbm, row_ref[g], m), osem.at[slot])

    def start_fetch(g):
        @pl.when(g < total_chunks)
        def _():
            fetch(g).start()

    def wait_flush(g):
        @pl.when(g >= 0)
        def _():
            flush(g).wait()

    def compute(g, n=EXPERT_ROWS):
        slot = g % RING
        xb = _unpack_rows(_load_flat(xbuf, slot * rows, n))
        gate = _dot(xb, wg_ref[0])
        up = _dot(xb, wu_ref[0])
        hid = (gate * _sigmoid(gate)) * up
        y = _dot(hid.astype(BF16), wdb[...])
        _store_flat(ybuf, slot * rows, n, _pack_rows(y.astype(BF16).astype(F32)))

    @pl.when(e == 0)
    def _prime():
        for a in range(AHEAD):
            start_fetch(a)

    wdb[...] = wd_ref[0].astype(BF16)

    begin = first_ref[e]
    count = nchunk_ref[e]

    def pair(p, carry):
        g0 = begin + 2 * p
        g1 = g0 + 1
        start_fetch(g0 + AHEAD)
        start_fetch(g1 + AHEAD)
        fetch(g0).wait()
        fetch(g1).wait()
        wait_flush(g0 - RING)
        wait_flush(g1 - RING)
        compute(g0)
        compute(g1)
        flush(g0).start()
        flush(g1).start()
        return carry

    lax.fori_loop(0, count // 2, pair, 0)

    def single(g, n):
        start_fetch(g + AHEAD)
        fetch(g).wait()
        wait_flush(g - RING)
        compute(g, n)
        flush(g).start()

    @pl.when(count % 2 == 1)
    def _odd_chunk():
        single(begin + count - 1, EXPERT_ROWS)

    for units in range(1, TAIL_MAX_UNITS + 1):
        @pl.when(ntail_ref[e] == units)
        def _short_chunk():
            single(begin + count, units * TAIL_UNIT_ROWS)

    @pl.when(last_step)
    def _drain():
        def wait(g, carry):
            flush(g).wait()
            return carry
        lax.fori_loop(jnp.maximum(total_chunks - RING, 0), total_chunks, wait, 0)

    @pl.when(last_step)
    def _define_unused_rows():
        used = info_ref[1]
        rest = y_hbm.shape[0] // PACK_S - used

        @pl.when(rest > 0)
        def _():
            ybuf[0:N_EXPERTS * PACK_S, :] = jnp.zeros((N_EXPERTS * PACK_S, LANES), U32)
            cp = pltpu.make_async_copy(slab(ybuf, 0, rest), slab(y_hbm, used, rest), osem.at[0])
            cp.start()
            cp.wait()


def _expert_call(first, nchunk, ntail, chunk_row, chunk_len, info, xs, w_gate, w_up, w_down, nrows):
    ne, d, hid = w_gate.shape
    rows = EXPERT_ROWS
    wsel = lambda e, *_: (e, 0, 0)
    grid_spec = pltpu.PrefetchScalarGridSpec(
        num_scalar_prefetch=6,
        grid=(ne,),
        in_specs=[
            pl.BlockSpec((1, d, hid), wsel),
            pl.BlockSpec((1, d, hid), wsel),
            pl.BlockSpec((1, hid, d), wsel),
            pl.BlockSpec(memory_space=pl.ANY),
        ],
        out_specs=pl.BlockSpec(memory_space=pl.ANY),
        scratch_shapes=[
            pltpu.VMEM((RING * rows * PACK_S, LANES), U32),
            pltpu.VMEM((RING * rows * PACK_S, LANES), U32),
            pltpu.SemaphoreType.DMA((RING,)),
            pltpu.SemaphoreType.DMA((RING,)),
            pltpu.VMEM((hid, d), BF16),
        ],
    )
    return pl.pallas_call(
        _expert_kernel,
        grid_spec=grid_spec,
        out_shape=jax.ShapeDtypeStruct((nrows * PACK_S, LANES), U32),
        compiler_params=pltpu.CompilerParams(
            dimension_semantics=("arbitrary",), vmem_limit_bytes=VMEM_LIMIT),
        name="experts",
    )(first, nchunk, ntail, chunk_row, chunk_len, info, w_gate, w_up, w_down,
      xs.reshape(-1, LANES)).reshape(nrows, PACK_S, LANES)


def _combine_kernel(tab_ref, tabn_ref, lp_ref, gw_ref, xs1_ref, mod_ref, fg_ref, y_hbm, o_ref, yl0, yl1, sem):
    s = pl.program_id(0)
    ns = pl.num_programs(0)

    @pl.when(s == 0)
    def _():
        _segment_copies(tab_ref, 0, yl0, y_hbm, sem.at[0], to_global=False)

    g2 = mod_ref[0][5:6]
    for half, yl in ((0, yl0), (1, yl1)):
        _segment_wait(yl, y_hbm, sem.at[half], tab_ref[0, 0, _T_EMPTY + half])
        if half == 0:
            _segment_copies(tab_ref, 1, yl1, y_hbm, sem.at[1], to_global=False)
        lp = lp_ref[:, half * TILE:(half + 1) * TILE]
        gw = gw_ref[:, half * TILE:(half + 1) * TILE]
        acc = jnp.zeros((TILE, D_MODEL), F32)
        gwb = [gw[k:k + 1, :].astype(BF16) for k in range(TOP_K)]
        j = _chunk_row_ids()
        for c in range(LROWS // SORT_CHUNK):
            wt = jnp.zeros((SORT_CHUNK, TILE), BF16)
            for k in range(TOP_K):
                wt = jnp.where(j == _chunk_relative(lp[k:k + 1, :], c), gwb[k], wt)
            yv = _unpack_rows(_load_packed(yl, c * SORT_CHUNK, SORT_CHUNK))
            acc = acc + _dot_tn(wt, yv)
        if half == 0:
            _segment_copies(tabn_ref, 0, yl0, y_hbm, sem.at[0], to_global=False)
        x2 = xs1_ref[half * TILE:(half + 1) * TILE, :] + g2 * acc
        o_ref[half * TILE:(half + 1) * TILE, :] = _rms(x2) * fg_ref[...]

    @pl.when(s == ns - 1)
    def _():
        _segment_wait(yl0, y_hbm, sem.at[0], tab_ref[0, 0, _T_EMPTY])


def _combine_call(tab, lp, gw, xs1, mod3, final_g, y, seq):
    t, d = xs1.shape
    ns = t // (2 * TILE)
    per_seq = seq // (2 * TILE)
    return pl.pallas_call(
        _combine_kernel,
        grid=(ns,),
        in_specs=[
            pl.BlockSpec((1, 1, _T_SIZE), lambda s: (s, 0, 0), memory_space=pltpu.SMEM),
            pl.BlockSpec((1, 1, _T_SIZE), lambda s: (jnp.minimum(s + 1, ns - 1), 0, 0),
                         memory_space=pltpu.SMEM),
            pl.BlockSpec((TOP_K, 2 * TILE), lambda s: (0, s)),
            pl.BlockSpec((TOP_K, 2 * TILE), lambda s: (0, s)),
            pl.BlockSpec((2 * TILE, d), lambda s: (s, 0)),
            pl.BlockSpec((1, 6, d), lambda s: (s // per_seq, 0, 0)),
            pl.BlockSpec((1, d), lambda s: (0, 0)),
            pl.BlockSpec(memory_space=pl.ANY),
        ],
        out_specs=pl.BlockSpec((2 * TILE, d), lambda s: (s, 0)),
        out_shape=jax.ShapeDtypeStruct((t, d), F32),
        scratch_shapes=[
            pltpu.VMEM((LBUF_ROWS, PACK_S, LANES), U32),
            pltpu.VMEM((LBUF_ROWS, PACK_S, LANES), U32),
            pltpu.SemaphoreType.DMA((2,)),
        ],
        compiler_params=pltpu.CompilerParams(
            dimension_semantics=("arbitrary",), vmem_limit_bytes=VMEM_LIMIT),
        name="combine",
    )(tab, tab, lp, gw, xs1, mod3, final_g.reshape(1, d), y)


def kernel(x, c, w_ada, b_ada, w_in, attn_sinks, rel_bias, attn_norm_g, conv_w, conv_b, conv_ln_g,
           conv_ln_b, w_out, w_router, router_bias, w_exp_gate, w_exp_up, w_exp_down, w_sh_gate,
           w_sh_up, w_sh_down, final_norm_g):
    bsz, seq, d = x.shape
    t = bsz * seq
    assert w_ada.shape[0] == 1 and d == D_MODEL
    assert seq % MIX_ROWS == 0 and seq % (2 * TILE) == 0

    mod3 = _ada_call(c, w_ada[0], b_ada[0]).reshape(bsz, 6, d)
    xs1, h2, lgt, wg_b, wu_b = _mixer_call(
        x, mod3, w_in[0].astype(BF16), attn_sinks[0], rel_bias,
        attn_norm_g[0].reshape(1, Q_COLS), conv_w[0], conv_b[0].reshape(1, CONV_CH),
        conv_ln_g[0].reshape(1, CONV_CH), conv_ln_b[0].reshape(1, CONV_CH),
        w_out[0].astype(BF16), w_router[0].T,
        w_sh_gate[0].astype(BF16), w_sh_up[0].astype(BF16), w_sh_down[0].astype(BF16),
        w_exp_gate[0], w_exp_up[0])
    xs1 = xs1.reshape(t, d)
    h2 = h2.reshape(t, d)

    lp, gw, cnt = _route_call(lgt, router_bias[0])

    ntiles = t // TILE
    nassign = t * TOP_K
    n = cnt[:, :ntiles].T.astype(I32)
    local = jnp.cumsum(n, axis=1) - n
    earlier = jnp.cumsum(n, axis=0) - n
    total = jnp.sum(n, axis=0)
    region = (total + 1) // 2 * 2
    starts = jnp.cumsum(region) - region
    nalloc = nassign + N_EXPERTS
    used = jnp.sum(region)
    pad = jnp.stack([starts + total, region - total,
                     jnp.broadcast_to(used, (N_EXPERTS,))], axis=0)
    glob = starts[None, :] + earlier
    empty = n == 0
    length = jnp.maximum(n, 1)
    eid = jnp.arange(N_EXPERTS, dtype=I32)[None, :]
    parity = (jnp.arange(ntiles, dtype=I32) % 2)[:, None]
    spare_global = nalloc + EXPERT_ROWS + parity * N_EXPERTS + eid
    nempty = jnp.sum(empty.astype(I32), axis=1).reshape(ntiles // 2, 2)

    def table(loc, glo):
        fields = [f.reshape(ntiles // 2, _T_STRIDE) for f in (loc, length, glo)]
        return jnp.concatenate(fields + [nempty], axis=1).reshape(ntiles // 2, 1, _T_SIZE)

    tab_out = table(jnp.where(empty, 0, local), jnp.where(empty, spare_global, glob))
    tab_back = table(jnp.where(empty, LROWS + eid, local), jnp.where(empty, 0, glob))

    rows = EXPERT_ROWS
    tail_max = TAIL_MAX_UNITS * TAIL_UNIT_ROWS
    left_over = total % (2 * rows)
    beyond = jnp.where(left_over > rows, left_over - rows, left_over)
    short = (beyond > 0) & (beyond <= tail_max) & ((left_over <= tail_max) | (left_over > rows))
    ntail = jnp.where(short, (beyond + TAIL_UNIT_ROWS - 1) // TAIL_UNIT_ROWS, 0).astype(I32)
    nwhole = (2 * (total // (2 * rows)) + (left_over > tail_max).astype(I32)
              + (left_over > rows + tail_max).astype(I32))
    nchunk = nwhole + (ntail > 0).astype(I32)
    chunk_end = jnp.cumsum(nchunk)
    first = chunk_end - nchunk
    max_chunks = nassign // rows + 2 * N_EXPERTS
    gid = jnp.arange(max_chunks, dtype=I32)
    owner = jnp.minimum(jnp.sum((chunk_end[None, :] <= gid[:, None]).astype(I32), axis=1), N_EXPERTS - 1)
    pick = owner[:, None] == eid
    within = gid - jnp.sum(jnp.where(pick, first[None, :], 0), axis=1)
    chunk_row = jnp.sum(jnp.where(pick, starts[None, :], 0), axis=1) + within * rows
    left = jnp.sum(jnp.where(pick, region[None, :], 0), axis=1) - within * rows
    live = gid < chunk_end[-1]
    chunk_row = jnp.where(live, chunk_row, 0)
    chunk_len = jnp.where(live, jnp.clip(left, 2, rows), 2)
    info = jnp.stack([chunk_end[-1], used])

    xs = _dispatch_call(tab_out, pad, h2, lp, nalloc + TAIL_ROWS)
    y = _expert_call(first, nwhole, ntail, chunk_row, chunk_len, info, xs, wg_b, wu_b, w_exp_down[0], nalloc)
    out = _combine_call(tab_back, lp, gw, xs1, mod3, final_norm_g, y, seq)
    return out.reshape(bsz, seq, d)
```

```python
import numpy as np
import jax
import jax.numpy as jnp
from jax import lax
from jax.experimental import pallas as pl
from jax.experimental.pallas import tpu as pltpu

F32 = jnp.float32
BF16 = jnp.bfloat16
U32 = jnp.uint32
I32 = jnp.int32

D_MODEL = 1024
HEAD_DIM = 64
N_HEADS = 8
N_KV_HEADS = 2
GROUP = N_HEADS // N_KV_HEADS
Q_COLS = N_HEADS * HEAD_DIM
KV_COLS = N_KV_HEADS * HEAD_DIM
ATTN_BLOCK = 128
WINDOW = 128
NUM_BUCKETS = 32
MAX_DISTANCE = 128
CONV_CH = D_MODEL - Q_COLS
CONV_WIDTH = 31
IN_COLS = Q_COLS + 2 * KV_COLS + 2 * CONV_CH
N_EXPERTS = 256
TOP_K = 8
N_GROUPS = 8
GROUP_SIZE = N_EXPERTS // N_GROUPS
TOPK_GROUPS = 4
EXPERT_HIDDEN = 256
ROUTED_SCALE = 2.5
EPS = 1e-6

MIX_ROWS = 512
HALO = 32
TILE = 256
LROWS = TILE * TOP_K
SORT_CHUNK = 256
EXPERT_ROWS = 576
TAIL_CHUNK_ROWS = 256
LANES = 128
SUBLANES = 8
PACK_W = D_MODEL // 2
PACK_S = PACK_W // LANES
VMEM_LIMIT = 56 * 1024 * 1024


def _sigmoid(v):
    return 1.0 / (1.0 + jnp.exp(-v))


def _rms(v):
    return v * lax.rsqrt(jnp.mean(v * v, axis=-1, keepdims=True) + EPS)


def _split_bf16(a):
    hi = a.astype(BF16)
    lo = (a - hi.astype(F32)).astype(BF16)
    return hi, lo


def _dot(a, b):
    return jnp.dot(a, b, preferred_element_type=F32)


def _dot_nt(a, b):
    return lax.dot_general(a, b, (((1,), (1,)), ((), ())), preferred_element_type=F32)


def _dot_tn(a, b):
    return lax.dot_general(a, b, (((0,), (0,)), ((), ())), preferred_element_type=F32)


def _dot3(a, b, dot):
    ah, al = _split_bf16(a)
    bh, bl = _split_bf16(b)
    return dot(ah, bh) + (dot(ah, bl) + dot(al, bh))


def _pack_rows(v):
    hi = lax.bitcast_convert_type(v[:, :PACK_W], U32) & jnp.uint32(0xFFFF0000)
    lo = lax.bitcast_convert_type(v[:, PACK_W:], U32) >> 16
    return hi | lo


def _unpack_rows(u):
    hi = lax.bitcast_convert_type(u & jnp.uint32(0xFFFF0000), F32)
    lo = lax.bitcast_convert_type(u << 16, F32)
    return jnp.concatenate([hi, lo], axis=1).astype(BF16)


def _load_flat(flat, r0, n):
    return jnp.concatenate(
        [flat[pl.ds(r0 * PACK_S + c, n, stride=PACK_S), :] for c in range(PACK_S)], axis=1)


def _store_flat(flat, r0, n, u):
    for c in range(PACK_S):
        flat[pl.ds(r0 * PACK_S + c, n, stride=PACK_S), :] = u[:, c * LANES:(c + 1) * LANES]


def _load_packed(ref3, r0, n):
    return _load_flat(ref3.reshape(ref3.shape[0] * PACK_S, LANES), r0, n)


def _store_packed(ref3, r0, n, u):
    _store_flat(ref3.reshape(ref3.shape[0] * PACK_S, LANES), r0, n, u)


def _ada_kernel(c_ref, w_ref, b_ref, o_ref):
    c = c_ref[...]
    s = c * _sigmoid(c)
    o_ref[...] = _dot3(s, w_ref[...], _dot) + b_ref[...]


def _ada_call(c, w_ada, b_ada):
    bsz, d = c.shape
    n = w_ada.shape[1]
    tn = 1536
    return pl.pallas_call(
        _ada_kernel,
        grid=(n // tn,),
        in_specs=[
            pl.BlockSpec((bsz, d), lambda i: (0, 0)),
            pl.BlockSpec((d, tn), lambda i: (0, i)),
            pl.BlockSpec((1, tn), lambda i: (0, i)),
        ],
        out_specs=pl.BlockSpec((bsz, tn), lambda i: (0, i)),
        out_shape=jax.ShapeDtypeStruct((bsz, n), F32),
        compiler_params=pltpu.CompilerParams(
            dimension_semantics=("arbitrary",), vmem_limit_bytes=VMEM_LIMIT),
        name="ada",
    )(c, w_ada, b_ada.reshape(1, n))


def _bucket_table():
    qi = np.arange(ATTN_BLOCK)[:, None]
    ki = np.arange(2 * ATTN_BLOCK)[None, :]
    dist = qi + ATTN_BLOCK - ki
    n = np.maximum(dist, 0)
    max_exact = NUM_BUCKETS // 2
    large = max_exact + (np.log(np.maximum(n, 1) / max_exact) / np.log(MAX_DISTANCE / max_exact)
                         * (NUM_BUCKETS - max_exact)).astype(np.int32)
    large = np.minimum(large, NUM_BUCKETS - 1)
    bkt = np.where(n < max_exact, n, large).astype(np.int32)
    band = (dist >= 0) & (dist < WINDOW)
    return np.ascontiguousarray(np.where(band, bkt, -1).astype(np.int32).T)


def _mixer_kernel(x_ref, mod_ref, win_ref, sinks_ref, relb_ref, bkt_ref, ang_ref, cw_ref, cb_ref,
                  lng_ref, lnb_ref, wout_ref, wrt_ref, wsg_ref, wsu_ref, wsd_ref, ewg_ref, ewu_ref,
                  xs1_ref, h2_ref, lgt_ref, ewgb_ref, ewub_ref,
                  kv_ref, uext_ref, bias_ref):
    tm = x_ref.shape[1]
    nblk = tm // ATTN_BLOCK
    j = pl.program_id(1)

    @pl.when((pl.program_id(0) == 0) & (j == 0))
    def _build_bias():
        bkt = bkt_ref[...]
        for h in range(N_HEADS):
            acc = jnp.full(bkt.shape, -jnp.inf, F32)
            for b in range(NUM_BUCKETS):
                acc = jnp.where(bkt == b, relb_ref[b, h], acc)
            g = h % GROUP
            bias_ref[h // GROUP, :, g * ATTN_BLOCK:(g + 1) * ATTN_BLOCK] = acc

    @pl.when(j == 0)
    def _reset_history():
        kv_ref[...] = jnp.zeros(kv_ref.shape, F32)
        uext_ref[0:HALO, :] = jnp.zeros((HALO, CONV_CH), F32)
        uext_ref[HALO + tm:HALO + tm + SUBLANES, :] = jnp.zeros((SUBLANES, CONV_CH), F32)

    x = x_ref[0]
    mod = mod_ref[0]
    sh1, sc1, g1 = mod[0:1], mod[1:2], mod[2:3]
    sh2, sc2, g2 = mod[3:4], mod[4:5], mod[5:6]

    h = _rms(x) * (1.0 + sc1) + sh1
    hb = h.astype(BF16)
    n_qkv = Q_COLS + 2 * KV_COLS
    proj_conv = _dot(hb, win_ref[:, n_qkv:])
    proj = _dot(hb, win_ref[:, :n_qkv])

    qb = (proj[:, :Q_COLS] * (HEAD_DIM ** -0.5)).astype(BF16)
    kv_cur = proj[:, Q_COLS:Q_COLS + 2 * KV_COLS]
    kvb = jnp.concatenate([kv_ref[...], kv_cur], axis=0).astype(BF16)
    kv_ref[...] = kv_cur[tm - ATTN_BLOCK:, :]
    not_first = j > 0
    key = lax.broadcasted_iota(jnp.int32, (2 * ATTN_BLOCK, GROUP * ATTN_BLOCK), 0)
    attn_rows = []
    for i in range(nblk):
        r0, r1, r2 = i * ATTN_BLOCK, (i + 1) * ATTN_BLOCK, (i + 2) * ATTN_BLOCK
        heads_t = [None] * N_HEADS
        for kh in range(N_KV_HEADS):
            kpc = kvb[r0:r2, kh * HEAD_DIM:(kh + 1) * HEAD_DIM]
            vpc = kvb[r0:r2, KV_COLS + kh * HEAD_DIM:KV_COLS + (kh + 1) * HEAD_DIM]
            qg = jnp.concatenate(
                [qb[r0:r1, (kh * GROUP + g) * HEAD_DIM:(kh * GROUP + g + 1) * HEAD_DIM]
                 for g in range(GROUP)], axis=0)
            logit = _dot_nt(kpc, qg) + bias_ref[kh]
            if i == 0:
                logit = jnp.where((key >= ATTN_BLOCK) | not_first, logit, -jnp.inf)
            sink = jnp.concatenate(
                [jnp.full((1, ATTN_BLOCK), sinks_ref[kh * GROUP + g], F32) for g in range(GROUP)], axis=1)
            m = jnp.maximum(jnp.max(logit, axis=0, keepdims=True), sink)
            p = jnp.exp(logit - m)
            den = jnp.sum(p, axis=0, keepdims=True) + jnp.exp(sink - m)
            o_t = _dot_tn(vpc, p.astype(BF16)) / den
            for g in range(GROUP):
                heads_t[kh * GROUP + g] = o_t[:, g * ATTN_BLOCK:(g + 1) * ATTN_BLOCK]
        attn_t = jnp.concatenate(heads_t, axis=0)
        scale = lax.rsqrt(jnp.mean(attn_t * attn_t, axis=0, keepdims=True) + EPS)
        attn_rows.append((attn_t * scale).T)
    attn = jnp.concatenate(attn_rows, axis=0) * ang_ref[...]

    a = proj_conv[:, :CONV_CH]
    gt = proj_conv[:, CONV_CH:]
    uext_ref[HALO:HALO + tm, :] = a * _sigmoid(gt)
    cw = cw_ref[...]
    base = HALO - (CONV_WIDTH - 1)
    acc = None
    for res in range(SUBLANES):
        part = None
        for hi in range((HALO + SUBLANES - 1) // SUBLANES + 1):
            t = hi * SUBLANES + res - base
            if 0 <= t < CONV_WIDTH:
                term = cw[t:t + 1, :] * uext_ref[hi * SUBLANES:hi * SUBLANES + tm + SUBLANES, :]
                part = term if part is None else part + term
        part = part[res:res + tm, :]
        acc = part if acc is None else acc + part
    uext_ref[0:HALO, :] = uext_ref[tm:tm + HALO, :]
    cv = acc + cb_ref[...]
    mu = jnp.mean(cv, axis=-1, keepdims=True)
    var = jnp.mean(jnp.square(cv - mu), axis=-1, keepdims=True)
    cv = (cv - mu) * lax.rsqrt(var + EPS) * lng_ref[...] + lnb_ref[...]
    cv = cv * _sigmoid(cv)

    mixed = (_dot(attn.astype(BF16), wout_ref[0:Q_COLS, :])
             + _dot(cv.astype(BF16), wout_ref[Q_COLS:, :]))
    x1 = x + g1 * mixed

    h2 = _rms(x1) * (1.0 + sc2) + sh2
    h2b = h2.astype(BF16)
    h2_ref[0] = h2b
    lgt_ref[...] = _dot_nt(wrt_ref[...], h2b)
    sg = _dot(h2b, wsg_ref[...])
    su = _dot(h2b, wsu_ref[...])
    shared = _dot(((sg * _sigmoid(sg)) * su).astype(BF16), wsd_ref[...])
    xs1_ref[0] = x1 + g2 * shared

    ewgb_ref[...] = ewg_ref[...].astype(BF16)
    ewub_ref[...] = ewu_ref[...].astype(BF16)


def _mixer_call(x, mod3, w_in_b, sinks, rel_bias, ang, cw, cb, lng, lnb, w_out_b, w_rt, wsg_b, wsu_b, wsd_b,
                w_gate, w_up):
    bsz, seq, d = x.shape
    tm = MIX_ROWS
    nj = seq // tm
    ne, _, hid = w_gate.shape
    per_step = ne // (bsz * nj)
    assert per_step * bsz * nj == ne
    bkt = jnp.asarray(_bucket_table())
    full = lambda shape: pl.BlockSpec(shape, lambda b, j: (0,) * len(shape), pipeline_mode=pl.Buffered(1))
    share = lambda shape: pl.BlockSpec((per_step,) + shape, lambda b, j: (b * nj + j, 0, 0))
    smem = pl.BlockSpec(memory_space=pltpu.SMEM)
    return pl.pallas_call(
        _mixer_kernel,
        grid=(bsz, nj),
        in_specs=[
            pl.BlockSpec((1, tm, d), lambda b, j: (b, j, 0)),
            pl.BlockSpec((1, 6, d), lambda b, j: (b, 0, 0)),
            full((d, IN_COLS)),
            smem, smem,
            full((2 * ATTN_BLOCK, ATTN_BLOCK)),
            full((1, Q_COLS)),
            full((CONV_WIDTH, CONV_CH)),
            full((1, CONV_CH)), full((1, CONV_CH)), full((1, CONV_CH)),
            full((d, d)),
            full((N_EXPERTS, d)),
            full((d, EXPERT_HIDDEN)), full((d, EXPERT_HIDDEN)), full((EXPERT_HIDDEN, d)),
            share((d, hid)), share((d, hid)),
        ],
        out_specs=[
            pl.BlockSpec((1, tm, d), lambda b, j: (b, j, 0)),
            pl.BlockSpec((1, tm, d), lambda b, j: (b, j, 0)),
            pl.BlockSpec((N_EXPERTS, tm), lambda b, j: (0, b * nj + j)),
            share((d, hid)), share((d, hid)),
        ],
        out_shape=[
            jax.ShapeDtypeStruct((bsz, seq, d), F32),
            jax.ShapeDtypeStruct((bsz, seq, d), BF16),
            jax.ShapeDtypeStruct((N_EXPERTS, bsz * seq), F32),
            jax.ShapeDtypeStruct((ne, d, hid), BF16),
            jax.ShapeDtypeStruct((ne, d, hid), BF16),
        ],
        scratch_shapes=[
            pltpu.VMEM((ATTN_BLOCK, 2 * KV_COLS), F32),
            pltpu.VMEM((HALO + tm + SUBLANES, CONV_CH), F32),
            pltpu.VMEM((N_KV_HEADS, 2 * ATTN_BLOCK, GROUP * ATTN_BLOCK), F32),
        ],
        compiler_params=pltpu.CompilerParams(
            dimension_semantics=("arbitrary", "arbitrary"), vmem_limit_bytes=VMEM_LIMIT),
        name="mixer",
    )(x, mod3, w_in_b, sinks, rel_bias, bkt, ang, cw, cb, lng, lnb, w_out_b, w_rt, wsg_b, wsu_b, wsd_b,
      w_gate, w_up)


def _route_kernel(lgt_ref, rb_ref, lp_ref, gw_ref, cnt_ref, earlier_ref, lower_ref):
    tr = lgt_ref.shape[1]
    i = pl.program_id(0)

    @pl.when(i == 0)
    def _():
        cnt_ref[...] = jnp.zeros(cnt_ref.shape, F32)
        earlier_ref[...] = jnp.where(
            lax.broadcasted_iota(I32, (tr, tr), 0) < lax.broadcasted_iota(I32, (tr, tr), 1), 1.0, 0.0).astype(BF16)
        lower_ref[...] = jnp.where(
            lax.broadcasted_iota(I32, (N_EXPERTS, N_EXPERTS), 1)
            < lax.broadcasted_iota(I32, (N_EXPERTS, N_EXPERTS), 0), 1.0, 0.0).astype(BF16)

    scores = _sigmoid(lgt_ref[...])
    sel = scores + rb_ref[...]
    neg = -jnp.inf

    sel3 = sel.reshape(N_GROUPS, GROUP_SIZE, tr)
    loc = lax.broadcasted_iota(I32, sel3.shape, 1)
    m1 = jnp.max(sel3, axis=1, keepdims=True)
    i1 = jnp.min(jnp.where(sel3 == m1, loc, GROUP_SIZE), axis=1, keepdims=True)
    m2 = jnp.max(jnp.where(loc == i1, neg, sel3), axis=1, keepdims=True)
    gscore = (m1 + m2).reshape(N_GROUPS, tr)

    gio = lax.broadcasted_iota(I32, gscore.shape, 0)
    gmask = jnp.zeros(gscore.shape, jnp.bool_)
    cur = gscore
    for _ in range(TOPK_GROUPS):
        m = jnp.max(cur, axis=0, keepdims=True)
        idx = jnp.min(jnp.where(cur == m, gio, N_GROUPS), axis=0, keepdims=True)
        pick = gio == idx
        gmask = gmask | pick
        cur = jnp.where(pick, neg, cur)
    emask = jnp.broadcast_to(gmask.reshape(N_GROUPS, 1, tr), sel3.shape).reshape(N_EXPERTS, tr)

    rio = lax.broadcasted_iota(I32, sel.shape, 0)
    cur = jnp.where(emask, sel, neg)
    picks, gsc = [], []
    for _ in range(TOP_K):
        m = jnp.max(cur, axis=0, keepdims=True)
        idx = jnp.min(jnp.where(cur == m, rio, N_EXPERTS), axis=0, keepdims=True)
        pick = rio == idx
        picks.append(pick)
        gsc.append(jnp.sum(jnp.where(pick, scores, 0.0), axis=0, keepdims=True))
        cur = jnp.where(pick, neg, cur)
    gsum = gsc[0]
    for k in range(1, TOP_K):
        gsum = gsum + gsc[k]
    gw_ref[...] = jnp.concatenate([g / gsum * ROUTED_SCALE for g in gsc], axis=0)

    onehot = jnp.where(cur == neg, jnp.where(emask, 1.0, 0.0), 0.0)
    before = _dot(onehot.astype(BF16), earlier_ref[...])
    n = jnp.sum(onehot, axis=1, keepdims=True)
    start = _dot(lower_ref[...],
                 jnp.broadcast_to(n, (N_EXPERTS, LANES)).astype(BF16))[:, 0:1]
    pos = start + before
    lp_ref[...] = jnp.concatenate(
        [jnp.sum(jnp.where(p, pos, 0.0), axis=0, keepdims=True) for p in picks], axis=0).astype(I32)

    lane = lax.broadcasted_iota(I32, cnt_ref.shape, 1)
    cnt_ref[...] += jnp.where(lane == i, n, 0.0)


def _route_call(lgt, router_bias):
    e, t = lgt.shape
    ntiles = t // TILE
    ntp = (ntiles + LANES - 1) // LANES * LANES
    return pl.pallas_call(
        _route_kernel,
        grid=(ntiles,),
        in_specs=[
            pl.BlockSpec((e, TILE), lambda i: (0, i)),
            pl.BlockSpec((e, 1), lambda i: (0, 0)),
        ],
        out_specs=[
            pl.BlockSpec((TOP_K, TILE), lambda i: (0, i)),
            pl.BlockSpec((TOP_K, TILE), lambda i: (0, i)),
            pl.BlockSpec((e, ntp), lambda i: (0, 0)),
        ],
        out_shape=[
            jax.ShapeDtypeStruct((TOP_K, t), I32),
            jax.ShapeDtypeStruct((TOP_K, t), F32),
            jax.ShapeDtypeStruct((e, ntp), F32),
        ],
        scratch_shapes=[pltpu.VMEM((TILE, TILE), BF16), pltpu.VMEM((e, e), BF16)],
        compiler_params=pltpu.CompilerParams(
            dimension_semantics=("arbitrary",), vmem_limit_bytes=VMEM_LIMIT),
        name="route",
    )(lgt, router_bias.reshape(e, 1))


_T_LOCAL, _T_LEN, _T_GLOBAL, _T_FIELDS = 0, 1, 2, 3
_T_STRIDE = 2 * N_EXPERTS
_T_EMPTY = _T_STRIDE * _T_FIELDS
_T_SIZE = _T_EMPTY + 2
LBUF_ROWS = LROWS + N_EXPERTS


def _segment_copies(tab_ref, half, local_ref, global_hbm, sem, to_global):
    def body(e, carry):
        at = half * N_EXPERTS + e
        n = tab_ref[0, 0, _T_LEN * _T_STRIDE + at]
        loc = local_ref.at[pl.ds(tab_ref[0, 0, _T_LOCAL * _T_STRIDE + at], n)]
        glo = global_hbm.at[pl.ds(tab_ref[0, 0, _T_GLOBAL * _T_STRIDE + at], n)]
        if to_global:
            pltpu.make_async_copy(loc, glo, sem).start()
        else:
            pltpu.make_async_copy(glo, loc, sem).start()
        return carry

    for e in range(N_EXPERTS):
        body(e, 0)


def _chunk_relative(lp_row, c):
    rel = lp_row - c * SORT_CHUNK
    inside = (rel >= 0) & (rel < SORT_CHUNK)
    return jnp.where(inside, rel, -1).astype(F32).astype(BF16)


def _chunk_row_ids():
    return lax.broadcasted_iota(I32, (SORT_CHUNK, TILE), 0).astype(F32).astype(BF16)


def _segment_wait(local_ref, global_hbm, sem, nempty):
    rows = LROWS + nempty
    pltpu.make_async_copy(global_hbm.at[pl.ds(0, rows)], local_ref.at[pl.ds(0, rows)], sem).wait()


TAIL_ROWS = EXPERT_ROWS + 2 * N_EXPERTS


def _dispatch_kernel(tab_ref, pad_ref, h2_ref, lp_ref, xs_hbm, xl0, xl1, zb, sem, pending):
    s = pl.program_id(0)
    ns = pl.num_programs(0)

    @pl.when(s == 0)
    def _zero_unassigned_rows():
        zb[...] = jnp.zeros(zb.shape, U32)
        used = pad_ref[2, 0]
        rest = xs_hbm.shape[0] - used
        cp = pltpu.make_async_copy(zb.at[pl.ds(0, rest)], xs_hbm.at[pl.ds(used, rest)], sem.at[2])
        cp.start()
        cp.wait()

        def pad_row(e):
            return pltpu.make_async_copy(zb.at[pl.ds(0, 1)], xs_hbm.at[pl.ds(pad_ref[0, e], 1)], sem.at[2])

        def start(e, carry):
            @pl.when(pad_ref[1, e] > 0)
            def _():
                pad_row(e).start()
            return carry

        def wait(e, carry):
            @pl.when(pad_ref[1, e] > 0)
            def _():
                pad_row(e).wait()
            return carry

        lax.fori_loop(0, N_EXPERTS, start, 0)
        lax.fori_loop(0, N_EXPERTS, wait, 0)

    for half, xl in ((0, xl0), (1, xl1)):
        @pl.when(s > 0)
        def _():
            _segment_wait(xl, xs_hbm, sem.at[half], pending[half])

        rows = h2_ref[half * TILE:(half + 1) * TILE, :]
        lp = lp_ref[:, half * TILE:(half + 1) * TILE]
        j = _chunk_row_ids()
        for c in range(LROWS // SORT_CHUNK):
            onehot = jnp.zeros((SORT_CHUNK, TILE), BF16)
            for k in range(TOP_K):
                onehot = jnp.where(j == _chunk_relative(lp[k:k + 1, :], c), jnp.ones((), BF16), onehot)
            _store_packed(xl, c * SORT_CHUNK, SORT_CHUNK, _pack_rows(_dot(onehot, rows)))
        _segment_copies(tab_ref, half, xl, xs_hbm, sem.at[half], to_global=True)
        pending[half] = tab_ref[0, 0, _T_EMPTY + half]

    @pl.when(s == ns - 1)
    def _():
        _segment_wait(xl0, xs_hbm, sem.at[0], pending[0])
        _segment_wait(xl1, xs_hbm, sem.at[1], pending[1])


def _dispatch_call(tab, pad, h2, lp, nrows):
    t, d = h2.shape
    ns = t // (2 * TILE)
    return pl.pallas_call(
        _dispatch_kernel,
        grid=(ns,),
        in_specs=[
            pl.BlockSpec((1, 1, _T_SIZE), lambda s: (s, 0, 0), memory_space=pltpu.SMEM),
            pl.BlockSpec(memory_space=pltpu.SMEM),
            pl.BlockSpec((2 * TILE, d), lambda s: (s, 0)),
            pl.BlockSpec((TOP_K, 2 * TILE), lambda s: (0, s)),
        ],
        out_specs=pl.BlockSpec(memory_space=pl.ANY),
        out_shape=jax.ShapeDtypeStruct((nrows, PACK_S, LANES), U32),
        scratch_shapes=[
            pltpu.VMEM((LBUF_ROWS, PACK_S, LANES), U32),
            pltpu.VMEM((LBUF_ROWS, PACK_S, LANES), U32),
            pltpu.VMEM((TAIL_ROWS + N_EXPERTS, PACK_S, LANES), U32),
            pltpu.SemaphoreType.DMA((3,)),
            pltpu.SMEM((2,), I32),
        ],
        compiler_params=pltpu.CompilerParams(
            dimension_semantics=("arbitrary",), vmem_limit_bytes=VMEM_LIMIT),
        name="dispatch",
    )(tab, pad, h2, lp)


RING = 6
AHEAD = RING - 2


def _expert_kernel(first_ref, nchunk_ref, ntail_ref, row_ref, len_ref, info_ref, wg_ref, wu_ref, wd_ref,
                   xs_hbm, y_hbm, xbuf, ybuf, isem, osem, wdb):
    e = pl.program_id(0)
    last_step = e == pl.num_programs(0) - 1
    rows = EXPERT_ROWS
    total_chunks = info_ref[0]

    def slab(ref, row0, m):
        size = m * PACK_S if isinstance(m, int) else pl.multiple_of(m * PACK_S, SUBLANES)
        first = row0 * PACK_S if isinstance(row0, int) else pl.multiple_of(row0 * PACK_S, SUBLANES)
        return ref.at[pl.ds(first, size)]

    def fetch(g):
        slot = g % RING
        return pltpu.make_async_copy(slab(xs_hbm, row_ref[g], rows), slab(xbuf, slot * rows, rows), isem.at[slot])

    def flush(g):
        slot = g % RING
        m = len_ref[g]
        return pltpu.make_async_copy(slab(ybuf, slot * rows, m), slab(y_hbm, row_ref[g], m), osem.at[slot])

    def start_fetch(g):
        @pl.when(g < total_chunks)
        def _():
            fetch(g).start()

    def wait_flush(g):
        @pl.when(g >= 0)
        def _():
            flush(g).wait()

    def compute(g, n=EXPERT_ROWS):
        slot = g % RING
        xb = _unpack_rows(_load_flat(xbuf, slot * rows, n))
        gate = _dot(xb, wg_ref[0])
        up = _dot(xb, wu_ref[0])
        hid = (gate * _sigmoid(gate)) * up
        y = _dot(hid.astype(BF16), wdb[...])
        _store_flat(ybuf, slot * rows, n, _pack_rows(y.astype(BF16).astype(F32)))

    @pl.when(e == 0)
    def _prime():
        for a in range(AHEAD):
            start_fetch(a)

    wdb[...] = wd_ref[0].astype(BF16)

    begin = first_ref[e]
    count = nchunk_ref[e]

    def pair(p, carry):
        g0 = begin + 2 * p
        g1 = g0 + 1
        start_fetch(g0 + AHEAD)
        start_fetch(g1 + AHEAD)
        fetch(g0).wait()
        fetch(g1).wait()
        wait_flush(g0 - RING)
        wait_flush(g1 - RING)
        compute(g0)
        compute(g1)
        flush(g0).start()
        flush(g1).start()
        return carry

    lax.fori_loop(0, count // 2, pair, 0)

    def single(g, n):
        start_fetch(g + AHEAD)
        fetch(g).wait()
        wait_flush(g - RING)
        compute(g, n)
        flush(g).start()

    @pl.when(count % 2 == 1)
    def _odd_chunk():
        single(begin + count - 1, EXPERT_ROWS)

    @pl.when(ntail_ref[e] > 0)
    def _short_chunk():
        single(begin + count, TAIL_CHUNK_ROWS)

    @pl.when(last_step)
    def _drain():
        def wait(g, carry):
            flush(g).wait()
            return carry
        lax.fori_loop(jnp.maximum(total_chunks - RING, 0), total_chunks, wait, 0)

    @pl.when(last_step)
    def _define_unused_rows():
        used = info_ref[1]
        rest = y_hbm.shape[0] // PACK_S - used

        @pl.when(rest > 0)
        def _():
            ybuf[0:N_EXPERTS * PACK_S, :] = jnp.zeros((N_EXPERTS * PACK_S, LANES), U32)
            cp = pltpu.make_async_copy(slab(ybuf, 0, rest), slab(y_hbm, used, rest), osem.at[0])
            cp.start()
            cp.wait()


def _expert_call(first, nchunk, ntail, chunk_row, chunk_len, info, xs, w_gate, w_up, w_down, nrows):
    ne, d, hid = w_gate.shape
    rows = EXPERT_ROWS
    wsel = lambda e, *_: (e, 0, 0)
    grid_spec = pltpu.PrefetchScalarGridSpec(
        num_scalar_prefetch=6,
        grid=(ne,),
        in_specs=[
            pl.BlockSpec((1, d, hid), wsel),
            pl.BlockSpec((1, d, hid), wsel),
            pl.BlockSpec((1, hid, d), wsel),
            pl.BlockSpec(memory_space=pl.ANY),
        ],
        out_specs=pl.BlockSpec(memory_space=pl.ANY),
        scratch_shapes=[
            pltpu.VMEM((RING * rows * PACK_S, LANES), U32),
            pltpu.VMEM((RING * rows * PACK_S, LANES), U32),
            pltpu.SemaphoreType.DMA((RING,)),
            pltpu.SemaphoreType.DMA((RING,)),
            pltpu.VMEM((hid, d), BF16),
        ],
    )
    return pl.pallas_call(
        _expert_kernel,
        grid_spec=grid_spec,
        out_shape=jax.ShapeDtypeStruct((nrows * PACK_S, LANES), U32),
        compiler_params=pltpu.CompilerParams(
            dimension_semantics=("arbitrary",), vmem_limit_bytes=VMEM_LIMIT),
        name="experts",
    )(first, nchunk, ntail, chunk_row, chunk_len, info, w_gate, w_up, w_down,
      xs.reshape(-1, LANES)).reshape(nrows, PACK_S, LANES)


def _combine_kernel(tab_ref, tabn_ref, lp_ref, gw_ref, xs1_ref, mod_ref, fg_ref, y_hbm, o_ref, yl0, yl1, sem):
    s = pl.program_id(0)
    ns = pl.num_programs(0)

    @pl.when(s == 0)
    def _():
        _segment_copies(tab_ref, 0, yl0, y_hbm, sem.at[0], to_global=False)

    g2 = mod_ref[0][5:6]
    for half, yl in ((0, yl0), (1, yl1)):
        _segment_wait(yl, y_hbm, sem.at[half], tab_ref[0, 0, _T_EMPTY + half])
        if half == 0:
            _segment_copies(tab_ref, 1, yl1, y_hbm, sem.at[1], to_global=False)
        lp = lp_ref[:, half * TILE:(half + 1) * TILE]
        gw = gw_ref[:, half * TILE:(half + 1) * TILE]
        acc = jnp.zeros((TILE, D_MODEL), F32)
        gwb = [gw[k:k + 1, :].astype(BF16) for k in range(TOP_K)]
        j = _chunk_row_ids()
        for c in range(LROWS // SORT_CHUNK):
            wt = jnp.zeros((SORT_CHUNK, TILE), BF16)
            for k in range(TOP_K):
                wt = jnp.where(j == _chunk_relative(lp[k:k + 1, :], c), gwb[k], wt)
            yv = _unpack_rows(_load_packed(yl, c * SORT_CHUNK, SORT_CHUNK))
            acc = acc + _dot_tn(wt, yv)
        if half == 0:
            _segment_copies(tabn_ref, 0, yl0, y_hbm, sem.at[0], to_global=False)
        x2 = xs1_ref[half * TILE:(half + 1) * TILE, :] + g2 * acc
        o_ref[half * TILE:(half + 1) * TILE, :] = _rms(x2) * fg_ref[...]

    @pl.when(s == ns - 1)
    def _():
        _segment_wait(yl0, y_hbm, sem.at[0], tab_ref[0, 0, _T_EMPTY])


def _combine_call(tab, lp, gw, xs1, mod3, final_g, y, seq):
    t, d = xs1.shape
    ns = t // (2 * TILE)
    per_seq = seq // (2 * TILE)
    return pl.pallas_call(
        _combine_kernel,
        grid=(ns,),
        in_specs=[
            pl.BlockSpec((1, 1, _T_SIZE), lambda s: (s, 0, 0), memory_space=pltpu.SMEM),
            pl.BlockSpec((1, 1, _T_SIZE), lambda s: (jnp.minimum(s + 1, ns - 1), 0, 0),
                         memory_space=pltpu.SMEM),
            pl.BlockSpec((TOP_K, 2 * TILE), lambda s: (0, s)),
            pl.BlockSpec((TOP_K, 2 * TILE), lambda s: (0, s)),
            pl.BlockSpec((2 * TILE, d), lambda s: (s, 0)),
            pl.BlockSpec((1, 6, d), lambda s: (s // per_seq, 0, 0)),
            pl.BlockSpec((1, d), lambda s: (0, 0)),
            pl.BlockSpec(memory_space=pl.ANY),
        ],
        out_specs=pl.BlockSpec((2 * TILE, d), lambda s: (s, 0)),
        out_shape=jax.ShapeDtypeStruct((t, d), F32),
        scratch_shapes=[
            pltpu.VMEM((LBUF_ROWS, PACK_S, LANES), U32),
            pltpu.VMEM((LBUF_ROWS, PACK_S, LANES), U32),
            pltpu.SemaphoreType.DMA((2,)),
        ],
        compiler_params=pltpu.CompilerParams(
            dimension_semantics=("arbitrary",), vmem_limit_bytes=VMEM_LIMIT),
        name="combine",
    )(tab, tab, lp, gw, xs1, mod3, final_g.reshape(1, d), y)


def kernel(x, c, w_ada, b_ada, w_in, attn_sinks, rel_bias, attn_norm_g, conv_w, conv_b, conv_ln_g,
           conv_ln_b, w_out, w_router, router_bias, w_exp_gate, w_exp_up, w_exp_down, w_sh_gate,
           w_sh_up, w_sh_down, final_norm_g):
    bsz, seq, d = x.shape
    t = bsz * seq
    assert w_ada.shape[0] == 1 and d == D_MODEL
    assert seq % MIX_ROWS == 0 and seq % (2 * TILE) == 0

    mod3 = _ada_call(c, w_ada[0], b_ada[0]).reshape(bsz, 6, d)
    xs1, h2, lgt, wg_b, wu_b = _mixer_call(
        x, mod3, w_in[0].astype(BF16), attn_sinks[0], rel_bias,
        attn_norm_g[0].reshape(1, Q_COLS), conv_w[0], conv_b[0].reshape(1, CONV_CH),
        conv_ln_g[0].reshape(1, CONV_CH), conv_ln_b[0].reshape(1, CONV_CH),
        w_out[0].astype(BF16), w_router[0].T.astype(BF16),
        w_sh_gate[0].astype(BF16), w_sh_up[0].astype(BF16), w_sh_down[0].astype(BF16),
        w_exp_gate[0], w_exp_up[0])
    xs1 = xs1.reshape(t, d)
    h2 = h2.reshape(t, d)

    lp, gw, cnt = _route_call(lgt, router_bias[0])

    ntiles = t // TILE
    nassign = t * TOP_K
    n = cnt[:, :ntiles].T.astype(I32)
    local = jnp.cumsum(n, axis=1) - n
    earlier = jnp.cumsum(n, axis=0) - n
    total = jnp.sum(n, axis=0)
    region = (total + 1) // 2 * 2
    starts = jnp.cumsum(region) - region
    nalloc = nassign + N_EXPERTS
    used = jnp.sum(region)
    pad = jnp.stack([starts + total, region - total,
                     jnp.broadcast_to(used, (N_EXPERTS,))], axis=0)
    glob = starts[None, :] + earlier
    empty = n == 0
    length = jnp.maximum(n, 1)
    eid = jnp.arange(N_EXPERTS, dtype=I32)[None, :]
    parity = (jnp.arange(ntiles, dtype=I32) % 2)[:, None]
    spare_global = nalloc + EXPERT_ROWS + parity * N_EXPERTS + eid
    nempty = jnp.sum(empty.astype(I32), axis=1).reshape(ntiles // 2, 2)

    def table(loc, glo):
        fields = [f.reshape(ntiles // 2, _T_STRIDE) for f in (loc, length, glo)]
        return jnp.concatenate(fields + [nempty], axis=1).reshape(ntiles // 2, 1, _T_SIZE)

    tab_out = table(jnp.where(empty, 0, local), jnp.where(empty, spare_global, glob))
    tab_back = table(jnp.where(empty, LROWS + eid, local), jnp.where(empty, 0, glob))

    rows = EXPERT_ROWS
    left_over = total % (2 * rows)
    ntail = (((left_over > 0) & (left_over <= TAIL_CHUNK_ROWS))
             | ((left_over > rows) & (left_over <= rows + TAIL_CHUNK_ROWS))).astype(I32)
    nwhole = (2 * (total // (2 * rows)) + (left_over > TAIL_CHUNK_ROWS).astype(I32)
              + (left_over > rows + TAIL_CHUNK_ROWS).astype(I32))
    nchunk = nwhole + ntail
    chunk_end = jnp.cumsum(nchunk)
    first = chunk_end - nchunk
    max_chunks = nassign // rows + 2 * N_EXPERTS
    gid = jnp.arange(max_chunks, dtype=I32)
    owner = jnp.minimum(jnp.sum((chunk_end[None, :] <= gid[:, None]).astype(I32), axis=1), N_EXPERTS - 1)
    pick = owner[:, None] == eid
    within = gid - jnp.sum(jnp.where(pick, first[None, :], 0), axis=1)
    chunk_row = jnp.sum(jnp.where(pick, starts[None, :], 0), axis=1) + within * rows
    left = jnp.sum(jnp.where(pick, region[None, :], 0), axis=1) - within * rows
    live = gid < chunk_end[-1]
    chunk_row = jnp.where(live, chunk_row, 0)
    chunk_len = jnp.where(live, jnp.clip(left, 2, rows), 2)
    info = jnp.stack([chunk_end[-1], used])

    xs = _dispatch_call(tab_out, pad, h2, lp, nalloc + TAIL_ROWS)
    y = _expert_call(first, nwhole, ntail, chunk_row, chunk_len, info, xs, wg_b, wu_b, w_exp_down[0], nalloc)
    out = _combine_call(tab_back, lp, gw, xs1, mod3, final_norm_g, y, seq)
    return out.reshape(bsz, seq, d)
```

```python
import numpy as np
import jax
import jax.numpy as jnp
from jax import lax
from jax.experimental import pallas as pl
from jax.experimental.pallas import tpu as pltpu

F32 = jnp.float32
BF16 = jnp.bfloat16
U32 = jnp.uint32
I32 = jnp.int32

D_MODEL = 1024
HEAD_DIM = 64
N_HEADS = 8
N_KV_HEADS = 2
GROUP = N_HEADS // N_KV_HEADS
Q_COLS = N_HEADS * HEAD_DIM
KV_COLS = N_KV_HEADS * HEAD_DIM
ATTN_BLOCK = 128
WINDOW = 128
NUM_BUCKETS = 32
MAX_DISTANCE = 128
CONV_CH = D_MODEL - Q_COLS
CONV_WIDTH = 31
IN_COLS = Q_COLS + 2 * KV_COLS + 2 * CONV_CH
N_EXPERTS = 256
TOP_K = 8
N_GROUPS = 8
GROUP_SIZE = N_EXPERTS // N_GROUPS
TOPK_GROUPS = 4
EXPERT_HIDDEN = 256
ROUTED_SCALE = 2.5
EPS = 1e-6

MIX_ROWS = 512
HALO = 32
TILE = 256
LROWS = TILE * TOP_K
SORT_CHUNK = 256
EXPERT_ROWS = 576
TAIL_CHUNK_ROWS = 256
LANES = 128
SUBLANES = 8
PACK_W = D_MODEL // 2
PACK_S = PACK_W // LANES
VMEM_LIMIT = 56 * 1024 * 1024


def _sigmoid(v):
    return 1.0 / (1.0 + jnp.exp(-v))


def _rms(v):
    return v * lax.rsqrt(jnp.mean(v * v, axis=-1, keepdims=True) + EPS)


def _split_bf16(a):
    hi = a.astype(BF16)
    lo = (a - hi.astype(F32)).astype(BF16)
    return hi, lo


def _dot(a, b):
    return jnp.dot(a, b, preferred_element_type=F32)


def _dot_nt(a, b):
    return lax.dot_general(a, b, (((1,), (1,)), ((), ())), preferred_element_type=F32)


def _dot_tn(a, b):
    return lax.dot_general(a, b, (((0,), (0,)), ((), ())), preferred_element_type=F32)


def _dot3(a, b, dot):
    ah, al = _split_bf16(a)
    bh, bl = _split_bf16(b)
    return dot(ah, bh) + (dot(ah, bl) + dot(al, bh))


def _pack_rows(v):
    hi = lax.bitcast_convert_type(v[:, :PACK_W], U32) & jnp.uint32(0xFFFF0000)
    lo = lax.bitcast_convert_type(v[:, PACK_W:], U32) >> 16
    return hi | lo


def _unpack_rows(u):
    hi = lax.bitcast_convert_type(u & jnp.uint32(0xFFFF0000), F32)
    lo = lax.bitcast_convert_type(u << 16, F32)
    return jnp.concatenate([hi, lo], axis=1).astype(BF16)


def _load_flat(flat, r0, n):
    return jnp.concatenate(
        [flat[pl.ds(r0 * PACK_S + c, n, stride=PACK_S), :] for c in range(PACK_S)], axis=1)


def _store_flat(flat, r0, n, u):
    for c in range(PACK_S):
        flat[pl.ds(r0 * PACK_S + c, n, stride=PACK_S), :] = u[:, c * LANES:(c + 1) * LANES]


def _load_packed(ref3, r0, n):
    return _load_flat(ref3.reshape(ref3.shape[0] * PACK_S, LANES), r0, n)


def _store_packed(ref3, r0, n, u):
    _store_flat(ref3.reshape(ref3.shape[0] * PACK_S, LANES), r0, n, u)


def _ada_kernel(c_ref, w_ref, b_ref, o_ref):
    c = c_ref[...]
    s = c * _sigmoid(c)
    o_ref[...] = _dot3(s, w_ref[...], _dot) + b_ref[...]


def _ada_call(c, w_ada, b_ada):
    bsz, d = c.shape
    n = w_ada.shape[1]
    tn = 1536
    return pl.pallas_call(
        _ada_kernel,
        grid=(n // tn,),
        in_specs=[
            pl.BlockSpec((bsz, d), lambda i: (0, 0)),
            pl.BlockSpec((d, tn), lambda i: (0, i)),
            pl.BlockSpec((1, tn), lambda i: (0, i)),
        ],
        out_specs=pl.BlockSpec((bsz, tn), lambda i: (0, i)),
        out_shape=jax.ShapeDtypeStruct((bsz, n), F32),
        compiler_params=pltpu.CompilerParams(
            dimension_semantics=("arbitrary",), vmem_limit_bytes=VMEM_LIMIT),
        name="ada",
    )(c, w_ada, b_ada.reshape(1, n))


def _bucket_table():
    qi = np.arange(ATTN_BLOCK)[:, None]
    ki = np.arange(2 * ATTN_BLOCK)[None, :]
    dist = qi + ATTN_BLOCK - ki
    n = np.maximum(dist, 0)
    max_exact = NUM_BUCKETS // 2
    large = max_exact + (np.log(np.maximum(n, 1) / max_exact) / np.log(MAX_DISTANCE / max_exact)
                         * (NUM_BUCKETS - max_exact)).astype(np.int32)
    large = np.minimum(large, NUM_BUCKETS - 1)
    bkt = np.where(n < max_exact, n, large).astype(np.int32)
    band = (dist >= 0) & (dist < WINDOW)
    return np.ascontiguousarray(np.where(band, bkt, -1).astype(np.int32).T)


def _mixer_kernel(x_ref, mod_ref, win_ref, sinks_ref, relb_ref, bkt_ref, ang_ref, cw_ref, cb_ref,
                  lng_ref, lnb_ref, wout_ref, wrt_ref, wsg_ref, wsu_ref, wsd_ref, ewg_ref, ewu_ref,
                  xs1_ref, h2_ref, lgt_ref, ewgb_ref, ewub_ref,
                  kv_ref, uext_ref, bias_ref):
    tm = x_ref.shape[1]
    nblk = tm // ATTN_BLOCK
    j = pl.program_id(1)

    @pl.when((pl.program_id(0) == 0) & (j == 0))
    def _build_bias():
        bkt = bkt_ref[...]
        for h in range(N_HEADS):
            acc = jnp.full(bkt.shape, -jnp.inf, F32)
            for b in range(NUM_BUCKETS):
                acc = jnp.where(bkt == b, relb_ref[b, h], acc)
            g = h % GROUP
            bias_ref[h // GROUP, :, g * ATTN_BLOCK:(g + 1) * ATTN_BLOCK] = acc

    @pl.when(j == 0)
    def _reset_history():
        kv_ref[...] = jnp.zeros(kv_ref.shape, F32)
        uext_ref[0:HALO, :] = jnp.zeros((HALO, CONV_CH), F32)
        uext_ref[HALO + tm:HALO + tm + SUBLANES, :] = jnp.zeros((SUBLANES, CONV_CH), F32)

    x = x_ref[0]
    mod = mod_ref[0]
    sh1, sc1, g1 = mod[0:1], mod[1:2], mod[2:3]
    sh2, sc2, g2 = mod[3:4], mod[4:5], mod[5:6]

    h = _rms(x) * (1.0 + sc1) + sh1
    hb = h.astype(BF16)
    n_qkv = Q_COLS + 2 * KV_COLS
    proj_conv = _dot(hb, win_ref[:, n_qkv:])
    proj = _dot(hb, win_ref[:, :n_qkv])

    qb = (proj[:, :Q_COLS] * (HEAD_DIM ** -0.5)).astype(BF16)
    kv_cur = proj[:, Q_COLS:Q_COLS + 2 * KV_COLS]
    kvb = jnp.concatenate([kv_ref[...], kv_cur], axis=0).astype(BF16)
    kv_ref[...] = kv_cur[tm - ATTN_BLOCK:, :]
    not_first = j > 0
    key = lax.broadcasted_iota(jnp.int32, (2 * ATTN_BLOCK, GROUP * ATTN_BLOCK), 0)
    attn_rows = []
    for i in range(nblk):
        r0, r1, r2 = i * ATTN_BLOCK, (i + 1) * ATTN_BLOCK, (i + 2) * ATTN_BLOCK
        heads_t = [None] * N_HEADS
        for kh in range(N_KV_HEADS):
            kpc = kvb[r0:r2, kh * HEAD_DIM:(kh + 1) * HEAD_DIM]
            vpc = kvb[r0:r2, KV_COLS + kh * HEAD_DIM:KV_COLS + (kh + 1) * HEAD_DIM]
            qg = jnp.concatenate(
                [qb[r0:r1, (kh * GROUP + g) * HEAD_DIM:(kh * GROUP + g + 1) * HEAD_DIM]
                 for g in range(GROUP)], axis=0)
            logit = _dot_nt(kpc, qg) + bias_ref[kh]
            if i == 0:
                logit = jnp.where((key >= ATTN_BLOCK) | not_first, logit, -jnp.inf)
            sink = jnp.concatenate(
                [jnp.full((1, ATTN_BLOCK), sinks_ref[kh * GROUP + g], F32) for g in range(GROUP)], axis=1)
            m = jnp.maximum(jnp.max(logit, axis=0, keepdims=True), sink)
            p = jnp.exp(logit - m)
            den = jnp.sum(p, axis=0, keepdims=True) + jnp.exp(sink - m)
            o_t = _dot_tn(vpc, p.astype(BF16)) / den
            for g in range(GROUP):
                heads_t[kh * GROUP + g] = o_t[:, g * ATTN_BLOCK:(g + 1) * ATTN_BLOCK]
        attn_t = jnp.concatenate(heads_t, axis=0)
        scale = lax.rsqrt(jnp.mean(attn_t * attn_t, axis=0, keepdims=True) + EPS)
        attn_rows.append((attn_t * scale).T)
    attn = jnp.concatenate(attn_rows, axis=0) * ang_ref[...]

    a = proj_conv[:, :CONV_CH]
    gt = proj_conv[:, CONV_CH:]
    uext_ref[HALO:HALO + tm, :] = a * _sigmoid(gt)
    cw = cw_ref[...]
    base = HALO - (CONV_WIDTH - 1)
    acc = None
    for res in range(SUBLANES):
        part = None
        for hi in range((HALO + SUBLANES - 1) // SUBLANES + 1):
            t = hi * SUBLANES + res - base
            if 0 <= t < CONV_WIDTH:
                term = cw[t:t + 1, :] * uext_ref[hi * SUBLANES:hi * SUBLANES + tm + SUBLANES, :]
                part = term if part is None else part + term
        part = part[res:res + tm, :]
        acc = part if acc is None else acc + part
    uext_ref[0:HALO, :] = uext_ref[tm:tm + HALO, :]
    cv = acc + cb_ref[...]
    mu = jnp.mean(cv, axis=-1, keepdims=True)
    var = jnp.mean(jnp.square(cv - mu), axis=-1, keepdims=True)
    cv = (cv - mu) * lax.rsqrt(var + EPS) * lng_ref[...] + lnb_ref[...]
    cv = cv * _sigmoid(cv)

    mixed = (_dot(attn.astype(BF16), wout_ref[0:Q_COLS, :])
             + _dot(cv.astype(BF16), wout_ref[Q_COLS:, :]))
    x1 = x + g1 * mixed

    h2 = _rms(x1) * (1.0 + sc2) + sh2
    h2b = h2.astype(BF16)
    h2_ref[0] = h2b
    lgt_ref[...] = _dot_nt(wrt_ref[...], h2b)
    sg = _dot(h2b, wsg_ref[...])
    su = _dot(h2b, wsu_ref[...])
    shared = _dot(((sg * _sigmoid(sg)) * su).astype(BF16), wsd_ref[...])
    xs1_ref[0] = x1 + g2 * shared

    ewgb_ref[...] = ewg_ref[...].astype(BF16)
    ewub_ref[...] = ewu_ref[...].astype(BF16)


def _mixer_call(x, mod3, w_in_b, sinks, rel_bias, ang, cw, cb, lng, lnb, w_out_b, w_rt, wsg_b, wsu_b, wsd_b,
                w_gate, w_up):
    bsz, seq, d = x.shape
    tm = MIX_ROWS
    nj = seq // tm
    ne, _, hid = w_gate.shape
    per_step = ne // (bsz * nj)
    assert per_step * bsz * nj == ne
    bkt = jnp.asarray(_bucket_table())
    full = lambda shape: pl.BlockSpec(shape, lambda b, j: (0,) * len(shape), pipeline_mode=pl.Buffered(1))
    share = lambda shape: pl.BlockSpec((per_step,) + shape, lambda b, j: (b * nj + j, 0, 0))
    smem = pl.BlockSpec(memory_space=pltpu.SMEM)
    return pl.pallas_call(
        _mixer_kernel,
        grid=(bsz, nj),
        in_specs=[
            pl.BlockSpec((1, tm, d), lambda b, j: (b, j, 0)),
            pl.BlockSpec((1, 6, d), lambda b, j: (b, 0, 0)),
            full((d, IN_COLS)),
            smem, smem,
            full((2 * ATTN_BLOCK, ATTN_BLOCK)),
            full((1, Q_COLS)),
            full((CONV_WIDTH, CONV_CH)),
            full((1, CONV_CH)), full((1, CONV_CH)), full((1, CONV_CH)),
            full((d, d)),
            full((N_EXPERTS, d)),
            full((d, EXPERT_HIDDEN)), full((d, EXPERT_HIDDEN)), full((EXPERT_HIDDEN, d)),
            share((d, hid)), share((d, hid)),
        ],
        out_specs=[
            pl.BlockSpec((1, tm, d), lambda b, j: (b, j, 0)),
            pl.BlockSpec((1, tm, d), lambda b, j: (b, j, 0)),
            pl.BlockSpec((N_EXPERTS, tm), lambda b, j: (0, b * nj + j)),
            share((d, hid)), share((d, hid)),
        ],
        out_shape=[
            jax.ShapeDtypeStruct((bsz, seq, d), F32),
            jax.ShapeDtypeStruct((bsz, seq, d), BF16),
            jax.ShapeDtypeStruct((N_EXPERTS, bsz * seq), F32),
            jax.ShapeDtypeStruct((ne, d, hid), BF16),
            jax.ShapeDtypeStruct((ne, d, hid), BF16),
        ],
        scratch_shapes=[
            pltpu.VMEM((ATTN_BLOCK, 2 * KV_COLS), F32),
            pltpu.VMEM((HALO + tm + SUBLANES, CONV_CH), F32),
            pltpu.VMEM((N_KV_HEADS, 2 * ATTN_BLOCK, GROUP * ATTN_BLOCK), F32),
        ],
        compiler_params=pltpu.CompilerParams(
            dimension_semantics=("arbitrary", "arbitrary"), vmem_limit_bytes=VMEM_LIMIT),
        name="mixer",
    )(x, mod3, w_in_b, sinks, rel_bias, bkt, ang, cw, cb, lng, lnb, w_out_b, w_rt, wsg_b, wsu_b, wsd_b,
      w_gate, w_up)


def _route_kernel(lgt_ref, rb_ref, lp_ref, gw_ref, cnt_ref, earlier_ref, lower_ref):
    tr = lgt_ref.shape[1]
    i = pl.program_id(0)

    @pl.when(i == 0)
    def _():
        cnt_ref[...] = jnp.zeros(cnt_ref.shape, F32)
        earlier_ref[...] = jnp.where(
            lax.broadcasted_iota(I32, (tr, tr), 0) < lax.broadcasted_iota(I32, (tr, tr), 1), 1.0, 0.0).astype(BF16)
        lower_ref[...] = jnp.where(
            lax.broadcasted_iota(I32, (N_EXPERTS, N_EXPERTS), 1)
            < lax.broadcasted_iota(I32, (N_EXPERTS, N_EXPERTS), 0), 1.0, 0.0).astype(BF16)

    scores = _sigmoid(lgt_ref[...])
    sel = scores + rb_ref[...]
    neg = -jnp.inf

    sel3 = sel.reshape(N_GROUPS, GROUP_SIZE, tr)
    loc = lax.broadcasted_iota(I32, sel3.shape, 1)
    m1 = jnp.max(sel3, axis=1, keepdims=True)
    i1 = jnp.min(jnp.where(sel3 == m1, loc, GROUP_SIZE), axis=1, keepdims=True)
    m2 = jnp.max(jnp.where(loc == i1, neg, sel3), axis=1, keepdims=True)
    gscore = (m1 + m2).reshape(N_GROUPS, tr)

    gio = lax.broadcasted_iota(I32, gscore.shape, 0)
    gmask = jnp.zeros(gscore.shape, jnp.bool_)
    cur = gscore
    for _ in range(TOPK_GROUPS):
        m = jnp.max(cur, axis=0, keepdims=True)
        idx = jnp.min(jnp.where(cur == m, gio, N_GROUPS), axis=0, keepdims=True)
        pick = gio == idx
        gmask = gmask | pick
        cur = jnp.where(pick, neg, cur)
    emask = jnp.broadcast_to(gmask.reshape(N_GROUPS, 1, tr), sel3.shape).reshape(N_EXPERTS, tr)

    rio = lax.broadcasted_iota(I32, sel.shape, 0)
    cur = jnp.where(emask, sel, neg)
    picks, gsc = [], []
    for _ in range(TOP_K):
        m = jnp.max(cur, axis=0, keepdims=True)
        idx = jnp.min(jnp.where(cur == m, rio, N_EXPERTS), axis=0, keepdims=True)
        pick = rio == idx
        picks.append(pick)
        gsc.append(jnp.sum(jnp.where(pick, scores, 0.0), axis=0, keepdims=True))
        cur = jnp.where(pick, neg, cur)
    gsum = gsc[0]
    for k in range(1, TOP_K):
        gsum = gsum + gsc[k]
    gw_ref[...] = jnp.concatenate([g / gsum * ROUTED_SCALE for g in gsc], axis=0)

    onehot = jnp.where(cur == neg, jnp.where(emask, 1.0, 0.0), 0.0)
    before = _dot(onehot.astype(BF16), earlier_ref[...])
    n = jnp.sum(onehot, axis=1, keepdims=True)
    start = _dot(lower_ref[...],
                 jnp.broadcast_to(n, (N_EXPERTS, LANES)).astype(BF16))[:, 0:1]
    pos = start + before
    lp_ref[...] = jnp.concatenate(
        [jnp.sum(jnp.where(p, pos, 0.0), axis=0, keepdims=True) for p in picks], axis=0).astype(I32)

    lane = lax.broadcasted_iota(I32, cnt_ref.shape, 1)
    cnt_ref[...] += jnp.where(lane == i, n, 0.0)


def _route_call(lgt, router_bias):
    e, t = lgt.shape
    ntiles = t // TILE
    ntp = (ntiles + LANES - 1) // LANES * LANES
    return pl.pallas_call(
        _route_kernel,
        grid=(ntiles,),
        in_specs=[
            pl.BlockSpec((e, TILE), lambda i: (0, i)),
            pl.BlockSpec((e, 1), lambda i: (0, 0)),
        ],
        out_specs=[
            pl.BlockSpec((TOP_K, TILE), lambda i: (0, i)),
            pl.BlockSpec((TOP_K, TILE), lambda i: (0, i)),
            pl.BlockSpec((e, ntp), lambda i: (0, 0)),
        ],
        out_shape=[
            jax.ShapeDtypeStruct((TOP_K, t), I32),
            jax.ShapeDtypeStruct((TOP_K, t), F32),
            jax.ShapeDtypeStruct((e, ntp), F32),
        ],
        scratch_shapes=[pltpu.VMEM((TILE, TILE), BF16), pltpu.VMEM((e, e), BF16)],
        compiler_params=pltpu.CompilerParams(
            dimension_semantics=("arbitrary",), vmem_limit_bytes=VMEM_LIMIT),
        name="route",
    )(lgt, router_bias.reshape(e, 1))


_T_LOCAL, _T_LEN, _T_GLOBAL, _T_FIELDS = 0, 1, 2, 3
_T_STRIDE = 2 * N_EXPERTS
_T_EMPTY = _T_STRIDE * _T_FIELDS
_T_SIZE = _T_EMPTY + 2
LBUF_ROWS = LROWS + N_EXPERTS


def _segment_copies(tab_ref, half, local_ref, global_hbm, sem, to_global):
    def body(e, carry):
        at = half * N_EXPERTS + e
        n = tab_ref[0, 0, _T_LEN * _T_STRIDE + at]
        loc = local_ref.at[pl.ds(tab_ref[0, 0, _T_LOCAL * _T_STRIDE + at], n)]
        glo = global_hbm.at[pl.ds(tab_ref[0, 0, _T_GLOBAL * _T_STRIDE + at], n)]
        if to_global:
            pltpu.make_async_copy(loc, glo, sem).start(priority=e % 2)
        else:
            pltpu.make_async_copy(glo, loc, sem).start(priority=e % 2)
        return carry

    for e in range(N_EXPERTS):
        body(e, 0)


def _chunk_relative(lp_row, c):
    rel = lp_row - c * SORT_CHUNK
    inside = (rel >= 0) & (rel < SORT_CHUNK)
    return jnp.where(inside, rel, -1).astype(F32).astype(BF16)


def _chunk_row_ids():
    return lax.broadcasted_iota(I32, (SORT_CHUNK, TILE), 0).astype(F32).astype(BF16)


def _segment_wait(local_ref, global_hbm, sem, nempty):
    rows = LROWS + nempty
    pltpu.make_async_copy(global_hbm.at[pl.ds(0, rows)], local_ref.at[pl.ds(0, rows)], sem).wait()


TAIL_ROWS = EXPERT_ROWS + 2 * N_EXPERTS


def _dispatch_kernel(tab_ref, pad_ref, h2_ref, lp_ref, xs_hbm, xl0, xl1, zb, sem, pending):
    s = pl.program_id(0)
    ns = pl.num_programs(0)

    @pl.when(s == 0)
    def _zero_unassigned_rows():
        zb[...] = jnp.zeros(zb.shape, U32)
        used = pad_ref[2, 0]
        rest = xs_hbm.shape[0] - used
        cp = pltpu.make_async_copy(zb.at[pl.ds(0, rest)], xs_hbm.at[pl.ds(used, rest)], sem.at[2])
        cp.start()
        cp.wait()

        def pad_row(e):
            return pltpu.make_async_copy(zb.at[pl.ds(0, 1)], xs_hbm.at[pl.ds(pad_ref[0, e], 1)], sem.at[2])

        def start(e, carry):
            @pl.when(pad_ref[1, e] > 0)
            def _():
                pad_row(e).start()
            return carry

        def wait(e, carry):
            @pl.when(pad_ref[1, e] > 0)
            def _():
                pad_row(e).wait()
            return carry

        lax.fori_loop(0, N_EXPERTS, start, 0)
        lax.fori_loop(0, N_EXPERTS, wait, 0)

    for half, xl in ((0, xl0), (1, xl1)):
        @pl.when(s > 0)
        def _():
            _segment_wait(xl, xs_hbm, sem.at[half], pending[half])

        rows = h2_ref[half * TILE:(half + 1) * TILE, :]
        lp = lp_ref[:, half * TILE:(half + 1) * TILE]
        j = _chunk_row_ids()
        for c in range(LROWS // SORT_CHUNK):
            onehot = jnp.zeros((SORT_CHUNK, TILE), BF16)
            for k in range(TOP_K):
                onehot = jnp.where(j == _chunk_relative(lp[k:k + 1, :], c), jnp.ones((), BF16), onehot)
            _store_packed(xl, c * SORT_CHUNK, SORT_CHUNK, _pack_rows(_dot(onehot, rows)))
        _segment_copies(tab_ref, half, xl, xs_hbm, sem.at[half], to_global=True)
        pending[half] = tab_ref[0, 0, _T_EMPTY + half]

    @pl.when(s == ns - 1)
    def _():
        _segment_wait(xl0, xs_hbm, sem.at[0], pending[0])
        _segment_wait(xl1, xs_hbm, sem.at[1], pending[1])


def _dispatch_call(tab, pad, h2, lp, nrows):
    t, d = h2.shape
    ns = t // (2 * TILE)
    return pl.pallas_call(
        _dispatch_kernel,
        grid=(ns,),
        in_specs=[
            pl.BlockSpec((1, 1, _T_SIZE), lambda s: (s, 0, 0), memory_space=pltpu.SMEM),
            pl.BlockSpec(memory_space=pltpu.SMEM),
            pl.BlockSpec((2 * TILE, d), lambda s: (s, 0)),
            pl.BlockSpec((TOP_K, 2 * TILE), lambda s: (0, s)),
        ],
        out_specs=pl.BlockSpec(memory_space=pl.ANY),
        out_shape=jax.ShapeDtypeStruct((nrows, PACK_S, LANES), U32),
        scratch_shapes=[
            pltpu.VMEM((LBUF_ROWS, PACK_S, LANES), U32),
            pltpu.VMEM((LBUF_ROWS, PACK_S, LANES), U32),
            pltpu.VMEM((TAIL_ROWS + N_EXPERTS, PACK_S, LANES), U32),
            pltpu.SemaphoreType.DMA((3,)),
            pltpu.SMEM((2,), I32),
        ],
        compiler_params=pltpu.CompilerParams(
            dimension_semantics=("arbitrary",), vmem_limit_bytes=VMEM_LIMIT),
        name="dispatch",
    )(tab, pad, h2, lp)


RING = 6
AHEAD = RING - 2


def _expert_kernel(first_ref, nchunk_ref, ntail_ref, row_ref, len_ref, info_ref, wg_ref, wu_ref, wd_ref,
                   xs_hbm, y_hbm, xbuf, ybuf, isem, osem, wdb):
    e = pl.program_id(0)
    last_step = e == pl.num_programs(0) - 1
    rows = EXPERT_ROWS
    total_chunks = info_ref[0]

    def slab(ref, row0, m):
        size = m * PACK_S if isinstance(m, int) else pl.multiple_of(m * PACK_S, SUBLANES)
        first = row0 * PACK_S if isinstance(row0, int) else pl.multiple_of(row0 * PACK_S, SUBLANES)
        return ref.at[pl.ds(first, size)]

    def fetch(g):
        slot = g % RING
        return pltpu.make_async_copy(slab(xs_hbm, row_ref[g], rows), slab(xbuf, slot * rows, rows), isem.at[slot])

    def flush(g):
        slot = g % RING
        m = len_ref[g]
        return pltpu.make_async_copy(slab(ybuf, slot * rows, m), slab(y_hbm, row_ref[g], m), osem.at[slot])

    def start_fetch(g):
        @pl.when(g < total_chunks)
        def _():
            fetch(g).start()

    def wait_flush(g):
        @pl.when(g >= 0)
        def _():
            flush(g).wait()

    def compute(g, n=EXPERT_ROWS):
        slot = g % RING
        xb = _unpack_rows(_load_flat(xbuf, slot * rows, n))
        gate = _dot(xb, wg_ref[0])
        up = _dot(xb, wu_ref[0])
        hid = (gate * _sigmoid(gate)) * up
        y = _dot(hid.astype(BF16), wdb[...])
        _store_flat(ybuf, slot * rows, n, _pack_rows(y.astype(BF16).astype(F32)))

    @pl.when(e == 0)
    def _prime():
        for a in range(AHEAD):
            start_fetch(a)

    wdb[...] = wd_ref[0].astype(BF16)

    begin = first_ref[e]
    count = nchunk_ref[e]

    def pair(p, carry):
        g0 = begin + 2 * p
        g1 = g0 + 1
        start_fetch(g0 + AHEAD)
        start_fetch(g1 + AHEAD)
        fetch(g0).wait()
        fetch(g1).wait()
        wait_flush(g0 - RING)
        wait_flush(g1 - RING)
        compute(g0)
        compute(g1)
        flush(g0).start()
        flush(g1).start()
        return carry

    lax.fori_loop(0, count // 2, pair, 0)

    def single(g, n):
        start_fetch(g + AHEAD)
        fetch(g).wait()
        wait_flush(g - RING)
        compute(g, n)
        flush(g).start()

    @pl.when(count % 2 == 1)
    def _odd_chunk():
        single(begin + count - 1, EXPERT_ROWS)

    @pl.when(ntail_ref[e] > 0)
    def _short_chunk():
        single(begin + count, TAIL_CHUNK_ROWS)

    @pl.when(last_step)
    def _drain():
        def wait(g, carry):
            flush(g).wait()
            return carry
        lax.fori_loop(jnp.maximum(total_chunks - RING, 0), total_chunks, wait, 0)

    @pl.when(last_step)
    def _define_unused_rows():
        used = info_ref[1]
        rest = y_hbm.shape[0] // PACK_S - used

        @pl.when(rest > 0)
        def _():
            ybuf[0:N_EXPERTS * PACK_S, :] = jnp.zeros((N_EXPERTS * PACK_S, LANES), U32)
            cp = pltpu.make_async_copy(slab(ybuf, 0, rest), slab(y_hbm, used, rest), osem.at[0])
            cp.start()
            cp.wait()


def _expert_call(first, nchunk, ntail, chunk_row, chunk_len, info, xs, w_gate, w_up, w_down, nrows):
    ne, d, hid = w_gate.shape
    rows = EXPERT_ROWS
    wsel = lambda e, *_: (e, 0, 0)
    grid_spec = pltpu.PrefetchScalarGridSpec(
        num_scalar_prefetch=6,
        grid=(ne,),
        in_specs=[
            pl.BlockSpec((1, d, hid), wsel),
            pl.BlockSpec((1, d, hid), wsel),
            pl.BlockSpec((1, hid, d), wsel),
            pl.BlockSpec(memory_space=pl.ANY),
        ],
        out_specs=pl.BlockSpec(memory_space=pl.ANY),
        scratch_shapes=[
            pltpu.VMEM((RING * rows * PACK_S, LANES), U32),
            pltpu.VMEM((RING * rows * PACK_S, LANES), U32),
            pltpu.SemaphoreType.DMA((RING,)),
            pltpu.SemaphoreType.DMA((RING,)),
            pltpu.VMEM((hid, d), BF16),
        ],
    )
    return pl.pallas_call(
        _expert_kernel,
        grid_spec=grid_spec,
        out_shape=jax.ShapeDtypeStruct((nrows * PACK_S, LANES), U32),
        compiler_params=pltpu.CompilerParams(
            dimension_semantics=("arbitrary",), vmem_limit_bytes=VMEM_LIMIT),
        name="experts",
    )(first, nchunk, ntail, chunk_row, chunk_len, info, w_gate, w_up, w_down,
      xs.reshape(-1, LANES)).reshape(nrows, PACK_S, LANES)


def _combine_kernel(tab_ref, tabn_ref, lp_ref, gw_ref, xs1_ref, mod_ref, fg_ref, y_hbm, o_ref, yl0, yl1, sem):
    s = pl.program_id(0)
    ns = pl.num_programs(0)

    @pl.when(s == 0)
    def _():
        _segment_copies(tab_ref, 0, yl0, y_hbm, sem.at[0], to_global=False)

    g2 = mod_ref[0][5:6]
    for half, yl in ((0, yl0), (1, yl1)):
        _segment_wait(yl, y_hbm, sem.at[half], tab_ref[0, 0, _T_EMPTY + half])
        if half == 0:
            _segment_copies(tab_ref, 1, yl1, y_hbm, sem.at[1], to_global=False)
        lp = lp_ref[:, half * TILE:(half + 1) * TILE]
        gw = gw_ref[:, half * TILE:(half + 1) * TILE]
        acc = jnp.zeros((TILE, D_MODEL), F32)
        gwb = [gw[k:k + 1, :].astype(BF16) for k in range(TOP_K)]
        j = _chunk_row_ids()
        for c in range(LROWS // SORT_CHUNK):
            wt = jnp.zeros((SORT_CHUNK, TILE), BF16)
            for k in range(TOP_K):
                wt = jnp.where(j == _chunk_relative(lp[k:k + 1, :], c), gwb[k], wt)
            yv = _unpack_rows(_load_packed(yl, c * SORT_CHUNK, SORT_CHUNK))
            acc = acc + _dot_tn(wt, yv)
        if half == 0:
            _segment_copies(tabn_ref, 0, yl0, y_hbm, sem.at[0], to_global=False)
        x2 = xs1_ref[half * TILE:(half + 1) * TILE, :] + g2 * acc
        o_ref[half * TILE:(half + 1) * TILE, :] = _rms(x2) * fg_ref[...]

    @pl.when(s == ns - 1)
    def _():
        _segment_wait(yl0, y_hbm, sem.at[0], tab_ref[0, 0, _T_EMPTY])


def _combine_call(tab, lp, gw, xs1, mod3, final_g, y, seq):
    t, d = xs1.shape
    ns = t // (2 * TILE)
    per_seq = seq // (2 * TILE)
    return pl.pallas_call(
        _combine_kernel,
        grid=(ns,),
        in_specs=[
            pl.BlockSpec((1, 1, _T_SIZE), lambda s: (s, 0, 0), memory_space=pltpu.SMEM),
            pl.BlockSpec((1, 1, _T_SIZE), lambda s: (jnp.minimum(s + 1, ns - 1), 0, 0),
                         memory_space=pltpu.SMEM),
            pl.BlockSpec((TOP_K, 2 * TILE), lambda s: (0, s)),
            pl.BlockSpec((TOP_K, 2 * TILE), lambda s: (0, s)),
            pl.BlockSpec((2 * TILE, d), lambda s: (s, 0)),
            pl.BlockSpec((1, 6, d), lambda s: (s // per_seq, 0, 0)),
            pl.BlockSpec((1, d), lambda s: (0, 0)),
            pl.BlockSpec(memory_space=pl.ANY),
        ],
        out_specs=pl.BlockSpec((2 * TILE, d), lambda s: (s, 0)),
        out_shape=jax.ShapeDtypeStruct((t, d), F32),
        scratch_shapes=[
            pltpu.VMEM((LBUF_ROWS, PACK_S, LANES), U32),
            pltpu.VMEM((LBUF_ROWS, PACK_S, LANES), U32),
            pltpu.SemaphoreType.DMA((2,)),
        ],
        compiler_params=pltpu.CompilerParams(
            dimension_semantics=("arbitrary",), vmem_limit_bytes=VMEM_LIMIT),
        name="combine",
    )(tab, tab, lp, gw, xs1, mod3, final_g.reshape(1, d), y)


def kernel(x, c, w_ada, b_ada, w_in, attn_sinks, rel_bias, attn_norm_g, conv_w, conv_b, conv_ln_g,
           conv_ln_b, w_out, w_router, router_bias, w_exp_gate, w_exp_up, w_exp_down, w_sh_gate,
           w_sh_up, w_sh_down, final_norm_g):
    bsz, seq, d = x.shape
    t = bsz * seq
    assert w_ada.shape[0] == 1 and d == D_MODEL
    assert seq % MIX_ROWS == 0 and seq % (2 * TILE) == 0

    mod3 = _ada_call(c, w_ada[0], b_ada[0]).reshape(bsz, 6, d)
    xs1, h2, lgt, wg_b, wu_b = _mixer_call(
        x, mod3, w_in[0].astype(BF16), attn_sinks[0], rel_bias,
        attn_norm_g[0].reshape(1, Q_COLS), conv_w[0], conv_b[0].reshape(1, CONV_CH),
        conv_ln_g[0].reshape(1, CONV_CH), conv_ln_b[0].reshape(1, CONV_CH),
        w_out[0].astype(BF16), w_router[0].T.astype(BF16),
        w_sh_gate[0].astype(BF16), w_sh_up[0].astype(BF16), w_sh_down[0].astype(BF16),
        w_exp_gate[0], w_exp_up[0])
    xs1 = xs1.reshape(t, d)
    h2 = h2.reshape(t, d)

    lp, gw, cnt = _route_call(lgt, router_bias[0])

    ntiles = t // TILE
    nassign = t * TOP_K
    n = cnt[:, :ntiles].T.astype(I32)
    local = jnp.cumsum(n, axis=1) - n
    earlier = jnp.cumsum(n, axis=0) - n
    total = jnp.sum(n, axis=0)
    region = (total + 1) // 2 * 2
    starts = jnp.cumsum(region) - region
    nalloc = nassign + N_EXPERTS
    used = jnp.sum(region)
    pad = jnp.stack([starts + total, region - total,
                     jnp.broadcast_to(used, (N_EXPERTS,))], axis=0)
    glob = starts[None, :] + earlier
    empty = n == 0
    length = jnp.maximum(n, 1)
    eid = jnp.arange(N_EXPERTS, dtype=I32)[None, :]
    parity = (jnp.arange(ntiles, dtype=I32) % 2)[:, None]
    spare_global = nalloc + EXPERT_ROWS + parity * N_EXPERTS + eid
    nempty = jnp.sum(empty.astype(I32), axis=1).reshape(ntiles // 2, 2)

    def table(loc, glo):
        fields = [f.reshape(ntiles // 2, _T_STRIDE) for f in (loc, length, glo)]
        return jnp.concatenate(fields + [nempty], axis=1).reshape(ntiles // 2, 1, _T_SIZE)

    tab_out = table(jnp.where(empty, 0, local), jnp.where(empty, spare_global, glob))
    tab_back = table(jnp.where(empty, LROWS + eid, local), jnp.where(empty, 0, glob))

    rows = EXPERT_ROWS
    left_over = total % (2 * rows)
    ntail = (((left_over > 0) & (left_over <= TAIL_CHUNK_ROWS))
             | ((left_over > rows) & (left_over <= rows + TAIL_CHUNK_ROWS))).astype(I32)
    nwhole = (2 * (total // (2 * rows)) + (left_over > TAIL_CHUNK_ROWS).astype(I32)
              + (left_over > rows + TAIL_CHUNK_ROWS).astype(I32))
    nchunk = nwhole + ntail
    chunk_end = jnp.cumsum(nchunk)
    first = chunk_end - nchunk
    max_chunks = nassign // rows + 2 * N_EXPERTS
    gid = jnp.arange(max_chunks, dtype=I32)
    owner = jnp.minimum(jnp.sum((chunk_end[None, :] <= gid[:, None]).astype(I32), axis=1), N_EXPERTS - 1)
    pick = owner[:, None] == eid
    within = gid - jnp.sum(jnp.where(pick, first[None, :], 0), axis=1)
    chunk_row = jnp.sum(jnp.where(pick, starts[None, :], 0), axis=1) + within * rows
    left = jnp.sum(jnp.where(pick, region[None, :], 0), axis=1) - within * rows
    live = gid < chunk_end[-1]
    chunk_row = jnp.where(live, chunk_row, 0)
    chunk_len = jnp.where(live, jnp.clip(left, 2, rows), 2)
    info = jnp.stack([chunk_end[-1], used])

    xs = _dispatch_call(tab_out, pad, h2, lp, nalloc + TAIL_ROWS)
    y = _expert_call(first, nwhole, ntail, chunk_row, chunk_len, info, xs, wg_b, wu_b, w_exp_down[0], nalloc)
    out = _combine_call(tab_back, lp, gw, xs1, mod3, final_norm_g, y, seq)
    return out.reshape(bsz, seq, d)
```

```python
import numpy as np
import jax
import jax.numpy as jnp
from jax import lax
from jax.experimental import pallas as pl
from jax.experimental.pallas import tpu as pltpu

F32 = jnp.float32
BF16 = jnp.bfloat16
U32 = jnp.uint32
I32 = jnp.int32

D_MODEL = 1024
HEAD_DIM = 64
N_HEADS = 8
N_KV_HEADS = 2
GROUP = N_HEADS // N_KV_HEADS
Q_COLS = N_HEADS * HEAD_DIM
KV_COLS = N_KV_HEADS * HEAD_DIM
ATTN_BLOCK = 128
WINDOW = 128
NUM_BUCKETS = 32
MAX_DISTANCE = 128
CONV_CH = D_MODEL - Q_COLS
CONV_WIDTH = 31
IN_COLS = Q_COLS + 2 * KV_COLS + 2 * CONV_CH
N_EXPERTS = 256
TOP_K = 8
N_GROUPS = 8
GROUP_SIZE = N_EXPERTS // N_GROUPS
TOPK_GROUPS = 4
EXPERT_HIDDEN = 256
ROUTED_SCALE = 2.5
EPS = 1e-6

MIX_ROWS = 512
HALO = 32
TILE = 256
LROWS = TILE * TOP_K
SORT_CHUNK = 256
EXPERT_ROWS = 576
TAIL_UNIT_ROWS = 128
TAIL_MAX_UNITS = 3
LANES = 128
SUBLANES = 8
PACK_W = D_MODEL // 2
PACK_S = PACK_W // LANES
VMEM_LIMIT = 56 * 1024 * 1024


def _sigmoid(v):
    return 1.0 / (1.0 + jnp.exp(-v))


def _rms(v):
    return v * lax.rsqrt(jnp.mean(v * v, axis=-1, keepdims=True) + EPS)


def _split_bf16(a):
    hi = a.astype(BF16)
    lo = (a - hi.astype(F32)).astype(BF16)
    return hi, lo


def _dot(a, b):
    return jnp.dot(a, b, preferred_element_type=F32)


def _dot_nt(a, b):
    return lax.dot_general(a, b, (((1,), (1,)), ((), ())), preferred_element_type=F32)


def _dot_tn(a, b):
    return lax.dot_general(a, b, (((0,), (0,)), ((), ())), preferred_element_type=F32)


def _dot3(a, b, dot):
    ah, al = _split_bf16(a)
    bh, bl = _split_bf16(b)
    return dot(ah, bh) + (dot(ah, bl) + dot(al, bh))


def _pack_rows(v):
    hi = lax.bitcast_convert_type(v[:, :PACK_W], U32) & jnp.uint32(0xFFFF0000)
    lo = lax.bitcast_convert_type(v[:, PACK_W:], U32) >> 16
    return hi | lo


def _unpack_rows(u):
    hi = lax.bitcast_convert_type(u & jnp.uint32(0xFFFF0000), F32)
    lo = lax.bitcast_convert_type(u << 16, F32)
    return jnp.concatenate([hi, lo], axis=1).astype(BF16)


def _load_flat(flat, r0, n):
    return jnp.concatenate(
        [flat[pl.ds(r0 * PACK_S + c, n, stride=PACK_S), :] for c in range(PACK_S)], axis=1)


def _store_flat(flat, r0, n, u):
    for c in range(PACK_S):
        flat[pl.ds(r0 * PACK_S + c, n, stride=PACK_S), :] = u[:, c * LANES:(c + 1) * LANES]


def _load_packed(ref3, r0, n):
    return _load_flat(ref3.reshape(ref3.shape[0] * PACK_S, LANES), r0, n)


def _store_packed(ref3, r0, n, u):
    _store_flat(ref3.reshape(ref3.shape[0] * PACK_S, LANES), r0, n, u)


def _ada_kernel(c_ref, w_ref, b_ref, o_ref):
    c = c_ref[...]
    s = c * _sigmoid(c)
    o_ref[...] = _dot3(s, w_ref[...], _dot) + b_ref[...]


def _ada_call(c, w_ada, b_ada):
    bsz, d = c.shape
    n = w_ada.shape[1]
    tn = 1536
    return pl.pallas_call(
        _ada_kernel,
        grid=(n // tn,),
        in_specs=[
            pl.BlockSpec((bsz, d), lambda i: (0, 0)),
            pl.BlockSpec((d, tn), lambda i: (0, i)),
            pl.BlockSpec((1, tn), lambda i: (0, i)),
        ],
        out_specs=pl.BlockSpec((bsz, tn), lambda i: (0, i)),
        out_shape=jax.ShapeDtypeStruct((bsz, n), F32),
        compiler_params=pltpu.CompilerParams(
            dimension_semantics=("arbitrary",), vmem_limit_bytes=VMEM_LIMIT),
        name="ada",
    )(c, w_ada, b_ada.reshape(1, n))


def _bucket_table():
    qi = np.arange(ATTN_BLOCK)[:, None]
    ki = np.arange(2 * ATTN_BLOCK)[None, :]
    dist = qi + ATTN_BLOCK - ki
    n = np.maximum(dist, 0)
    max_exact = NUM_BUCKETS // 2
    large = max_exact + (np.log(np.maximum(n, 1) / max_exact) / np.log(MAX_DISTANCE / max_exact)
                         * (NUM_BUCKETS - max_exact)).astype(np.int32)
    large = np.minimum(large, NUM_BUCKETS - 1)
    bkt = np.where(n < max_exact, n, large).astype(np.int32)
    band = (dist >= 0) & (dist < WINDOW)
    return np.ascontiguousarray(np.where(band, bkt, -1).astype(np.int32).T)


def _mixer_kernel(x_ref, mod_ref, win_ref, sinks_ref, relb_ref, bkt_ref, ang_ref, cw_ref, cb_ref,
                  lng_ref, lnb_ref, wout_ref, wrt_ref, wsg_ref, wsu_ref, wsd_ref, ewg_ref, ewu_ref,
                  xs1_ref, h2_ref, lgt_ref, ewgb_ref, ewub_ref,
                  kv_ref, uext_ref, bias_ref):
    tm = x_ref.shape[1]
    nblk = tm // ATTN_BLOCK
    j = pl.program_id(1)

    @pl.when((pl.program_id(0) == 0) & (j == 0))
    def _build_bias():
        bkt = bkt_ref[...]
        for h in range(N_HEADS):
            acc = jnp.full(bkt.shape, -jnp.inf, F32)
            for b in range(NUM_BUCKETS):
                acc = jnp.where(bkt == b, relb_ref[b, h], acc)
            g = h % GROUP
            bias_ref[h // GROUP, :, g * ATTN_BLOCK:(g + 1) * ATTN_BLOCK] = acc

    @pl.when(j == 0)
    def _reset_history():
        kv_ref[...] = jnp.zeros(kv_ref.shape, F32)
        uext_ref[0:HALO, :] = jnp.zeros((HALO, CONV_CH), F32)
        uext_ref[HALO + tm:HALO + tm + SUBLANES, :] = jnp.zeros((SUBLANES, CONV_CH), F32)

    x = x_ref[0]
    mod = mod_ref[0]
    sh1, sc1, g1 = mod[0:1], mod[1:2], mod[2:3]
    sh2, sc2, g2 = mod[3:4], mod[4:5], mod[5:6]

    h = _rms(x) * (1.0 + sc1) + sh1
    hb = h.astype(BF16)
    n_qkv = Q_COLS + 2 * KV_COLS
    proj_conv = _dot(hb, win_ref[:, n_qkv:])
    proj = _dot(hb, win_ref[:, :n_qkv])

    qb = (proj[:, :Q_COLS] * (HEAD_DIM ** -0.5)).astype(BF16)
    kv_cur = proj[:, Q_COLS:Q_COLS + 2 * KV_COLS]
    kvb = jnp.concatenate([kv_ref[...], kv_cur], axis=0).astype(BF16)
    kv_ref[...] = kv_cur[tm - ATTN_BLOCK:, :]
    not_first = j > 0
    key = lax.broadcasted_iota(jnp.int32, (2 * ATTN_BLOCK, GROUP * ATTN_BLOCK), 0)
    attn_rows = []
    for i in range(nblk):
        r0, r1, r2 = i * ATTN_BLOCK, (i + 1) * ATTN_BLOCK, (i + 2) * ATTN_BLOCK
        heads_t = [None] * N_HEADS
        for kh in range(N_KV_HEADS):
            kpc = kvb[r0:r2, kh * HEAD_DIM:(kh + 1) * HEAD_DIM]
            vpc = kvb[r0:r2, KV_COLS + kh * HEAD_DIM:KV_COLS + (kh + 1) * HEAD_DIM]
            qg = jnp.concatenate(
                [qb[r0:r1, (kh * GROUP + g) * HEAD_DIM:(kh * GROUP + g + 1) * HEAD_DIM]
                 for g in range(GROUP)], axis=0)
            logit = _dot_nt(kpc, qg) + bias_ref[kh]
            if i == 0:
                logit = jnp.where((key >= ATTN_BLOCK) | not_first, logit, -jnp.inf)
            sink = jnp.concatenate(
                [jnp.full((1, ATTN_BLOCK), sinks_ref[kh * GROUP + g], F32) for g in range(GROUP)], axis=1)
            m = jnp.maximum(jnp.max(logit, axis=0, keepdims=True), sink)
            p = jnp.exp(logit - m)
            den = jnp.sum(p, axis=0, keepdims=True) + jnp.exp(sink - m)
            o_t = _dot_tn(vpc, p.astype(BF16)) / den
            for g in range(GROUP):
                heads_t[kh * GROUP + g] = o_t[:, g * ATTN_BLOCK:(g + 1) * ATTN_BLOCK]
        attn_t = jnp.concatenate(heads_t, axis=0)
        scale = lax.rsqrt(jnp.mean(attn_t * attn_t, axis=0, keepdims=True) + EPS)
        attn_rows.append((attn_t * scale).T)
    attn = jnp.concatenate(attn_rows, axis=0) * ang_ref[...]

    a = proj_conv[:, :CONV_CH]
    gt = proj_conv[:, CONV_CH:]
    uext_ref[HALO:HALO + tm, :] = a * _sigmoid(gt)
    cw = cw_ref[...]
    base = HALO - (CONV_WIDTH - 1)
    acc = None
    for res in range(SUBLANES):
        part = None
        for hi in range((HALO + SUBLANES - 1) // SUBLANES + 1):
            t = hi * SUBLANES + res - base
            if 0 <= t < CONV_WIDTH:
                term = cw[t:t + 1, :] * uext_ref[hi * SUBLANES:hi * SUBLANES + tm + SUBLANES, :]
                part = term if part is None else part + term
        part = part[res:res + tm, :]
        acc = part if acc is None else acc + part
    uext_ref[0:HALO, :] = uext_ref[tm:tm + HALO, :]
    cv = acc + cb_ref[...]
    mu = jnp.mean(cv, axis=-1, keepdims=True)
    var = jnp.mean(jnp.square(cv - mu), axis=-1, keepdims=True)
    cv = (cv - mu) * lax.rsqrt(var + EPS) * lng_ref[...] + lnb_ref[...]
    cv = cv * _sigmoid(cv)

    mixed = (_dot(attn.astype(BF16), wout_ref[0:Q_COLS, :])
             + _dot(cv.astype(BF16), wout_ref[Q_COLS:, :]))
    x1 = x + g1 * mixed

    h2 = _rms(x1) * (1.0 + sc2) + sh2
    h2b = h2.astype(BF16)
    h2_ref[0] = h2b
    lgt_ref[...] = _dot_nt(wrt_ref[...], h2b)
    sg = _dot(h2b, wsg_ref[...])
    su = _dot(h2b, wsu_ref[...])
    shared = _dot(((sg * _sigmoid(sg)) * su).astype(BF16), wsd_ref[...])
    xs1_ref[0] = x1 + g2 * shared

    ewgb_ref[...] = ewg_ref[...].astype(BF16)
    ewub_ref[...] = ewu_ref[...].astype(BF16)


def _mixer_call(x, mod3, w_in_b, sinks, rel_bias, ang, cw, cb, lng, lnb, w_out_b, w_rt, wsg_b, wsu_b, wsd_b,
                w_gate, w_up):
    bsz, seq, d = x.shape
    tm = MIX_ROWS
    nj = seq // tm
    ne, _, hid = w_gate.shape
    per_step = ne // (bsz * nj)
    assert per_step * bsz * nj == ne
    bkt = jnp.asarray(_bucket_table())
    full = lambda shape: pl.BlockSpec(shape, lambda b, j: (0,) * len(shape), pipeline_mode=pl.Buffered(1))
    share = lambda shape: pl.BlockSpec((per_step,) + shape, lambda b, j: (b * nj + j, 0, 0))
    smem = pl.BlockSpec(memory_space=pltpu.SMEM)
    return pl.pallas_call(
        _mixer_kernel,
        grid=(bsz, nj),
        in_specs=[
            pl.BlockSpec((1, tm, d), lambda b, j: (b, j, 0)),
            pl.BlockSpec((1, 6, d), lambda b, j: (b, 0, 0)),
            full((d, IN_COLS)),
            smem, smem,
            full((2 * ATTN_BLOCK, ATTN_BLOCK)),
            full((1, Q_COLS)),
            full((CONV_WIDTH, CONV_CH)),
            full((1, CONV_CH)), full((1, CONV_CH)), full((1, CONV_CH)),
            full((d, d)),
            full((N_EXPERTS, d)),
            full((d, EXPERT_HIDDEN)), full((d, EXPERT_HIDDEN)), full((EXPERT_HIDDEN, d)),
            share((d, hid)), share((d, hid)),
        ],
        out_specs=[
            pl.BlockSpec((1, tm, d), lambda b, j: (b, j, 0)),
            pl.BlockSpec((1, tm, d), lambda b, j: (b, j, 0)),
            pl.BlockSpec((N_EXPERTS, tm), lambda b, j: (0, b * nj + j)),
            share((d, hid)), share((d, hid)),
        ],
        out_shape=[
            jax.ShapeDtypeStruct((bsz, seq, d), F32),
            jax.ShapeDtypeStruct((bsz, seq, d), BF16),
            jax.ShapeDtypeStruct((N_EXPERTS, bsz * seq), F32),
            jax.ShapeDtypeStruct((ne, d, hid), BF16),
            jax.ShapeDtypeStruct((ne, d, hid), BF16),
        ],
        scratch_shapes=[
            pltpu.VMEM((ATTN_BLOCK, 2 * KV_COLS), F32),
            pltpu.VMEM((HALO + tm + SUBLANES, CONV_CH), F32),
            pltpu.VMEM((N_KV_HEADS, 2 * ATTN_BLOCK, GROUP * ATTN_BLOCK), F32),
        ],
        compiler_params=pltpu.CompilerParams(
            dimension_semantics=("arbitrary", "arbitrary"), vmem_limit_bytes=VMEM_LIMIT),
        name="mixer",
    )(x, mod3, w_in_b, sinks, rel_bias, bkt, ang, cw, cb, lng, lnb, w_out_b, w_rt, wsg_b, wsu_b, wsd_b,
      w_gate, w_up)


def _route_kernel(lgt_ref, rb_ref, lp_ref, gw_ref, cnt_ref, earlier_ref, lower_ref):
    tr = lgt_ref.shape[1]
    i = pl.program_id(0)

    @pl.when(i == 0)
    def _():
        cnt_ref[...] = jnp.zeros(cnt_ref.shape, F32)
        earlier_ref[...] = jnp.where(
            lax.broadcasted_iota(I32, (tr, tr), 0) < lax.broadcasted_iota(I32, (tr, tr), 1), 1.0, 0.0).astype(BF16)
        lower_ref[...] = jnp.where(
            lax.broadcasted_iota(I32, (N_EXPERTS, N_EXPERTS), 1)
            < lax.broadcasted_iota(I32, (N_EXPERTS, N_EXPERTS), 0), 1.0, 0.0).astype(BF16)

    scores = _sigmoid(lgt_ref[...])
    sel = scores + rb_ref[...]
    neg = -jnp.inf

    sel3 = sel.reshape(N_GROUPS, GROUP_SIZE, tr)
    loc = lax.broadcasted_iota(I32, sel3.shape, 1)
    m1 = jnp.max(sel3, axis=1, keepdims=True)
    i1 = jnp.min(jnp.where(sel3 == m1, loc, GROUP_SIZE), axis=1, keepdims=True)
    m2 = jnp.max(jnp.where(loc == i1, neg, sel3), axis=1, keepdims=True)
    gscore = (m1 + m2).reshape(N_GROUPS, tr)

    gio = lax.broadcasted_iota(I32, gscore.shape, 0)
    gmask = jnp.zeros(gscore.shape, jnp.bool_)
    cur = gscore
    for _ in range(TOPK_GROUPS):
        m = jnp.max(cur, axis=0, keepdims=True)
        idx = jnp.min(jnp.where(cur == m, gio, N_GROUPS), axis=0, keepdims=True)
        pick = gio == idx
        gmask = gmask | pick
        cur = jnp.where(pick, neg, cur)
    emask = jnp.broadcast_to(gmask.reshape(N_GROUPS, 1, tr), sel3.shape).reshape(N_EXPERTS, tr)

    rio = lax.broadcasted_iota(I32, sel.shape, 0)
    cur = jnp.where(emask, sel, neg)
    picks, gsc = [], []
    for _ in range(TOP_K):
        m = jnp.max(cur, axis=0, keepdims=True)
        idx = jnp.min(jnp.where(cur == m, rio, N_EXPERTS), axis=0, keepdims=True)
        pick = rio == idx
        picks.append(pick)
        gsc.append(jnp.sum(jnp.where(pick, scores, 0.0), axis=0, keepdims=True))
        cur = jnp.where(pick, neg, cur)
    gsum = gsc[0]
    for k in range(1, TOP_K):
        gsum = gsum + gsc[k]
    gw_ref[...] = jnp.concatenate([g / gsum * ROUTED_SCALE for g in gsc], axis=0)

    onehot = jnp.where(cur == neg, jnp.where(emask, 1.0, 0.0), 0.0)
    before = _dot(onehot.astype(BF16), earlier_ref[...])
    n = jnp.sum(onehot, axis=1, keepdims=True)
    start = _dot(lower_ref[...],
                 jnp.broadcast_to(n, (N_EXPERTS, LANES)).astype(BF16))[:, 0:1]
    pos = start + before
    lp_ref[...] = jnp.concatenate(
        [jnp.sum(jnp.where(p, pos, 0.0), axis=0, keepdims=True) for p in picks], axis=0).astype(I32)

    lane = lax.broadcasted_iota(I32, cnt_ref.shape, 1)
    cnt_ref[...] += jnp.where(lane == i, n, 0.0)


def _route_call(lgt, router_bias):
    e, t = lgt.shape
    ntiles = t // TILE
    ntp = (ntiles + LANES - 1) // LANES * LANES
    return pl.pallas_call(
        _route_kernel,
        grid=(ntiles,),
        in_specs=[
            pl.BlockSpec((e, TILE), lambda i: (0, i)),
            pl.BlockSpec((e, 1), lambda i: (0, 0)),
        ],
        out_specs=[
            pl.BlockSpec((TOP_K, TILE), lambda i: (0, i)),
            pl.BlockSpec((TOP_K, TILE), lambda i: (0, i)),
            pl.BlockSpec((e, ntp), lambda i: (0, 0)),
        ],
        out_shape=[
            jax.ShapeDtypeStruct((TOP_K, t), I32),
            jax.ShapeDtypeStruct((TOP_K, t), F32),
            jax.ShapeDtypeStruct((e, ntp), F32),
        ],
        scratch_shapes=[pltpu.VMEM((TILE, TILE), BF16), pltpu.VMEM((e, e), BF16)],
        compiler_params=pltpu.CompilerParams(
            dimension_semantics=("arbitrary",), vmem_limit_bytes=VMEM_LIMIT),
        name="route",
    )(lgt, router_bias.reshape(e, 1))


_T_LOCAL, _T_LEN, _T_GLOBAL, _T_FIELDS = 0, 1, 2, 3
_T_STRIDE = 2 * N_EXPERTS
_T_EMPTY = _T_STRIDE * _T_FIELDS
_T_SIZE = _T_EMPTY + 2
LBUF_ROWS = LROWS + N_EXPERTS


def _segment_copies(tab_ref, half, local_ref, global_hbm, sem, to_global):
    def body(e, carry):
        at = half * N_EXPERTS + e
        n = tab_ref[0, 0, _T_LEN * _T_STRIDE + at]
        loc = local_ref.at[pl.ds(tab_ref[0, 0, _T_LOCAL * _T_STRIDE + at], n)]
        glo = global_hbm.at[pl.ds(tab_ref[0, 0, _T_GLOBAL * _T_STRIDE + at], n)]
        if to_global:
            pltpu.make_async_copy(loc, glo, sem).start()
        else:
            pltpu.make_async_copy(glo, loc, sem).start()
        return carry

    for e in range(N_EXPERTS):
        body(e, 0)


def _chunk_relative(lp_row, c):
    rel = lp_row - c * SORT_CHUNK
    inside = (rel >= 0) & (rel < SORT_CHUNK)
    return jnp.where(inside, rel, -1).astype(F32).astype(BF16)


def _chunk_row_ids():
    return lax.broadcasted_iota(I32, (SORT_CHUNK, TILE), 0).astype(F32).astype(BF16)


def _segment_wait(local_ref, global_hbm, sem, nempty):
    rows = LROWS + nempty
    pltpu.make_async_copy(global_hbm.at[pl.ds(0, rows)], local_ref.at[pl.ds(0, rows)], sem).wait()


TAIL_ROWS = EXPERT_ROWS + 2 * N_EXPERTS


def _dispatch_kernel(tab_ref, pad_ref, h2_ref, lp_ref, xs_hbm, xl0, xl1, zb, sem, pending):
    s = pl.program_id(0)
    ns = pl.num_programs(0)

    @pl.when(s == 0)
    def _zero_unassigned_rows():
        zb[...] = jnp.zeros(zb.shape, U32)
        used = pad_ref[2, 0]
        rest = xs_hbm.shape[0] - used
        cp = pltpu.make_async_copy(zb.at[pl.ds(0, rest)], xs_hbm.at[pl.ds(used, rest)], sem.at[2])
        cp.start()
        cp.wait()

        def pad_row(e):
            return pltpu.make_async_copy(zb.at[pl.ds(0, 1)], xs_hbm.at[pl.ds(pad_ref[0, e], 1)], sem.at[2])

        def start(e, carry):
            @pl.when(pad_ref[1, e] > 0)
            def _():
                pad_row(e).start()
            return carry

        def wait(e, carry):
            @pl.when(pad_ref[1, e] > 0)
            def _():
                pad_row(e).wait()
            return carry

        lax.fori_loop(0, N_EXPERTS, start, 0)
        lax.fori_loop(0, N_EXPERTS, wait, 0)

    for half, xl in ((0, xl0), (1, xl1)):
        @pl.when(s > 0)
        def _():
            _segment_wait(xl, xs_hbm, sem.at[half], pending[half])

        rows = h2_ref[half * TILE:(half + 1) * TILE, :]
        lp = lp_ref[:, half * TILE:(half + 1) * TILE]
        j = _chunk_row_ids()
        for c in range(LROWS // SORT_CHUNK):
            onehot = jnp.zeros((SORT_CHUNK, TILE), BF16)
            for k in range(TOP_K):
                onehot = jnp.where(j == _chunk_relative(lp[k:k + 1, :], c), jnp.ones((), BF16), onehot)
            _store_packed(xl, c * SORT_CHUNK, SORT_CHUNK, _pack_rows(_dot(onehot, rows)))
        _segment_copies(tab_ref, half, xl, xs_hbm, sem.at[half], to_global=True)
        pending[half] = tab_ref[0, 0, _T_EMPTY + half]

    @pl.when(s == ns - 1)
    def _():
        _segment_wait(xl0, xs_hbm, sem.at[0], pending[0])
        _segment_wait(xl1, xs_hbm, sem.at[1], pending[1])


def _dispatch_call(tab, pad, h2, lp, nrows):
    t, d = h2.shape
    ns = t // (2 * TILE)
    return pl.pallas_call(
        _dispatch_kernel,
        grid=(ns,),
        in_specs=[
            pl.BlockSpec((1, 1, _T_SIZE), lambda s: (s, 0, 0), memory_space=pltpu.SMEM),
            pl.BlockSpec(memory_space=pltpu.SMEM),
            pl.BlockSpec((2 * TILE, d), lambda s: (s, 0)),
            pl.BlockSpec((TOP_K, 2 * TILE), lambda s: (0, s)),
        ],
        out_specs=pl.BlockSpec(memory_space=pl.ANY),
        out_shape=jax.ShapeDtypeStruct((nrows, PACK_S, LANES), U32),
        scratch_shapes=[
            pltpu.VMEM((LBUF_ROWS, PACK_S, LANES), U32),
            pltpu.VMEM((LBUF_ROWS, PACK_S, LANES), U32),
            pltpu.VMEM((TAIL_ROWS + N_EXPERTS, PACK_S, LANES), U32),
            pltpu.SemaphoreType.DMA((3,)),
            pltpu.SMEM((2,), I32),
        ],
        compiler_params=pltpu.CompilerParams(
            dimension_semantics=("arbitrary",), vmem_limit_bytes=VMEM_LIMIT),
        name="dispatch",
    )(tab, pad, h2, lp)


RING = 6
AHEAD = RING - 2


def _expert_kernel(first_ref, nchunk_ref, ntail_ref, row_ref, len_ref, info_ref, wg_ref, wu_ref, wd_ref,
                   xs_hbm, y_hbm, xbuf, ybuf, isem, osem, wdb):
    e = pl.program_id(0)
    last_step = e == pl.num_programs(0) - 1
    rows = EXPERT_ROWS
    total_chunks = info_ref[0]

    def slab(ref, row0, m):
        size = m * PACK_S if isinstance(m, int) else pl.multiple_of(m * PACK_S, SUBLANES)
        first = row0 * PACK_S if isinstance(row0, int) else pl.multiple_of(row0 * PACK_S, SUBLANES)
        return ref.at[pl.ds(first, size)]

    def fetch(g):
        slot = g % RING
        return pltpu.make_async_copy(slab(xs_hbm, row_ref[g], rows), slab(xbuf, slot * rows, rows), isem.at[slot])

    def flush(g):
        slot = g % RING
        m = len_ref[g]
        return pltpu.make_async_copy(slab(ybuf, slot * rows, m), slab(y_hbm, row_ref[g], m), osem.at[slot])

    def start_fetch(g):
        @pl.when(g < total_chunks)
        def _():
            fetch(g).start()

    def wait_flush(g):
        @pl.when(g >= 0)
        def _():
            flush(g).wait()

    def compute(g, n=EXPERT_ROWS):
        slot = g % RING
        xb = _unpack_rows(_load_flat(xbuf, slot * rows, n))
        gate = _dot(xb, wg_ref[0])
        up = _dot(xb, wu_ref[0])
        hid = (gate * _sigmoid(gate)) * up
        y = _dot(hid.astype(BF16), wdb[...])
        _store_flat(ybuf, slot * rows, n, _pack_rows(y.astype(BF16).astype(F32)))

    @pl.when(e == 0)
    def _prime():
        for a in range(AHEAD):
            start_fetch(a)

    wdb[...] = wd_ref[0].astype(BF16)

    begin = first_ref[e]
    count = nchunk_ref[e]

    def pair(p, carry):
        g0 = begin + 2 * p
        g1 = g0 + 1
        start_fetch(g0 + AHEAD)
        start_fetch(g1 + AHEAD)
        fetch(g0).wait()
        fetch(g1).wait()
        wait_flush(g0 - RING)
        wait_flush(g1 - RING)
        compute(g0)
        compute(g1)
        flush(g0).start()
        flush(g1).start()
        return carry

    lax.fori_loop(0, count // 2, pair, 0)

    def single(g, n):
        start_fetch(g + AHEAD)
        fetch(g).wait()
        wait_flush(g - RING)
        compute(g, n)
        flush(g).start()

    @pl.when(count % 2 == 1)
    def _odd_chunk():
        single(begin + count - 1, EXPERT_ROWS)

    for units in range(1, TAIL_MAX_UNITS + 1):
        @pl.when(ntail_ref[e] == units)
        def _short_chunk():
            single(begin + count, units * TAIL_UNIT_ROWS)

    @pl.when(last_step)
    def _drain():
        def wait(g, carry):
            flush(g).wait()
            return carry
        lax.fori_loop(jnp.maximum(total_chunks - RING, 0), total_chunks, wait, 0)

    @pl.when(last_step)
    def _define_unused_rows():
        used = info_ref[1]
        rest = y_hbm.shape[0] // PACK_S - used

        @pl.when(rest > 0)
        def _():
            ybuf[0:N_EXPERTS * PACK_S, :] = jnp.zeros((N_EXPERTS * PACK_S, LANES), U32)
            cp = pltpu.make_async_copy(slab(ybuf, 0, rest), slab(y_hbm, used, rest), osem.at[0])
            cp.start()
            cp.wait()


def _expert_call(first, nchunk, ntail, chunk_row, chunk_len, info, xs, w_gate, w_up, w_down, nrows):
    ne, d, hid = w_gate.shape
    rows = EXPERT_ROWS
    wsel = lambda e, *_: (e, 0, 0)
    grid_spec = pltpu.PrefetchScalarGridSpec(
        num_scalar_prefetch=6,
        grid=(ne,),
        in_specs=[
            pl.BlockSpec((1, d, hid), wsel),
            pl.BlockSpec((1, d, hid), wsel),
            pl.BlockSpec((1, hid, d), wsel),
            pl.BlockSpec(memory_space=pl.ANY),
        ],
        out_specs=pl.BlockSpec(memory_space=pl.ANY),
        scratch_shapes=[
            pltpu.VMEM((RING * rows * PACK_S, LANES), U32),
            pltpu.VMEM((RING * rows * PACK_S, LANES), U32),
            pltpu.SemaphoreType.DMA((RING,)),
            pltpu.SemaphoreType.DMA((RING,)),
            pltpu.VMEM((hid, d), BF16),
        ],
    )
    return pl.pallas_call(
        _expert_kernel,
        grid_spec=grid_spec,
        out_shape=jax.ShapeDtypeStruct((nrows * PACK_S, LANES), U32),
        compiler_params=pltpu.CompilerParams(
            dimension_semantics=("arbitrary",), vmem_limit_bytes=VMEM_LIMIT),
        name="experts",
    )(first, nchunk, ntail, chunk_row, chunk_len, info, w_gate, w_up, w_down,
      xs.reshape(-1, LANES)).reshape(nrows, PACK_S, LANES)


def _combine_kernel(tab_ref, tabn_ref, lp_ref, gw_ref, xs1_ref, mod_ref, fg_ref, y_hbm, o_ref, yl0, yl1, sem):
    s = pl.program_id(0)
    ns = pl.num_programs(0)

    @pl.when(s == 0)
    def _():
        _segment_copies(tab_ref, 0, yl0, y_hbm, sem.at[0], to_global=False)

    g2 = mod_ref[0][5:6]
    for half, yl in ((0, yl0), (1, yl1)):
        _segment_wait(yl, y_hbm, sem.at[half], tab_ref[0, 0, _T_EMPTY + half])
        if half == 0:
            _segment_copies(tab_ref, 1, yl1, y_hbm, sem.at[1], to_global=False)
        lp = lp_ref[:, half * TILE:(half + 1) * TILE]
        gw = gw_ref[:, half * TILE:(half + 1) * TILE]
        acc = jnp.zeros((TILE, D_MODEL), F32)
        gwb = [gw[k:k + 1, :].astype(BF16) for k in range(TOP_K)]
        j = _chunk_row_ids()
        for c in range(LROWS // SORT_CHUNK):
            wt = jnp.zeros((SORT_CHUNK, TILE), BF16)
            for k in range(TOP_K):
                wt = jnp.where(j == _chunk_relative(lp[k:k + 1, :], c), gwb[k], wt)
            yv = _unpack_rows(_load_packed(yl, c * SORT_CHUNK, SORT_CHUNK))
            acc = acc + _dot_tn(wt, yv)
        if half == 0:
            _segment_copies(tabn_ref, 0, yl0, y_hbm, sem.at[0], to_global=False)
        x2 = xs1_ref[half * TILE:(half + 1) * TILE, :] + g2 * acc
        o_ref[half * TILE:(half + 1) * TILE, :] = _rms(x2) * fg_ref[...]

    @pl.when(s == ns - 1)
    def _():
        _segment_wait(yl0, y_hbm, sem.at[0], tab_ref[0, 0, _T_EMPTY])


def _combine_call(tab, lp, gw, xs1, mod3, final_g, y, seq):
    t, d = xs1.shape
    ns = t // (2 * TILE)
    per_seq = seq // (2 * TILE)
    return pl.pallas_call(
        _combine_kernel,
        grid=(ns,),
        in_specs=[
            pl.BlockSpec((1, 1, _T_SIZE), lambda s: (s, 0, 0), memory_space=pltpu.SMEM),
            pl.BlockSpec((1, 1, _T_SIZE), lambda s: (jnp.minimum(s + 1, ns - 1), 0, 0),
                         memory_space=pltpu.SMEM),
            pl.BlockSpec((TOP_K, 2 * TILE), lambda s: (0, s)),
            pl.BlockSpec((TOP_K, 2 * TILE), lambda s: (0, s)),
            pl.BlockSpec((2 * TILE, d), lambda s: (s, 0)),
            pl.BlockSpec((1, 6, d), lambda s: (s // per_seq, 0, 0)),
            pl.BlockSpec((1, d), lambda s: (0, 0)),
            pl.BlockSpec(memory_space=pl.ANY),
        ],
        out_specs=pl.BlockSpec((2 * TILE, d), lambda s: (s, 0)),
        out_shape=jax.ShapeDtypeStruct((t, d), F32),
        scratch_shapes=[
            pltpu.VMEM((LBUF_ROWS, PACK_S, LANES), U32),
            pltpu.VMEM((LBUF_ROWS, PACK_S, LANES), U32),
            pltpu.SemaphoreType.DMA((2,)),
        ],
        compiler_params=pltpu.CompilerParams(
            dimension_semantics=("arbitrary",), vmem_limit_bytes=VMEM_LIMIT),
        name="combine",
    )(tab, tab, lp, gw, xs1, mod3, final_g.reshape(1, d), y)


def kernel(x, c, w_ada, b_ada, w_in, attn_sinks, rel_bias, attn_norm_g, conv_w, conv_b, conv_ln_g,
           conv_ln_b, w_out, w_router, router_bias, w_exp_gate, w_exp_up, w_exp_down, w_sh_gate,
           w_sh_up, w_sh_down, final_norm_g):
    bsz, seq, d = x.shape
    t = bsz * seq
    assert w_ada.shape[0] == 1 and d == D_MODEL
    assert seq % MIX_ROWS == 0 and seq % (2 * TILE) == 0

    mod3 = _ada_call(c, w_ada[0], b_ada[0]).reshape(bsz, 6, d)
    xs1, h2, lgt, wg_b, wu_b = _mixer_call(
        x, mod3, w_in[0].astype(BF16), attn_sinks[0], rel_bias,
        attn_norm_g[0].reshape(1, Q_COLS), conv_w[0], conv_b[0].reshape(1, CONV_CH),
        conv_ln_g[0].reshape(1, CONV_CH), conv_ln_b[0].reshape(1, CONV_CH),
        w_out[0].astype(BF16), w_router[0].T.astype(BF16),
        w_sh_gate[0].astype(BF16), w_sh_up[0].astype(BF16), w_sh_down[0].astype(BF16),
        w_exp_gate[0], w_exp_up[0])
    xs1 = xs1.reshape(t, d)
    h2 = h2.reshape(t, d)

    lp, gw, cnt = _route_call(lgt, router_bias[0])

    ntiles = t // TILE
    nassign = t * TOP_K
    n = cnt[:, :ntiles].T.astype(I32)
    local = jnp.cumsum(n, axis=1) - n
    earlier = jnp.cumsum(n, axis=0) - n
    total = jnp.sum(n, axis=0)
    region = (total + 1) // 2 * 2
    starts = jnp.cumsum(region) - region
    nalloc = nassign + N_EXPERTS
    used = jnp.sum(region)
    pad = jnp.stack([starts + total, region - total,
                     jnp.broadcast_to(used, (N_EXPERTS,))], axis=0)
    glob = starts[None, :] + earlier
    empty = n == 0
    length = jnp.maximum(n, 1)
    eid = jnp.arange(N_EXPERTS, dtype=I32)[None, :]
    parity = (jnp.arange(ntiles, dtype=I32) % 2)[:, None]
    spare_global = nalloc + EXPERT_ROWS + parity * N_EXPERTS + eid
    nempty = jnp.sum(empty.astype(I32), axis=1).reshape(ntiles // 2, 2)

    def table(loc, glo):
        fields = [f.reshape(ntiles // 2, _T_STRIDE) for f in (loc, length, glo)]
        return jnp.concatenate(fields + [nempty], axis=1).reshape(ntiles // 2, 1, _T_SIZE)

    tab_out = table(jnp.where(empty, 0, local), jnp.where(empty, spare_global, glob))
    tab_back = table(jnp.where(empty, LROWS + eid, local), jnp.where(empty, 0, glob))

    rows = EXPERT_ROWS
    tail_max = TAIL_MAX_UNITS * TAIL_UNIT_ROWS
    left_over = total % (2 * rows)
    beyond = jnp.where(left_over > rows, left_over - rows, left_over)
    short = (beyond > 0) & (beyond <= tail_max) & ((left_over <= tail_max) | (left_over > rows))
    ntail = jnp.where(short, (beyond + TAIL_UNIT_ROWS - 1) // TAIL_UNIT_ROWS, 0).astype(I32)
    nwhole = (2 * (total // (2 * rows)) + (left_over > tail_max).astype(I32)
              + (left_over > rows + tail_max).astype(I32))
    nchunk = nwhole + (ntail > 0).astype(I32)
    chunk_end = jnp.cumsum(nchunk)
    first = chunk_end - nchunk
    max_chunks = nassign // rows + 2 * N_EXPERTS
    gid = jnp.arange(max_chunks, dtype=I32)
    owner = jnp.minimum(jnp.sum((chunk_end[None, :] <= gid[:, None]).astype(I32), axis=1), N_EXPERTS - 1)
    pick = owner[:, None] == eid
    within = gid - jnp.sum(jnp.where(pick, first[None, :], 0), axis=1)
    chunk_row = jnp.sum(jnp.where(pick, starts[None, :], 0), axis=1) + within * rows
    left = jnp.sum(jnp.where(pick, region[None, :], 0), axis=1) - within * rows
    live = gid < chunk_end[-1]
    chunk_row = jnp.where(live, chunk_row, 0)
    chunk_len = jnp.where(live, jnp.clip(left, 2, rows), 2)
    info = jnp.stack([chunk_end[-1], used])

    xs = _dispatch_call(tab_out, pad, h2, lp, nalloc + TAIL_ROWS)
    y = _expert_call(first, nwhole, ntail, chunk_row, chunk_len, info, xs, wg_b, wu_b, w_exp_down[0], nalloc)
    out = _combine_call(tab_back, lp, gw, xs1, mod3, final_norm_g, y, seq)
    return out.reshape(bsz, seq, d)
```
